```python
import jax
import jax.numpy as jnp
from jax import lax
import numpy as np

D_MODEL = 1024
BATCH = 8
SEQ = 2048
DEPTH = 2
DEC_BATCH = 128
DEC_SEQ = 1
PAST_LEN = 16384
PAGE_SIZE = 128

N_EVEN = (DEPTH + 1) // 2
N_ODD = DEPTH // 2
GLA_HEADS = 4
GLA_DV = D_MODEL // 8
GLA_DK = GLA_DV // 2
GLA_RANK = 16
GLA_TAU = 16.0
GLA_CHUNK = 64
GLA_QK = GLA_HEADS * GLA_DK
GLA_V = GLA_HEADS * GLA_DV
SG_HEADS = 4
SG_CHUNK = 128
SG_WIDTH = D_MODEL // 2
SG_DH = SG_WIDTH // SG_HEADS
CONV_WIDTH = D_MODEL // 2
CONV_K = 31
CONV_BUF = CONV_K - 1
POOL_WIDTH = D_MODEL // 2
POOL_WINDOWS = (2, 4, 8, 16)
POOL_GROUPS = 4
POOL_DG = POOL_WIDTH // POOL_GROUPS
POOL_BUF = 15
D_FF = 2816
EPS = 1e-6

EVEN_IN = 2 * GLA_QK + 2 * GLA_V + GLA_RANK + 2 * SG_WIDTH
EVEN_MIX = GLA_V + SG_WIDTH
EVEN_SPLITS = (GLA_QK, 2 * GLA_QK, 2 * GLA_QK + GLA_V, 2 * GLA_QK + 2 * GLA_V,
               2 * GLA_QK + 2 * GLA_V + GLA_RANK, 2 * GLA_QK + 2 * GLA_V + GLA_RANK + SG_WIDTH)
ODD_IN = 2 * CONV_WIDTH + POOL_WIDTH
ODD_MIX = CONV_WIDTH + POOL_WIDTH
ODD_SPLITS = (CONV_WIDTH, 2 * CONV_WIDTH)

kernel_name = 'hybrid_gla_gmlp_conformer_pool_step'


def rmsnorm(x, g):
    xf = x.astype(jnp.float32)
    y = xf * lax.rsqrt(jnp.mean(xf * xf, axis=-1, keepdims=True) + EPS)
    return (y * g.astype(jnp.float32)).astype(x.dtype)


def layernorm(x, g, b):
    xf = x.astype(jnp.float32)
    mu = jnp.mean(xf, axis=-1, keepdims=True)
    xc = xf - mu
    var = jnp.mean(xc * xc, axis=-1, keepdims=True)
    y = xc * lax.rsqrt(var + EPS) * g.astype(jnp.float32) + b.astype(jnp.float32)
    return y.astype(x.dtype)


def swiglu(x, w_in, w_out):
    a, b = jnp.split(x @ w_in, 2, axis=-1)
    return (jax.nn.silu(a) * b) @ w_out


def gla_recurrence(q, k, v, log_a, s0):
    bsz, t = q.shape[0], q.shape[1]
    c = GLA_CHUNK if t % GLA_CHUNK == 0 else t
    n = t // c

    def to_chunks(z):
        return jnp.moveaxis(z.astype(jnp.float32).reshape(bsz, n, c, *z.shape[2:]), 1, 0)

    mask = jnp.tril(jnp.ones((c, c), bool))[None, :, :, None, None]

    def step(s, inp):
        qc, kc, vc, gc = inp
        cum = jnp.cumsum(gc, axis=1)
        o_inter = jnp.einsum('bthk,bhkv->bthv', qc * jnp.exp(cum), s)
        diff = cum[:, :, None] - cum[:, None, :]
        decay = jnp.exp(jnp.where(mask, diff, -jnp.inf))
        scores = jnp.einsum('bthk,bshk,btshk->bhts', qc, kc, decay)
        o_intra = jnp.einsum('bhts,bshv->bthv', scores, vc)
        last = cum[:, -1]
        s_new = jnp.exp(last)[..., None] * s + jnp.einsum(
            'bshk,bshv->bhkv', kc * jnp.exp(last[:, None] - cum), vc)
        return s_new, o_inter + o_intra

    s_fin, o = lax.scan(step, s0.astype(jnp.float32),
                        (to_chunks(q), to_chunks(k), to_chunks(v), to_chunks(log_a)))
    o = jnp.moveaxis(o, 0, 1).reshape(bsz, t, GLA_HEADS, GLA_DV)
    return o, s_fin.astype(s0.dtype)


def even_mixer(h, s0, w_in, w_gate, b_gate, gla_g, sg_ln_g, sg_ln_b, sg_w, sg_b, w_out):
    bsz, t, _ = h.shape
    p = h @ w_in
    q, k, v, r, z, u_sg, v_sg = jnp.split(p, EVEN_SPLITS, axis=-1)
    q = q.reshape(bsz, t, GLA_HEADS, GLA_DK) * (GLA_DK ** -0.5)
    k = k.reshape(bsz, t, GLA_HEADS, GLA_DK)
    v = v.reshape(bsz, t, GLA_HEADS, GLA_DV)
    log_a = jax.nn.log_sigmoid((z @ w_gate + b_gate).astype(jnp.float32)) / GLA_TAU
    log_a = log_a.reshape(bsz, t, GLA_HEADS, GLA_DK)
    o, s_new = gla_recurrence(q, k, v, log_a, s0)
    o = rmsnorm(o, gla_g).reshape(bsz, t, GLA_V).astype(h.dtype)
    out_a = o * jax.nn.silu(r)
    u_sg = jax.nn.gelu(u_sg)
    v_sg = layernorm(jax.nn.gelu(v_sg), sg_ln_g, sg_ln_b)
    c = min(t, SG_CHUNK)
    n = t // c
    w_s = jnp.where(jnp.tril(jnp.ones((c, c), bool))[None], sg_w[:, :c, :c], 0.0).astype(v_sg.dtype)
    vh = v_sg.reshape(bsz, n, c, SG_HEADS, SG_DH)
    mixed = jnp.einsum('hts,bnshd->bnthd', w_s, vh) + jnp.transpose(sg_b[:, :c])[None, None, :, :, None]
    out_b = u_sg * mixed.reshape(bsz, t, SG_WIDTH)
    y = jnp.concatenate([out_a, out_b], axis=-1) @ w_out
    return y, s_new, v_sg[:, t - c:]


def odd_mixer(h, conv_buf, pool_buf, have_past, w_in, conv_w, conv_b, ln_g, ln_b, pool_w, pool_scale, w_out):
    bsz, t, _ = h.shape
    p = h @ w_in
    a, gt, xp = jnp.split(p, ODD_SPLITS, axis=-1)
    glu = a * jax.nn.sigmoid(gt)
    gpad = jnp.concatenate([conv_buf.astype(glu.dtype), glu], axis=1)
    conv = lax.conv_general_dilated(gpad, conv_w[:, None, :].astype(glu.dtype), (1,), 'VALID',
                                    dimension_numbers=('NWC', 'WIO', 'NWC'),
                                    feature_group_count=CONV_WIDTH) + conv_b
    out_c = jax.nn.silu(layernorm(conv, ln_g, ln_b))
    xpad = jnp.concatenate([pool_buf.astype(xp.dtype), xp], axis=1)
    cs = jnp.concatenate([jnp.zeros((bsz, 1, POOL_WIDTH), jnp.float32),
                          jnp.cumsum(xpad.astype(jnp.float32), axis=1)], axis=1)
    valid = jnp.concatenate([jnp.full((POOL_BUF,), 1.0 if have_past else 0.0, jnp.float32),
                             jnp.ones((t,), jnp.float32)])
    cv = jnp.concatenate([jnp.zeros((1,), jnp.float32), jnp.cumsum(valid)])
    hi = POOL_BUF + 1
    groups = []
    for gi, win in enumerate(POOL_WINDOWS):
        lo = hi - win
        chan = slice(gi * POOL_DG, (gi + 1) * POOL_DG)
        tot = cs[:, hi:hi + t, chan] - cs[:, lo:lo + t, chan]
        cnt = cv[hi:hi + t] - cv[lo:lo + t]
        groups.append(tot / cnt[None, :, None])
    pooled = jnp.stack(groups, axis=2) - xp.astype(jnp.float32).reshape(bsz, t, POOL_GROUPS, POOL_DG)
    out_d = jnp.einsum('btgc,gcd->btgd', pooled.astype(xp.dtype), pool_w).reshape(bsz, t, POOL_WIDTH) * pool_scale
    y = jnp.concatenate([out_c, out_d], axis=-1) @ w_out
    return y, gpad[:, -CONV_BUF:], xpad[:, -POOL_BUF:]


def run_trunk(x, st_gla, st_conv, st_pool, have_past, norm_g, ff_in, ff_out,
              ev_w_in, ev_w_gate, ev_b_gate, ev_gla_g, ev_sg_ln_g, ev_sg_ln_b, ev_sg_w, ev_sg_b, ev_w_out,
              od_w_in, od_conv_w, od_conv_b, od_ln_g, od_ln_b, od_pool_w, od_pool_scale, od_w_out, norm_f):
    new_gla, new_sgv, new_conv, new_pool = [], [], [], []
    for layer in range(DEPTH):
        i = layer // 2
        x = x + 0.5 * swiglu(rmsnorm(x, norm_g[layer, 0]), ff_in[layer, 0], ff_out[layer, 0])
        hn = rmsnorm(x, norm_g[layer, 1])
        if layer % 2 == 0:
            mix, s_g, v_rows = even_mixer(hn, st_gla[i], ev_w_in[i], ev_w_gate[i], ev_b_gate[i], ev_gla_g[i],
                                          ev_sg_ln_g[i], ev_sg_ln_b[i], ev_sg_w[i], ev_sg_b[i], ev_w_out[i])
            new_gla.append(s_g)
            new_sgv.append(v_rows)
        else:
            mix, c_buf, p_buf = odd_mixer(hn, st_conv[i], st_pool[i], have_past, od_w_in[i], od_conv_w[i],
                                          od_conv_b[i], od_ln_g[i], od_ln_b[i], od_pool_w[i],
                                          od_pool_scale[i], od_w_out[i])
            new_conv.append(c_buf)
            new_pool.append(p_buf)
        x = x + mix
        x = x + 0.5 * swiglu(rmsnorm(x, norm_g[layer, 2]), ff_in[layer, 1], ff_out[layer, 1])
    y = rmsnorm(x, norm_f)
    return y, jnp.stack(new_gla), jnp.stack(new_sgv), jnp.stack(new_conv), jnp.stack(new_pool)


def setup_inputs(seed: int = 0) -> dict:
    key = jax.random.key(seed)
    ks = jax.random.split(key, 32)
    f32 = jnp.float32

    def nrm(k, shape, scale):
        return jax.random.normal(k, shape, f32) * scale

    return {
        'x_prompt': nrm(ks[0], (BATCH, SEQ, D_MODEL), 1.0),
        'x_sample': nrm(ks[1], (DEC_BATCH, DEC_SEQ, D_MODEL), 1.0),
        'state_gla': nrm(ks[2], (N_EVEN, DEC_BATCH, GLA_HEADS, GLA_DK, GLA_DV), 1.0),
        'state_conv': nrm(ks[3], (N_ODD, DEC_BATCH, CONV_BUF, CONV_WIDTH), 0.5),
        'state_pool': nrm(ks[4], (N_ODD, DEC_BATCH, POOL_BUF, POOL_WIDTH), 1.0),
        'norm_g': 1.0 + nrm(ks[5], (DEPTH, 3, D_MODEL), 0.1),
        'ff_in': nrm(ks[6], (DEPTH, 2, D_MODEL, 2 * D_FF), D_MODEL ** -0.5),
        'ff_out': nrm(ks[7], (DEPTH, 2, D_FF, D_MODEL), D_FF ** -0.5),
        'ev_w_in': nrm(ks[8], (N_EVEN, D_MODEL, EVEN_IN), D_MODEL ** -0.5),
        'ev_w_gate': nrm(ks[9], (N_EVEN, GLA_RANK, GLA_QK), GLA_RANK ** -0.5),
        'ev_b_gate': nrm(ks[10], (N_EVEN, GLA_QK), 0.1),
        'ev_gla_g': 1.0 + nrm(ks[11], (N_EVEN, GLA_HEADS, GLA_DV), 0.1),
        'ev_sg_ln_g': 1.0 + nrm(ks[12], (N_EVEN, SG_WIDTH), 0.1),
        'ev_sg_ln_b': nrm(ks[13], (N_EVEN, SG_WIDTH), 0.02),
        'ev_sg_w': nrm(ks[14], (N_EVEN, SG_HEADS, SG_CHUNK, SG_CHUNK), SG_CHUNK ** -0.5),
        'ev_sg_b': 1.0 + nrm(ks[15], (N_EVEN, SG_HEADS, SG_CHUNK), 0.1),
        'ev_w_out': nrm(ks[16], (N_EVEN, EVEN_MIX, D_MODEL), EVEN_MIX ** -0.5),
        'od_w_in': nrm(ks[17], (N_ODD, D_MODEL, ODD_IN), D_MODEL ** -0.5),
        'od_conv_w': nrm(ks[18], (N_ODD, CONV_K, CONV_WIDTH), CONV_K ** -0.5),
        'od_conv_b': nrm(ks[19], (N_ODD, CONV_WIDTH), 0.02),
        'od_ln_g': 1.0 + nrm(ks[20], (N_ODD, CONV_WIDTH), 0.1),
        'od_ln_b': nrm(ks[21], (N_ODD, CONV_WIDTH), 0.02),
        'od_pool_w': nrm(ks[22], (N_ODD, POOL_GROUPS, POOL_DG, POOL_DG), POOL_DG ** -0.5),
        'od_pool_scale': 1.0 + nrm(ks[23], (N_ODD, POOL_WIDTH), 0.1),
        'od_w_out': nrm(ks[24], (N_ODD, ODD_MIX, D_MODEL), ODD_MIX ** -0.5),
        'norm_f': 1.0 + nrm(ks[25], (D_MODEL,), 0.1),
    }


def reference(x_prompt, x_sample, state_gla, state_conv, state_pool, norm_g, ff_in, ff_out,
              ev_w_in, ev_w_gate, ev_b_gate, ev_gla_g, ev_sg_ln_g, ev_sg_ln_b, ev_sg_w, ev_sg_b, ev_w_out,
              od_w_in, od_conv_w, od_conv_b, od_ln_g, od_ln_b, od_pool_w, od_pool_scale, od_w_out, norm_f):
    weights = (norm_g, ff_in, ff_out, ev_w_in, ev_w_gate, ev_b_gate, ev_gla_g, ev_sg_ln_g, ev_sg_ln_b,
               ev_sg_w, ev_sg_b, ev_w_out, od_w_in, od_conv_w, od_conv_b, od_ln_g, od_ln_b, od_pool_w,
               od_pool_scale, od_w_out, norm_f)
    bp = x_prompt.shape[0]
    zero_gla = jnp.zeros((N_EVEN, bp, GLA_HEADS, GLA_DK, GLA_DV), x_prompt.dtype)
    zero_conv = jnp.zeros((N_ODD, bp, CONV_BUF, CONV_WIDTH), x_prompt.dtype)
    zero_pool = jnp.zeros((N_ODD, bp, POOL_BUF, POOL_WIDTH), x_prompt.dtype)
    y_prompt, gla_prompt, sgv_prompt, conv_prompt, pool_prompt = run_trunk(
        x_prompt, zero_gla, zero_conv, zero_pool, False, *weights)
    y_sample, gla_sample, sgv_sample, conv_sample, pool_sample = run_trunk(
        x_sample, state_gla, state_conv, state_pool, True, *weights)
    return (y_prompt, y_sample, gla_prompt, gla_sample, sgv_prompt, sgv_sample,
            conv_prompt, conv_sample, pool_prompt, pool_sample)
```

```python
import functools

import jax
import jax.numpy as jnp
from jax import lax
from jax.experimental import pallas as pl
from jax.experimental.pallas import tpu as pltpu

EPS = 1e-6
BF16 = jnp.bfloat16
F32 = jnp.float32

VMEM_LIMIT_BYTES = 56 * 1024 * 1024


def _rms(x, g):
    return x * lax.rsqrt(jnp.mean(x * x, axis=-1, keepdims=True) + EPS) * g


def _dot(a, b):
    return jnp.dot(a, b, preferred_element_type=F32)


def _ffn_body(x_ref, g_ref, win_ref, wout_ref, gf_ref, o_ref, act_ref, *, d_ff, chunk, final_norm):
    x = x_ref[...]
    h = _rms(x, g_ref[...]).astype(BF16)
    for c in range(d_ff // chunk):
        a = _dot(h, win_ref[:, c * chunk:(c + 1) * chunk])
        b = _dot(h, win_ref[:, d_ff + c * chunk:d_ff + (c + 1) * chunk])
        act_ref[:, c * chunk:(c + 1) * chunk] = (a * jax.nn.sigmoid(a) * b).astype(BF16)
    y = x + 0.5 * _dot(act_ref[...], wout_ref[...])
    if final_norm:
        y = _rms(y, gf_ref[...])
    o_ref[...] = y


def _ffn(x, g, w_in, w_out, g_final, *, final_norm, block_rows):
    rows, d = x.shape
    d_ff = w_out.shape[0]
    tm = min(block_rows, rows)
    chunk = 256
    body = functools.partial(_ffn_body, d_ff=d_ff, chunk=chunk, final_norm=final_norm)
    const = lambda i: (0, 0)
    return pl.pallas_call(
        body,
        grid=(rows // tm,),
        in_specs=[
            pl.BlockSpec((tm, d), lambda i: (i, 0)),
            pl.BlockSpec((1, d), const),
            pl.BlockSpec((d, 2 * d_ff), const, pipeline_mode=pl.Buffered(1)),
            pl.BlockSpec((d_ff, d), const, pipeline_mode=pl.Buffered(1)),
            pl.BlockSpec((1, d), const),
        ],
        out_specs=pl.BlockSpec((tm, d), lambda i: (i, 0)),
        out_shape=jax.ShapeDtypeStruct((rows, d), F32),
        scratch_shapes=[pltpu.VMEM((tm, d_ff), BF16)],
        compiler_params=pltpu.CompilerParams(
            dimension_semantics=("arbitrary",), vmem_limit_bytes=VMEM_LIMIT_BYTES),
        name="ffn",
    )(x, g.reshape(1, d), w_in, w_out, g_final.reshape(1, d))


GLA_H = 4
GLA_DK = 64
GLA_DV = 128
GLA_QK = GLA_H * GLA_DK
GLA_V = GLA_H * GLA_DV
GLA_RANK_PAD = 128
GLA_INV_TAU = 1.0 / 16.0
GLA_CH = 16
SG_H = 4
SG_DH = 128
SG_W = SG_H * SG_DH
SG_CHUNK = 128
EV_Q, EV_V, EV_R, EV_U, EV_VS, EV_Z, EV_END = 0, 512, 1024, 1536, 2048, 2560, 2688


def _log_sigmoid(x):
    return jnp.minimum(x, 0.0) - jnp.log(1.0 + jnp.exp(-jnp.abs(x)))


def _split3(x):
    hi = x.astype(BF16)
    r1 = x - hi.astype(F32)
    mid = r1.astype(BF16)
    lo = (r1 - mid.astype(F32)).astype(BF16)
    return hi, mid, lo


def _dot3(sel, parts):
    return _dot(sel, parts[0]) + _dot(sel, parts[1]) + _dot(sel, parts[2])


def _head_masks(width, per_head, dtype):
    lane = lax.broadcasted_iota(jnp.int32, (1, width), 1)
    return [jnp.where(lane // per_head == h, 1.0, 0.0).astype(dtype) for h in range(width // per_head)]


def _even_prompt_body(x_ref, g_ref, win_ref, wgate_ref, bgate_ref, glag_ref, lng_ref, lnb_ref,
                      sgw_ref, sgb_ref, e_ref, wout_ref,
                      xo_ref, gla_ref, sgv_ref,
                      st_ref, q_ref, k_ref, cum_ref, qin_ref, kd_ref, dec_ref, v_ref, o_ref,
                      pcat_ref, acat_ref, mix_ref):
    j = pl.program_id(1)
    tt = x_ref.shape[0]
    n_chunks = tt // GLA_CH

    @pl.when(j == 0)
    def _():
        st_ref[...] = jnp.zeros_like(st_ref)

    x = x_ref[...]
    h = _rms(x, g_ref[...]).astype(BF16)

    qk = _dot(h, win_ref[:, EV_Q:EV_V])
    q = qk[:, :GLA_QK] * (GLA_DK ** -0.5)
    k = qk[:, GLA_QK:]
    v_ref[...] = _dot(h, win_ref[:, EV_V:EV_R]).astype(BF16)
    z = _dot(h, win_ref[:, EV_Z:EV_END]).astype(BF16)
    log_a = _log_sigmoid(_dot(z, wgate_ref[...]) + bgate_ref[...]) * GLA_INV_TAU

    row = lax.broadcasted_iota(jnp.int32, (tt, tt), 0)
    col = lax.broadcasted_iota(jnp.int32, (tt, tt), 1)
    same = (row // GLA_CH) == (col // GLA_CH)
    sel_cum = jnp.where(same & (col <= row), 1.0, 0.0).astype(BF16)
    sel_tot = jnp.where(same, 1.0, 0.0).astype(BF16)
    parts = _split3(log_a)
    cum = _dot3(sel_cum, parts)
    tot = _dot3(sel_tot, parts)
    q_ref[...] = q
    k_ref[...] = k
    cum_ref[...] = cum
    qin_ref[...] = (q * jnp.exp(cum)).astype(BF16)
    kd_ref[...] = (k * jnp.exp(tot - cum)).astype(BF16)
    dec_ref[...] = jnp.exp(tot)

    def pcat_body(n, carry):
        base = pl.multiple_of(n * GLA_CH, GLA_CH)
        qb = q_ref[pl.ds(base, GLA_CH), :]
        cb = cum_ref[pl.ds(base, GLA_CH), :]
        for s in range(GLA_CH):
            ks = k_ref[pl.ds(base + s, 1), :]
            cs = cum_ref[pl.ds(base + s, 1), :]
            p = qb * ks * jnp.exp(jnp.minimum(cb - cs, 0.0))
            pcat_ref[pl.ds(base, GLA_CH), s * GLA_QK:(s + 1) * GLA_QK] = p.astype(BF16)
        return carry

    lax.fori_loop(0, n_chunks, pcat_body, 0)
    scores = _dot(pcat_ref[...], e_ref[...])
    t_loc = lax.broadcasted_iota(jnp.int32, scores.shape, 0) % GLA_CH
    c_idx = lax.broadcasted_iota(jnp.int32, scores.shape, 1)
    causal = (c_idx % GLA_CH <= t_loc) & (c_idx < GLA_H * GLA_CH)
    acat_ref[...] = jnp.where(causal, scores, 0.0).astype(BF16)

    qk_masks = _head_masks(GLA_QK, GLA_DK, BF16)
    v_masks = _head_masks(GLA_V, GLA_DV, BF16)

    def chunk_body(n, carry):
        base = pl.multiple_of(n * GLA_CH, GLA_CH)
        qn = qin_ref[pl.ds(base, GLA_CH), :]
        kn = kd_ref[pl.ds(base, GLA_CH), :]
        vn = v_ref[pl.ds(base, GLA_CH), :]
        an = acat_ref[pl.ds(base, GLA_CH), :]
        st = st_ref[...]
        lq = jnp.concatenate([qn * m for m in qk_masks], axis=0)
        lk = jnp.concatenate([kn * m for m in qk_masks], axis=0)
        oi = lax.dot_general(lq, st.astype(BF16), (((1,), (1,)), ((), ())),
                             preferred_element_type=F32)
        o_inter = jnp.concatenate([oi[hh * GLA_CH:(hh + 1) * GLA_CH] for hh in range(GLA_H)], axis=1)
        vbd = jnp.concatenate([vn * m for m in v_masks]
                              + [jnp.zeros((acat_ref.shape[1] - GLA_H * GLA_CH, GLA_V), BF16)], axis=0)
        o_ref[pl.ds(base, GLA_CH), :] = o_inter + _dot(an, vbd)
        vs = jnp.concatenate([vn[:, hh * GLA_DV:(hh + 1) * GLA_DV] for hh in range(GLA_H)], axis=0)
        add = lax.dot_general(vs, lk, (((0,), (0,)), ((), ())), preferred_element_type=F32)
        st_ref[...] = st * dec_ref[pl.ds(base, 1), :] + add
        return carry

    lax.fori_loop(0, n_chunks, chunk_body, 0)

    o = o_ref[...]
    o_n = jnp.concatenate(
        [_rms(o[:, hh * GLA_DV:(hh + 1) * GLA_DV], 1.0) for hh in range(GLA_H)], axis=1) * glag_ref[...]
    r = _dot(h, win_ref[:, EV_R:EV_U])
    out_a = o_n * (r * jax.nn.sigmoid(r))

    u = jax.nn.gelu(_dot(h, win_ref[:, EV_U:EV_VS]))
    vv = jax.nn.gelu(_dot(h, win_ref[:, EV_VS:EV_Z]))
    mu = jnp.mean(vv, axis=-1, keepdims=True)
    xc = vv - mu
    var = jnp.mean(xc * xc, axis=-1, keepdims=True)
    v_ln = xc * lax.rsqrt(var + EPS) * lng_ref[...] + lnb_ref[...]
    v_lb = v_ln.astype(BF16)
    for c in range(tt // SG_CHUNK):
        for hh in range(SG_H):
            mix_ref[c * SG_CHUNK:(c + 1) * SG_CHUNK, hh * SG_DH:(hh + 1) * SG_DH] = _dot(
                sgw_ref[hh], v_lb[c * SG_CHUNK:(c + 1) * SG_CHUNK, hh * SG_DH:(hh + 1) * SG_DH])
    bias = jnp.concatenate([sgb_ref[...]] * (tt // SG_CHUNK), axis=0)
    out_b = u * (mix_ref[...] + bias)

    y = _dot(jnp.concatenate([out_a, out_b], axis=1).astype(BF16), wout_ref[...])
    xo_ref[...] = x + y

    @pl.when(j == pl.num_programs(1) - 1)
    def _():
        sgv_ref[0] = v_ln[tt - SG_CHUNK:, :]
        gla_ref[0] = st_ref[...].T.reshape(GLA_H, GLA_DK, GLA_DV)


def _score_sum_matrix():
    r = jnp.arange(GLA_CH * GLA_QK)
    c = jnp.arange(128)
    s, hh = r // GLA_QK, (r % GLA_QK) // GLA_DK
    return ((c[None, :] == (hh * GLA_CH + s)[:, None])).astype(BF16)


def _even_prompt(x, batch, g, w_in, w_gate, b_gate, gla_g, ln_g, ln_b, sg_w, sg_bias, w_out, *, block_rows):
    rows, d = x.shape
    t = rows // batch
    tt = min(block_rows, t)
    nj = t // tt
    const2 = lambda b, j: (0, 0)
    full = lambda a: pl.BlockSpec(a.shape, (lambda b, j: (0,) * a.ndim), pipeline_mode=pl.Buffered(1))
    e = _score_sum_matrix()
    operands = (g, w_in, w_gate, b_gate, gla_g, ln_g, ln_b, sg_w, sg_bias, e, w_out)
    return pl.pallas_call(
        _even_prompt_body,
        grid=(batch, nj),
        in_specs=[pl.BlockSpec((tt, d), lambda b, j: (b * nj + j, 0))] + [full(a) for a in operands],
        out_specs=[
            pl.BlockSpec((tt, d), lambda b, j: (b * nj + j, 0)),
            pl.BlockSpec((1, GLA_H, GLA_DK, GLA_DV), lambda b, j: (b, 0, 0, 0)),
            pl.BlockSpec((1, SG_CHUNK, SG_W), lambda b, j: (b, 0, 0)),
        ],
        out_shape=[
            jax.ShapeDtypeStruct((rows, d), F32),
            jax.ShapeDtypeStruct((batch, GLA_H, GLA_DK, GLA_DV), F32),
            jax.ShapeDtypeStruct((batch, SG_CHUNK, SG_W), F32),
        ],
        scratch_shapes=[
            pltpu.VMEM((GLA_DV, GLA_QK), F32),
            pltpu.VMEM((tt, GLA_QK), F32),
            pltpu.VMEM((tt, GLA_QK), F32),
            pltpu.VMEM((tt, GLA_QK), F32),
            pltpu.VMEM((tt, GLA_QK), BF16),
            pltpu.VMEM((tt, GLA_QK), BF16),
            pltpu.VMEM((tt, GLA_QK), F32),
            pltpu.VMEM((tt, GLA_V), BF16),
            pltpu.VMEM((tt, GLA_V), F32),
            pltpu.VMEM((tt, GLA_CH * GLA_QK), BF16),
            pltpu.VMEM((tt, 128), BF16),
            pltpu.VMEM((tt, SG_W), F32),
        ],
        compiler_params=pltpu.CompilerParams(
            dimension_semantics=("arbitrary", "arbitrary"), vmem_limit_bytes=VMEM_LIMIT_BYTES),
        name="even_prompt",
    )(x, *operands)


def _prep_even(w_in, w_gate, b_gate, gla_g, ln_g, ln_b, sg_w, sg_b, w_out):
    z0 = 2 * GLA_QK + 2 * GLA_V
    z1 = z0 + w_gate.shape[0]
    w_z = jnp.pad(w_in[:, z0:z1], ((0, 0), (0, GLA_RANK_PAD - (z1 - z0))))
    w_in2 = jnp.concatenate([w_in[:, :z0], w_in[:, z1:], w_z], axis=1).astype(BF16)
    w_gate2 = jnp.pad(w_gate, ((0, GLA_RANK_PAD - w_gate.shape[0]), (0, 0))).astype(BF16)
    tril = jnp.tril(jnp.ones((SG_CHUNK, SG_CHUNK), bool))
    sg_w_tril = jnp.where(tril[None], sg_w, 0.0).astype(BF16)
    sg_bias = jnp.repeat(jnp.transpose(sg_b), SG_DH, axis=1)
    return dict(
        w_in=w_in2, w_gate=w_gate2, b_gate=b_gate.reshape(1, -1), gla_g=gla_g.reshape(1, -1),
        ln_g=ln_g.reshape(1, -1), ln_b=ln_b.reshape(1, -1),
        sg_w_tril=sg_w_tril, sg_bias=sg_bias, w_out=w_out.astype(BF16),
        w_qk_t=jnp.transpose(w_in2[:, EV_Q:EV_V]), w_z_t=jnp.transpose(w_in2[:, EV_Z:EV_END]),
        w_gate_t=jnp.transpose(w_gate2), b_gate_col=b_gate.reshape(-1, 1),
        sg_w0=jnp.repeat(sg_w[:, 0, 0], SG_DH).reshape(1, -1), sg_b0=jnp.repeat(sg_b[:, 0], SG_DH).reshape(1, -1))


def _even_sample_body(x_ref, s_ref, g_ref, win_ref, wqkt_ref, wzt_ref, wgate_ref, wgatet_ref, bgate_ref,
                      bgatec_ref, glag_ref, lng_ref, lnb_ref, sgw0_ref, sgb0_ref, wout_ref,
                      xo_ref, so_ref, sgv_ref, o_ref):
    bb = x_ref.shape[0]
    x = x_ref[...]
    h = _rms(x, g_ref[...]).astype(BF16)
    nt = (((1,), (1,)), ((), ()))
    qk_t = lax.dot_general(wqkt_ref[...], h, nt, preferred_element_type=F32)
    q_t = qk_t[:GLA_QK] * (GLA_DK ** -0.5)
    k_t = qk_t[GLA_QK:].astype(BF16).astype(F32)
    z_t = lax.dot_general(wzt_ref[...], h, nt, preferred_element_type=F32).astype(BF16)
    a_t = jnp.exp(_log_sigmoid(_dot(wgatet_ref[...], z_t) + bgatec_ref[...]) * GLA_INV_TAU)
    v = _dot(h, win_ref[:, EV_V:EV_R])
    v_r = v.astype(BF16).astype(F32)
    for b in range(bb):
        s_old = s_ref[b].reshape(GLA_QK, GLA_DV)
        a_c = jnp.broadcast_to(a_t[:, b:b + 1], (GLA_QK, GLA_DV))
        k_c = jnp.broadcast_to(k_t[:, b:b + 1], (GLA_QK, GLA_DV))
        q_c = jnp.broadcast_to(q_t[:, b:b + 1], (GLA_QK, GLA_DV))
        v_rows = jnp.concatenate(
            [jnp.broadcast_to(v_r[b:b + 1, hh * GLA_DV:(hh + 1) * GLA_DV], (GLA_DK, GLA_DV))
             for hh in range(GLA_H)], axis=0)
        s_new = a_c * s_old + k_c * v_rows
        so_ref[b] = s_new.reshape(GLA_H, GLA_DK, GLA_DV)
        ob = jnp.sum((q_c * s_new).reshape(GLA_H, GLA_DK, GLA_DV), axis=1)
        o_ref[b:b + 1, :] = jnp.concatenate([ob[hh:hh + 1] for hh in range(GLA_H)], axis=1)
    o = o_ref[...]
    o_n = jnp.concatenate(
        [_rms(o[:, hh * GLA_DV:(hh + 1) * GLA_DV], 1.0) for hh in range(GLA_H)], axis=1) * glag_ref[...]
    r = _dot(h, win_ref[:, EV_R:EV_U])
    out_a = o_n * (r * jax.nn.sigmoid(r))
    u = jax.nn.gelu(_dot(h, win_ref[:, EV_U:EV_VS]))
    vv = jax.nn.gelu(_dot(h, win_ref[:, EV_VS:EV_Z]))
    mu = jnp.mean(vv, axis=-1, keepdims=True)
    xc = vv - mu
    var = jnp.mean(xc * xc, axis=-1, keepdims=True)
    v_ln = xc * lax.rsqrt(var + EPS) * lng_ref[...] + lnb_ref[...]
    sgv_ref[...] = v_ln
    out_b = u * (sgw0_ref[...] * v_ln + sgb0_ref[...])
    y = _dot(jnp.concatenate([out_a, out_b], axis=1).astype(BF16), wout_ref[...])
    xo_ref[...] = x + y


def _even_sample(x, state, g, ev, *, block_rows=32):
    rows, d = x.shape
    bb = min(block_rows, rows)
    full = lambda a: pl.BlockSpec(a.shape, (lambda i: (0,) * a.ndim), pipeline_mode=pl.Buffered(1))
    operands = (g, ev["w_in"], ev["w_qk_t"], ev["w_z_t"], ev["w_gate"], ev["w_gate_t"], ev["b_gate"],
                ev["b_gate_col"], ev["gla_g"], ev["ln_g"], ev["ln_b"], ev["sg_w0"], ev["sg_b0"], ev["w_out"])
    return pl.pallas_call(
        _even_sample_body,
        grid=(rows // bb,),
        in_specs=[pl.BlockSpec((bb, d), lambda i: (i, 0)),
                  pl.BlockSpec((bb, GLA_H, GLA_DK, GLA_DV), lambda i: (i, 0, 0, 0))]
                 + [full(a) for a in operands],
        out_specs=[
            pl.BlockSpec((bb, d), lambda i: (i, 0)),
            pl.BlockSpec((bb, GLA_H, GLA_DK, GLA_DV), lambda i: (i, 0, 0, 0)),
            pl.BlockSpec((bb, SG_W), lambda i: (i, 0)),
        ],
        out_shape=[
            jax.ShapeDtypeStruct((rows, d), F32),
            jax.ShapeDtypeStruct(state.shape, F32),
            jax.ShapeDtypeStruct((rows, SG_W), F32),
        ],
        scratch_shapes=[pltpu.VMEM((bb, GLA_V), F32)],
        compiler_params=pltpu.CompilerParams(
            dimension_semantics=("arbitrary",), vmem_limit_bytes=VMEM_LIMIT_BYTES),
        name="even_sample",
    )(x, state, *operands)


CONV_W = 512
CONV_K = 31
CONV_BUF = CONV_K - 1
CONV_PAD = 32
POOL_W = 512
POOL_WINDOWS = (2, 4, 8, 16)
POOL_DG = POOL_W // len(POOL_WINDOWS)
POOL_BUF = 15
POOL_PAD = 16
SUBLANES = 8


def _layernorm(x, g, b):
    mu = jnp.mean(x, axis=-1, keepdims=True)
    xc = x - mu
    var = jnp.mean(xc * xc, axis=-1, keepdims=True)
    return xc * lax.rsqrt(var + EPS) * g + b


def _odd_prompt_body(x_ref, g_ref, win_ref, cw_ref, cb_ref, lng_ref, lnb_ref, pw_ref, ps_ref, wout_ref,
                     xo_ref, conv_ref, pool_ref, gbuf_ref, pbuf_ref):
    j = pl.program_id(1)
    tt = x_ref.shape[0]

    @pl.when(j == 0)
    def _():
        gbuf_ref[0:CONV_PAD, :] = jnp.zeros((CONV_PAD, CONV_W), F32)
        pbuf_ref[0:POOL_PAD, :] = jnp.zeros((POOL_PAD, POOL_W), F32)

    x = x_ref[...]
    h = _rms(x, g_ref[...]).astype(BF16)
    a = _dot(h, win_ref[:, 0:CONV_W])
    gt = _dot(h, win_ref[:, CONV_W:2 * CONV_W])
    xp = _dot(h, win_ref[:, 2 * CONV_W:])
    gbuf_ref[CONV_PAD:CONV_PAD + tt, :] = a * jax.nn.sigmoid(gt)
    pbuf_ref[POOL_PAD:POOL_PAD + tt, :] = xp

    acc = jnp.zeros((tt, CONV_W), F32) + cb_ref[...]
    first = CONV_PAD - CONV_BUF
    for rr in range(SUBLANES):
        offs = [o for o in range(first, first + CONV_K) if o % SUBLANES == rr]
        span = max(offs) - rr + tt
        win = gbuf_ref[rr:rr + span, :]
        for o in offs:
            acc = acc + win[o - rr:o - rr + tt, :] * cw_ref[o - first:o - first + 1, :]
    out_c = _layernorm(acc, lng_ref[...], lnb_ref[...])
    out_c = out_c * jax.nn.sigmoid(out_c)

    t_glob = j * tt + lax.broadcasted_iota(jnp.int32, (tt, 1), 0)
    outs = []
    for gi, win_len in enumerate(POOL_WINDOWS):
        lanes = slice(gi * POOL_DG, (gi + 1) * POOL_DG)
        tot = pbuf_ref[POOL_PAD:POOL_PAD + tt, lanes]
        for dd in range(1, win_len):
            tot = tot + pbuf_ref[POOL_PAD - dd:POOL_PAD - dd + tt, lanes]
        cnt = jnp.minimum(win_len, t_glob + 1).astype(F32)
        pooled = tot / cnt - xp[:, lanes]
        outs.append(_dot(pooled.astype(BF16), pw_ref[gi]))
    out_d = jnp.concatenate(outs, axis=1) * ps_ref[...]

    y = _dot(jnp.concatenate([out_c, out_d], axis=1).astype(BF16), wout_ref[...])
    xo_ref[...] = x + y

    tail_g = gbuf_ref[tt:tt + CONV_PAD, :]
    tail_p = pbuf_ref[tt:tt + POOL_PAD, :]
    gbuf_ref[0:CONV_PAD, :] = tail_g
    pbuf_ref[0:POOL_PAD, :] = tail_p

    @pl.when(j == pl.num_programs(1) - 1)
    def _():
        conv_ref[0] = tail_g[CONV_PAD - CONV_BUF:, :]
        pool_ref[0] = tail_p[POOL_PAD - POOL_BUF:, :]


def _odd_prompt(x, batch, g, od, *, block_rows):
    rows, d = x.shape
    t = rows // batch
    tt = min(block_rows, t)
    nj = t // tt
    full = lambda a: pl.BlockSpec(a.shape, (lambda b, j: (0,) * a.ndim), pipeline_mode=pl.Buffered(1))
    operands = (g, od["w_in"], od["conv_w"], od["conv_b"], od["ln_g"], od["ln_b"], od["pool_w"],
                od["pool_scale"], od["w_out"])
    return pl.pallas_call(
        _odd_prompt_body,
        grid=(batch, nj),
        in_specs=[pl.BlockSpec((tt, d), lambda b, j: (b * nj + j, 0))] + [full(a) for a in operands],
        out_specs=[
            pl.BlockSpec((tt, d), lambda b, j: (b * nj + j, 0)),
            pl.BlockSpec((1, CONV_BUF, CONV_W), lambda b, j: (b, 0, 0)),
            pl.BlockSpec((1, POOL_BUF, POOL_W), lambda b, j: (b, 0, 0)),
        ],
        out_shape=[
            jax.ShapeDtypeStruct((rows, d), F32),
            jax.ShapeDtypeStruct((batch, CONV_BUF, CONV_W), F32),
            jax.ShapeDtypeStruct((batch, POOL_BUF, POOL_W), F32),
        ],
        scratch_shapes=[
            pltpu.VMEM((CONV_PAD + tt + SUBLANES, CONV_W), F32),
            pltpu.VMEM((POOL_PAD + tt, POOL_W), F32),
        ],
        compiler_params=pltpu.CompilerParams(
            dimension_semantics=("arbitrary", "arbitrary"), vmem_limit_bytes=VMEM_LIMIT_BYTES),
        name="odd_prompt",
    )(x, *operands)


def _odd_sample_body(x_ref, cbuf_ref, pbuf_ref, g_ref, win_ref, cw_ref, cb_ref, lng_ref, lnb_ref, pw_ref,
                     ps_ref, wout_ref, xo_ref, conv_ref, pool_ref):
    x = x_ref[...]
    h = _rms(x, g_ref[...]).astype(BF16)
    a = _dot(h, win_ref[:, 0:CONV_W])
    gt = _dot(h, win_ref[:, CONV_W:2 * CONV_W])
    xp = _dot(h, win_ref[:, 2 * CONV_W:])
    glu = a * jax.nn.sigmoid(gt)
    cbuf = cbuf_ref[...]
    pbuf = pbuf_ref[...]
    conv = (jnp.sum(cbuf * cw_ref[0:CONV_BUF, :][None], axis=1)
            + glu * cw_ref[CONV_BUF:CONV_K, :] + cb_ref[...])
    out_c = _layernorm(conv, lng_ref[...], lnb_ref[...])
    out_c = out_c * jax.nn.sigmoid(out_c)
    outs = []
    for gi, win_len in enumerate(POOL_WINDOWS):
        lanes = slice(gi * POOL_DG, (gi + 1) * POOL_DG)
        tot = xp[:, lanes] + jnp.sum(pbuf[:, POOL_BUF - (win_len - 1):, lanes], axis=1)
        pooled = tot / float(win_len) - xp[:, lanes]
        outs.append(_dot(pooled.astype(BF16), pw_ref[gi]))
    out_d = jnp.concatenate(outs, axis=1) * ps_ref[...]
    y = _dot(jnp.concatenate([out_c, out_d], axis=1).astype(BF16), wout_ref[...])
    xo_ref[...] = x + y
    conv_ref[:, 0:CONV_BUF - 1, :] = cbuf_ref[:, 1:CONV_BUF, :]
    conv_ref[:, CONV_BUF - 1:CONV_BUF, :] = glu[:, None, :]
    pool_ref[:, 0:POOL_BUF - 1, :] = pbuf_ref[:, 1:POOL_BUF, :]
    pool_ref[:, POOL_BUF - 1:POOL_BUF, :] = xp[:, None, :]


def _odd_sample(x, conv_buf, pool_buf, g, od, *, block_rows=32):
    rows, d = x.shape
    bb = min(block_rows, rows)
    full = lambda a: pl.BlockSpec(a.shape, (lambda i: (0,) * a.ndim), pipeline_mode=pl.Buffered(1))
    operands = (g, od["w_in"], od["conv_w"], od["conv_b"], od["ln_g"], od["ln_b"], od["pool_w"],
                od["pool_scale"], od["w_out"])
    return pl.pallas_call(
        _odd_sample_body,
        grid=(rows // bb,),
        in_specs=[pl.BlockSpec((bb, d), lambda i: (i, 0)),
                  pl.BlockSpec((bb, CONV_BUF, CONV_W), lambda i: (i, 0, 0)),
                  pl.BlockSpec((bb, POOL_BUF, POOL_W), lambda i: (i, 0, 0))]
                 + [full(a) for a in operands],
        out_specs=[
            pl.BlockSpec((bb, d), lambda i: (i, 0)),
            pl.BlockSpec((bb, CONV_BUF, CONV_W), lambda i: (i, 0, 0)),
            pl.BlockSpec((bb, POOL_BUF, POOL_W), lambda i: (i, 0, 0)),
        ],
        out_shape=[
            jax.ShapeDtypeStruct((rows, d), F32),
            jax.ShapeDtypeStruct(conv_buf.shape, F32),
            jax.ShapeDtypeStruct(pool_buf.shape, F32),
        ],
        compiler_params=pltpu.CompilerParams(
            dimension_semantics=("arbitrary",), vmem_limit_bytes=VMEM_LIMIT_BYTES),
        name="odd_sample",
    )(x, conv_buf, pool_buf, *operands)


def _prep_odd(w_in, conv_w, conv_b, ln_g, ln_b, pool_w, pool_scale, w_out):
    return dict(w_in=w_in.astype(BF16), conv_w=conv_w, conv_b=conv_b.reshape(1, -1), ln_g=ln_g.reshape(1, -1),
                ln_b=ln_b.reshape(1, -1), pool_w=pool_w.astype(BF16), pool_scale=pool_scale.reshape(1, -1),
                w_out=w_out.astype(BF16))


def kernel(x_prompt, x_sample, state_gla, state_conv, state_pool, norm_g, ff_in, ff_out, ev_w_in, ev_w_gate, ev_b_gate, ev_gla_g, ev_sg_ln_g, ev_sg_ln_b, ev_sg_w, ev_sg_b, ev_w_out, od_w_in, od_conv_w, od_conv_b, od_ln_g, od_ln_b, od_pool_w, od_pool_scale, od_w_out, norm_f):
    bp, t, d = x_prompt.shape
    bs = x_sample.shape[0]
    depth = norm_g.shape[0]
    xp = x_prompt.reshape(bp * t, d)
    xs = x_sample.reshape(bs, d)
    gla_p, gla_s, sgv_p, sgv_s, conv_p, conv_s, pool_p, pool_s = [], [], [], [], [], [], [], []
    for layer in range(depth):
        i = layer // 2
        last = layer == depth - 1
        w1_in, w1_out = ff_in[layer, 0].astype(BF16), ff_out[layer, 0].astype(BF16)
        xp = _ffn(xp, norm_g[layer, 0], w1_in, w1_out, norm_f, final_norm=False, block_rows=512)
        xs = _ffn(xs, norm_g[layer, 0], w1_in, w1_out, norm_f, final_norm=False, block_rows=512)
        g_mix = norm_g[layer, 1].reshape(1, d)
        if layer % 2 == 0:
            ev = _prep_even(ev_w_in[i], ev_w_gate[i], ev_b_gate[i], ev_gla_g[i], ev_sg_ln_g[i], ev_sg_ln_b[i],
                            ev_sg_w[i], ev_sg_b[i], ev_w_out[i])
            xp, s_p, v_p = _even_prompt(xp, bp, g_mix, ev["w_in"], ev["w_gate"], ev["b_gate"], ev["gla_g"],
                                        ev["ln_g"], ev["ln_b"], ev["sg_w_tril"], ev["sg_bias"], ev["w_out"],
                                        block_rows=256)
            xs, s_s, v_s = _even_sample(xs, state_gla[i], g_mix, ev)
            gla_p.append(s_p); gla_s.append(s_s); sgv_p.append(v_p); sgv_s.append(v_s.reshape(bs, 1, SG_W))
        else:
            od = _prep_odd(od_w_in[i], od_conv_w[i], od_conv_b[i], od_ln_g[i], od_ln_b[i], od_pool_w[i],
                           od_pool_scale[i], od_w_out[i])
            xp, c_p, p_p = _odd_prompt(xp, bp, g_mix, od, block_rows=256)
            xs, c_s, p_s = _odd_sample(xs, state_conv[i], state_pool[i], g_mix, od)
            conv_p.append(c_p); conv_s.append(c_s); pool_p.append(p_p); pool_s.append(p_s)
        w2_in, w2_out = ff_in[layer, 1].astype(BF16), ff_out[layer, 1].astype(BF16)
        xp = _ffn(xp, norm_g[layer, 2], w2_in, w2_out, norm_f, final_norm=last, block_rows=512)
        xs = _ffn(xs, norm_g[layer, 2], w2_in, w2_out, norm_f, final_norm=last, block_rows=512)
    return (xp.reshape(bp, t, d), xs.reshape(bs, 1, d), jnp.stack(gla_p), jnp.stack(gla_s),
            jnp.stack(sgv_p), jnp.stack(sgv_s), jnp.stack(conv_p), jnp.stack(conv_s),
            jnp.stack(pool_p), jnp.stack(pool_s))
```

```python
import functools

import jax
import jax.numpy as jnp
from jax import lax
from jax.experimental import pallas as pl
from jax.experimental.pallas import tpu as pltpu

EPS = 1e-6
LOG2_E = 1.4426950408889634
BF16 = jnp.bfloat16
F32 = jnp.float32

VMEM_LIMIT_BYTES = 56 * 1024 * 1024


def _rms(x, g):
    return x * lax.rsqrt(jnp.mean(x * x, axis=-1, keepdims=True) + EPS) * g


def _dot(a, b):
    return jnp.dot(a, b, preferred_element_type=F32)


def _ffn_body(x_ref, g_ref, win_ref, wout_ref, gf_ref, o_ref, act_ref, *, d_ff, chunk, final_norm):
    x = x_ref[...]
    h = _rms(x, g_ref[...]).astype(BF16)
    for c in range(d_ff // chunk):
        a = _dot(h, win_ref[:, c * chunk:(c + 1) * chunk])
        b = _dot(h, win_ref[:, d_ff + c * chunk:d_ff + (c + 1) * chunk])
        act_ref[:, c * chunk:(c + 1) * chunk] = (a * jax.nn.sigmoid(a) * b).astype(BF16)
    y = x + 0.5 * _dot(act_ref[...], wout_ref[...])
    if final_norm:
        y = _rms(y, gf_ref[...])
    o_ref[...] = y


def _ffn(x, g, w_in, w_out, g_final, *, final_norm, block_rows):
    rows, d = x.shape
    d_ff = w_out.shape[0]
    tm = min(block_rows, rows)
    chunk = 256
    body = functools.partial(_ffn_body, d_ff=d_ff, chunk=chunk, final_norm=final_norm)
    const = lambda i: (0, 0)
    return pl.pallas_call(
        body,
        grid=(rows // tm,),
        in_specs=[
            pl.BlockSpec((tm, d), lambda i: (i, 0)),
            pl.BlockSpec((1, d), const),
            pl.BlockSpec((d, 2 * d_ff), const, pipeline_mode=pl.Buffered(1)),
            pl.BlockSpec((d_ff, d), const, pipeline_mode=pl.Buffered(1)),
            pl.BlockSpec((1, d), const),
        ],
        out_specs=pl.BlockSpec((tm, d), lambda i: (i, 0)),
        out_shape=jax.ShapeDtypeStruct((rows, d), F32),
        scratch_shapes=[pltpu.VMEM((tm, d_ff), BF16)],
        compiler_params=pltpu.CompilerParams(
            dimension_semantics=("arbitrary",), vmem_limit_bytes=VMEM_LIMIT_BYTES),
        name="ffn",
    )(x, g.reshape(1, d), w_in, w_out, g_final.reshape(1, d))


GLA_H = 4
GLA_DK = 64
GLA_DV = 128
GLA_QK = GLA_H * GLA_DK
GLA_V = GLA_H * GLA_DV
GLA_RANK_PAD = 128
GLA_INV_TAU = 1.0 / 16.0
GLA_CH = 16
SG_H = 4
SG_DH = 128
SG_W = SG_H * SG_DH
SG_CHUNK = 128
EV_Q, EV_V, EV_R, EV_U, EV_VS, EV_Z, EV_END = 0, 512, 1024, 1536, 2048, 2560, 2688


def _log_sigmoid(x):
    return jnp.minimum(x, 0.0) - jnp.log(1.0 + jnp.exp(-jnp.abs(x)))


def _split3(x):
    hi = x.astype(BF16)
    r1 = x - hi.astype(F32)
    mid = r1.astype(BF16)
    lo = (r1 - mid.astype(F32)).astype(BF16)
    return hi, mid, lo


def _dot3(sel, parts):
    return _dot(sel, parts[0]) + _dot(sel, parts[1]) + _dot(sel, parts[2])


def _head_masks(width, per_head, dtype):
    lane = lax.broadcasted_iota(jnp.int32, (1, width), 1)
    return [jnp.where(lane // per_head == h, 1.0, 0.0).astype(dtype) for h in range(width // per_head)]


def _even_prompt_body(x_ref, g_ref, win_ref, wgate_ref, bgate_ref, glag_ref, lng_ref, lnb_ref,
                      sgw_ref, sgb_ref, e_ref, wout_ref,
                      xo_ref, gla_ref, sgv_ref,
                      st_ref, q_ref, k_ref, cum2_ref, qin_ref, kd_ref, dec_ref, v_ref, o_ref,
                      pcat_ref, acat_ref, mix_ref, add_ref, stb_ref, ruv_ref):
    j = pl.program_id(1)
    tt = x_ref.shape[0]
    n_chunks = tt // GLA_CH

    @pl.when(j == 0)
    def _():
        st_ref[...] = jnp.zeros_like(st_ref)

    x = x_ref[...]
    h = _rms(x, g_ref[...]).astype(BF16)

    qk = _dot(h, win_ref[:, EV_Q:EV_V])
    q = qk[:, :GLA_QK] * (GLA_DK ** -0.5)
    k = qk[:, GLA_QK:]
    v_ref[...] = _dot(h, win_ref[:, EV_V:EV_R]).astype(BF16)
    z = _dot(h, win_ref[:, EV_Z:EV_END]).astype(BF16)
    log_a = _log_sigmoid(_dot(z, wgate_ref[...]) + bgate_ref[...]) * GLA_INV_TAU

    row = lax.broadcasted_iota(jnp.int32, (tt, tt), 0)
    col = lax.broadcasted_iota(jnp.int32, (tt, tt), 1)
    same = (row // GLA_CH) == (col // GLA_CH)
    sel_cum = jnp.where(same & (col <= row), 1.0, 0.0).astype(BF16)
    sel_tot = jnp.where(same, 1.0, 0.0).astype(BF16)
    parts = _split3(log_a)
    cum = _dot3(sel_cum, parts)
    tot = _dot3(sel_tot, parts)
    q_ref[...] = q
    k_ref[...] = k
    cum2_ref[...] = cum * LOG2_E
    qin_ref[...] = (q * jnp.exp(cum)).astype(BF16)
    kd_ref[...] = (k * jnp.exp(tot - cum)).astype(BF16)
    dec_ref[...] = jnp.exp(tot)

    half = GLA_CH // 2
    proj_block = 256
    proj_cols = list(range(EV_R, EV_Z, proj_block))
    for n in range(n_chunks):
        if n % 2 == 1 and n // 2 < len(proj_cols):
            c0 = proj_cols[n // 2]
            ruv_ref[:, c0 - EV_R:c0 - EV_R + proj_block] = _dot(h, win_ref[:, c0:c0 + proj_block])
        base = n * GLA_CH
        qb = q_ref[base:base + GLA_CH, :]
        cb = cum2_ref[base:base + GLA_CH, :]
        for s in range(GLA_CH):
            ks = k_ref[base + s:base + s + 1, :]
            cs = cum2_ref[base + s:base + s + 1, :]
            if s < half:
                p = qb * ks * jnp.exp2(jnp.minimum(cb - cs, 0.0))
            else:
                p_hi = qb[half:] * ks * jnp.exp2(jnp.minimum(cb[half:] - cs, 0.0))
                p = jnp.concatenate([jnp.zeros_like(p_hi), p_hi], axis=0)
            pcat_ref[base:base + GLA_CH, s * GLA_QK:(s + 1) * GLA_QK] = p.astype(BF16)
    scores = _dot(pcat_ref[...], e_ref[...])

    u = jax.nn.gelu(ruv_ref[:, EV_U - EV_R:EV_VS - EV_R])
    v_ln = _layernorm(jax.nn.gelu(ruv_ref[:, EV_VS - EV_R:EV_Z - EV_R]), lng_ref[...], lnb_ref[...])
    v_lb = v_ln.astype(BF16)

    t_loc = lax.broadcasted_iota(jnp.int32, scores.shape, 0) % GLA_CH
    c_idx = lax.broadcasted_iota(jnp.int32, scores.shape, 1)
    causal = (c_idx % GLA_CH <= t_loc) & (c_idx < GLA_H * GLA_CH)
    acat_ref[...] = jnp.where(causal, scores, 0.0).astype(BF16)

    qk_masks = _head_masks(GLA_QK, GLA_DK, BF16)
    v_masks = _head_masks(GLA_V, GLA_DV, BF16)

    for n in range(n_chunks):
        rows = slice(n * GLA_CH, (n + 1) * GLA_CH)
        kn = kd_ref[rows, :]
        vn = v_ref[rows, :]
        lk = jnp.concatenate([kn * m for m in qk_masks], axis=0)
        vs = jnp.concatenate([vn[:, hh * GLA_DV:(hh + 1) * GLA_DV] for hh in range(GLA_H)], axis=0)
        add_ref[n] = lax.dot_general(vs, lk, (((0,), (0,)), ((), ())), preferred_element_type=F32)
    r = ruv_ref[:, 0:EV_U - EV_R]
    gate_r = r * jax.nn.sigmoid(r)
    sg_tasks = [(c, hh) for c in range(tt // SG_CHUNK) for hh in range(SG_H)]
    st = st_ref[...]
    for n in range(n_chunks):
        stb_ref[n] = st.astype(BF16)
        st = st * dec_ref[n * GLA_CH:n * GLA_CH + 1, :] + add_ref[n]
        if n < len(sg_tasks):
            c, hh = sg_tasks[n]
            mix_ref[c * SG_CHUNK:(c + 1) * SG_CHUNK, hh * SG_DH:(hh + 1) * SG_DH] = _dot(
                sgw_ref[hh], v_lb[c * SG_CHUNK:(c + 1) * SG_CHUNK, hh * SG_DH:(hh + 1) * SG_DH])
    st_ref[...] = st
    assert len(sg_tasks) <= n_chunks
    for n in range(n_chunks):
        rows = slice(n * GLA_CH, (n + 1) * GLA_CH)
        qn = qin_ref[rows, :]
        vn = v_ref[rows, :]
        lq = jnp.concatenate([qn * m for m in qk_masks], axis=0)
        oi = lax.dot_general(lq, stb_ref[n], (((1,), (1,)), ((), ())),
                             preferred_element_type=F32)
        o_inter = jnp.concatenate([oi[hh * GLA_CH:(hh + 1) * GLA_CH] for hh in range(GLA_H)], axis=1)
        vbd = jnp.concatenate([vn * m for m in v_masks]
                              + [jnp.zeros((acat_ref.shape[1] - GLA_H * GLA_CH, GLA_V), BF16)], axis=0)
        o_ref[rows, :] = o_inter + _dot(acat_ref[rows, :], vbd)

    o = o_ref[...]
    o_n = jnp.concatenate(
        [_rms(o[:, hh * GLA_DV:(hh + 1) * GLA_DV], 1.0) for hh in range(GLA_H)], axis=1) * glag_ref[...]
    out_a = o_n * gate_r
    bias = jnp.concatenate([sgb_ref[...]] * (tt // SG_CHUNK), axis=0)
    out_b = u * (mix_ref[...] + bias)

    y = _dot(jnp.concatenate([out_a, out_b], axis=1).astype(BF16), wout_ref[...])
    xo_ref[...] = x + y

    @pl.when(j == pl.num_programs(1) - 1)
    def _():
        sgv_ref[0] = v_ln[tt - SG_CHUNK:, :]
        gla_ref[0] = st_ref[...].T.reshape(GLA_H, GLA_DK, GLA_DV)


def _score_sum_matrix():
    r = jnp.arange(GLA_CH * GLA_QK)
    c = jnp.arange(128)
    s, hh = r // GLA_QK, (r % GLA_QK) // GLA_DK
    return ((c[None, :] == (hh * GLA_CH + s)[:, None])).astype(BF16)


def _even_prompt(x, batch, g, w_in, w_gate, b_gate, gla_g, ln_g, ln_b, sg_w, sg_bias, w_out, *, block_rows):
    rows, d = x.shape
    t = rows // batch
    tt = min(block_rows, t)
    nj = t // tt
    const2 = lambda b, j: (0, 0)
    full = lambda a: pl.BlockSpec(a.shape, (lambda b, j: (0,) * a.ndim), pipeline_mode=pl.Buffered(1))
    e = _score_sum_matrix()
    operands = (g, w_in, w_gate, b_gate, gla_g, ln_g, ln_b, sg_w, sg_bias, e, w_out)
    return pl.pallas_call(
        _even_prompt_body,
        grid=(batch, nj),
        in_specs=[pl.BlockSpec((tt, d), lambda b, j: (b * nj + j, 0))] + [full(a) for a in operands],
        out_specs=[
            pl.BlockSpec((tt, d), lambda b, j: (b * nj + j, 0)),
            pl.BlockSpec((1, GLA_H, GLA_DK, GLA_DV), lambda b, j: (b, 0, 0, 0)),
            pl.BlockSpec((1, SG_CHUNK, SG_W), lambda b, j: (b, 0, 0)),
        ],
        out_shape=[
            jax.ShapeDtypeStruct((rows, d), F32),
            jax.ShapeDtypeStruct((batch, GLA_H, GLA_DK, GLA_DV), F32),
            jax.ShapeDtypeStruct((batch, SG_CHUNK, SG_W), F32),
        ],
        scratch_shapes=[
            pltpu.VMEM((GLA_DV, GLA_QK), F32),
            pltpu.VMEM((tt, GLA_QK), F32),
            pltpu.VMEM((tt, GLA_QK), F32),
            pltpu.VMEM((tt, GLA_QK), F32),
            pltpu.VMEM((tt, GLA_QK), BF16),
            pltpu.VMEM((tt, GLA_QK), BF16),
            pltpu.VMEM((tt, GLA_QK), F32),
            pltpu.VMEM((tt, GLA_V), BF16),
            pltpu.VMEM((tt, GLA_V), F32),
            pltpu.VMEM((tt, GLA_CH * GLA_QK), BF16),
            pltpu.VMEM((tt, 128), BF16),
            pltpu.VMEM((tt, SG_W), F32),
            pltpu.VMEM((tt // GLA_CH, GLA_DV, GLA_QK), F32),
            pltpu.VMEM((tt // GLA_CH, GLA_DV, GLA_QK), BF16),
            pltpu.VMEM((tt, EV_Z - EV_R), F32),
        ],
        compiler_params=pltpu.CompilerParams(
            dimension_semantics=("arbitrary", "arbitrary"), vmem_limit_bytes=VMEM_LIMIT_BYTES),
        name="even_prompt",
    )(x, *operands)


def _prep_even(w_in, w_gate, b_gate, gla_g, ln_g, ln_b, sg_w, sg_b, w_out):
    z0 = 2 * GLA_QK + 2 * GLA_V
    z1 = z0 + w_gate.shape[0]
    w_z = jnp.pad(w_in[:, z0:z1], ((0, 0), (0, GLA_RANK_PAD - (z1 - z0))))
    w_in2 = jnp.concatenate([w_in[:, :z0], w_in[:, z1:], w_z], axis=1).astype(BF16)
    w_gate2 = jnp.pad(w_gate, ((0, GLA_RANK_PAD - w_gate.shape[0]), (0, 0))).astype(BF16)
    tril = jnp.tril(jnp.ones((SG_CHUNK, SG_CHUNK), bool))
    sg_w_tril = jnp.where(tril[None], sg_w, 0.0).astype(BF16)
    sg_bias = jnp.repeat(jnp.transpose(sg_b), SG_DH, axis=1)
    return dict(
        w_in=w_in2, w_gate=w_gate2, b_gate=b_gate.reshape(1, -1), gla_g=gla_g.reshape(1, -1),
        ln_g=ln_g.reshape(1, -1), ln_b=ln_b.reshape(1, -1),
        sg_w_tril=sg_w_tril, sg_bias=sg_bias, w_out=w_out.astype(BF16),
        w_qk_t=jnp.transpose(w_in2[:, EV_Q:EV_V]), w_z_t=jnp.transpose(w_in2[:, EV_Z:EV_END]),
        w_gate_t=jnp.transpose(w_gate2), b_gate_col=b_gate.reshape(-1, 1),
        sg_w0=jnp.repeat(sg_w[:, 0, 0], SG_DH).reshape(1, -1), sg_b0=jnp.repeat(sg_b[:, 0], SG_DH).reshape(1, -1))


def _even_sample_body(x_ref, s_ref, g_ref, win_ref, wqkt_ref, wzt_ref, wgate_ref, wgatet_ref, bgate_ref,
                      bgatec_ref, glag_ref, lng_ref, lnb_ref, sgw0_ref, sgb0_ref, wout_ref,
                      xo_ref, so_ref, sgv_ref, o_ref):
    bb = x_ref.shape[0]
    x = x_ref[...]
    h = _rms(x, g_ref[...]).astype(BF16)
    nt = (((1,), (1,)), ((), ()))
    qk_t = lax.dot_general(wqkt_ref[...], h, nt, preferred_element_type=F32)
    q_t = qk_t[:GLA_QK] * (GLA_DK ** -0.5)
    k_t = qk_t[GLA_QK:].astype(BF16).astype(F32)
    z_t = lax.dot_general(wzt_ref[...], h, nt, preferred_element_type=F32).astype(BF16)
    a_t = jnp.exp(_log_sigmoid(_dot(wgatet_ref[...], z_t) + bgatec_ref[...]) * GLA_INV_TAU)
    v = _dot(h, win_ref[:, EV_V:EV_R])
    v_r = v.astype(BF16).astype(F32)
    for b in range(bb):
        s_old = s_ref[b].reshape(GLA_QK, GLA_DV)
        a_c = jnp.broadcast_to(a_t[:, b:b + 1], (GLA_QK, GLA_DV))
        k_c = jnp.broadcast_to(k_t[:, b:b + 1], (GLA_QK, GLA_DV))
        q_c = jnp.broadcast_to(q_t[:, b:b + 1], (GLA_QK, GLA_DV))
        v_rows = jnp.concatenate(
            [jnp.broadcast_to(v_r[b:b + 1, hh * GLA_DV:(hh + 1) * GLA_DV], (GLA_DK, GLA_DV))
             for hh in range(GLA_H)], axis=0)
        s_new = a_c * s_old + k_c * v_rows
        so_ref[b] = s_new.reshape(GLA_H, GLA_DK, GLA_DV)
        ob = jnp.sum((q_c * s_new).reshape(GLA_H, GLA_DK, GLA_DV), axis=1)
        o_ref[b:b + 1, :] = jnp.concatenate([ob[hh:hh + 1] for hh in range(GLA_H)], axis=1)
    o = o_ref[...]
    o_n = jnp.concatenate(
        [_rms(o[:, hh * GLA_DV:(hh + 1) * GLA_DV], 1.0) for hh in range(GLA_H)], axis=1) * glag_ref[...]
    r = _dot(h, win_ref[:, EV_R:EV_U])
    out_a = o_n * (r * jax.nn.sigmoid(r))
    u = jax.nn.gelu(_dot(h, win_ref[:, EV_U:EV_VS]))
    vv = jax.nn.gelu(_dot(h, win_ref[:, EV_VS:EV_Z]))
    mu = jnp.mean(vv, axis=-1, keepdims=True)
    xc = vv - mu
    var = jnp.mean(xc * xc, axis=-1, keepdims=True)
    v_ln = xc * lax.rsqrt(var + EPS) * lng_ref[...] + lnb_ref[...]
    sgv_ref[...] = v_ln
    out_b = u * (sgw0_ref[...] * v_ln + sgb0_ref[...])
    y = _dot(jnp.concatenate([out_a, out_b], axis=1).astype(BF16), wout_ref[...])
    xo_ref[...] = x + y


def _even_sample(x, state, g, ev, *, block_rows=32):
    rows, d = x.shape
    bb = min(block_rows, rows)
    full = lambda a: pl.BlockSpec(a.shape, (lambda i: (0,) * a.ndim), pipeline_mode=pl.Buffered(1))
    operands = (g, ev["w_in"], ev["w_qk_t"], ev["w_z_t"], ev["w_gate"], ev["w_gate_t"], ev["b_gate"],
                ev["b_gate_col"], ev["gla_g"], ev["ln_g"], ev["ln_b"], ev["sg_w0"], ev["sg_b0"], ev["w_out"])
    return pl.pallas_call(
        _even_sample_body,
        grid=(rows // bb,),
        in_specs=[pl.BlockSpec((bb, d), lambda i: (i, 0)),
                  pl.BlockSpec((bb, GLA_H, GLA_DK, GLA_DV), lambda i: (i, 0, 0, 0))]
                 + [full(a) for a in operands],
        out_specs=[
            pl.BlockSpec((bb, d), lambda i: (i, 0)),
            pl.BlockSpec((bb, GLA_H, GLA_DK, GLA_DV), lambda i: (i, 0, 0, 0)),
            pl.BlockSpec((bb, SG_W), lambda i: (i, 0)),
        ],
        out_shape=[
            jax.ShapeDtypeStruct((rows, d), F32),
            jax.ShapeDtypeStruct(state.shape, F32),
            jax.ShapeDtypeStruct((rows, SG_W), F32),
        ],
        scratch_shapes=[pltpu.VMEM((bb, GLA_V), F32)],
        compiler_params=pltpu.CompilerParams(
            dimension_semantics=("arbitrary",), vmem_limit_bytes=VMEM_LIMIT_BYTES),
        name="even_sample",
    )(x, state, *operands)


CONV_W = 512
CONV_K = 31
CONV_BUF = CONV_K - 1
CONV_PAD = 32
POOL_W = 512
POOL_WINDOWS = (2, 4, 8, 16)
POOL_DG = POOL_W // len(POOL_WINDOWS)
POOL_BUF = 15
POOL_PAD = 16
SUBLANES = 8


def _layernorm(x, g, b):
    mu = jnp.mean(x, axis=-1, keepdims=True)
    xc = x - mu
    var = jnp.mean(xc * xc, axis=-1, keepdims=True)
    return xc * lax.rsqrt(var + EPS) * g + b


def _odd_prompt_body(x_ref, g_ref, win_ref, cw_ref, cb_ref, lng_ref, lnb_ref, pw_ref, ps_ref, wout_ref,
                     xo_ref, conv_ref, pool_ref, gbuf_ref, pbuf_ref):
    j = pl.program_id(1)
    tt = x_ref.shape[0]

    @pl.when(j == 0)
    def _():
        gbuf_ref[0:CONV_PAD, :] = jnp.zeros((CONV_PAD, CONV_W), F32)
        pbuf_ref[0:POOL_PAD, :] = jnp.zeros((POOL_PAD, POOL_W), F32)

    x = x_ref[...]
    h = _rms(x, g_ref[...]).astype(BF16)
    a = _dot(h, win_ref[:, 0:CONV_W])
    gt = _dot(h, win_ref[:, CONV_W:2 * CONV_W])
    xp = _dot(h, win_ref[:, 2 * CONV_W:])
    gbuf_ref[CONV_PAD:CONV_PAD + tt, :] = a * jax.nn.sigmoid(gt)
    pbuf_ref[POOL_PAD:POOL_PAD + tt, :] = xp

    acc = jnp.zeros((tt, CONV_W), F32) + cb_ref[...]
    first = CONV_PAD - CONV_BUF
    for rr in range(SUBLANES):
        offs = [o for o in range(first, first + CONV_K) if o % SUBLANES == rr]
        span = max(offs) - rr + tt
        win = gbuf_ref[rr:rr + span, :]
        for o in offs:
            acc = acc + win[o - rr:o - rr + tt, :] * cw_ref[o - first:o - first + 1, :]
    out_c = _layernorm(acc, lng_ref[...], lnb_ref[...])
    out_c = out_c * jax.nn.sigmoid(out_c)

    t_glob = j * tt + lax.broadcasted_iota(jnp.int32, (tt, 1), 0)
    outs = []
    for gi, win_len in enumerate(POOL_WINDOWS):
        lanes = slice(gi * POOL_DG, (gi + 1) * POOL_DG)
        tot = pbuf_ref[POOL_PAD:POOL_PAD + tt, lanes]
        for dd in range(1, win_len):
            tot = tot + pbuf_ref[POOL_PAD - dd:POOL_PAD - dd + tt, lanes]
        cnt = jnp.minimum(win_len, t_glob + 1).astype(F32)
        pooled = tot / cnt - xp[:, lanes]
        outs.append(_dot(pooled.astype(BF16), pw_ref[gi]))
    out_d = jnp.concatenate(outs, axis=1) * ps_ref[...]

    y = _dot(jnp.concatenate([out_c, out_d], axis=1).astype(BF16), wout_ref[...])
    xo_ref[...] = x + y

    tail_g = gbuf_ref[tt:tt + CONV_PAD, :]
    tail_p = pbuf_ref[tt:tt + POOL_PAD, :]
    gbuf_ref[0:CONV_PAD, :] = tail_g
    pbuf_ref[0:POOL_PAD, :] = tail_p

    @pl.when(j == pl.num_programs(1) - 1)
    def _():
        conv_ref[0] = tail_g[CONV_PAD - CONV_BUF:, :]
        pool_ref[0] = tail_p[POOL_PAD - POOL_BUF:, :]


def _odd_prompt(x, batch, g, od, *, block_rows):
    rows, d = x.shape
    t = rows // batch
    tt = min(block_rows, t)
    nj = t // tt
    full = lambda a: pl.BlockSpec(a.shape, (lambda b, j: (0,) * a.ndim), pipeline_mode=pl.Buffered(1))
    operands = (g, od["w_in"], od["conv_w"], od["conv_b"], od["ln_g"], od["ln_b"], od["pool_w"],
                od["pool_scale"], od["w_out"])
    return pl.pallas_call(
        _odd_prompt_body,
        grid=(batch, nj),
        in_specs=[pl.BlockSpec((tt, d), lambda b, j: (b * nj + j, 0))] + [full(a) for a in operands],
        out_specs=[
            pl.BlockSpec((tt, d), lambda b, j: (b * nj + j, 0)),
            pl.BlockSpec((1, CONV_BUF, CONV_W), lambda b, j: (b, 0, 0)),
            pl.BlockSpec((1, POOL_BUF, POOL_W), lambda b, j: (b, 0, 0)),
        ],
        out_shape=[
            jax.ShapeDtypeStruct((rows, d), F32),
            jax.ShapeDtypeStruct((batch, CONV_BUF, CONV_W), F32),
            jax.ShapeDtypeStruct((batch, POOL_BUF, POOL_W), F32),
        ],
        scratch_shapes=[
            pltpu.VMEM((CONV_PAD + tt + SUBLANES, CONV_W), F32),
            pltpu.VMEM((POOL_PAD + tt, POOL_W), F32),
        ],
        compiler_params=pltpu.CompilerParams(
            dimension_semantics=("arbitrary", "arbitrary"), vmem_limit_bytes=VMEM_LIMIT_BYTES),
        name="odd_prompt",
    )(x, *operands)


def _odd_sample_body(x_ref, cbuf_ref, pbuf_ref, g_ref, win_ref, cw_ref, cb_ref, lng_ref, lnb_ref, pw_ref,
                     ps_ref, wout_ref, xo_ref, conv_ref, pool_ref):
    x = x_ref[...]
    h = _rms(x, g_ref[...]).astype(BF16)
    a = _dot(h, win_ref[:, 0:CONV_W])
    gt = _dot(h, win_ref[:, CONV_W:2 * CONV_W])
    xp = _dot(h, win_ref[:, 2 * CONV_W:])
    glu = a * jax.nn.sigmoid(gt)
    cbuf = cbuf_ref[...]
    pbuf = pbuf_ref[...]
    conv = (jnp.sum(cbuf * cw_ref[0:CONV_BUF, :][None], axis=1)
            + glu * cw_ref[CONV_BUF:CONV_K, :] + cb_ref[...])
    out_c = _layernorm(conv, lng_ref[...], lnb_ref[...])
    out_c = out_c * jax.nn.sigmoid(out_c)
    outs = []
    for gi, win_len in enumerate(POOL_WINDOWS):
        lanes = slice(gi * POOL_DG, (gi + 1) * POOL_DG)
        tot = xp[:, lanes] + jnp.sum(pbuf[:, POOL_BUF - (win_len - 1):, lanes], axis=1)
        pooled = tot / float(win_len) - xp[:, lanes]
        outs.append(_dot(pooled.astype(BF16), pw_ref[gi]))
    out_d = jnp.concatenate(outs, axis=1) * ps_ref[...]
    y = _dot(jnp.concatenate([out_c, out_d], axis=1).astype(BF16), wout_ref[...])
    xo_ref[...] = x + y
    conv_ref[:, 0:CONV_BUF - 1, :] = cbuf_ref[:, 1:CONV_BUF, :]
    conv_ref[:, CONV_BUF - 1:CONV_BUF, :] = glu[:, None, :]
    pool_ref[:, 0:POOL_BUF - 1, :] = pbuf_ref[:, 1:POOL_BUF, :]
    pool_ref[:, POOL_BUF - 1:POOL_BUF, :] = xp[:, None, :]


def _odd_sample(x, conv_buf, pool_buf, g, od, *, block_rows=32):
    rows, d = x.shape
    bb = min(block_rows, rows)
    full = lambda a: pl.BlockSpec(a.shape, (lambda i: (0,) * a.ndim), pipeline_mode=pl.Buffered(1))
    operands = (g, od["w_in"], od["conv_w"], od["conv_b"], od["ln_g"], od["ln_b"], od["pool_w"],
                od["pool_scale"], od["w_out"])
    return pl.pallas_call(
        _odd_sample_body,
        grid=(rows // bb,),
        in_specs=[pl.BlockSpec((bb, d), lambda i: (i, 0)),
                  pl.BlockSpec((bb, CONV_BUF, CONV_W), lambda i: (i, 0, 0)),
                  pl.BlockSpec((bb, POOL_BUF, POOL_W), lambda i: (i, 0, 0))]
                 + [full(a) for a in operands],
        out_specs=[
            pl.BlockSpec((bb, d), lambda i: (i, 0)),
            pl.BlockSpec((bb, CONV_BUF, CONV_W), lambda i: (i, 0, 0)),
            pl.BlockSpec((bb, POOL_BUF, POOL_W), lambda i: (i, 0, 0)),
        ],
        out_shape=[
            jax.ShapeDtypeStruct((rows, d), F32),
            jax.ShapeDtypeStruct(conv_buf.shape, F32),
            jax.ShapeDtypeStruct(pool_buf.shape, F32),
        ],
        compiler_params=pltpu.CompilerParams(
            dimension_semantics=("arbitrary",), vmem_limit_bytes=VMEM_LIMIT_BYTES),
        name="odd_sample",
    )(x, conv_buf, pool_buf, *operands)


def _prep_odd(w_in, conv_w, conv_b, ln_g, ln_b, pool_w, pool_scale, w_out):
    return dict(w_in=w_in.astype(BF16), conv_w=conv_w, conv_b=conv_b.reshape(1, -1), ln_g=ln_g.reshape(1, -1),
                ln_b=ln_b.reshape(1, -1), pool_w=pool_w.astype(BF16), pool_scale=pool_scale.reshape(1, -1),
                w_out=w_out.astype(BF16))


def kernel(x_prompt, x_sample, state_gla, state_conv, state_pool, norm_g, ff_in, ff_out, ev_w_in, ev_w_gate, ev_b_gate, ev_gla_g, ev_sg_ln_g, ev_sg_ln_b, ev_sg_w, ev_sg_b, ev_w_out, od_w_in, od_conv_w, od_conv_b, od_ln_g, od_ln_b, od_pool_w, od_pool_scale, od_w_out, norm_f):
    bp, t, d = x_prompt.shape
    bs = x_sample.shape[0]
    depth = norm_g.shape[0]
    xp = x_prompt.reshape(bp * t, d)
    xs = x_sample.reshape(bs, d)
    gla_p, gla_s, sgv_p, sgv_s, conv_p, conv_s, pool_p, pool_s = [], [], [], [], [], [], [], []
    for layer in range(depth):
        i = layer // 2
        last = layer == depth - 1
        w1_in, w1_out = ff_in[layer, 0].astype(BF16), ff_out[layer, 0].astype(BF16)
        xp = _ffn(xp, norm_g[layer, 0], w1_in, w1_out, norm_f, final_norm=False, block_rows=512)
        xs = _ffn(xs, norm_g[layer, 0], w1_in, w1_out, norm_f, final_norm=False, block_rows=512)
        g_mix = norm_g[layer, 1].reshape(1, d)
        if layer % 2 == 0:
            ev = _prep_even(ev_w_in[i], ev_w_gate[i], ev_b_gate[i], ev_gla_g[i], ev_sg_ln_g[i], ev_sg_ln_b[i],
                            ev_sg_w[i], ev_sg_b[i], ev_w_out[i])
            xp, s_p, v_p = _even_prompt(xp, bp, g_mix, ev["w_in"], ev["w_gate"], ev["b_gate"], ev["gla_g"],
                                        ev["ln_g"], ev["ln_b"], ev["sg_w_tril"], ev["sg_bias"], ev["w_out"],
                                        block_rows=256)
            xs, s_s, v_s = _even_sample(xs, state_gla[i], g_mix, ev)
            gla_p.append(s_p); gla_s.append(s_s); sgv_p.append(v_p); sgv_s.append(v_s.reshape(bs, 1, SG_W))
        else:
            od = _prep_odd(od_w_in[i], od_conv_w[i], od_conv_b[i], od_ln_g[i], od_ln_b[i], od_pool_w[i],
                           od_pool_scale[i], od_w_out[i])
            xp, c_p, p_p = _odd_prompt(xp, bp, g_mix, od, block_rows=256)
            xs, c_s, p_s = _odd_sample(xs, state_conv[i], state_pool[i], g_mix, od)
            conv_p.append(c_p); conv_s.append(c_s); pool_p.append(p_p); pool_s.append(p_s)
        w2_in, w2_out = ff_in[layer, 1].astype(BF16), ff_out[layer, 1].astype(BF16)
        xp = _ffn(xp, norm_g[layer, 2], w2_in, w2_out, norm_f, final_norm=last, block_rows=512)
        xs = _ffn(xs, norm_g[layer, 2], w2_in, w2_out, norm_f, final_norm=last, block_rows=512)
    return (xp.reshape(bp, t, d), xs.reshape(bs, 1, d), jnp.stack(gla_p), jnp.stack(gla_s),
            jnp.stack(sgv_p), jnp.stack(sgv_s), jnp.stack(conv_p), jnp.stack(conv_s),
            jnp.stack(pool_p), jnp.stack(pool_s))
```

```python
import functools

import jax
import jax.numpy as jnp
from jax import lax
from jax.experimental import pallas as pl
from jax.experimental.pallas import tpu as pltpu

EPS = 1e-6
LOG2_E = 1.4426950408889634
BF16 = jnp.bfloat16
F32 = jnp.float32

VMEM_LIMIT_BYTES = 56 * 1024 * 1024
FFN_ROWS = 1024


def _rms(x, g):
    return x * lax.rsqrt(jnp.mean(x * x, axis=-1, keepdims=True) + EPS) * g


def _dot(a, b):
    return jnp.dot(a, b, preferred_element_type=F32)


def _ffn_body(x_ref, g_ref, win_ref, wout_ref, gf_ref, o_ref, act_ref, *, d_ff, chunk, final_norm):
    x = x_ref[...]
    h = _rms(x, g_ref[...]).astype(BF16)
    for c in range(d_ff // chunk):
        a = _dot(h, win_ref[:, c * chunk:(c + 1) * chunk])
        b = _dot(h, win_ref[:, d_ff + c * chunk:d_ff + (c + 1) * chunk])
        act_ref[:, c * chunk:(c + 1) * chunk] = (a * jax.nn.sigmoid(a) * b).astype(BF16)
    y = x + 0.5 * _dot(act_ref[...], wout_ref[...])
    if final_norm:
        y = _rms(y, gf_ref[...])
    o_ref[...] = y


def _ffn(x, g, w_in, w_out, g_final, *, final_norm, block_rows):
    rows, d = x.shape
    d_ff = w_out.shape[0]
    tm = min(block_rows, rows)
    chunk = 256
    body = functools.partial(_ffn_body, d_ff=d_ff, chunk=chunk, final_norm=final_norm)
    const = lambda i: (0, 0)
    return pl.pallas_call(
        body,
        grid=(rows // tm,),
        in_specs=[
            pl.BlockSpec((tm, d), lambda i: (i, 0)),
            pl.BlockSpec((1, d), const),
            pl.BlockSpec((d, 2 * d_ff), const, pipeline_mode=pl.Buffered(1)),
            pl.BlockSpec((d_ff, d), const, pipeline_mode=pl.Buffered(1)),
            pl.BlockSpec((1, d), const),
        ],
        out_specs=pl.BlockSpec((tm, d), lambda i: (i, 0)),
        out_shape=jax.ShapeDtypeStruct((rows, d), F32),
        scratch_shapes=[pltpu.VMEM((tm, d_ff), BF16)],
        compiler_params=pltpu.CompilerParams(
            dimension_semantics=("arbitrary",), vmem_limit_bytes=VMEM_LIMIT_BYTES),
        name="ffn",
    )(x, g.reshape(1, d), w_in, w_out, g_final.reshape(1, d))


GLA_H = 4
GLA_DK = 64
GLA_DV = 128
GLA_QK = GLA_H * GLA_DK
GLA_V = GLA_H * GLA_DV
GLA_RANK_PAD = 128
GLA_INV_TAU = 1.0 / 16.0
GLA_CH = 16
SG_H = 4
SG_DH = 128
SG_W = SG_H * SG_DH
SG_CHUNK = 128
EV_Q, EV_V, EV_R, EV_U, EV_VS, EV_Z, EV_END = 0, 512, 1024, 1536, 2048, 2560, 2688


def _log_sigmoid(x):
    return jnp.minimum(x, 0.0) - jnp.log(1.0 + jnp.exp(-jnp.abs(x)))


def _split3(x):
    hi = x.astype(BF16)
    r1 = x - hi.astype(F32)
    mid = r1.astype(BF16)
    lo = (r1 - mid.astype(F32)).astype(BF16)
    return hi, mid, lo


def _dot3(sel, parts):
    return _dot(sel, parts[0]) + _dot(sel, parts[1]) + _dot(sel, parts[2])


def _head_masks(width, per_head, dtype):
    lane = lax.broadcasted_iota(jnp.int32, (1, width), 1)
    return [jnp.where(lane // per_head == h, 1.0, 0.0).astype(dtype) for h in range(width // per_head)]


def _even_prompt_body(x_ref, g_ref, win_ref, wgate_ref, bgate_ref, glag_ref, lng_ref, lnb_ref,
                      sgw_ref, sgb_ref, e_ref, wout_ref,
                      xo_ref, gla_ref, sgv_ref,
                      st_ref, q_ref, k_ref, cum2_ref, qin_ref, kd_ref, dec_ref, v_ref, o_ref,
                      pcat_ref, acat_ref, mix_ref, add_ref, stb_ref, ruv_ref):
    j = pl.program_id(1)
    tt = x_ref.shape[0]
    n_chunks = tt // GLA_CH

    @pl.when(j == 0)
    def _():
        st_ref[...] = jnp.zeros_like(st_ref)

    x = x_ref[...]
    h = _rms(x, g_ref[...]).astype(BF16)

    qk = _dot(h, win_ref[:, EV_Q:EV_V])
    q = qk[:, :GLA_QK] * (GLA_DK ** -0.5)
    k = qk[:, GLA_QK:]
    v_ref[...] = _dot(h, win_ref[:, EV_V:EV_R]).astype(BF16)
    z = _dot(h, win_ref[:, EV_Z:EV_END]).astype(BF16)
    log_a = _log_sigmoid(_dot(z, wgate_ref[...]) + bgate_ref[...]) * GLA_INV_TAU

    row = lax.broadcasted_iota(jnp.int32, (tt, tt), 0)
    col = lax.broadcasted_iota(jnp.int32, (tt, tt), 1)
    same = (row // GLA_CH) == (col // GLA_CH)
    sel_cum = jnp.where(same & (col <= row), 1.0, 0.0).astype(BF16)
    sel_tot = jnp.where(same, 1.0, 0.0).astype(BF16)
    parts = _split3(log_a)
    cum = _dot3(sel_cum, parts)
    tot = _dot3(sel_tot, parts)
    q_ref[...] = q
    k_ref[...] = k
    cum2_ref[...] = cum * LOG2_E
    qin_ref[...] = (q * jnp.exp(cum)).astype(BF16)
    kd_ref[...] = (k * jnp.exp(tot - cum)).astype(BF16)
    dec_ref[...] = jnp.exp(tot)

    half = GLA_CH // 2
    proj_block = 256
    proj_cols = list(range(EV_R, EV_Z, proj_block))
    for n in range(n_chunks):
        if n % 2 == 1 and n // 2 < len(proj_cols):
            c0 = proj_cols[n // 2]
            ruv_ref[:, c0 - EV_R:c0 - EV_R + proj_block] = _dot(h, win_ref[:, c0:c0 + proj_block])
        base = n * GLA_CH
        qb = q_ref[base:base + GLA_CH, :]
        cb = cum2_ref[base:base + GLA_CH, :]
        for s in range(GLA_CH):
            ks = k_ref[base + s:base + s + 1, :]
            cs = cum2_ref[base + s:base + s + 1, :]
            if s < half:
                p = qb * ks * jnp.exp2(jnp.minimum(cb - cs, 0.0))
            else:
                p_hi = qb[half:] * ks * jnp.exp2(jnp.minimum(cb[half:] - cs, 0.0))
                p = jnp.concatenate([jnp.zeros_like(p_hi), p_hi], axis=0)
            pcat_ref[base:base + GLA_CH, s * GLA_QK:(s + 1) * GLA_QK] = p.astype(BF16)
    scores = _dot(pcat_ref[...], e_ref[...])

    u = jax.nn.gelu(ruv_ref[:, EV_U - EV_R:EV_VS - EV_R])
    v_ln = _layernorm(jax.nn.gelu(ruv_ref[:, EV_VS - EV_R:EV_Z - EV_R]), lng_ref[...], lnb_ref[...])
    v_lb = v_ln.astype(BF16)

    t_loc = lax.broadcasted_iota(jnp.int32, scores.shape, 0) % GLA_CH
    c_idx = lax.broadcasted_iota(jnp.int32, scores.shape, 1)
    causal = (c_idx % GLA_CH <= t_loc) & (c_idx < GLA_H * GLA_CH)
    acat_ref[...] = jnp.where(causal, scores, 0.0).astype(BF16)

    qk_masks = _head_masks(GLA_QK, GLA_DK, BF16)
    v_masks = _head_masks(GLA_V, GLA_DV, BF16)

    for n in range(n_chunks):
        rows = slice(n * GLA_CH, (n + 1) * GLA_CH)
        kn = kd_ref[rows, :]
        vn = v_ref[rows, :]
        lk = jnp.concatenate([kn * m for m in qk_masks], axis=0)
        vs = jnp.concatenate([vn[:, hh * GLA_DV:(hh + 1) * GLA_DV] for hh in range(GLA_H)], axis=0)
        add_ref[n] = lax.dot_general(vs, lk, (((0,), (0,)), ((), ())), preferred_element_type=F32)
    r = ruv_ref[:, 0:EV_U - EV_R]
    gate_r = r * jax.nn.sigmoid(r)
    sg_tasks = [(c, hh) for c in range(tt // SG_CHUNK) for hh in range(SG_H)]
    st = st_ref[...]
    for n in range(n_chunks):
        stb_ref[n] = st.astype(BF16)
        st = st * dec_ref[n * GLA_CH:n * GLA_CH + 1, :] + add_ref[n]
        if n < len(sg_tasks):
            c, hh = sg_tasks[n]
            mix_ref[c * SG_CHUNK:(c + 1) * SG_CHUNK, hh * SG_DH:(hh + 1) * SG_DH] = _dot(
                sgw_ref[hh], v_lb[c * SG_CHUNK:(c + 1) * SG_CHUNK, hh * SG_DH:(hh + 1) * SG_DH])
    st_ref[...] = st
    assert len(sg_tasks) <= n_chunks
    for n in range(n_chunks):
        rows = slice(n * GLA_CH, (n + 1) * GLA_CH)
        qn = qin_ref[rows, :]
        vn = v_ref[rows, :]
        lq = jnp.concatenate([qn * m for m in qk_masks], axis=0)
        oi = lax.dot_general(lq, stb_ref[n], (((1,), (1,)), ((), ())),
                             preferred_element_type=F32)
        o_inter = jnp.concatenate([oi[hh * GLA_CH:(hh + 1) * GLA_CH] for hh in range(GLA_H)], axis=1)
        vbd = jnp.concatenate([vn * m for m in v_masks]
                              + [jnp.zeros((acat_ref.shape[1] - GLA_H * GLA_CH, GLA_V), BF16)], axis=0)
        o_ref[rows, :] = o_inter + _dot(acat_ref[rows, :], vbd)

    o = o_ref[...]
    o_n = jnp.concatenate(
        [_rms(o[:, hh * GLA_DV:(hh + 1) * GLA_DV], 1.0) for hh in range(GLA_H)], axis=1) * glag_ref[...]
    out_a = o_n * gate_r
    bias = jnp.concatenate([sgb_ref[...]] * (tt // SG_CHUNK), axis=0)
    out_b = u * (mix_ref[...] + bias)

    y = _dot(jnp.concatenate([out_a, out_b], axis=1).astype(BF16), wout_ref[...])
    xo_ref[...] = x + y

    @pl.when(j == pl.num_programs(1) - 1)
    def _():
        sgv_ref[0] = v_ln[tt - SG_CHUNK:, :]
        gla_ref[0] = st_ref[...].T.reshape(GLA_H, GLA_DK, GLA_DV)


def _score_sum_matrix():
    r = jnp.arange(GLA_CH * GLA_QK)
    c = jnp.arange(128)
    s, hh = r // GLA_QK, (r % GLA_QK) // GLA_DK
    return ((c[None, :] == (hh * GLA_CH + s)[:, None])).astype(BF16)


def _even_prompt(x, batch, g, w_in, w_gate, b_gate, gla_g, ln_g, ln_b, sg_w, sg_bias, w_out, *, block_rows):
    rows, d = x.shape
    t = rows // batch
    tt = min(block_rows, t)
    nj = t // tt
    const2 = lambda b, j: (0, 0)
    full = lambda a: pl.BlockSpec(a.shape, (lambda b, j: (0,) * a.ndim), pipeline_mode=pl.Buffered(1))
    e = _score_sum_matrix()
    operands = (g, w_in, w_gate, b_gate, gla_g, ln_g, ln_b, sg_w, sg_bias, e, w_out)
    return pl.pallas_call(
        _even_prompt_body,
        grid=(batch, nj),
        in_specs=[pl.BlockSpec((tt, d), lambda b, j: (b * nj + j, 0))] + [full(a) for a in operands],
        out_specs=[
            pl.BlockSpec((tt, d), lambda b, j: (b * nj + j, 0)),
            pl.BlockSpec((1, GLA_H, GLA_DK, GLA_DV), lambda b, j: (b, 0, 0, 0)),
            pl.BlockSpec((1, SG_CHUNK, SG_W), lambda b, j: (b, 0, 0)),
        ],
        out_shape=[
            jax.ShapeDtypeStruct((rows, d), F32),
            jax.ShapeDtypeStruct((batch, GLA_H, GLA_DK, GLA_DV), F32),
            jax.ShapeDtypeStruct((batch, SG_CHUNK, SG_W), F32),
        ],
        scratch_shapes=[
            pltpu.VMEM((GLA_DV, GLA_QK), F32),
            pltpu.VMEM((tt, GLA_QK), F32),
            pltpu.VMEM((tt, GLA_QK), F32),
            pltpu.VMEM((tt, GLA_QK), F32),
            pltpu.VMEM((tt, GLA_QK), BF16),
            pltpu.VMEM((tt, GLA_QK), BF16),
            pltpu.VMEM((tt, GLA_QK), F32),
            pltpu.VMEM((tt, GLA_V), BF16),
            pltpu.VMEM((tt, GLA_V), F32),
            pltpu.VMEM((tt, GLA_CH * GLA_QK), BF16),
            pltpu.VMEM((tt, 128), BF16),
            pltpu.VMEM((tt, SG_W), F32),
            pltpu.VMEM((tt // GLA_CH, GLA_DV, GLA_QK), F32),
            pltpu.VMEM((tt // GLA_CH, GLA_DV, GLA_QK), BF16),
            pltpu.VMEM((tt, EV_Z - EV_R), F32),
        ],
        compiler_params=pltpu.CompilerParams(
            dimension_semantics=("arbitrary", "arbitrary"), vmem_limit_bytes=VMEM_LIMIT_BYTES),
        name="even_prompt",
    )(x, *operands)


def _prep_even(w_in, w_gate, b_gate, gla_g, ln_g, ln_b, sg_w, sg_b, w_out):
    z0 = 2 * GLA_QK + 2 * GLA_V
    z1 = z0 + w_gate.shape[0]
    w_z = jnp.pad(w_in[:, z0:z1], ((0, 0), (0, GLA_RANK_PAD - (z1 - z0))))
    w_in2 = jnp.concatenate([w_in[:, :z0], w_in[:, z1:], w_z], axis=1).astype(BF16)
    w_gate2 = jnp.pad(w_gate, ((0, GLA_RANK_PAD - w_gate.shape[0]), (0, 0))).astype(BF16)
    tril = jnp.tril(jnp.ones((SG_CHUNK, SG_CHUNK), bool))
    sg_w_tril = jnp.where(tril[None], sg_w, 0.0).astype(BF16)
    sg_bias = jnp.repeat(jnp.transpose(sg_b), SG_DH, axis=1)
    return dict(
        w_in=w_in2, w_gate=w_gate2, b_gate=b_gate.reshape(1, -1), gla_g=gla_g.reshape(1, -1),
        ln_g=ln_g.reshape(1, -1), ln_b=ln_b.reshape(1, -1),
        sg_w_tril=sg_w_tril, sg_bias=sg_bias, w_out=w_out.astype(BF16),
        w_qk_t=jnp.transpose(w_in2[:, EV_Q:EV_V]), w_z_t=jnp.transpose(w_in2[:, EV_Z:EV_END]),
        w_gate_t=jnp.transpose(w_gate2), b_gate_col=b_gate.reshape(-1, 1),
        sg_w0=jnp.repeat(sg_w[:, 0, 0], SG_DH).reshape(1, -1), sg_b0=jnp.repeat(sg_b[:, 0], SG_DH).reshape(1, -1))


def _even_sample_body(x_ref, s_ref, g_ref, win_ref, wqkt_ref, wzt_ref, wgate_ref, wgatet_ref, bgate_ref,
                      bgatec_ref, glag_ref, lng_ref, lnb_ref, sgw0_ref, sgb0_ref, wout_ref,
                      xo_ref, so_ref, sgv_ref, o_ref):
    bb = x_ref.shape[0]
    x = x_ref[...]
    h = _rms(x, g_ref[...]).astype(BF16)
    nt = (((1,), (1,)), ((), ()))
    qk_t = lax.dot_general(wqkt_ref[...], h, nt, preferred_element_type=F32)
    q_t = qk_t[:GLA_QK] * (GLA_DK ** -0.5)
    k_t = qk_t[GLA_QK:].astype(BF16).astype(F32)
    z_t = lax.dot_general(wzt_ref[...], h, nt, preferred_element_type=F32).astype(BF16)
    a_t = jnp.exp(_log_sigmoid(_dot(wgatet_ref[...], z_t) + bgatec_ref[...]) * GLA_INV_TAU)
    v = _dot(h, win_ref[:, EV_V:EV_R])
    v_r = v.astype(BF16).astype(F32)
    for b in range(bb):
        s_old = s_ref[b].reshape(GLA_QK, GLA_DV)
        a_c = jnp.broadcast_to(a_t[:, b:b + 1], (GLA_QK, GLA_DV))
        k_c = jnp.broadcast_to(k_t[:, b:b + 1], (GLA_QK, GLA_DV))
        q_c = jnp.broadcast_to(q_t[:, b:b + 1], (GLA_QK, GLA_DV))
        v_rows = jnp.concatenate(
            [jnp.broadcast_to(v_r[b:b + 1, hh * GLA_DV:(hh + 1) * GLA_DV], (GLA_DK, GLA_DV))
             for hh in range(GLA_H)], axis=0)
        s_new = a_c * s_old + k_c * v_rows
        so_ref[b] = s_new.reshape(GLA_H, GLA_DK, GLA_DV)
        ob = jnp.sum((q_c * s_new).reshape(GLA_H, GLA_DK, GLA_DV), axis=1)
        o_ref[b:b + 1, :] = jnp.concatenate([ob[hh:hh + 1] for hh in range(GLA_H)], axis=1)
    o = o_ref[...]
    o_n = jnp.concatenate(
        [_rms(o[:, hh * GLA_DV:(hh + 1) * GLA_DV], 1.0) for hh in range(GLA_H)], axis=1) * glag_ref[...]
    r = _dot(h, win_ref[:, EV_R:EV_U])
    out_a = o_n * (r * jax.nn.sigmoid(r))
    u = jax.nn.gelu(_dot(h, win_ref[:, EV_U:EV_VS]))
    vv = jax.nn.gelu(_dot(h, win_ref[:, EV_VS:EV_Z]))
    mu = jnp.mean(vv, axis=-1, keepdims=True)
    xc = vv - mu
    var = jnp.mean(xc * xc, axis=-1, keepdims=True)
    v_ln = xc * lax.rsqrt(var + EPS) * lng_ref[...] + lnb_ref[...]
    sgv_ref[...] = v_ln
    out_b = u * (sgw0_ref[...] * v_ln + sgb0_ref[...])
    y = _dot(jnp.concatenate([out_a, out_b], axis=1).astype(BF16), wout_ref[...])
    xo_ref[...] = x + y


def _even_sample(x, state, g, ev, *, block_rows=32):
    rows, d = x.shape
    bb = min(block_rows, rows)
    full = lambda a: pl.BlockSpec(a.shape, (lambda i: (0,) * a.ndim), pipeline_mode=pl.Buffered(1))
    operands = (g, ev["w_in"], ev["w_qk_t"], ev["w_z_t"], ev["w_gate"], ev["w_gate_t"], ev["b_gate"],
                ev["b_gate_col"], ev["gla_g"], ev["ln_g"], ev["ln_b"], ev["sg_w0"], ev["sg_b0"], ev["w_out"])
    return pl.pallas_call(
        _even_sample_body,
        grid=(rows // bb,),
        in_specs=[pl.BlockSpec((bb, d), lambda i: (i, 0)),
                  pl.BlockSpec((bb, GLA_H, GLA_DK, GLA_DV), lambda i: (i, 0, 0, 0))]
                 + [full(a) for a in operands],
        out_specs=[
            pl.BlockSpec((bb, d), lambda i: (i, 0)),
            pl.BlockSpec((bb, GLA_H, GLA_DK, GLA_DV), lambda i: (i, 0, 0, 0)),
            pl.BlockSpec((bb, SG_W), lambda i: (i, 0)),
        ],
        out_shape=[
            jax.ShapeDtypeStruct((rows, d), F32),
            jax.ShapeDtypeStruct(state.shape, F32),
            jax.ShapeDtypeStruct((rows, SG_W), F32),
        ],
        scratch_shapes=[pltpu.VMEM((bb, GLA_V), F32)],
        compiler_params=pltpu.CompilerParams(
            dimension_semantics=("arbitrary",), vmem_limit_bytes=VMEM_LIMIT_BYTES),
        name="even_sample",
    )(x, state, *operands)


CONV_W = 512
CONV_K = 31
CONV_BUF = CONV_K - 1
CONV_PAD = 32
POOL_W = 512
POOL_WINDOWS = (2, 4, 8, 16)
POOL_DG = POOL_W // len(POOL_WINDOWS)
POOL_BUF = 15
POOL_PAD = 16
SUBLANES = 8
CONV_ROWS = 32


def _layernorm(x, g, b):
    mu = jnp.mean(x, axis=-1, keepdims=True)
    xc = x - mu
    var = jnp.mean(xc * xc, axis=-1, keepdims=True)
    return xc * lax.rsqrt(var + EPS) * g + b


def _odd_prompt_body(x_ref, g_ref, win_ref, cw_ref, cb_ref, lng_ref, lnb_ref, pw_ref, ps_ref, wout_ref,
                     xo_ref, conv_ref, pool_ref, gbuf_ref, pbuf_ref, shift_ref, convo_ref, cwb_ref, psum_ref):
    j = pl.program_id(1)
    tt = x_ref.shape[0]
    assert POOL_WINDOWS == (2, 4, 8, 16)

    p0 = SUBLANES + POOL_PAD

    @pl.when(j == 0)
    def _():
        gbuf_ref[0:CONV_PAD, :] = jnp.zeros((CONV_PAD, CONV_W), F32)
        pbuf_ref[0:p0, :] = jnp.zeros((p0, POOL_W), F32)
        psum_ref[:, 0:SUBLANES, :] = jnp.zeros((psum_ref.shape[0], SUBLANES, POOL_W), F32)
        for o in range(CONV_K):
            cwb_ref[o] = jnp.broadcast_to(cw_ref[o:o + 1, :], (SUBLANES, CONV_W))

    x = x_ref[...]
    h = _rms(x, g_ref[...]).astype(BF16)
    a = _dot(h, win_ref[:, 0:CONV_W])
    gt = _dot(h, win_ref[:, CONV_W:2 * CONV_W])
    xp = _dot(h, win_ref[:, 2 * CONV_W:])
    gbuf_ref[CONV_PAD:CONV_PAD + tt, :] = a * jax.nn.sigmoid(gt)
    pbuf_ref[p0:p0 + tt, :] = xp

    first = CONV_PAD - CONV_BUF
    ext = tt + CONV_PAD - SUBLANES
    for rr in range(1, SUBLANES):
        shift_ref[rr - 1] = gbuf_ref[rr:rr + ext, :]
    groups = CONV_ROWS // SUBLANES
    for r0 in range(0, tt, CONV_ROWS):
        accs = [jnp.zeros((SUBLANES, CONV_W), F32) + cb_ref[...] for _ in range(groups)]
        for o in range(first, first + CONV_K):
            rr = o % SUBLANES
            w8 = cwb_ref[o - first]
            for gq in range(groups):
                lo = o - rr + r0 + gq * SUBLANES
                src = gbuf_ref[lo:lo + SUBLANES, :] if rr == 0 else shift_ref[rr - 1, lo:lo + SUBLANES, :]
                accs[gq] = accs[gq] + src * w8
        for gq in range(groups):
            convo_ref[r0 + gq * SUBLANES:r0 + (gq + 1) * SUBLANES, :] = accs[gq]
    out_c = _layernorm(convo_ref[...], lng_ref[...], lnb_ref[...])
    out_c = out_c * jax.nn.sigmoid(out_c)

    n_rows = POOL_PAD + tt
    lo = SUBLANES
    psum_ref[0, lo:lo + n_rows, :] = pbuf_ref[lo:lo + n_rows, :] + pbuf_ref[lo - 1:lo - 1 + n_rows, :]
    psum_ref[1, lo:lo + n_rows, POOL_DG:] = (psum_ref[0, lo:lo + n_rows, POOL_DG:]
                                             + psum_ref[0, lo - 2:lo - 2 + n_rows, POOL_DG:])
    psum_ref[2, lo:lo + n_rows, 2 * POOL_DG:] = (psum_ref[1, lo:lo + n_rows, 2 * POOL_DG:]
                                                 + psum_ref[1, lo - 4:lo - 4 + n_rows, 2 * POOL_DG:])
    tots = [psum_ref[0, p0:p0 + tt, 0:POOL_DG],
            psum_ref[1, p0:p0 + tt, POOL_DG:2 * POOL_DG],
            psum_ref[2, p0:p0 + tt, 2 * POOL_DG:3 * POOL_DG],
            psum_ref[2, p0:p0 + tt, 3 * POOL_DG:] + psum_ref[2, p0 - 8:p0 - 8 + tt, 3 * POOL_DG:]]
    t_glob = j * tt + lax.broadcasted_iota(jnp.int32, (tt, 1), 0)
    outs = []
    for gi, win_len in enumerate(POOL_WINDOWS):
        lanes = slice(gi * POOL_DG, (gi + 1) * POOL_DG)
        cnt = jnp.minimum(win_len, t_glob + 1).astype(F32)
        pooled = tots[gi] / cnt - xp[:, lanes]
        outs.append(_dot(pooled.astype(BF16), pw_ref[gi]))
    out_d = jnp.concatenate(outs, axis=1) * ps_ref[...]

    y = _dot(jnp.concatenate([out_c, out_d], axis=1).astype(BF16), wout_ref[...])
    xo_ref[...] = x + y

    tail_g = gbuf_ref[tt:tt + CONV_PAD, :]
    tail_p = pbuf_ref[SUBLANES + tt:p0 + tt, :]
    gbuf_ref[0:CONV_PAD, :] = tail_g
    pbuf_ref[SUBLANES:p0, :] = tail_p

    @pl.when(j == pl.num_programs(1) - 1)
    def _():
        conv_ref[0] = tail_g[CONV_PAD - CONV_BUF:, :]
        pool_ref[0] = tail_p[POOL_PAD - POOL_BUF:, :]


def _odd_prompt(x, batch, g, od, *, block_rows):
    rows, d = x.shape
    t = rows // batch
    tt = min(block_rows, t)
    nj = t // tt
    full = lambda a: pl.BlockSpec(a.shape, (lambda b, j: (0,) * a.ndim), pipeline_mode=pl.Buffered(1))
    operands = (g, od["w_in"], od["conv_w"], od["conv_b"], od["ln_g"], od["ln_b"], od["pool_w"],
                od["pool_scale"], od["w_out"])
    return pl.pallas_call(
        _odd_prompt_body,
        grid=(batch, nj),
        in_specs=[pl.BlockSpec((tt, d), lambda b, j: (b * nj + j, 0))] + [full(a) for a in operands],
        out_specs=[
            pl.BlockSpec((tt, d), lambda b, j: (b * nj + j, 0)),
            pl.BlockSpec((1, CONV_BUF, CONV_W), lambda b, j: (b, 0, 0)),
            pl.BlockSpec((1, POOL_BUF, POOL_W), lambda b, j: (b, 0, 0)),
        ],
        out_shape=[
            jax.ShapeDtypeStruct((rows, d), F32),
            jax.ShapeDtypeStruct((batch, CONV_BUF, CONV_W), F32),
            jax.ShapeDtypeStruct((batch, POOL_BUF, POOL_W), F32),
        ],
        scratch_shapes=[
            pltpu.VMEM((CONV_PAD + tt + SUBLANES, CONV_W), F32),
            pltpu.VMEM((SUBLANES + POOL_PAD + tt, POOL_W), F32),
            pltpu.VMEM((SUBLANES - 1, tt + CONV_PAD - SUBLANES, CONV_W), F32),
            pltpu.VMEM((tt, CONV_W), F32),
            pltpu.VMEM((CONV_K, SUBLANES, CONV_W), F32),
            pltpu.VMEM((3, SUBLANES + POOL_PAD + tt, POOL_W), F32),
        ],
        compiler_params=pltpu.CompilerParams(
            dimension_semantics=("arbitrary", "arbitrary"), vmem_limit_bytes=VMEM_LIMIT_BYTES),
        name="odd_prompt",
    )(x, *operands)


def _odd_sample_body(x_ref, cbuf_ref, pbuf_ref, g_ref, win_ref, cw_ref, cb_ref, lng_ref, lnb_ref, pw_ref,
                     ps_ref, wout_ref, xo_ref, conv_ref, pool_ref):
    x = x_ref[...]
    h = _rms(x, g_ref[...]).astype(BF16)
    a = _dot(h, win_ref[:, 0:CONV_W])
    gt = _dot(h, win_ref[:, CONV_W:2 * CONV_W])
    xp = _dot(h, win_ref[:, 2 * CONV_W:])
    glu = a * jax.nn.sigmoid(gt)
    cbuf = cbuf_ref[...]
    pbuf = pbuf_ref[...]
    conv = (jnp.sum(cbuf * cw_ref[0:CONV_BUF, :][None], axis=1)
            + glu * cw_ref[CONV_BUF:CONV_K, :] + cb_ref[...])
    out_c = _layernorm(conv, lng_ref[...], lnb_ref[...])
    out_c = out_c * jax.nn.sigmoid(out_c)
    outs = []
    for gi, win_len in enumerate(POOL_WINDOWS):
        lanes = slice(gi * POOL_DG, (gi + 1) * POOL_DG)
        tot = xp[:, lanes] + jnp.sum(pbuf[:, POOL_BUF - (win_len - 1):, lanes], axis=1)
        pooled = tot / float(win_len) - xp[:, lanes]
        outs.append(_dot(pooled.astype(BF16), pw_ref[gi]))
    out_d = jnp.concatenate(outs, axis=1) * ps_ref[...]
    y = _dot(jnp.concatenate([out_c, out_d], axis=1).astype(BF16), wout_ref[...])
    xo_ref[...] = x + y
    conv_ref[:, 0:CONV_BUF - 1, :] = cbuf_ref[:, 1:CONV_BUF, :]
    conv_ref[:, CONV_BUF - 1:CONV_BUF, :] = glu[:, None, :]
    pool_ref[:, 0:POOL_BUF - 1, :] = pbuf_ref[:, 1:POOL_BUF, :]
    pool_ref[:, POOL_BUF - 1:POOL_BUF, :] = xp[:, None, :]


def _odd_sample(x, conv_buf, pool_buf, g, od, *, block_rows=32):
    rows, d = x.shape
    bb = min(block_rows, rows)
    full = lambda a: pl.BlockSpec(a.shape, (lambda i: (0,) * a.ndim), pipeline_mode=pl.Buffered(1))
    operands = (g, od["w_in"], od["conv_w"], od["conv_b"], od["ln_g"], od["ln_b"], od["pool_w"],
                od["pool_scale"], od["w_out"])
    return pl.pallas_call(
        _odd_sample_body,
        grid=(rows // bb,),
        in_specs=[pl.BlockSpec((bb, d), lambda i: (i, 0)),
                  pl.BlockSpec((bb, CONV_BUF, CONV_W), lambda i: (i, 0, 0)),
                  pl.BlockSpec((bb, POOL_BUF, POOL_W), lambda i: (i, 0, 0))]
                 + [full(a) for a in operands],
        out_specs=[
            pl.BlockSpec((bb, d), lambda i: (i, 0)),
            pl.BlockSpec((bb, CONV_BUF, CONV_W), lambda i: (i, 0, 0)),
            pl.BlockSpec((bb, POOL_BUF, POOL_W), lambda i: (i, 0, 0)),
        ],
        out_shape=[
            jax.ShapeDtypeStruct((rows, d), F32),
            jax.ShapeDtypeStruct(conv_buf.shape, F32),
            jax.ShapeDtypeStruct(pool_buf.shape, F32),
        ],
        compiler_params=pltpu.CompilerParams(
            dimension_semantics=("arbitrary",), vmem_limit_bytes=VMEM_LIMIT_BYTES),
        name="odd_sample",
    )(x, conv_buf, pool_buf, *operands)


def _prep_odd(w_in, conv_w, conv_b, ln_g, ln_b, pool_w, pool_scale, w_out):
    return dict(w_in=w_in.astype(BF16), conv_w=conv_w, conv_b=conv_b.reshape(1, -1), ln_g=ln_g.reshape(1, -1),
                ln_b=ln_b.reshape(1, -1), pool_w=pool_w.astype(BF16), pool_scale=pool_scale.reshape(1, -1),
                w_out=w_out.astype(BF16))


def kernel(x_prompt, x_sample, state_gla, state_conv, state_pool, norm_g, ff_in, ff_out, ev_w_in, ev_w_gate, ev_b_gate, ev_gla_g, ev_sg_ln_g, ev_sg_ln_b, ev_sg_w, ev_sg_b, ev_w_out, od_w_in, od_conv_w, od_conv_b, od_ln_g, od_ln_b, od_pool_w, od_pool_scale, od_w_out, norm_f):
    bp, t, d = x_prompt.shape
    bs = x_sample.shape[0]
    depth = norm_g.shape[0]
    xp = x_prompt.reshape(bp * t, d)
    xs = x_sample.reshape(bs, d)
    gla_p, gla_s, sgv_p, sgv_s, conv_p, conv_s, pool_p, pool_s = [], [], [], [], [], [], [], []
    for layer in range(depth):
        i = layer // 2
        last = layer == depth - 1
        w1_in, w1_out = ff_in[layer, 0].astype(BF16), ff_out[layer, 0].astype(BF16)
        xp = _ffn(xp, norm_g[layer, 0], w1_in, w1_out, norm_f, final_norm=False, block_rows=FFN_ROWS)
        xs = _ffn(xs, norm_g[layer, 0], w1_in, w1_out, norm_f, final_norm=False, block_rows=FFN_ROWS)
        g_mix = norm_g[layer, 1].reshape(1, d)
        if layer % 2 == 0:
            ev = _prep_even(ev_w_in[i], ev_w_gate[i], ev_b_gate[i], ev_gla_g[i], ev_sg_ln_g[i], ev_sg_ln_b[i],
                            ev_sg_w[i], ev_sg_b[i], ev_w_out[i])
            xp, s_p, v_p = _even_prompt(xp, bp, g_mix, ev["w_in"], ev["w_gate"], ev["b_gate"], ev["gla_g"],
                                        ev["ln_g"], ev["ln_b"], ev["sg_w_tril"], ev["sg_bias"], ev["w_out"],
                                        block_rows=256)
            xs, s_s, v_s = _even_sample(xs, state_gla[i], g_mix, ev)
            gla_p.append(s_p); gla_s.append(s_s); sgv_p.append(v_p); sgv_s.append(v_s.reshape(bs, 1, SG_W))
        else:
            od = _prep_odd(od_w_in[i], od_conv_w[i], od_conv_b[i], od_ln_g[i], od_ln_b[i], od_pool_w[i],
                           od_pool_scale[i], od_w_out[i])
            xp, c_p, p_p = _odd_prompt(xp, bp, g_mix, od, block_rows=256)
            xs, c_s, p_s = _odd_sample(xs, state_conv[i], state_pool[i], g_mix, od)
            conv_p.append(c_p); conv_s.append(c_s); pool_p.append(p_p); pool_s.append(p_s)
        w2_in, w2_out = ff_in[layer, 1].astype(BF16), ff_out[layer, 1].astype(BF16)
        xp = _ffn(xp, norm_g[layer, 2], w2_in, w2_out, norm_f, final_norm=last, block_rows=FFN_ROWS)
        xs = _ffn(xs, norm_g[layer, 2], w2_in, w2_out, norm_f, final_norm=last, block_rows=FFN_ROWS)
    return (xp.reshape(bp, t, d), xs.reshape(bs, 1, d), jnp.stack(gla_p), jnp.stack(gla_s),
            jnp.stack(sgv_p), jnp.stack(sgv_s), jnp.stack(conv_p), jnp.stack(conv_s),
            jnp.stack(pool_p), jnp.stack(pool_s))
```

```python
import functools

import jax
import jax.numpy as jnp
from jax import lax
from jax.experimental import pallas as pl
from jax.experimental.pallas import tpu as pltpu

EPS = 1e-6
LOG2_E = 1.4426950408889634
BF16 = jnp.bfloat16
F32 = jnp.float32

VMEM_LIMIT_BYTES = 56 * 1024 * 1024
FFN_ROWS = 1024


def _rms(x, g):
    return x * lax.rsqrt(jnp.mean(x * x, axis=-1, keepdims=True) + EPS) * g


def _dot(a, b):
    return jnp.dot(a, b, preferred_element_type=F32)


FFN_CHUNK = 256


def _ffn_body(x_ref, g_ref, wa_ref, wb_ref, wout_ref, gf_ref, o_ref, act_ref, *, final_norm):
    d_ff = wout_ref.shape[0]
    x = x_ref[...]
    h = _rms(x, g_ref[...]).astype(BF16)
    for c in range(d_ff // FFN_CHUNK):
        cols = slice(c * FFN_CHUNK, (c + 1) * FFN_CHUNK)
        a = _dot(h, wa_ref[:, cols])
        b = _dot(h, wb_ref[:, cols])
        act_ref[:, cols] = (a * jax.nn.sigmoid(a) * b).astype(BF16)
    y = x + 0.5 * _dot(act_ref[...], wout_ref[...])
    if final_norm:
        y = _rms(y, gf_ref[...])
    o_ref[...] = y


def _ffn(x, g, w_a, w_b, w_out, g_final, *, final_norm, block_rows):
    rows, d = x.shape
    d_ff = w_out.shape[0]
    tm = min(block_rows, rows)
    body = functools.partial(_ffn_body, final_norm=final_norm)
    const = lambda i: (0, 0)
    return pl.pallas_call(
        body,
        grid=(rows // tm,),
        in_specs=[
            pl.BlockSpec((tm, d), lambda i: (i, 0)),
            pl.BlockSpec((1, d), const),
            pl.BlockSpec((d, d_ff), const, pipeline_mode=pl.Buffered(1)),
            pl.BlockSpec((d, d_ff), const, pipeline_mode=pl.Buffered(1)),
            pl.BlockSpec((d_ff, d), const, pipeline_mode=pl.Buffered(1)),
            pl.BlockSpec((1, d), const),
        ],
        out_specs=pl.BlockSpec((tm, d), lambda i: (i, 0)),
        out_shape=jax.ShapeDtypeStruct((rows, d), F32),
        scratch_shapes=[pltpu.VMEM((tm, d_ff), BF16)],
        compiler_params=pltpu.CompilerParams(
            dimension_semantics=("arbitrary",), vmem_limit_bytes=VMEM_LIMIT_BYTES),
        name="ffn",
    )(x, g.reshape(1, d), w_a, w_b, w_out, g_final.reshape(1, d))


def _ffn_stream_body(x_ref, g_ref, wa_ref, wb_ref, wout_ref, gf_ref,
                     o_ref, wa_o_ref, wb_o_ref, wout_o_ref, h_ref, acc_ref, *, final_norm):
    c = pl.program_id(0)

    @pl.when(c == 0)
    def _():
        h_ref[...] = _rms(x_ref[...], g_ref[...]).astype(BF16)
        acc_ref[...] = jnp.zeros_like(acc_ref)

    wa = wa_ref[...].astype(BF16)
    wb = wb_ref[...].astype(BF16)
    wout = wout_ref[...].astype(BF16)
    wa_o_ref[...] = wa
    wb_o_ref[...] = wb
    wout_o_ref[...] = wout
    h = h_ref[...]
    a = _dot(h, wa)
    b = _dot(h, wb)
    acc_ref[...] += _dot((a * jax.nn.sigmoid(a) * b).astype(BF16), wout)

    @pl.when(c == pl.num_programs(0) - 1)
    def _():
        y = x_ref[...] + 0.5 * acc_ref[...]
        if final_norm:
            y = _rms(y, gf_ref[...])
        o_ref[...] = y


def _ffn_stream(x, g, ff_in, ff_out, layer, slot, g_final, *, final_norm):
    rows, d = x.shape
    d_ff = ff_out.shape[2]
    n_chunks = d_ff // FFN_CHUNK
    const = lambda c: (0, 0)
    body = functools.partial(_ffn_stream_body, final_norm=final_norm)
    return pl.pallas_call(
        body,
        grid=(n_chunks,),
        in_specs=[
            pl.BlockSpec((rows, d), const),
            pl.BlockSpec((1, d), const),
            pl.BlockSpec((None, None, d, FFN_CHUNK), lambda c: (layer, slot, 0, c)),
            pl.BlockSpec((None, None, d, FFN_CHUNK), lambda c: (layer, slot, 0, n_chunks + c)),
            pl.BlockSpec((None, None, FFN_CHUNK, d), lambda c: (layer, slot, c, 0)),
            pl.BlockSpec((1, d), const),
        ],
        out_specs=[
            pl.BlockSpec((rows, d), const),
            pl.BlockSpec((d, FFN_CHUNK), lambda c: (0, c)),
            pl.BlockSpec((d, FFN_CHUNK), lambda c: (0, c)),
            pl.BlockSpec((FFN_CHUNK, d), lambda c: (c, 0)),
        ],
        out_shape=[
            jax.ShapeDtypeStruct((rows, d), F32),
            jax.ShapeDtypeStruct((d, d_ff), BF16),
            jax.ShapeDtypeStruct((d, d_ff), BF16),
            jax.ShapeDtypeStruct((d_ff, d), BF16),
        ],
        scratch_shapes=[pltpu.VMEM((rows, d), BF16), pltpu.VMEM((rows, d), F32)],
        compiler_params=pltpu.CompilerParams(
            dimension_semantics=("arbitrary",), vmem_limit_bytes=VMEM_LIMIT_BYTES),
        name="ffn_stream",
    )(x, g.reshape(1, d), ff_in, ff_in, ff_out, g_final.reshape(1, d))


GLA_H = 4
GLA_DK = 64
GLA_DV = 128
GLA_QK = GLA_H * GLA_DK
GLA_V = GLA_H * GLA_DV
GLA_RANK_PAD = 128
GLA_INV_TAU = 1.0 / 16.0
GLA_CH = 16
SG_H = 4
SG_DH = 128
SG_W = SG_H * SG_DH
SG_CHUNK = 128
EV_Q, EV_V, EV_R, EV_U, EV_VS, EV_Z, EV_END = 0, 512, 1024, 1536, 2048, 2560, 2688


def _log_sigmoid(x):
    return jnp.minimum(x, 0.0) - jnp.log(1.0 + jnp.exp(-jnp.abs(x)))


def _split3(x):
    hi = x.astype(BF16)
    r1 = x - hi.astype(F32)
    mid = r1.astype(BF16)
    lo = (r1 - mid.astype(F32)).astype(BF16)
    return hi, mid, lo


def _dot3(sel, parts):
    return _dot(sel, parts[0]) + _dot(sel, parts[1]) + _dot(sel, parts[2])


def _head_masks(width, per_head, dtype):
    lane = lax.broadcasted_iota(jnp.int32, (1, width), 1)
    return [jnp.where(lane // per_head == h, 1.0, 0.0).astype(dtype) for h in range(width // per_head)]


def _even_prompt_body(x_ref, g_ref, win_ref, wgate_ref, bgate_ref, glag_ref, lng_ref, lnb_ref,
                      sgw_ref, sgb_ref, e_ref, wout_ref,
                      xo_ref, gla_ref, sgv_ref,
                      st_ref, q_ref, k_ref, cum2_ref, qin_ref, kd_ref, dec_ref, v_ref, o_ref,
                      pcat_ref, acat_ref, mix_ref, add_ref, stb_ref, ruv_ref):
    j = pl.program_id(1)
    tt = x_ref.shape[0]
    n_chunks = tt // GLA_CH

    @pl.when(j == 0)
    def _():
        st_ref[...] = jnp.zeros_like(st_ref)

    x = x_ref[...]
    h = _rms(x, g_ref[...]).astype(BF16)

    qk = _dot(h, win_ref[:, EV_Q:EV_V])
    q = qk[:, :GLA_QK] * (GLA_DK ** -0.5)
    k = qk[:, GLA_QK:]
    v_ref[...] = _dot(h, win_ref[:, EV_V:EV_R]).astype(BF16)
    z = _dot(h, win_ref[:, EV_Z:EV_END]).astype(BF16)
    log_a = _log_sigmoid(_dot(z, wgate_ref[...]) + bgate_ref[...]) * GLA_INV_TAU

    row = lax.broadcasted_iota(jnp.int32, (tt, tt), 0)
    col = lax.broadcasted_iota(jnp.int32, (tt, tt), 1)
    same = (row // GLA_CH) == (col // GLA_CH)
    sel_cum = jnp.where(same & (col <= row), 1.0, 0.0).astype(BF16)
    sel_tot = jnp.where(same, 1.0, 0.0).astype(BF16)
    parts = _split3(log_a)
    cum = _dot3(sel_cum, parts)
    tot = _dot3(sel_tot, parts)
    q_ref[...] = q
    k_ref[...] = k
    cum2_ref[...] = cum * LOG2_E
    qin_ref[...] = (q * jnp.exp(cum)).astype(BF16)
    kd_ref[...] = (k * jnp.exp(tot - cum)).astype(BF16)
    dec_ref[...] = jnp.exp(tot)

    half = GLA_CH // 2
    proj_block = 256
    proj_cols = list(range(EV_R, EV_Z, proj_block))
    for n in range(n_chunks):
        if n % 2 == 1 and n // 2 < len(proj_cols):
            c0 = proj_cols[n // 2]
            ruv_ref[:, c0 - EV_R:c0 - EV_R + proj_block] = _dot(h, win_ref[:, c0:c0 + proj_block])
        base = n * GLA_CH
        qb = q_ref[base:base + GLA_CH, :]
        cb = cum2_ref[base:base + GLA_CH, :]
        for s in range(GLA_CH):
            ks = k_ref[base + s:base + s + 1, :]
            cs = cum2_ref[base + s:base + s + 1, :]
            if s < half:
                p = qb * ks * jnp.exp2(jnp.minimum(cb - cs, 0.0))
            else:
                p_hi = qb[half:] * ks * jnp.exp2(jnp.minimum(cb[half:] - cs, 0.0))
                p = jnp.concatenate([jnp.zeros_like(p_hi), p_hi], axis=0)
            pcat_ref[base:base + GLA_CH, s * GLA_QK:(s + 1) * GLA_QK] = p.astype(BF16)
    scores = _dot(pcat_ref[...], e_ref[...])

    u = jax.nn.gelu(ruv_ref[:, EV_U - EV_R:EV_VS - EV_R])
    v_ln = _layernorm(jax.nn.gelu(ruv_ref[:, EV_VS - EV_R:EV_Z - EV_R]), lng_ref[...], lnb_ref[...])
    v_lb = v_ln.astype(BF16)

    t_loc = lax.broadcasted_iota(jnp.int32, scores.shape, 0) % GLA_CH
    c_idx = lax.broadcasted_iota(jnp.int32, scores.shape, 1)
    causal = (c_idx % GLA_CH <= t_loc) & (c_idx < GLA_H * GLA_CH)
    acat_ref[...] = jnp.where(causal, scores, 0.0).astype(BF16)

    qk_masks = _head_masks(GLA_QK, GLA_DK, BF16)
    v_masks = _head_masks(GLA_V, GLA_DV, BF16)

    for n in range(n_chunks):
        rows = slice(n * GLA_CH, (n + 1) * GLA_CH)
        kn = kd_ref[rows, :]
        vn = v_ref[rows, :]
        lk = jnp.concatenate([kn * m for m in qk_masks], axis=0)
        vs = jnp.concatenate([vn[:, hh * GLA_DV:(hh + 1) * GLA_DV] for hh in range(GLA_H)], axis=0)
        add_ref[n] = lax.dot_general(vs, lk, (((0,), (0,)), ((), ())), preferred_element_type=F32)
    r = ruv_ref[:, 0:EV_U - EV_R]
    gate_r = r * jax.nn.sigmoid(r)
    sg_tasks = [(c, hh) for c in range(tt // SG_CHUNK) for hh in range(SG_H)]
    st = st_ref[...]
    for n in range(n_chunks):
        stb_ref[n] = st.astype(BF16)
        st = st * dec_ref[n * GLA_CH:n * GLA_CH + 1, :] + add_ref[n]
        if n < len(sg_tasks):
            c, hh = sg_tasks[n]
            mix_ref[c * SG_CHUNK:(c + 1) * SG_CHUNK, hh * SG_DH:(hh + 1) * SG_DH] = _dot(
                sgw_ref[hh], v_lb[c * SG_CHUNK:(c + 1) * SG_CHUNK, hh * SG_DH:(hh + 1) * SG_DH])
    st_ref[...] = st
    assert len(sg_tasks) <= n_chunks
    for n in range(n_chunks):
        rows = slice(n * GLA_CH, (n + 1) * GLA_CH)
        qn = qin_ref[rows, :]
        vn = v_ref[rows, :]
        lq = jnp.concatenate([qn * m for m in qk_masks], axis=0)
        oi = lax.dot_general(lq, stb_ref[n], (((1,), (1,)), ((), ())),
                             preferred_element_type=F32)
        o_inter = jnp.concatenate([oi[hh * GLA_CH:(hh + 1) * GLA_CH] for hh in range(GLA_H)], axis=1)
        vbd = jnp.concatenate([vn * m for m in v_masks]
                              + [jnp.zeros((acat_ref.shape[1] - GLA_H * GLA_CH, GLA_V), BF16)], axis=0)
        o_ref[rows, :] = o_inter + _dot(acat_ref[rows, :], vbd)

    o = o_ref[...]
    o_n = jnp.concatenate(
        [_rms(o[:, hh * GLA_DV:(hh + 1) * GLA_DV], 1.0) for hh in range(GLA_H)], axis=1) * glag_ref[...]
    out_a = o_n * gate_r
    bias = jnp.concatenate([sgb_ref[...]] * (tt // SG_CHUNK), axis=0)
    out_b = u * (mix_ref[...] + bias)

    y = _dot(jnp.concatenate([out_a, out_b], axis=1).astype(BF16), wout_ref[...])
    xo_ref[...] = x + y

    @pl.when(j == pl.num_programs(1) - 1)
    def _():
        sgv_ref[0] = v_ln[tt - SG_CHUNK:, :]
        gla_ref[0] = st_ref[...].T.reshape(GLA_H, GLA_DK, GLA_DV)


def _score_sum_matrix():
    r = jnp.arange(GLA_CH * GLA_QK)
    c = jnp.arange(128)
    s, hh = r // GLA_QK, (r % GLA_QK) // GLA_DK
    return ((c[None, :] == (hh * GLA_CH + s)[:, None])).astype(BF16)


def _even_prompt(x, batch, g, w_in, w_gate, b_gate, gla_g, ln_g, ln_b, sg_w, sg_bias, w_out, *, block_rows):
    rows, d = x.shape
    t = rows // batch
    tt = min(block_rows, t)
    nj = t // tt
    const2 = lambda b, j: (0, 0)
    full = lambda a: pl.BlockSpec(a.shape, (lambda b, j: (0,) * a.ndim), pipeline_mode=pl.Buffered(1))
    e = _score_sum_matrix()
    operands = (g, w_in, w_gate, b_gate, gla_g, ln_g, ln_b, sg_w, sg_bias, e, w_out)
    return pl.pallas_call(
        _even_prompt_body,
        grid=(batch, nj),
        in_specs=[pl.BlockSpec((tt, d), lambda b, j: (b * nj + j, 0))] + [full(a) for a in operands],
        out_specs=[
            pl.BlockSpec((tt, d), lambda b, j: (b * nj + j, 0)),
            pl.BlockSpec((1, GLA_H, GLA_DK, GLA_DV), lambda b, j: (b, 0, 0, 0)),
            pl.BlockSpec((1, SG_CHUNK, SG_W), lambda b, j: (b, 0, 0)),
        ],
        out_shape=[
            jax.ShapeDtypeStruct((rows, d), F32),
            jax.ShapeDtypeStruct((batch, GLA_H, GLA_DK, GLA_DV), F32),
            jax.ShapeDtypeStruct((batch, SG_CHUNK, SG_W), F32),
        ],
        scratch_shapes=[
            pltpu.VMEM((GLA_DV, GLA_QK), F32),
            pltpu.VMEM((tt, GLA_QK), F32),
            pltpu.VMEM((tt, GLA_QK), F32),
            pltpu.VMEM((tt, GLA_QK), F32),
            pltpu.VMEM((tt, GLA_QK), BF16),
            pltpu.VMEM((tt, GLA_QK), BF16),
            pltpu.VMEM((tt, GLA_QK), F32),
            pltpu.VMEM((tt, GLA_V), BF16),
            pltpu.VMEM((tt, GLA_V), F32),
            pltpu.VMEM((tt, GLA_CH * GLA_QK), BF16),
            pltpu.VMEM((tt, 128), BF16),
            pltpu.VMEM((tt, SG_W), F32),
            pltpu.VMEM((tt // GLA_CH, GLA_DV, GLA_QK), F32),
            pltpu.VMEM((tt // GLA_CH, GLA_DV, GLA_QK), BF16),
            pltpu.VMEM((tt, EV_Z - EV_R), F32),
        ],
        compiler_params=pltpu.CompilerParams(
            dimension_semantics=("arbitrary", "arbitrary"), vmem_limit_bytes=VMEM_LIMIT_BYTES),
        name="even_prompt",
    )(x, *operands)


def _prep_even(w_in, w_gate, b_gate, gla_g, ln_g, ln_b, sg_w, sg_b, w_out):
    z0 = 2 * GLA_QK + 2 * GLA_V
    z1 = z0 + w_gate.shape[0]
    w_z = jnp.pad(w_in[:, z0:z1], ((0, 0), (0, GLA_RANK_PAD - (z1 - z0))))
    w_in2 = jnp.concatenate([w_in[:, :z0], w_in[:, z1:], w_z], axis=1).astype(BF16)
    w_gate2 = jnp.pad(w_gate, ((0, GLA_RANK_PAD - w_gate.shape[0]), (0, 0))).astype(BF16)
    tril = jnp.tril(jnp.ones((SG_CHUNK, SG_CHUNK), bool))
    sg_w_tril = jnp.where(tril[None], sg_w, 0.0).astype(BF16)
    sg_bias = jnp.repeat(jnp.transpose(sg_b), SG_DH, axis=1)
    return dict(
        w_in=w_in2, w_gate=w_gate2, b_gate=b_gate.reshape(1, -1), gla_g=gla_g.reshape(1, -1),
        ln_g=ln_g.reshape(1, -1), ln_b=ln_b.reshape(1, -1),
        sg_w_tril=sg_w_tril, sg_bias=sg_bias, w_out=w_out.astype(BF16),
        w_qk_t=jnp.transpose(w_in2[:, EV_Q:EV_V]), w_z_t=jnp.transpose(w_in2[:, EV_Z:EV_END]),
        w_gate_t=jnp.transpose(w_gate2), b_gate_col=b_gate.reshape(-1, 1),
        sg_w0=jnp.repeat(sg_w[:, 0, 0], SG_DH).reshape(1, -1), sg_b0=jnp.repeat(sg_b[:, 0], SG_DH).reshape(1, -1))


def _even_sample_body(x_ref, s_ref, g_ref, win_ref, wqkt_ref, wzt_ref, wgate_ref, wgatet_ref, bgate_ref,
                      bgatec_ref, glag_ref, lng_ref, lnb_ref, sgw0_ref, sgb0_ref, wout_ref,
                      xo_ref, so_ref, sgv_ref, o_ref):
    bb = x_ref.shape[0]
    x = x_ref[...]
    h = _rms(x, g_ref[...]).astype(BF16)
    nt = (((1,), (1,)), ((), ()))
    qk_t = lax.dot_general(wqkt_ref[...], h, nt, preferred_element_type=F32)
    q_t = qk_t[:GLA_QK] * (GLA_DK ** -0.5)
    k_t = qk_t[GLA_QK:].astype(BF16).astype(F32)
    z_t = lax.dot_general(wzt_ref[...], h, nt, preferred_element_type=F32).astype(BF16)
    a_t = jnp.exp(_log_sigmoid(_dot(wgatet_ref[...], z_t) + bgatec_ref[...]) * GLA_INV_TAU)
    v = _dot(h, win_ref[:, EV_V:EV_R])
    v_r = v.astype(BF16).astype(F32)
    for b in range(bb):
        s_old = s_ref[b].reshape(GLA_QK, GLA_DV)
        a_c = jnp.broadcast_to(a_t[:, b:b + 1], (GLA_QK, GLA_DV))
        k_c = jnp.broadcast_to(k_t[:, b:b + 1], (GLA_QK, GLA_DV))
        q_c = jnp.broadcast_to(q_t[:, b:b + 1], (GLA_QK, GLA_DV))
        v_rows = jnp.concatenate(
            [jnp.broadcast_to(v_r[b:b + 1, hh * GLA_DV:(hh + 1) * GLA_DV], (GLA_DK, GLA_DV))
             for hh in range(GLA_H)], axis=0)
        s_new = a_c * s_old + k_c * v_rows
        so_ref[b] = s_new.reshape(GLA_H, GLA_DK, GLA_DV)
        ob = jnp.sum((q_c * s_new).reshape(GLA_H, GLA_DK, GLA_DV), axis=1)
        o_ref[b:b + 1, :] = jnp.concatenate([ob[hh:hh + 1] for hh in range(GLA_H)], axis=1)
    o = o_ref[...]
    o_n = jnp.concatenate(
        [_rms(o[:, hh * GLA_DV:(hh + 1) * GLA_DV], 1.0) for hh in range(GLA_H)], axis=1) * glag_ref[...]
    r = _dot(h, win_ref[:, EV_R:EV_U])
    out_a = o_n * (r * jax.nn.sigmoid(r))
    u = jax.nn.gelu(_dot(h, win_ref[:, EV_U:EV_VS]))
    vv = jax.nn.gelu(_dot(h, win_ref[:, EV_VS:EV_Z]))
    mu = jnp.mean(vv, axis=-1, keepdims=True)
    xc = vv - mu
    var = jnp.mean(xc * xc, axis=-1, keepdims=True)
    v_ln = xc * lax.rsqrt(var + EPS) * lng_ref[...] + lnb_ref[...]
    sgv_ref[...] = v_ln
    out_b = u * (sgw0_ref[...] * v_ln + sgb0_ref[...])
    y = _dot(jnp.concatenate([out_a, out_b], axis=1).astype(BF16), wout_ref[...])
    xo_ref[...] = x + y


def _even_sample(x, state, g, ev, *, block_rows=32):
    rows, d = x.shape
    bb = min(block_rows, rows)
    full = lambda a: pl.BlockSpec(a.shape, (lambda i: (0,) * a.ndim), pipeline_mode=pl.Buffered(1))
    operands = (g, ev["w_in"], ev["w_qk_t"], ev["w_z_t"], ev["w_gate"], ev["w_gate_t"], ev["b_gate"],
                ev["b_gate_col"], ev["gla_g"], ev["ln_g"], ev["ln_b"], ev["sg_w0"], ev["sg_b0"], ev["w_out"])
    return pl.pallas_call(
        _even_sample_body,
        grid=(rows // bb,),
        in_specs=[pl.BlockSpec((bb, d), lambda i: (i, 0)),
                  pl.BlockSpec((bb, GLA_H, GLA_DK, GLA_DV), lambda i: (i, 0, 0, 0))]
                 + [full(a) for a in operands],
        out_specs=[
            pl.BlockSpec((bb, d), lambda i: (i, 0)),
            pl.BlockSpec((bb, GLA_H, GLA_DK, GLA_DV), lambda i: (i, 0, 0, 0)),
            pl.BlockSpec((bb, SG_W), lambda i: (i, 0)),
        ],
        out_shape=[
            jax.ShapeDtypeStruct((rows, d), F32),
            jax.ShapeDtypeStruct(state.shape, F32),
            jax.ShapeDtypeStruct((rows, SG_W), F32),
        ],
        scratch_shapes=[pltpu.VMEM((bb, GLA_V), F32)],
        compiler_params=pltpu.CompilerParams(
            dimension_semantics=("arbitrary",), vmem_limit_bytes=VMEM_LIMIT_BYTES),
        name="even_sample",
    )(x, state, *operands)


CONV_W = 512
CONV_K = 31
CONV_BUF = CONV_K - 1
CONV_PAD = 32
POOL_W = 512
POOL_WINDOWS = (2, 4, 8, 16)
POOL_DG = POOL_W // len(POOL_WINDOWS)
POOL_BUF = 15
POOL_PAD = 16
SUBLANES = 8
CONV_ROWS = 32


def _layernorm(x, g, b):
    mu = jnp.mean(x, axis=-1, keepdims=True)
    xc = x - mu
    var = jnp.mean(xc * xc, axis=-1, keepdims=True)
    return xc * lax.rsqrt(var + EPS) * g + b


def _odd_prompt_body(x_ref, g_ref, win_ref, cw_ref, cb_ref, lng_ref, lnb_ref, pw_ref, ps_ref, wout_ref,
                     xo_ref, conv_ref, pool_ref, gbuf_ref, pbuf_ref, shift_ref, convo_ref, cwb_ref, psum_ref):
    j = pl.program_id(1)
    tt = x_ref.shape[0]
    assert POOL_WINDOWS == (2, 4, 8, 16)

    p0 = SUBLANES + POOL_PAD

    @pl.when(j == 0)
    def _():
        gbuf_ref[0:CONV_PAD, :] = jnp.zeros((CONV_PAD, CONV_W), F32)
        pbuf_ref[0:p0, :] = jnp.zeros((p0, POOL_W), F32)
        psum_ref[:, 0:SUBLANES, :] = jnp.zeros((psum_ref.shape[0], SUBLANES, POOL_W), F32)
        for o in range(CONV_K):
            cwb_ref[o] = jnp.broadcast_to(cw_ref[o:o + 1, :], (SUBLANES, CONV_W))

    x = x_ref[...]
    h = _rms(x, g_ref[...]).astype(BF16)
    a = _dot(h, win_ref[:, 0:CONV_W])
    gt = _dot(h, win_ref[:, CONV_W:2 * CONV_W])
    xp = _dot(h, win_ref[:, 2 * CONV_W:])
    gbuf_ref[CONV_PAD:CONV_PAD + tt, :] = a * jax.nn.sigmoid(gt)
    pbuf_ref[p0:p0 + tt, :] = xp

    first = CONV_PAD - CONV_BUF
    ext = tt + CONV_PAD - SUBLANES
    for rr in range(1, SUBLANES):
        shift_ref[rr - 1] = gbuf_ref[rr:rr + ext, :]
    groups = CONV_ROWS // SUBLANES
    for r0 in range(0, tt, CONV_ROWS):
        accs = [jnp.zeros((SUBLANES, CONV_W), F32) + cb_ref[...] for _ in range(groups)]
        for o in range(first, first + CONV_K):
            rr = o % SUBLANES
            w8 = cwb_ref[o - first]
            for gq in range(groups):
                lo = o - rr + r0 + gq * SUBLANES
                src = gbuf_ref[lo:lo + SUBLANES, :] if rr == 0 else shift_ref[rr - 1, lo:lo + SUBLANES, :]
                accs[gq] = accs[gq] + src * w8
        for gq in range(groups):
            convo_ref[r0 + gq * SUBLANES:r0 + (gq + 1) * SUBLANES, :] = accs[gq]
    out_c = _layernorm(convo_ref[...], lng_ref[...], lnb_ref[...])
    out_c = out_c * jax.nn.sigmoid(out_c)

    n_rows = POOL_PAD + tt
    lo = SUBLANES
    psum_ref[0, lo:lo + n_rows, :] = pbuf_ref[lo:lo + n_rows, :] + pbuf_ref[lo - 1:lo - 1 + n_rows, :]
    psum_ref[1, lo:lo + n_rows, POOL_DG:] = (psum_ref[0, lo:lo + n_rows, POOL_DG:]
                                             + psum_ref[0, lo - 2:lo - 2 + n_rows, POOL_DG:])
    psum_ref[2, lo:lo + n_rows, 2 * POOL_DG:] = (psum_ref[1, lo:lo + n_rows, 2 * POOL_DG:]
                                                 + psum_ref[1, lo - 4:lo - 4 + n_rows, 2 * POOL_DG:])
    tots = [psum_ref[0, p0:p0 + tt, 0:POOL_DG],
            psum_ref[1, p0:p0 + tt, POOL_DG:2 * POOL_DG],
            psum_ref[2, p0:p0 + tt, 2 * POOL_DG:3 * POOL_DG],
            psum_ref[2, p0:p0 + tt, 3 * POOL_DG:] + psum_ref[2, p0 - 8:p0 - 8 + tt, 3 * POOL_DG:]]
    t_glob = j * tt + lax.broadcasted_iota(jnp.int32, (tt, 1), 0)
    outs = []
    for gi, win_len in enumerate(POOL_WINDOWS):
        lanes = slice(gi * POOL_DG, (gi + 1) * POOL_DG)
        cnt = jnp.minimum(win_len, t_glob + 1).astype(F32)
        pooled = tots[gi] / cnt - xp[:, lanes]
        outs.append(_dot(pooled.astype(BF16), pw_ref[gi]))
    out_d = jnp.concatenate(outs, axis=1) * ps_ref[...]

    y = _dot(jnp.concatenate([out_c, out_d], axis=1).astype(BF16), wout_ref[...])
    xo_ref[...] = x + y

    tail_g = gbuf_ref[tt:tt + CONV_PAD, :]
    tail_p = pbuf_ref[SUBLANES + tt:p0 + tt, :]
    gbuf_ref[0:CONV_PAD, :] = tail_g
    pbuf_ref[SUBLANES:p0, :] = tail_p

    @pl.when(j == pl.num_programs(1) - 1)
    def _():
        conv_ref[0] = tail_g[CONV_PAD - CONV_BUF:, :]
        pool_ref[0] = tail_p[POOL_PAD - POOL_BUF:, :]


def _odd_prompt(x, batch, g, od, *, block_rows):
    rows, d = x.shape
    t = rows // batch
    tt = min(block_rows, t)
    nj = t // tt
    full = lambda a: pl.BlockSpec(a.shape, (lambda b, j: (0,) * a.ndim), pipeline_mode=pl.Buffered(1))
    operands = (g, od["w_in"], od["conv_w"], od["conv_b"], od["ln_g"], od["ln_b"], od["pool_w"],
                od["pool_scale"], od["w_out"])
    return pl.pallas_call(
        _odd_prompt_body,
        grid=(batch, nj),
        in_specs=[pl.BlockSpec((tt, d), lambda b, j: (b * nj + j, 0))] + [full(a) for a in operands],
        out_specs=[
            pl.BlockSpec((tt, d), lambda b, j: (b * nj + j, 0)),
            pl.BlockSpec((1, CONV_BUF, CONV_W), lambda b, j: (b, 0, 0)),
            pl.BlockSpec((1, POOL_BUF, POOL_W), lambda b, j: (b, 0, 0)),
        ],
        out_shape=[
            jax.ShapeDtypeStruct((rows, d), F32),
            jax.ShapeDtypeStruct((batch, CONV_BUF, CONV_W), F32),
            jax.ShapeDtypeStruct((batch, POOL_BUF, POOL_W), F32),
        ],
        scratch_shapes=[
            pltpu.VMEM((CONV_PAD + tt + SUBLANES, CONV_W), F32),
            pltpu.VMEM((SUBLANES + POOL_PAD + tt, POOL_W), F32),
            pltpu.VMEM((SUBLANES - 1, tt + CONV_PAD - SUBLANES, CONV_W), F32),
            pltpu.VMEM((tt, CONV_W), F32),
            pltpu.VMEM((CONV_K, SUBLANES, CONV_W), F32),
            pltpu.VMEM((3, SUBLANES + POOL_PAD + tt, POOL_W), F32),
        ],
        compiler_params=pltpu.CompilerParams(
            dimension_semantics=("arbitrary", "arbitrary"), vmem_limit_bytes=VMEM_LIMIT_BYTES),
        name="odd_prompt",
    )(x, *operands)


def _odd_sample_body(x_ref, cbuf_ref, pbuf_ref, g_ref, win_ref, cw_ref, cb_ref, lng_ref, lnb_ref, pw_ref,
                     ps_ref, wout_ref, xo_ref, conv_ref, pool_ref):
    x = x_ref[...]
    h = _rms(x, g_ref[...]).astype(BF16)
    a = _dot(h, win_ref[:, 0:CONV_W])
    gt = _dot(h, win_ref[:, CONV_W:2 * CONV_W])
    xp = _dot(h, win_ref[:, 2 * CONV_W:])
    glu = a * jax.nn.sigmoid(gt)
    cbuf = cbuf_ref[...]
    pbuf = pbuf_ref[...]
    conv = (jnp.sum(cbuf * cw_ref[0:CONV_BUF, :][None], axis=1)
            + glu * cw_ref[CONV_BUF:CONV_K, :] + cb_ref[...])
    out_c = _layernorm(conv, lng_ref[...], lnb_ref[...])
    out_c = out_c * jax.nn.sigmoid(out_c)
    outs = []
    for gi, win_len in enumerate(POOL_WINDOWS):
        lanes = slice(gi * POOL_DG, (gi + 1) * POOL_DG)
        tot = xp[:, lanes] + jnp.sum(pbuf[:, POOL_BUF - (win_len - 1):, lanes], axis=1)
        pooled = tot / float(win_len) - xp[:, lanes]
        outs.append(_dot(pooled.astype(BF16), pw_ref[gi]))
    out_d = jnp.concatenate(outs, axis=1) * ps_ref[...]
    y = _dot(jnp.concatenate([out_c, out_d], axis=1).astype(BF16), wout_ref[...])
    xo_ref[...] = x + y
    conv_ref[:, 0:CONV_BUF - 1, :] = cbuf_ref[:, 1:CONV_BUF, :]
    conv_ref[:, CONV_BUF - 1:CONV_BUF, :] = glu[:, None, :]
    pool_ref[:, 0:POOL_BUF - 1, :] = pbuf_ref[:, 1:POOL_BUF, :]
    pool_ref[:, POOL_BUF - 1:POOL_BUF, :] = xp[:, None, :]


def _odd_sample(x, conv_buf, pool_buf, g, od, *, block_rows=32):
    rows, d = x.shape
    bb = min(block_rows, rows)
    full = lambda a: pl.BlockSpec(a.shape, (lambda i: (0,) * a.ndim), pipeline_mode=pl.Buffered(1))
    operands = (g, od["w_in"], od["conv_w"], od["conv_b"], od["ln_g"], od["ln_b"], od["pool_w"],
                od["pool_scale"], od["w_out"])
    return pl.pallas_call(
        _odd_sample_body,
        grid=(rows // bb,),
        in_specs=[pl.BlockSpec((bb, d), lambda i: (i, 0)),
                  pl.BlockSpec((bb, CONV_BUF, CONV_W), lambda i: (i, 0, 0)),
                  pl.BlockSpec((bb, POOL_BUF, POOL_W), lambda i: (i, 0, 0))]
                 + [full(a) for a in operands],
        out_specs=[
            pl.BlockSpec((bb, d), lambda i: (i, 0)),
            pl.BlockSpec((bb, CONV_BUF, CONV_W), lambda i: (i, 0, 0)),
            pl.BlockSpec((bb, POOL_BUF, POOL_W), lambda i: (i, 0, 0)),
        ],
        out_shape=[
            jax.ShapeDtypeStruct((rows, d), F32),
            jax.ShapeDtypeStruct(conv_buf.shape, F32),
            jax.ShapeDtypeStruct(pool_buf.shape, F32),
        ],
        compiler_params=pltpu.CompilerParams(
            dimension_semantics=("arbitrary",), vmem_limit_bytes=VMEM_LIMIT_BYTES),
        name="odd_sample",
    )(x, conv_buf, pool_buf, *operands)


def _prep_odd(w_in, conv_w, conv_b, ln_g, ln_b, pool_w, pool_scale, w_out):
    return dict(w_in=w_in.astype(BF16), conv_w=conv_w, conv_b=conv_b.reshape(1, -1), ln_g=ln_g.reshape(1, -1),
                ln_b=ln_b.reshape(1, -1), pool_w=pool_w.astype(BF16), pool_scale=pool_scale.reshape(1, -1),
                w_out=w_out.astype(BF16))


def kernel(x_prompt, x_sample, state_gla, state_conv, state_pool, norm_g, ff_in, ff_out, ev_w_in, ev_w_gate, ev_b_gate, ev_gla_g, ev_sg_ln_g, ev_sg_ln_b, ev_sg_w, ev_sg_b, ev_w_out, od_w_in, od_conv_w, od_conv_b, od_ln_g, od_ln_b, od_pool_w, od_pool_scale, od_w_out, norm_f):
    bp, t, d = x_prompt.shape
    bs = x_sample.shape[0]
    depth = norm_g.shape[0]
    xp = x_prompt.reshape(bp * t, d)
    xs = x_sample.reshape(bs, d)
    gla_p, gla_s, sgv_p, sgv_s, conv_p, conv_s, pool_p, pool_s = [], [], [], [], [], [], [], []
    for layer in range(depth):
        i = layer // 2
        last = layer == depth - 1
        xs, w_a, w_b, w_o = _ffn_stream(xs, norm_g[layer, 0], ff_in, ff_out, layer, 0, norm_f, final_norm=False)
        xp = _ffn(xp, norm_g[layer, 0], w_a, w_b, w_o, norm_f, final_norm=False, block_rows=FFN_ROWS)
        g_mix = norm_g[layer, 1].reshape(1, d)
        if layer % 2 == 0:
            ev = _prep_even(ev_w_in[i], ev_w_gate[i], ev_b_gate[i], ev_gla_g[i], ev_sg_ln_g[i], ev_sg_ln_b[i],
                            ev_sg_w[i], ev_sg_b[i], ev_w_out[i])
            xp, s_p, v_p = _even_prompt(xp, bp, g_mix, ev["w_in"], ev["w_gate"], ev["b_gate"], ev["gla_g"],
                                        ev["ln_g"], ev["ln_b"], ev["sg_w_tril"], ev["sg_bias"], ev["w_out"],
                                        block_rows=256)
            xs, s_s, v_s = _even_sample(xs, state_gla[i], g_mix, ev)
            gla_p.append(s_p); gla_s.append(s_s); sgv_p.append(v_p); sgv_s.append(v_s.reshape(bs, 1, SG_W))
        else:
            od = _prep_odd(od_w_in[i], od_conv_w[i], od_conv_b[i], od_ln_g[i], od_ln_b[i], od_pool_w[i],
                           od_pool_scale[i], od_w_out[i])
            xp, c_p, p_p = _odd_prompt(xp, bp, g_mix, od, block_rows=256)
            xs, c_s, p_s = _odd_sample(xs, state_conv[i], state_pool[i], g_mix, od)
            conv_p.append(c_p); conv_s.append(c_s); pool_p.append(p_p); pool_s.append(p_s)
        xs, w_a, w_b, w_o = _ffn_stream(xs, norm_g[layer, 2], ff_in, ff_out, layer, 1, norm_f, final_norm=last)
        xp = _ffn(xp, norm_g[layer, 2], w_a, w_b, w_o, norm_f, final_norm=last, block_rows=FFN_ROWS)
    return (xp.reshape(bp, t, d), xs.reshape(bs, 1, d), jnp.stack(gla_p), jnp.stack(gla_s),
            jnp.stack(sgv_p), jnp.stack(sgv_s), jnp.stack(conv_p), jnp.stack(conv_s),
            jnp.stack(pool_p), jnp.stack(pool_s))
```

```python
import functools

import jax
import jax.numpy as jnp
from jax import lax
from jax.experimental import pallas as pl
from jax.experimental.pallas import tpu as pltpu

EPS = 1e-6
LOG2_E = 1.4426950408889634
BF16 = jnp.bfloat16
F32 = jnp.float32

VMEM_LIMIT_BYTES = 56 * 1024 * 1024
FFN_ROWS = 1024


def _rms(x, g):
    return x * lax.rsqrt(jnp.mean(x * x, axis=-1, keepdims=True) + EPS) * g


def _dot(a, b):
    return jnp.dot(a, b, preferred_element_type=F32)


FFN_CHUNK = 256


def _ffn_body(x_ref, g_ref, wa_ref, wb_ref, wout_ref, gf_ref, o_ref, act_ref, *, final_norm):
    d_ff = wout_ref.shape[0]
    x = x_ref[...]
    h = _rms(x, g_ref[...]).astype(BF16)
    for c in range(d_ff // FFN_CHUNK):
        cols = slice(c * FFN_CHUNK, (c + 1) * FFN_CHUNK)
        a = _dot(h, wa_ref[:, cols])
        b = _dot(h, wb_ref[:, cols])
        act_ref[:, cols] = (a * jax.nn.sigmoid(a) * b).astype(BF16)
    y = x + 0.5 * _dot(act_ref[...], wout_ref[...])
    if final_norm:
        y = _rms(y, gf_ref[...])
    o_ref[...] = y


def _ffn(x, g, w_a, w_b, w_out, g_final, *, final_norm, block_rows):
    rows, d = x.shape
    d_ff = w_out.shape[0]
    tm = min(block_rows, rows)
    body = functools.partial(_ffn_body, final_norm=final_norm)
    const = lambda i: (0, 0)
    return pl.pallas_call(
        body,
        grid=(rows // tm,),
        in_specs=[
            pl.BlockSpec((tm, d), lambda i: (i, 0)),
            pl.BlockSpec((1, d), const),
            pl.BlockSpec((d, d_ff), const, pipeline_mode=pl.Buffered(1)),
            pl.BlockSpec((d, d_ff), const, pipeline_mode=pl.Buffered(1)),
            pl.BlockSpec((d_ff, d), const, pipeline_mode=pl.Buffered(1)),
            pl.BlockSpec((1, d), const),
        ],
        out_specs=pl.BlockSpec((tm, d), lambda i: (i, 0)),
        out_shape=jax.ShapeDtypeStruct((rows, d), F32),
        scratch_shapes=[pltpu.VMEM((tm, d_ff), BF16)],
        compiler_params=pltpu.CompilerParams(
            dimension_semantics=("arbitrary",), vmem_limit_bytes=VMEM_LIMIT_BYTES),
        name="ffn",
    )(x, g.reshape(1, d), w_a, w_b, w_out, g_final.reshape(1, d))


def _ffn_stream_body(x_ref, g_ref, wa_ref, wb_ref, wout_ref, gf_ref,
                     o_ref, wa_o_ref, wb_o_ref, wout_o_ref, h_ref, acc_ref, *, final_norm):
    c = pl.program_id(0)

    @pl.when(c == 0)
    def _():
        h_ref[...] = _rms(x_ref[...], g_ref[...]).astype(BF16)
        acc_ref[...] = jnp.zeros_like(acc_ref)

    wa = wa_ref[...].astype(BF16)
    wb = wb_ref[...].astype(BF16)
    wout = wout_ref[...].astype(BF16)
    wa_o_ref[...] = wa
    wb_o_ref[...] = wb
    wout_o_ref[...] = wout
    h = h_ref[...]
    a = _dot(h, wa)
    b = _dot(h, wb)
    acc_ref[...] += _dot((a * jax.nn.sigmoid(a) * b).astype(BF16), wout)

    @pl.when(c == pl.num_programs(0) - 1)
    def _():
        y = x_ref[...] + 0.5 * acc_ref[...]
        if final_norm:
            y = _rms(y, gf_ref[...])
        o_ref[...] = y


def _ffn_stream(x, g, ff_in, ff_out, layer, slot, g_final, *, final_norm):
    rows, d = x.shape
    d_ff = ff_out.shape[2]
    n_chunks = d_ff // FFN_CHUNK
    const = lambda c: (0, 0)
    body = functools.partial(_ffn_stream_body, final_norm=final_norm)
    return pl.pallas_call(
        body,
        grid=(n_chunks,),
        in_specs=[
            pl.BlockSpec((rows, d), const),
            pl.BlockSpec((1, d), const),
            pl.BlockSpec((None, None, d, FFN_CHUNK), lambda c: (layer, slot, 0, c)),
            pl.BlockSpec((None, None, d, FFN_CHUNK), lambda c: (layer, slot, 0, n_chunks + c)),
            pl.BlockSpec((None, None, FFN_CHUNK, d), lambda c: (layer, slot, c, 0)),
            pl.BlockSpec((1, d), const),
        ],
        out_specs=[
            pl.BlockSpec((rows, d), const),
            pl.BlockSpec((d, FFN_CHUNK), lambda c: (0, c)),
            pl.BlockSpec((d, FFN_CHUNK), lambda c: (0, c)),
            pl.BlockSpec((FFN_CHUNK, d), lambda c: (c, 0)),
        ],
        out_shape=[
            jax.ShapeDtypeStruct((rows, d), F32),
            jax.ShapeDtypeStruct((d, d_ff), BF16),
            jax.ShapeDtypeStruct((d, d_ff), BF16),
            jax.ShapeDtypeStruct((d_ff, d), BF16),
        ],
        scratch_shapes=[pltpu.VMEM((rows, d), BF16), pltpu.VMEM((rows, d), F32)],
        compiler_params=pltpu.CompilerParams(
            dimension_semantics=("arbitrary",), vmem_limit_bytes=VMEM_LIMIT_BYTES),
        name="ffn_stream",
    )(x, g.reshape(1, d), ff_in, ff_in, ff_out, g_final.reshape(1, d))


GLA_H = 4
GLA_DK = 64
GLA_DV = 128
GLA_QK = GLA_H * GLA_DK
GLA_V = GLA_H * GLA_DV
GLA_RANK_PAD = 128
GLA_INV_TAU = 1.0 / 16.0
GLA_CH = 16
GLA_SEL_ROWS = 256
SG_H = 4
SG_DH = 128
SG_W = SG_H * SG_DH
SG_CHUNK = 128
EV_Q, EV_V, EV_R, EV_U, EV_VS, EV_Z, EV_END = 0, 512, 1024, 1536, 2048, 2560, 2688


def _log_sigmoid(x):
    return jnp.minimum(x, 0.0) - jnp.log(1.0 + jnp.exp(-jnp.abs(x)))


def _split3(x):
    hi = x.astype(BF16)
    r1 = x - hi.astype(F32)
    mid = r1.astype(BF16)
    lo = (r1 - mid.astype(F32)).astype(BF16)
    return hi, mid, lo


def _dot3(sel, parts):
    return _dot(sel, parts[0]) + _dot(sel, parts[1]) + _dot(sel, parts[2])


def _head_masks(width, per_head, dtype):
    lane = lax.broadcasted_iota(jnp.int32, (1, width), 1)
    return [jnp.where(lane // per_head == h, 1.0, 0.0).astype(dtype) for h in range(width // per_head)]


def _even_prompt_body(x_ref, g_ref, win_ref, wgate_ref, bgate_ref, glag_ref, lng_ref, lnb_ref,
                      sgw_ref, sgb_ref, e_ref, wout_ref,
                      xo_ref, gla_ref, sgv_ref,
                      st_ref, q_ref, k_ref, cum2_ref, qin_ref, kd_ref, dec_ref, v_ref, o_ref,
                      pcat_ref, acat_ref, mix_ref, add_ref, stb_ref, ruv_ref):
    j = pl.program_id(1)
    tt = x_ref.shape[0]
    n_chunks = tt // GLA_CH

    @pl.when(j == 0)
    def _():
        st_ref[...] = jnp.zeros_like(st_ref)

    x = x_ref[...]
    h = _rms(x, g_ref[...]).astype(BF16)

    qk = _dot(h, win_ref[:, EV_Q:EV_V])
    q = qk[:, :GLA_QK] * (GLA_DK ** -0.5)
    k = qk[:, GLA_QK:]
    v_ref[...] = _dot(h, win_ref[:, EV_V:EV_R]).astype(BF16)
    z = _dot(h, win_ref[:, EV_Z:EV_END]).astype(BF16)
    log_a = _log_sigmoid(_dot(z, wgate_ref[...]) + bgate_ref[...]) * GLA_INV_TAU

    sel_rows = min(tt, GLA_SEL_ROWS)
    row = lax.broadcasted_iota(jnp.int32, (sel_rows, sel_rows), 0)
    col = lax.broadcasted_iota(jnp.int32, (sel_rows, sel_rows), 1)
    same = (row // GLA_CH) == (col // GLA_CH)
    sel_cum = jnp.where(same & (col <= row), 1.0, 0.0).astype(BF16)
    sel_tot = jnp.where(same, 1.0, 0.0).astype(BF16)
    parts = _split3(log_a)
    groups_of_rows = [slice(r0, r0 + sel_rows) for r0 in range(0, tt, sel_rows)]
    cum = jnp.concatenate([_dot3(sel_cum, [p[rs] for p in parts]) for rs in groups_of_rows], axis=0)
    tot = jnp.concatenate([_dot3(sel_tot, [p[rs] for p in parts]) for rs in groups_of_rows], axis=0)
    q_ref[...] = q
    k_ref[...] = k
    cum2_ref[...] = cum * LOG2_E
    qin_ref[...] = (q * jnp.exp(cum)).astype(BF16)
    kd_ref[...] = (k * jnp.exp(tot - cum)).astype(BF16)
    dec_ref[...] = jnp.exp(tot)

    half = GLA_CH // 2
    proj_block = 256
    proj_cols = list(range(EV_R, EV_Z, proj_block))
    proj_every = n_chunks // len(proj_cols)
    assert proj_every >= 1
    for n in range(n_chunks):
        if n % proj_every == proj_every - 1 and n // proj_every < len(proj_cols):
            c0 = proj_cols[n // proj_every]
            ruv_ref[:, c0 - EV_R:c0 - EV_R + proj_block] = _dot(h, win_ref[:, c0:c0 + proj_block])
        base = n * GLA_CH
        qb = q_ref[base:base + GLA_CH, :]
        cb = cum2_ref[base:base + GLA_CH, :]
        for s in range(GLA_CH):
            ks = k_ref[base + s:base + s + 1, :]
            cs = cum2_ref[base + s:base + s + 1, :]
            if s < half:
                p = qb * ks * jnp.exp2(jnp.minimum(cb - cs, 0.0))
            else:
                p_hi = qb[half:] * ks * jnp.exp2(jnp.minimum(cb[half:] - cs, 0.0))
                p = jnp.concatenate([jnp.zeros_like(p_hi), p_hi], axis=0)
            pcat_ref[base:base + GLA_CH, s * GLA_QK:(s + 1) * GLA_QK] = p.astype(BF16)
    scores = _dot(pcat_ref[...], e_ref[...])

    u = jax.nn.gelu(ruv_ref[:, EV_U - EV_R:EV_VS - EV_R])
    v_ln = _layernorm(jax.nn.gelu(ruv_ref[:, EV_VS - EV_R:EV_Z - EV_R]), lng_ref[...], lnb_ref[...])
    v_lb = v_ln.astype(BF16)

    t_loc = lax.broadcasted_iota(jnp.int32, scores.shape, 0) % GLA_CH
    c_idx = lax.broadcasted_iota(jnp.int32, scores.shape, 1)
    causal = (c_idx % GLA_CH <= t_loc) & (c_idx < GLA_H * GLA_CH)
    acat_ref[...] = jnp.where(causal, scores, 0.0).astype(BF16)

    qk_masks = _head_masks(GLA_QK, GLA_DK, BF16)
    v_masks = _head_masks(GLA_V, GLA_DV, BF16)

    for n in range(n_chunks):
        rows = slice(n * GLA_CH, (n + 1) * GLA_CH)
        kn = kd_ref[rows, :]
        vn = v_ref[rows, :]
        lk = jnp.concatenate([kn * m for m in qk_masks], axis=0)
        vs = jnp.concatenate([vn[:, hh * GLA_DV:(hh + 1) * GLA_DV] for hh in range(GLA_H)], axis=0)
        add_ref[n] = lax.dot_general(vs, lk, (((0,), (0,)), ((), ())), preferred_element_type=F32)
    r = ruv_ref[:, 0:EV_U - EV_R]
    gate_r = r * jax.nn.sigmoid(r)
    sg_tasks = [(c, hh) for c in range(tt // SG_CHUNK) for hh in range(SG_H)]
    st = st_ref[...]
    for n in range(n_chunks):
        stb_ref[n] = st.astype(BF16)
        st = st * dec_ref[n * GLA_CH:n * GLA_CH + 1, :] + add_ref[n]
        if n < len(sg_tasks):
            c, hh = sg_tasks[n]
            mix_ref[c * SG_CHUNK:(c + 1) * SG_CHUNK, hh * SG_DH:(hh + 1) * SG_DH] = _dot(
                sgw_ref[hh], v_lb[c * SG_CHUNK:(c + 1) * SG_CHUNK, hh * SG_DH:(hh + 1) * SG_DH])
    st_ref[...] = st
    assert len(sg_tasks) <= n_chunks
    for n in range(n_chunks):
        rows = slice(n * GLA_CH, (n + 1) * GLA_CH)
        qn = qin_ref[rows, :]
        vn = v_ref[rows, :]
        lq = jnp.concatenate([qn * m for m in qk_masks], axis=0)
        oi = lax.dot_general(lq, stb_ref[n], (((1,), (1,)), ((), ())),
                             preferred_element_type=F32)
        o_inter = jnp.concatenate([oi[hh * GLA_CH:(hh + 1) * GLA_CH] for hh in range(GLA_H)], axis=1)
        vbd = jnp.concatenate([vn * m for m in v_masks]
                              + [jnp.zeros((acat_ref.shape[1] - GLA_H * GLA_CH, GLA_V), BF16)], axis=0)
        o_ref[rows, :] = o_inter + _dot(acat_ref[rows, :], vbd)

    o = o_ref[...]
    o_n = jnp.concatenate(
        [_rms(o[:, hh * GLA_DV:(hh + 1) * GLA_DV], 1.0) for hh in range(GLA_H)], axis=1) * glag_ref[...]
    out_a = o_n * gate_r
    bias = jnp.concatenate([sgb_ref[...]] * (tt // SG_CHUNK), axis=0)
    out_b = u * (mix_ref[...] + bias)

    y = _dot(jnp.concatenate([out_a, out_b], axis=1).astype(BF16), wout_ref[...])
    xo_ref[...] = x + y

    @pl.when(j == pl.num_programs(1) - 1)
    def _():
        sgv_ref[0] = v_ln[tt - SG_CHUNK:, :]
        gla_ref[0] = st_ref[...].T.reshape(GLA_H, GLA_DK, GLA_DV)


def _score_sum_matrix():
    r = jnp.arange(GLA_CH * GLA_QK)
    c = jnp.arange(128)
    s, hh = r // GLA_QK, (r % GLA_QK) // GLA_DK
    return ((c[None, :] == (hh * GLA_CH + s)[:, None])).astype(BF16)


def _even_prompt(x, batch, g, w_in, w_gate, b_gate, gla_g, ln_g, ln_b, sg_w, sg_bias, w_out, *, block_rows):
    rows, d = x.shape
    t = rows // batch
    tt = min(block_rows, t)
    nj = t // tt
    const2 = lambda b, j: (0, 0)
    full = lambda a: pl.BlockSpec(a.shape, (lambda b, j: (0,) * a.ndim), pipeline_mode=pl.Buffered(1))
    e = _score_sum_matrix()
    operands = (g, w_in, w_gate, b_gate, gla_g, ln_g, ln_b, sg_w, sg_bias, e, w_out)
    return pl.pallas_call(
        _even_prompt_body,
        grid=(batch, nj),
        in_specs=[pl.BlockSpec((tt, d), lambda b, j: (b * nj + j, 0))] + [full(a) for a in operands],
        out_specs=[
            pl.BlockSpec((tt, d), lambda b, j: (b * nj + j, 0)),
            pl.BlockSpec((1, GLA_H, GLA_DK, GLA_DV), lambda b, j: (b, 0, 0, 0)),
            pl.BlockSpec((1, SG_CHUNK, SG_W), lambda b, j: (b, 0, 0)),
        ],
        out_shape=[
            jax.ShapeDtypeStruct((rows, d), F32),
            jax.ShapeDtypeStruct((batch, GLA_H, GLA_DK, GLA_DV), F32),
            jax.ShapeDtypeStruct((batch, SG_CHUNK, SG_W), F32),
        ],
        scratch_shapes=[
            pltpu.VMEM((GLA_DV, GLA_QK), F32),
            pltpu.VMEM((tt, GLA_QK), F32),
            pltpu.VMEM((tt, GLA_QK), F32),
            pltpu.VMEM((tt, GLA_QK), F32),
            pltpu.VMEM((tt, GLA_QK), BF16),
            pltpu.VMEM((tt, GLA_QK), BF16),
            pltpu.VMEM((tt, GLA_QK), F32),
            pltpu.VMEM((tt, GLA_V), BF16),
            pltpu.VMEM((tt, GLA_V), F32),
            pltpu.VMEM((tt, GLA_CH * GLA_QK), BF16),
            pltpu.VMEM((tt, 128), BF16),
            pltpu.VMEM((tt, SG_W), F32),
            pltpu.VMEM((tt // GLA_CH, GLA_DV, GLA_QK), F32),
            pltpu.VMEM((tt // GLA_CH, GLA_DV, GLA_QK), BF16),
            pltpu.VMEM((tt, EV_Z - EV_R), F32),
        ],
        compiler_params=pltpu.CompilerParams(
            dimension_semantics=("arbitrary", "arbitrary"), vmem_limit_bytes=VMEM_LIMIT_BYTES),
        name="even_prompt",
    )(x, *operands)


def _prep_even(b_gate, gla_g, ln_g, ln_b, sg_w, sg_b):
    return dict(
        b_gate=b_gate.reshape(1, -1), b_gate_col=b_gate.reshape(-1, 1), gla_g=gla_g.reshape(1, -1),
        ln_g=ln_g.reshape(1, -1), ln_b=ln_b.reshape(1, -1),
        sg_bias=jnp.repeat(jnp.transpose(sg_b), SG_DH, axis=1),
        sg_w0=jnp.repeat(sg_w[:, 0, 0], SG_DH).reshape(1, -1), sg_b0=jnp.repeat(sg_b[:, 0], SG_DH).reshape(1, -1))


def _even_sample_body(x_ref, s_ref, g_ref, win32_ref, wgate32_ref, wout32_ref, sgw32_ref, bgate_ref,
                      bgatec_ref, glag_ref, lng_ref, lnb_ref, sgw0_ref, sgb0_ref,
                      xo_ref, so_ref, sgv_ref, win_ref, wgate_ref, wout_ref, sgw_ref,
                      o_ref, wqkt_ref, wzt_ref, wgatet_ref):
    bb = x_ref.shape[0]

    @pl.when(pl.program_id(0) == 0)
    def _():
        z0 = 2 * GLA_QK + 2 * GLA_V
        rank = wgate32_ref.shape[0]
        win_ref[:, 0:z0] = win32_ref[:, 0:z0].astype(BF16)
        win_ref[:, z0:EV_Z] = win32_ref[:, z0 + rank:].astype(BF16)
        z_tile = win32_ref[:, z0:z0 + GLA_RANK_PAD]
        lane = lax.broadcasted_iota(jnp.int32, z_tile.shape, 1)
        win_ref[:, EV_Z:EV_END] = jnp.where(lane < rank, z_tile, 0.0).astype(BF16)
        wgate_ref[...] = jnp.concatenate(
            [wgate32_ref[...].astype(BF16), jnp.zeros((GLA_RANK_PAD - rank, GLA_QK), BF16)], axis=0)
        wout_ref[...] = wout32_ref[...].astype(BF16)
        row = lax.broadcasted_iota(jnp.int32, (SG_CHUNK, SG_CHUNK), 0)
        col = lax.broadcasted_iota(jnp.int32, (SG_CHUNK, SG_CHUNK), 1)
        for hh in range(SG_H):
            sgw_ref[hh] = jnp.where(col <= row, sgw32_ref[hh], 0.0).astype(BF16)
        wqkt_ref[...] = win_ref[:, EV_Q:EV_V].T
        wzt_ref[...] = win_ref[:, EV_Z:EV_END].T
        wgatet_ref[...] = wgate_ref[...].T

    x = x_ref[...]
    h = _rms(x, g_ref[...]).astype(BF16)
    nt = (((1,), (1,)), ((), ()))
    qk_t = lax.dot_general(wqkt_ref[...], h, nt, preferred_element_type=F32)
    q_t = qk_t[:GLA_QK] * (GLA_DK ** -0.5)
    k_t = qk_t[GLA_QK:].astype(BF16).astype(F32)
    z_t = lax.dot_general(wzt_ref[...], h, nt, preferred_element_type=F32).astype(BF16)
    a_t = jnp.exp(_log_sigmoid(_dot(wgatet_ref[...], z_t) + bgatec_ref[...]) * GLA_INV_TAU)
    v = _dot(h, win_ref[:, EV_V:EV_R])
    v_r = v.astype(BF16).astype(F32)
    for b in range(bb):
        s_old = s_ref[b].reshape(GLA_QK, GLA_DV)
        a_c = jnp.broadcast_to(a_t[:, b:b + 1], (GLA_QK, GLA_DV))
        k_c = jnp.broadcast_to(k_t[:, b:b + 1], (GLA_QK, GLA_DV))
        q_c = jnp.broadcast_to(q_t[:, b:b + 1], (GLA_QK, GLA_DV))
        v_rows = jnp.concatenate(
            [jnp.broadcast_to(v_r[b:b + 1, hh * GLA_DV:(hh + 1) * GLA_DV], (GLA_DK, GLA_DV))
             for hh in range(GLA_H)], axis=0)
        s_new = a_c * s_old + k_c * v_rows
        so_ref[b] = s_new.reshape(GLA_H, GLA_DK, GLA_DV)
        ob = jnp.sum((q_c * s_new).reshape(GLA_H, GLA_DK, GLA_DV), axis=1)
        o_ref[b:b + 1, :] = jnp.concatenate([ob[hh:hh + 1] for hh in range(GLA_H)], axis=1)
    o = o_ref[...]
    o_n = jnp.concatenate(
        [_rms(o[:, hh * GLA_DV:(hh + 1) * GLA_DV], 1.0) for hh in range(GLA_H)], axis=1) * glag_ref[...]
    r = _dot(h, win_ref[:, EV_R:EV_U])
    out_a = o_n * (r * jax.nn.sigmoid(r))
    u = jax.nn.gelu(_dot(h, win_ref[:, EV_U:EV_VS]))
    vv = jax.nn.gelu(_dot(h, win_ref[:, EV_VS:EV_Z]))
    mu = jnp.mean(vv, axis=-1, keepdims=True)
    xc = vv - mu
    var = jnp.mean(xc * xc, axis=-1, keepdims=True)
    v_ln = xc * lax.rsqrt(var + EPS) * lng_ref[...] + lnb_ref[...]
    sgv_ref[...] = v_ln
    out_b = u * (sgw0_ref[...] * v_ln + sgb0_ref[...])
    y = _dot(jnp.concatenate([out_a, out_b], axis=1).astype(BF16), wout_ref[...])
    xo_ref[...] = x + y


def _even_sample(x, state, g, w_in, w_gate, w_out, sg_w, ev, *, block_rows=32):
    rows, d = x.shape
    bb = min(block_rows, rows)
    full = lambda a: pl.BlockSpec(a.shape, (lambda i: (0,) * a.ndim), pipeline_mode=pl.Buffered(1))
    whole = lambda shape: pl.BlockSpec(shape, (lambda i: (0,) * len(shape)))
    operands = (g, w_in, w_gate, w_out, sg_w, ev["b_gate"], ev["b_gate_col"], ev["gla_g"], ev["ln_g"],
                ev["ln_b"], ev["sg_w0"], ev["sg_b0"])
    w_shapes = [(d, EV_END), (GLA_RANK_PAD, GLA_QK), w_out.shape, sg_w.shape]
    return pl.pallas_call(
        _even_sample_body,
        grid=(rows // bb,),
        in_specs=[pl.BlockSpec((bb, d), lambda i: (i, 0)),
                  pl.BlockSpec((bb, GLA_H, GLA_DK, GLA_DV), lambda i: (i, 0, 0, 0))]
                 + [full(a) for a in operands],
        out_specs=[
            pl.BlockSpec((bb, d), lambda i: (i, 0)),
            pl.BlockSpec((bb, GLA_H, GLA_DK, GLA_DV), lambda i: (i, 0, 0, 0)),
            pl.BlockSpec((bb, SG_W), lambda i: (i, 0)),
        ] + [whole(s) for s in w_shapes],
        out_shape=[
            jax.ShapeDtypeStruct((rows, d), F32),
            jax.ShapeDtypeStruct(state.shape, F32),
            jax.ShapeDtypeStruct((rows, SG_W), F32),
        ] + [jax.ShapeDtypeStruct(s, BF16) for s in w_shapes],
        scratch_shapes=[
            pltpu.VMEM((bb, GLA_V), F32),
            pltpu.VMEM((2 * GLA_QK, d), BF16),
            pltpu.VMEM((GLA_RANK_PAD, d), BF16),
            pltpu.VMEM((GLA_QK, GLA_RANK_PAD), BF16),
        ],
        compiler_params=pltpu.CompilerParams(
            dimension_semantics=("arbitrary",), vmem_limit_bytes=VMEM_LIMIT_BYTES),
        name="even_sample",
    )(x, state, *operands)


CONV_W = 512
CONV_K = 31
CONV_BUF = CONV_K - 1
CONV_PAD = 32
POOL_W = 512
POOL_WINDOWS = (2, 4, 8, 16)
POOL_DG = POOL_W // len(POOL_WINDOWS)
POOL_BUF = 15
POOL_PAD = 16
SUBLANES = 8
CONV_ROWS = 32
ODD_SUB = 128
ODD_COLS = 256


def _layernorm(x, g, b):
    mu = jnp.mean(x, axis=-1, keepdims=True)
    xc = x - mu
    var = jnp.mean(xc * xc, axis=-1, keepdims=True)
    return xc * lax.rsqrt(var + EPS) * g + b


def _odd_prompt_body(x_ref, g_ref, win_ref, cw_ref, cb_ref, lng_ref, lnb_ref, pw_ref, ps_ref, wout_ref,
                     xo_ref, conv_ref, pool_ref, gbuf_ref, pbuf_ref, shift_ref, cwb_ref, psum_ref,
                     h_ref, mixin_ref, pooled_ref, raw_ref, convo_ref):
    j = pl.program_id(1)
    tt = x_ref.shape[0]
    assert POOL_WINDOWS == (2, 4, 8, 16)

    p0 = SUBLANES + POOL_PAD

    @pl.when(j == 0)
    def _():
        gbuf_ref[0:CONV_PAD, :] = jnp.zeros((CONV_PAD, CONV_W), F32)
        pbuf_ref[0:p0, :] = jnp.zeros((p0, POOL_W), F32)
        psum_ref[:, 0:SUBLANES, :] = jnp.zeros((psum_ref.shape[0], SUBLANES, POOL_W), F32)
        for o in range(CONV_K):
            cwb_ref[o] = jnp.broadcast_to(cw_ref[o:o + 1, :], (SUBLANES, CONV_W))

    h_ref[...] = _rms(x_ref[...], g_ref[...]).astype(BF16)
    first = CONV_PAD - CONV_BUF
    groups = CONV_ROWS // SUBLANES
    n_sub = tt // ODD_SUB

    def project(k):
        rows = slice(k * ODD_SUB, (k + 1) * ODD_SUB)

        def piece(c0):
            def run():
                raw_ref[rows, c0:c0 + ODD_COLS] = _dot(h_ref[rows, :], win_ref[:, c0:c0 + ODD_COLS])
            return run

        return [piece(c0) for c0 in range(0, 2 * CONV_W, ODD_COLS)]

    def glu(k):
        rows = slice(k * ODD_SUB, (k + 1) * ODD_SUB)
        gbuf_ref[CONV_PAD + k * ODD_SUB:CONV_PAD + (k + 1) * ODD_SUB, :] = (
            raw_ref[rows, 0:CONV_W] * jax.nn.sigmoid(raw_ref[rows, CONV_W:2 * CONV_W]))

    def mix(k):
        r_lo = k * ODD_SUB

        def shift_copies():
            lo = 0 if k == 0 else r_lo + CONV_PAD - SUBLANES
            hi = r_lo + ODD_SUB + CONV_PAD - SUBLANES
            for rr in range(1, SUBLANES):
                shift_ref[rr - 1, lo:hi, :] = gbuf_ref[rr + lo:rr + hi, :]

        def conv_block(r0):
            def run():
                accs = [jnp.zeros((SUBLANES, CONV_W), F32) + cb_ref[...] for _ in range(groups)]
                for o in range(first, first + CONV_K):
                    rr = o % SUBLANES
                    w8 = cwb_ref[o - first]
                    for gq in range(groups):
                        lo = o - rr + r0 + gq * SUBLANES
                        src = (gbuf_ref[lo:lo + SUBLANES, :] if rr == 0
                               else shift_ref[rr - 1, lo:lo + SUBLANES, :])
                        accs[gq] = accs[gq] + src * w8
                for gq in range(groups):
                    convo_ref[r0 + gq * SUBLANES:r0 + (gq + 1) * SUBLANES, :] = accs[gq]
            return run

        def norm_act():
            out_c = _layernorm(convo_ref[r_lo:r_lo + ODD_SUB, :], lng_ref[...], lnb_ref[...])
            mixin_ref[r_lo:r_lo + ODD_SUB, 0:CONV_W] = (out_c * jax.nn.sigmoid(out_c)).astype(BF16)

        def pooling():
            lo = SUBLANES if k == 0 else p0 + r_lo
            hi = p0 + r_lo + ODD_SUB
            psum_ref[0, lo:hi, :] = pbuf_ref[lo:hi, :] + pbuf_ref[lo - 1:hi - 1, :]
            psum_ref[1, lo:hi, POOL_DG:] = psum_ref[0, lo:hi, POOL_DG:] + psum_ref[0, lo - 2:hi - 2, POOL_DG:]
            psum_ref[2, lo:hi, 2 * POOL_DG:] = (psum_ref[1, lo:hi, 2 * POOL_DG:]
                                                + psum_ref[1, lo - 4:hi - 4, 2 * POOL_DG:])
            o_lo = p0 + r_lo
            o_hi = o_lo + ODD_SUB
            tots = [psum_ref[0, o_lo:o_hi, 0:POOL_DG],
                    psum_ref[1, o_lo:o_hi, POOL_DG:2 * POOL_DG],
                    psum_ref[2, o_lo:o_hi, 2 * POOL_DG:3 * POOL_DG],
                    psum_ref[2, o_lo:o_hi, 3 * POOL_DG:] + psum_ref[2, o_lo - 8:o_hi - 8, 3 * POOL_DG:]]
            t_glob = j * tt + r_lo + lax.broadcasted_iota(jnp.int32, (ODD_SUB, 1), 0)
            for gi, win_len in enumerate(POOL_WINDOWS):
                lanes = slice(gi * POOL_DG, (gi + 1) * POOL_DG)
                cnt = jnp.minimum(win_len, t_glob + 1).astype(F32)
                pooled = tots[gi] / cnt - pbuf_ref[o_lo:o_hi, lanes]
                pooled_ref[r_lo:r_lo + ODD_SUB, lanes] = pooled.astype(BF16)

        return ([shift_copies] + [conv_block(r0) for r0 in range(r_lo, r_lo + ODD_SUB, CONV_ROWS)]
                + [norm_act, pooling])

    def output(k):
        rows = slice(k * ODD_SUB, (k + 1) * ODD_SUB)

        def pool_matmuls():
            outs = [_dot(pooled_ref[rows, gi * POOL_DG:(gi + 1) * POOL_DG], pw_ref[gi])
                    for gi in range(len(POOL_WINDOWS))]
            mixin_ref[rows, CONV_W:] = (jnp.concatenate(outs, axis=1) * ps_ref[...]).astype(BF16)

        def out_piece(c0):
            def run():
                xo_ref[rows, c0:c0 + ODD_COLS] = x_ref[rows, c0:c0 + ODD_COLS] + _dot(
                    mixin_ref[rows, :], wout_ref[:, c0:c0 + ODD_COLS])
            return run

        return [pool_matmuls] + [out_piece(c0) for c0 in range(0, xo_ref.shape[1], ODD_COLS)]

    pbuf_ref[p0:p0 + tt, :] = _dot(h_ref[...], win_ref[:, 2 * CONV_W:])
    for task in project(0):
        task()
    glu(0)
    for k in range(n_sub):
        valu_tasks = mix(k)
        mxu_tasks = output(k - 1) if k >= 1 else []
        if k + 1 < n_sub:
            mxu_tasks = project(k + 1) + mxu_tasks
            valu_tasks = valu_tasks + [functools.partial(glu, k + 1)]
        for idx, task in enumerate(valu_tasks):
            lo_m = idx * len(mxu_tasks) // len(valu_tasks)
            hi_m = (idx + 1) * len(mxu_tasks) // len(valu_tasks)
            for m_task in mxu_tasks[lo_m:hi_m]:
                m_task()
            task()
    for task in output(n_sub - 1):
        task()

    tail_g = gbuf_ref[tt:tt + CONV_PAD, :]
    tail_p = pbuf_ref[SUBLANES + tt:p0 + tt, :]
    gbuf_ref[0:CONV_PAD, :] = tail_g
    pbuf_ref[SUBLANES:p0, :] = tail_p

    @pl.when(j == pl.num_programs(1) - 1)
    def _():
        conv_ref[0] = tail_g[CONV_PAD - CONV_BUF:, :]
        pool_ref[0] = tail_p[POOL_PAD - POOL_BUF:, :]


def _odd_prompt(x, batch, g, od, *, block_rows):
    rows, d = x.shape
    t = rows // batch
    tt = min(block_rows, t)
    nj = t // tt
    full = lambda a: pl.BlockSpec(a.shape, (lambda b, j: (0,) * a.ndim), pipeline_mode=pl.Buffered(1))
    operands = (g, od["w_in"], od["conv_w"], od["conv_b"], od["ln_g"], od["ln_b"], od["pool_w"],
                od["pool_scale"], od["w_out"])
    return pl.pallas_call(
        _odd_prompt_body,
        grid=(batch, nj),
        in_specs=[pl.BlockSpec((tt, d), lambda b, j: (b * nj + j, 0))] + [full(a) for a in operands],
        out_specs=[
            pl.BlockSpec((tt, d), lambda b, j: (b * nj + j, 0)),
            pl.BlockSpec((1, CONV_BUF, CONV_W), lambda b, j: (b, 0, 0)),
            pl.BlockSpec((1, POOL_BUF, POOL_W), lambda b, j: (b, 0, 0)),
        ],
        out_shape=[
            jax.ShapeDtypeStruct((rows, d), F32),
            jax.ShapeDtypeStruct((batch, CONV_BUF, CONV_W), F32),
            jax.ShapeDtypeStruct((batch, POOL_BUF, POOL_W), F32),
        ],
        scratch_shapes=[
            pltpu.VMEM((CONV_PAD + tt + SUBLANES, CONV_W), F32),
            pltpu.VMEM((SUBLANES + POOL_PAD + tt, POOL_W), F32),
            pltpu.VMEM((SUBLANES - 1, tt + CONV_PAD - SUBLANES, CONV_W), F32),
            pltpu.VMEM((CONV_K, SUBLANES, CONV_W), F32),
            pltpu.VMEM((3, SUBLANES + POOL_PAD + tt, POOL_W), F32),
            pltpu.VMEM((tt, d), BF16),
            pltpu.VMEM((tt, CONV_W + POOL_W), BF16),
            pltpu.VMEM((tt, POOL_W), BF16),
            pltpu.VMEM((tt, 2 * CONV_W), F32),
            pltpu.VMEM((tt, CONV_W), F32),
        ],
        compiler_params=pltpu.CompilerParams(
            dimension_semantics=("arbitrary", "arbitrary"), vmem_limit_bytes=VMEM_LIMIT_BYTES),
        name="odd_prompt",
    )(x, *operands)


def _odd_sample_body(x_ref, cbuf_ref, pbuf_ref, g_ref, win32_ref, cw_ref, cb_ref, lng_ref, lnb_ref, pw32_ref,
                     ps_ref, wout32_ref, xo_ref, conv_ref, pool_ref, win_ref, pw_ref, wout_ref):
    @pl.when(pl.program_id(0) == 0)
    def _():
        win_ref[...] = win32_ref[...].astype(BF16)
        pw_ref[...] = pw32_ref[...].astype(BF16)
        wout_ref[...] = wout32_ref[...].astype(BF16)

    x = x_ref[...]
    h = _rms(x, g_ref[...]).astype(BF16)
    a = _dot(h, win_ref[:, 0:CONV_W])
    gt = _dot(h, win_ref[:, CONV_W:2 * CONV_W])
    xp = _dot(h, win_ref[:, 2 * CONV_W:])
    glu = a * jax.nn.sigmoid(gt)
    cbuf = cbuf_ref[...]
    pbuf = pbuf_ref[...]
    conv = (jnp.sum(cbuf * cw_ref[0:CONV_BUF, :][None], axis=1)
            + glu * cw_ref[CONV_BUF:CONV_K, :] + cb_ref[...])
    out_c = _layernorm(conv, lng_ref[...], lnb_ref[...])
    out_c = out_c * jax.nn.sigmoid(out_c)
    outs = []
    for gi, win_len in enumerate(POOL_WINDOWS):
        lanes = slice(gi * POOL_DG, (gi + 1) * POOL_DG)
        tot = xp[:, lanes] + jnp.sum(pbuf[:, POOL_BUF - (win_len - 1):, lanes], axis=1)
        pooled = tot / float(win_len) - xp[:, lanes]
        outs.append(_dot(pooled.astype(BF16), pw_ref[gi]))
    out_d = jnp.concatenate(outs, axis=1) * ps_ref[...]
    y = _dot(jnp.concatenate([out_c, out_d], axis=1).astype(BF16), wout_ref[...])
    xo_ref[...] = x + y
    conv_ref[:, 0:CONV_BUF - 1, :] = cbuf_ref[:, 1:CONV_BUF, :]
    conv_ref[:, CONV_BUF - 1:CONV_BUF, :] = glu[:, None, :]
    pool_ref[:, 0:POOL_BUF - 1, :] = pbuf_ref[:, 1:POOL_BUF, :]
    pool_ref[:, POOL_BUF - 1:POOL_BUF, :] = xp[:, None, :]


def _odd_sample(x, conv_buf, pool_buf, g, od, *, block_rows=32):
    rows, d = x.shape
    bb = min(block_rows, rows)
    full = lambda a: pl.BlockSpec(a.shape, (lambda i: (0,) * a.ndim), pipeline_mode=pl.Buffered(1))
    whole = lambda shape: pl.BlockSpec(shape, (lambda i: (0,) * len(shape)))
    operands = (g, od["w_in"], od["conv_w"], od["conv_b"], od["ln_g"], od["ln_b"], od["pool_w"],
                od["pool_scale"], od["w_out"])
    w_shapes = [od["w_in"].shape, od["pool_w"].shape, od["w_out"].shape]
    return pl.pallas_call(
        _odd_sample_body,
        grid=(rows // bb,),
        in_specs=[pl.BlockSpec((bb, d), lambda i: (i, 0)),
                  pl.BlockSpec((bb, CONV_BUF, CONV_W), lambda i: (i, 0, 0)),
                  pl.BlockSpec((bb, POOL_BUF, POOL_W), lambda i: (i, 0, 0))]
                 + [full(a) for a in operands],
        out_specs=[
            pl.BlockSpec((bb, d), lambda i: (i, 0)),
            pl.BlockSpec((bb, CONV_BUF, CONV_W), lambda i: (i, 0, 0)),
            pl.BlockSpec((bb, POOL_BUF, POOL_W), lambda i: (i, 0, 0)),
        ] + [whole(s) for s in w_shapes],
        out_shape=[
            jax.ShapeDtypeStruct((rows, d), F32),
            jax.ShapeDtypeStruct(conv_buf.shape, F32),
            jax.ShapeDtypeStruct(pool_buf.shape, F32),
        ] + [jax.ShapeDtypeStruct(s, BF16) for s in w_shapes],
        compiler_params=pltpu.CompilerParams(
            dimension_semantics=("arbitrary",), vmem_limit_bytes=VMEM_LIMIT_BYTES),
        name="odd_sample",
    )(x, conv_buf, pool_buf, *operands)


def _prep_odd(w_in, conv_w, conv_b, ln_g, ln_b, pool_w, pool_scale, w_out):
    return dict(w_in=w_in, conv_w=conv_w, conv_b=conv_b.reshape(1, -1), ln_g=ln_g.reshape(1, -1),
                ln_b=ln_b.reshape(1, -1), pool_w=pool_w, pool_scale=pool_scale.reshape(1, -1), w_out=w_out)


def kernel(x_prompt, x_sample, state_gla, state_conv, state_pool, norm_g, ff_in, ff_out, ev_w_in, ev_w_gate, ev_b_gate, ev_gla_g, ev_sg_ln_g, ev_sg_ln_b, ev_sg_w, ev_sg_b, ev_w_out, od_w_in, od_conv_w, od_conv_b, od_ln_g, od_ln_b, od_pool_w, od_pool_scale, od_w_out, norm_f):
    bp, t, d = x_prompt.shape
    bs = x_sample.shape[0]
    depth = norm_g.shape[0]
    xp = x_prompt.reshape(bp * t, d)
    xs = x_sample.reshape(bs, d)
    gla_p, gla_s, sgv_p, sgv_s, conv_p, conv_s, pool_p, pool_s = [], [], [], [], [], [], [], []
    for layer in range(depth):
        i = layer // 2
        last = layer == depth - 1
        xs, w_a, w_b, w_o = _ffn_stream(xs, norm_g[layer, 0], ff_in, ff_out, layer, 0, norm_f, final_norm=False)
        xp = _ffn(xp, norm_g[layer, 0], w_a, w_b, w_o, norm_f, final_norm=False, block_rows=FFN_ROWS)
        g_mix = norm_g[layer, 1].reshape(1, d)
        if layer % 2 == 0:
            ev = _prep_even(ev_b_gate[i], ev_gla_g[i], ev_sg_ln_g[i], ev_sg_ln_b[i], ev_sg_w[i], ev_sg_b[i])
            xs, s_s, v_s, w_in_b, w_gate_b, w_out_b, sg_w_b = _even_sample(
                xs, state_gla[i], g_mix, ev_w_in[i], ev_w_gate[i], ev_w_out[i], ev_sg_w[i], ev)
            xp, s_p, v_p = _even_prompt(xp, bp, g_mix, w_in_b, w_gate_b, ev["b_gate"], ev["gla_g"],
                                        ev["ln_g"], ev["ln_b"], sg_w_b, ev["sg_bias"], w_out_b,
                                        block_rows=512)
            gla_p.append(s_p); gla_s.append(s_s); sgv_p.append(v_p); sgv_s.append(v_s.reshape(bs, 1, SG_W))
        else:
            od = _prep_odd(od_w_in[i], od_conv_w[i], od_conv_b[i], od_ln_g[i], od_ln_b[i], od_pool_w[i],
                           od_pool_scale[i], od_w_out[i])
            xs, c_s, p_s, w_in_b, pool_w_b, w_out_b = _odd_sample(xs, state_conv[i], state_pool[i], g_mix, od)
            xp, c_p, p_p = _odd_prompt(xp, bp, g_mix, dict(od, w_in=w_in_b, pool_w=pool_w_b, w_out=w_out_b),
                                       block_rows=256)
            conv_p.append(c_p); conv_s.append(c_s); pool_p.append(p_p); pool_s.append(p_s)
        xs, w_a, w_b, w_o = _ffn_stream(xs, norm_g[layer, 2], ff_in, ff_out, layer, 1, norm_f, final_norm=last)
        xp = _ffn(xp, norm_g[layer, 2], w_a, w_b, w_o, norm_f, final_norm=last, block_rows=FFN_ROWS)
    return (xp.reshape(bp, t, d), xs.reshape(bs, 1, d), jnp.stack(gla_p), jnp.stack(gla_s),
            jnp.stack(sgv_p), jnp.stack(sgv_s), jnp.stack(conv_p), jnp.stack(conv_s),
            jnp.stack(pool_p), jnp.stack(pool_s))
```

```python
import functools

import jax
import jax.numpy as jnp
import numpy as np
from jax import lax
from jax.experimental import pallas as pl
from jax.experimental.pallas import tpu as pltpu

EPS = 1e-6
LOG2_E = 1.4426950408889634
BF16 = jnp.bfloat16
F32 = jnp.float32

LANES = 128
SUBLANES = 8
VMEM_LIMIT_BYTES = 56 * 1024 * 1024
FFN_ROWS = 1024


def _rms(x, g):
    return x * lax.rsqrt(jnp.mean(x * x, axis=-1, keepdims=True) + EPS) * g


def _dot(a, b):
    return jnp.dot(a, b, preferred_element_type=F32)


FFN_CHUNK = 256


def _ffn_body(x_ref, g_ref, wa_ref, wb_ref, wout_ref, gf_ref, o_ref, act_ref, *, final_norm):
    d_ff = wout_ref.shape[0]
    x = x_ref[...]
    h = _rms(x, g_ref[...]).astype(BF16)
    for c in range(d_ff // FFN_CHUNK):
        cols = slice(c * FFN_CHUNK, (c + 1) * FFN_CHUNK)
        a = _dot(h, wa_ref[:, cols])
        b = _dot(h, wb_ref[:, cols])
        act_ref[:, cols] = (a * jax.nn.sigmoid(a) * b).astype(BF16)
    y = x + 0.5 * _dot(act_ref[...], wout_ref[...])
    if final_norm:
        y = _rms(y, gf_ref[...])
    o_ref[...] = y


def _ffn(x, g, w_a, w_b, w_out, g_final, *, final_norm, block_rows):
    rows, d = x.shape
    d_ff = w_out.shape[0]
    tm = min(block_rows, rows)
    body = functools.partial(_ffn_body, final_norm=final_norm)
    const = lambda i: (0, 0)
    return pl.pallas_call(
        body,
        grid=(rows // tm,),
        in_specs=[
            pl.BlockSpec((tm, d), lambda i: (i, 0)),
            pl.BlockSpec((1, d), const),
            pl.BlockSpec((d, d_ff), const, pipeline_mode=pl.Buffered(1)),
            pl.BlockSpec((d, d_ff), const, pipeline_mode=pl.Buffered(1)),
            pl.BlockSpec((d_ff, d), const, pipeline_mode=pl.Buffered(1)),
            pl.BlockSpec((1, d), const),
        ],
        out_specs=pl.BlockSpec((tm, d), lambda i: (i, 0)),
        out_shape=jax.ShapeDtypeStruct((rows, d), F32),
        scratch_shapes=[pltpu.VMEM((tm, d_ff), BF16)],
        compiler_params=pltpu.CompilerParams(
            dimension_semantics=("arbitrary",), vmem_limit_bytes=VMEM_LIMIT_BYTES),
        name="ffn",
    )(x, g.reshape(1, d), w_a, w_b, w_out, g_final.reshape(1, d))


def _ffn_stream_body(x_ref, g_ref, wa_ref, wb_ref, wout_ref, gf_ref,
                     o_ref, wa_o_ref, wb_o_ref, wout_o_ref, h_ref, acc_ref, *, final_norm):
    c = pl.program_id(0)

    @pl.when(c == 0)
    def _():
        h_ref[...] = _rms(x_ref[...], g_ref[...]).astype(BF16)
        acc_ref[...] = jnp.zeros_like(acc_ref)

    wa = wa_ref[...].astype(BF16)
    wb = wb_ref[...].astype(BF16)
    wout = wout_ref[...].astype(BF16)
    wa_o_ref[...] = wa
    wb_o_ref[...] = wb
    wout_o_ref[...] = wout
    h = h_ref[...]
    a = _dot(h, wa)
    b = _dot(h, wb)
    acc_ref[...] += _dot((a * jax.nn.sigmoid(a) * b).astype(BF16), wout)

    @pl.when(c == pl.num_programs(0) - 1)
    def _():
        y = x_ref[...] + 0.5 * acc_ref[...]
        if final_norm:
            y = _rms(y, gf_ref[...])
        o_ref[...] = y


def _ffn_stream(x, g, ff_in, ff_out, layer, slot, g_final, *, final_norm):
    rows, d = x.shape
    d_ff = ff_out.shape[2]
    n_chunks = d_ff // FFN_CHUNK
    const = lambda c: (0, 0)
    body = functools.partial(_ffn_stream_body, final_norm=final_norm)
    return pl.pallas_call(
        body,
        grid=(n_chunks,),
        in_specs=[
            pl.BlockSpec((rows, d), const),
            pl.BlockSpec((1, d), const),
            pl.BlockSpec((None, None, d, FFN_CHUNK), lambda c: (layer, slot, 0, c)),
            pl.BlockSpec((None, None, d, FFN_CHUNK), lambda c: (layer, slot, 0, n_chunks + c)),
            pl.BlockSpec((None, None, FFN_CHUNK, d), lambda c: (layer, slot, c, 0)),
            pl.BlockSpec((1, d), const),
        ],
        out_specs=[
            pl.BlockSpec((rows, d), const),
            pl.BlockSpec((d, FFN_CHUNK), lambda c: (0, c)),
            pl.BlockSpec((d, FFN_CHUNK), lambda c: (0, c)),
            pl.BlockSpec((FFN_CHUNK, d), lambda c: (c, 0)),
        ],
        out_shape=[
            jax.ShapeDtypeStruct((rows, d), F32),
            jax.ShapeDtypeStruct((d, d_ff), BF16),
            jax.ShapeDtypeStruct((d, d_ff), BF16),
            jax.ShapeDtypeStruct((d_ff, d), BF16),
        ],
        scratch_shapes=[pltpu.VMEM((rows, d), BF16), pltpu.VMEM((rows, d), F32)],
        compiler_params=pltpu.CompilerParams(
            dimension_semantics=("arbitrary",), vmem_limit_bytes=VMEM_LIMIT_BYTES),
        name="ffn_stream",
    )(x, g.reshape(1, d), ff_in, ff_in, ff_out, g_final.reshape(1, d))


GLA_H = 4
GLA_DK = 64
GLA_DV = 128
GLA_QK = GLA_H * GLA_DK
GLA_V = GLA_H * GLA_DV
GLA_RANK_PAD = 128
GLA_INV_TAU = 1.0 / 16.0
GLA_CH = 16
GLA_SEL_ROWS = 256
SG_H = 4
SG_DH = 128
SG_W = SG_H * SG_DH
SG_CHUNK = 128
EV_Q, EV_V, EV_R, EV_U, EV_VS, EV_Z, EV_END = 0, 512, 1024, 1536, 2048, 2560, 2688


def _log_sigmoid(x):
    return jnp.minimum(x, 0.0) - jnp.log(1.0 + jnp.exp(-jnp.abs(x)))


def _split3(x):
    hi = x.astype(BF16)
    r1 = x - hi.astype(F32)
    mid = r1.astype(BF16)
    lo = (r1 - mid.astype(F32)).astype(BF16)
    return hi, mid, lo


def _dot3(sel, parts):
    return _dot(sel, parts[0]) + _dot(sel, parts[1]) + _dot(sel, parts[2])


def _head_masks(width, per_head, dtype):
    lane = lax.broadcasted_iota(jnp.int32, (1, width), 1)
    return [jnp.where(lane // per_head == h, 1.0, 0.0).astype(dtype) for h in range(width // per_head)]


def _even_prompt_body(x_ref, g_ref, win_ref, wgate_ref, bgate_ref, glag_ref, lng_ref, lnb_ref,
                      sgw_ref, sgb_ref, e_ref, wout_ref,
                      xo_ref, gla_ref, sgv_ref,
                      st_ref, q_ref, k_ref, cum2_ref, qin_ref, kd_ref, dec_ref, v_ref, o_ref,
                      pcat_ref, acat_ref, mix_ref, add_ref, stb_ref, ruv_ref):
    j = pl.program_id(1)
    tt = x_ref.shape[0]
    n_chunks = tt // GLA_CH

    @pl.when(j == 0)
    def _():
        st_ref[...] = jnp.zeros_like(st_ref)

    x = x_ref[...]
    h = _rms(x, g_ref[...]).astype(BF16)

    qk = _dot(h, win_ref[:, EV_Q:EV_V])
    q = qk[:, :GLA_QK] * (GLA_DK ** -0.5)
    k = qk[:, GLA_QK:]
    v_ref[...] = _dot(h, win_ref[:, EV_V:EV_R]).astype(BF16)
    z = _dot(h, win_ref[:, EV_Z:EV_END]).astype(BF16)
    log_a = _log_sigmoid(_dot(z, wgate_ref[...]) + bgate_ref[...]) * GLA_INV_TAU

    sel_rows = min(tt, GLA_SEL_ROWS)
    row = lax.broadcasted_iota(jnp.int32, (sel_rows, sel_rows), 0)
    col = lax.broadcasted_iota(jnp.int32, (sel_rows, sel_rows), 1)
    same = (row // GLA_CH) == (col // GLA_CH)
    sel_cum = jnp.where(same & (col <= row), 1.0, 0.0).astype(BF16)
    sel_tot = jnp.where(same, 1.0, 0.0).astype(BF16)
    parts = _split3(log_a)
    groups_of_rows = [slice(r0, r0 + sel_rows) for r0 in range(0, tt, sel_rows)]
    cum = jnp.concatenate([_dot3(sel_cum, [p[rs] for p in parts]) for rs in groups_of_rows], axis=0)
    tot = jnp.concatenate([_dot3(sel_tot, [p[rs] for p in parts]) for rs in groups_of_rows], axis=0)
    q_ref[...] = q
    k_ref[...] = k
    cum2_ref[...] = cum * LOG2_E
    qin_ref[...] = (q * jnp.exp(cum)).astype(BF16)
    kd_ref[...] = (k * jnp.exp(tot - cum)).astype(BF16)
    dec_ref[...] = jnp.exp(tot)

    half = GLA_CH // 2
    proj_block = 256
    proj_cols = list(range(EV_R, EV_Z, proj_block))
    proj_every = n_chunks // len(proj_cols)
    assert proj_every >= 1
    for n in range(n_chunks):
        if n % proj_every == proj_every - 1 and n // proj_every < len(proj_cols):
            c0 = proj_cols[n // proj_every]
            ruv_ref[:, c0 - EV_R:c0 - EV_R + proj_block] = _dot(h, win_ref[:, c0:c0 + proj_block])
        base = n * GLA_CH
        qb = q_ref[base:base + GLA_CH, :]
        cb = cum2_ref[base:base + GLA_CH, :]
        for s in range(GLA_CH):
            ks = k_ref[base + s:base + s + 1, :]
            cs = cum2_ref[base + s:base + s + 1, :]
            if s < half:
                p = qb * ks * jnp.exp2(jnp.minimum(cb - cs, 0.0))
            else:
                p_hi = qb[half:] * ks * jnp.exp2(jnp.minimum(cb[half:] - cs, 0.0))
                p = jnp.concatenate([jnp.zeros_like(p_hi), p_hi], axis=0)
            pcat_ref[base:base + GLA_CH, s * GLA_QK:(s + 1) * GLA_QK] = p.astype(BF16)
    scores = _dot(pcat_ref[...], e_ref[...])

    u = jax.nn.gelu(ruv_ref[:, EV_U - EV_R:EV_VS - EV_R])
    v_ln = _layernorm(jax.nn.gelu(ruv_ref[:, EV_VS - EV_R:EV_Z - EV_R]), lng_ref[...], lnb_ref[...])
    v_lb = v_ln.astype(BF16)

    t_loc = lax.broadcasted_iota(jnp.int32, scores.shape, 0) % GLA_CH
    c_idx = lax.broadcasted_iota(jnp.int32, scores.shape, 1)
    causal = (c_idx % GLA_CH <= t_loc) & (c_idx < GLA_H * GLA_CH)
    acat_ref[...] = jnp.where(causal, scores, 0.0).astype(BF16)

    qk_masks = _head_masks(GLA_QK, GLA_DK, BF16)
    v_masks = _head_masks(GLA_V, GLA_DV, BF16)

    for n in range(n_chunks):
        rows = slice(n * GLA_CH, (n + 1) * GLA_CH)
        kn = kd_ref[rows, :]
        vn = v_ref[rows, :]
        lk = jnp.concatenate([kn * m for m in qk_masks], axis=0)
        vs = jnp.concatenate([vn[:, hh * GLA_DV:(hh + 1) * GLA_DV] for hh in range(GLA_H)], axis=0)
        add_ref[n] = lax.dot_general(vs, lk, (((0,), (0,)), ((), ())), preferred_element_type=F32)
    r = ruv_ref[:, 0:EV_U - EV_R]
    gate_r = r * jax.nn.sigmoid(r)
    sg_tasks = [(c, hh) for c in range(tt // SG_CHUNK) for hh in range(SG_H)]
    st = st_ref[...]
    for n in range(n_chunks):
        stb_ref[n] = st.astype(BF16)
        st = st * dec_ref[n * GLA_CH:n * GLA_CH + 1, :] + add_ref[n]
        if n < len(sg_tasks):
            c, hh = sg_tasks[n]
            mix_ref[c * SG_CHUNK:(c + 1) * SG_CHUNK, hh * SG_DH:(hh + 1) * SG_DH] = _dot(
                sgw_ref[hh], v_lb[c * SG_CHUNK:(c + 1) * SG_CHUNK, hh * SG_DH:(hh + 1) * SG_DH])
    st_ref[...] = st
    assert len(sg_tasks) <= n_chunks
    for n in range(n_chunks):
        rows = slice(n * GLA_CH, (n + 1) * GLA_CH)
        qn = qin_ref[rows, :]
        vn = v_ref[rows, :]
        lq = jnp.concatenate([qn * m for m in qk_masks], axis=0)
        oi = lax.dot_general(lq, stb_ref[n], (((1,), (1,)), ((), ())),
                             preferred_element_type=F32)
        o_inter = jnp.concatenate([oi[hh * GLA_CH:(hh + 1) * GLA_CH] for hh in range(GLA_H)], axis=1)
        vbd = jnp.concatenate([vn * m for m in v_masks]
                              + [jnp.zeros((acat_ref.shape[1] - GLA_H * GLA_CH, GLA_V), BF16)], axis=0)
        o_ref[rows, :] = o_inter + _dot(acat_ref[rows, :], vbd)

    o = o_ref[...]
    o_n = jnp.concatenate(
        [_rms(o[:, hh * GLA_DV:(hh + 1) * GLA_DV], 1.0) for hh in range(GLA_H)], axis=1) * glag_ref[...]
    out_a = o_n * gate_r
    bias = jnp.concatenate([sgb_ref[...]] * (tt // SG_CHUNK), axis=0)
    out_b = u * (mix_ref[...] + bias)

    y = _dot(jnp.concatenate([out_a, out_b], axis=1).astype(BF16), wout_ref[...])
    xo_ref[...] = x + y

    @pl.when(j == pl.num_programs(1) - 1)
    def _():
        sgv_ref[0] = v_ln[tt - SG_CHUNK:, :]
        gla_ref[0] = st_ref[...].T.reshape(GLA_H, GLA_DK, GLA_DV)


def _score_sum_matrix():
    r = np.arange(GLA_CH * GLA_QK)
    c = np.arange(LANES)
    s, hh = r // GLA_QK, (r % GLA_QK) // GLA_DK
    return jnp.asarray((c[None, :] == (hh * GLA_CH + s)[:, None]).astype(np.float32), dtype=BF16)


def _even_prompt(x, batch, g, w_in, w_gate, b_gate, gla_g, ln_g, ln_b, sg_w, sg_bias, w_out, *, block_rows):
    rows, d = x.shape
    t = rows // batch
    tt = min(block_rows, t)
    nj = t // tt
    const2 = lambda b, j: (0, 0)
    full = lambda a: pl.BlockSpec(a.shape, (lambda b, j: (0,) * a.ndim), pipeline_mode=pl.Buffered(1))
    e = _score_sum_matrix()
    operands = (g, w_in, w_gate, b_gate, gla_g, ln_g, ln_b, sg_w, sg_bias, e, w_out)
    return pl.pallas_call(
        _even_prompt_body,
        grid=(batch, nj),
        in_specs=[pl.BlockSpec((tt, d), lambda b, j: (b * nj + j, 0))] + [full(a) for a in operands],
        out_specs=[
            pl.BlockSpec((tt, d), lambda b, j: (b * nj + j, 0)),
            pl.BlockSpec((1, GLA_H, GLA_DK, GLA_DV), lambda b, j: (b, 0, 0, 0)),
            pl.BlockSpec((1, SG_CHUNK, SG_W), lambda b, j: (b, 0, 0)),
        ],
        out_shape=[
            jax.ShapeDtypeStruct((rows, d), F32),
            jax.ShapeDtypeStruct((batch, GLA_H, GLA_DK, GLA_DV), F32),
            jax.ShapeDtypeStruct((batch, SG_CHUNK, SG_W), F32),
        ],
        scratch_shapes=[
            pltpu.VMEM((GLA_DV, GLA_QK), F32),
            pltpu.VMEM((tt, GLA_QK), F32),
            pltpu.VMEM((tt, GLA_QK), F32),
            pltpu.VMEM((tt, GLA_QK), F32),
            pltpu.VMEM((tt, GLA_QK), BF16),
            pltpu.VMEM((tt, GLA_QK), BF16),
            pltpu.VMEM((tt, GLA_QK), F32),
            pltpu.VMEM((tt, GLA_V), BF16),
            pltpu.VMEM((tt, GLA_V), F32),
            pltpu.VMEM((tt, GLA_CH * GLA_QK), BF16),
            pltpu.VMEM((tt, LANES), BF16),
            pltpu.VMEM((tt, SG_W), F32),
            pltpu.VMEM((tt // GLA_CH, GLA_DV, GLA_QK), F32),
            pltpu.VMEM((tt // GLA_CH, GLA_DV, GLA_QK), BF16),
            pltpu.VMEM((tt, EV_Z - EV_R), F32),
        ],
        compiler_params=pltpu.CompilerParams(
            dimension_semantics=("arbitrary", "arbitrary"), vmem_limit_bytes=VMEM_LIMIT_BYTES),
        name="even_prompt",
    )(x, *operands)


def _prep_even(b_gate, gla_g, ln_g, ln_b, sg_w, sg_b):
    return dict(
        b_gate=b_gate.reshape(1, -1), b_gate_col=b_gate.reshape(-1, 1), gla_g=gla_g.reshape(1, -1),
        ln_g=ln_g.reshape(1, -1), ln_b=ln_b.reshape(1, -1),
        sg_bias=jnp.repeat(jnp.transpose(sg_b), SG_DH, axis=1),
        sg_w0=jnp.repeat(sg_w[:, 0, 0], SG_DH).reshape(1, -1), sg_b0=jnp.repeat(sg_b[:, 0], SG_DH).reshape(1, -1))


def _even_sample_body(x_ref, s_ref, g_ref, win32_ref, wgate32_ref, wout32_ref, sgw32_ref, bgate_ref,
                      bgatec_ref, glag_ref, lng_ref, lnb_ref, sgw0_ref, sgb0_ref,
                      xo_ref, so_ref, sgv_ref, win_ref, wgate_ref, wout_ref, sgw_ref,
                      o_ref, wqkt_ref, wzt_ref, wgatet_ref):
    bb = x_ref.shape[0]

    @pl.when(pl.program_id(0) == 0)
    def _():
        z0 = 2 * GLA_QK + 2 * GLA_V
        rank = wgate32_ref.shape[0]
        win_ref[:, 0:z0] = win32_ref[:, 0:z0].astype(BF16)
        win_ref[:, z0:EV_Z] = win32_ref[:, z0 + rank:].astype(BF16)
        z_tile = win32_ref[:, z0:z0 + GLA_RANK_PAD]
        lane = lax.broadcasted_iota(jnp.int32, z_tile.shape, 1)
        win_ref[:, EV_Z:EV_END] = jnp.where(lane < rank, z_tile, 0.0).astype(BF16)
        wgate_ref[...] = jnp.concatenate(
            [wgate32_ref[...].astype(BF16), jnp.zeros((GLA_RANK_PAD - rank, GLA_QK), BF16)], axis=0)
        wout_ref[...] = wout32_ref[...].astype(BF16)
        row = lax.broadcasted_iota(jnp.int32, (SG_CHUNK, SG_CHUNK), 0)
        col = lax.broadcasted_iota(jnp.int32, (SG_CHUNK, SG_CHUNK), 1)
        for hh in range(SG_H):
            sgw_ref[hh] = jnp.where(col <= row, sgw32_ref[hh], 0.0).astype(BF16)
        wqkt_ref[...] = win_ref[:, EV_Q:EV_V].T
        wzt_ref[...] = win_ref[:, EV_Z:EV_END].T
        wgatet_ref[...] = wgate_ref[...].T

    x = x_ref[...]
    h = _rms(x, g_ref[...]).astype(BF16)
    nt = (((1,), (1,)), ((), ()))
    qk_t = lax.dot_general(wqkt_ref[...], h, nt, preferred_element_type=F32)
    q_t = qk_t[:GLA_QK] * (GLA_DK ** -0.5)
    k_t = qk_t[GLA_QK:].astype(BF16).astype(F32)
    z_t = lax.dot_general(wzt_ref[...], h, nt, preferred_element_type=F32).astype(BF16)
    a_t = jnp.exp(_log_sigmoid(_dot(wgatet_ref[...], z_t) + bgatec_ref[...]) * GLA_INV_TAU)
    v = _dot(h, win_ref[:, EV_V:EV_R])
    v_r = v.astype(BF16).astype(F32)
    for b in range(bb):
        s_old = s_ref[b].reshape(GLA_QK, GLA_DV)
        a_c = jnp.broadcast_to(a_t[:, b:b + 1], (GLA_QK, GLA_DV))
        k_c = jnp.broadcast_to(k_t[:, b:b + 1], (GLA_QK, GLA_DV))
        q_c = jnp.broadcast_to(q_t[:, b:b + 1], (GLA_QK, GLA_DV))
        v_rows = jnp.concatenate(
            [jnp.broadcast_to(v_r[b:b + 1, hh * GLA_DV:(hh + 1) * GLA_DV], (GLA_DK, GLA_DV))
             for hh in range(GLA_H)], axis=0)
        s_new = a_c * s_old + k_c * v_rows
        so_ref[b] = s_new.reshape(GLA_H, GLA_DK, GLA_DV)
        ob = jnp.sum((q_c * s_new).reshape(GLA_H, GLA_DK, GLA_DV), axis=1)
        o_ref[b:b + 1, :] = jnp.concatenate([ob[hh:hh + 1] for hh in range(GLA_H)], axis=1)
    o = o_ref[...]
    o_n = jnp.concatenate(
        [_rms(o[:, hh * GLA_DV:(hh + 1) * GLA_DV], 1.0) for hh in range(GLA_H)], axis=1) * glag_ref[...]
    r = _dot(h, win_ref[:, EV_R:EV_U])
    out_a = o_n * (r * jax.nn.sigmoid(r))
    u = jax.nn.gelu(_dot(h, win_ref[:, EV_U:EV_VS]))
    vv = jax.nn.gelu(_dot(h, win_ref[:, EV_VS:EV_Z]))
    mu = jnp.mean(vv, axis=-1, keepdims=True)
    xc = vv - mu
    var = jnp.mean(xc * xc, axis=-1, keepdims=True)
    v_ln = xc * lax.rsqrt(var + EPS) * lng_ref[...] + lnb_ref[...]
    sgv_ref[...] = v_ln
    out_b = u * (sgw0_ref[...] * v_ln + sgb0_ref[...])
    y = _dot(jnp.concatenate([out_a, out_b], axis=1).astype(BF16), wout_ref[...])
    xo_ref[...] = x + y


def _even_sample(x, state, g, w_in, w_gate, w_out, sg_w, ev, *, block_rows=32):
    rows, d = x.shape
    bb = min(block_rows, rows)
    full = lambda a: pl.BlockSpec(a.shape, (lambda i: (0,) * a.ndim), pipeline_mode=pl.Buffered(1))
    whole = lambda shape: pl.BlockSpec(shape, (lambda i: (0,) * len(shape)))
    operands = (g, w_in, w_gate, w_out, sg_w, ev["b_gate"], ev["b_gate_col"], ev["gla_g"], ev["ln_g"],
                ev["ln_b"], ev["sg_w0"], ev["sg_b0"])
    w_shapes = [(d, EV_END), (GLA_RANK_PAD, GLA_QK), w_out.shape, sg_w.shape]
    return pl.pallas_call(
        _even_sample_body,
        grid=(rows // bb,),
        in_specs=[pl.BlockSpec((bb, d), lambda i: (i, 0)),
                  pl.BlockSpec((bb, GLA_H, GLA_DK, GLA_DV), lambda i: (i, 0, 0, 0))]
                 + [full(a) for a in operands],
        out_specs=[
            pl.BlockSpec((bb, d), lambda i: (i, 0)),
            pl.BlockSpec((bb, GLA_H, GLA_DK, GLA_DV), lambda i: (i, 0, 0, 0)),
            pl.BlockSpec((bb, SG_W), lambda i: (i, 0)),
        ] + [whole(s) for s in w_shapes],
        out_shape=[
            jax.ShapeDtypeStruct((rows, d), F32),
            jax.ShapeDtypeStruct(state.shape, F32),
            jax.ShapeDtypeStruct((rows, SG_W), F32),
        ] + [jax.ShapeDtypeStruct(s, BF16) for s in w_shapes],
        scratch_shapes=[
            pltpu.VMEM((bb, GLA_V), F32),
            pltpu.VMEM((2 * GLA_QK, d), BF16),
            pltpu.VMEM((GLA_RANK_PAD, d), BF16),
            pltpu.VMEM((GLA_QK, GLA_RANK_PAD), BF16),
        ],
        compiler_params=pltpu.CompilerParams(
            dimension_semantics=("arbitrary",), vmem_limit_bytes=VMEM_LIMIT_BYTES),
        name="even_sample",
    )(x, state, *operands)


CONV_W = 512
CONV_K = 31
CONV_BUF = CONV_K - 1
CONV_PAD = 32
POOL_W = 512
POOL_WINDOWS = (2, 4, 8, 16)
POOL_DG = POOL_W // len(POOL_WINDOWS)
POOL_BUF = 15
POOL_PAD = 16
CONV_ROWS = 32
ODD_SUB = 128
ODD_COLS = 256


def _layernorm(x, g, b):
    mu = jnp.mean(x, axis=-1, keepdims=True)
    xc = x - mu
    var = jnp.mean(xc * xc, axis=-1, keepdims=True)
    return xc * lax.rsqrt(var + EPS) * g + b


def _odd_prompt_body(x_ref, g_ref, win_ref, cw_ref, cb_ref, lng_ref, lnb_ref, pw_ref, ps_ref, wout_ref,
                     xo_ref, conv_ref, pool_ref, gbuf_ref, pbuf_ref, shift_ref, cwb_ref, psum_ref,
                     h_ref, mixin_ref, pooled_ref, raw_ref, convo_ref):
    j = pl.program_id(1)
    tt = x_ref.shape[0]
    assert POOL_WINDOWS == (2, 4, 8, 16)

    p0 = SUBLANES + POOL_PAD

    @pl.when(j == 0)
    def _():
        gbuf_ref[0:CONV_PAD, :] = jnp.zeros((CONV_PAD, CONV_W), F32)
        pbuf_ref[0:p0, :] = jnp.zeros((p0, POOL_W), F32)
        psum_ref[:, 0:SUBLANES, :] = jnp.zeros((psum_ref.shape[0], SUBLANES, POOL_W), F32)
        for o in range(CONV_K):
            cwb_ref[o] = jnp.broadcast_to(cw_ref[o:o + 1, :], (SUBLANES, CONV_W))

    h_ref[...] = _rms(x_ref[...], g_ref[...]).astype(BF16)
    first = CONV_PAD - CONV_BUF
    groups = CONV_ROWS // SUBLANES
    n_sub = tt // ODD_SUB

    def project(k):
        rows = slice(k * ODD_SUB, (k + 1) * ODD_SUB)

        def piece(c0):
            def run():
                raw_ref[rows, c0:c0 + ODD_COLS] = _dot(h_ref[rows, :], win_ref[:, c0:c0 + ODD_COLS])
            return run

        return [piece(c0) for c0 in range(0, 2 * CONV_W, ODD_COLS)]

    def glu(k):
        rows = slice(k * ODD_SUB, (k + 1) * ODD_SUB)
        gbuf_ref[CONV_PAD + k * ODD_SUB:CONV_PAD + (k + 1) * ODD_SUB, :] = (
            raw_ref[rows, 0:CONV_W] * jax.nn.sigmoid(raw_ref[rows, CONV_W:2 * CONV_W]))

    def mix(k):
        r_lo = k * ODD_SUB

        def shift_copies():
            lo = 0 if k == 0 else r_lo + CONV_PAD - SUBLANES
            hi = r_lo + ODD_SUB + CONV_PAD - SUBLANES
            for rr in range(1, SUBLANES):
                shift_ref[rr - 1, lo:hi, :] = gbuf_ref[rr + lo:rr + hi, :]

        def conv_block(r0):
            def run():
                accs = [jnp.zeros((SUBLANES, CONV_W), F32) + cb_ref[...] for _ in range(groups)]
                for o in range(first, first + CONV_K):
                    rr = o % SUBLANES
                    w8 = cwb_ref[o - first]
                    for gq in range(groups):
                        lo = o - rr + r0 + gq * SUBLANES
                        src = (gbuf_ref[lo:lo + SUBLANES, :] if rr == 0
                               else shift_ref[rr - 1, lo:lo + SUBLANES, :])
                        accs[gq] = accs[gq] + src * w8
                for gq in range(groups):
                    convo_ref[r0 + gq * SUBLANES:r0 + (gq + 1) * SUBLANES, :] = accs[gq]
            return run

        def norm_act():
            out_c = _layernorm(convo_ref[r_lo:r_lo + ODD_SUB, :], lng_ref[...], lnb_ref[...])
            mixin_ref[r_lo:r_lo + ODD_SUB, 0:CONV_W] = (out_c * jax.nn.sigmoid(out_c)).astype(BF16)

        def pooling():
            lo = SUBLANES if k == 0 else p0 + r_lo
            hi = p0 + r_lo + ODD_SUB
            psum_ref[0, lo:hi, :] = pbuf_ref[lo:hi, :] + pbuf_ref[lo - 1:hi - 1, :]
            psum_ref[1, lo:hi, POOL_DG:] = psum_ref[0, lo:hi, POOL_DG:] + psum_ref[0, lo - 2:hi - 2, POOL_DG:]
            psum_ref[2, lo:hi, 2 * POOL_DG:] = (psum_ref[1, lo:hi, 2 * POOL_DG:]
                                                + psum_ref[1, lo - 4:hi - 4, 2 * POOL_DG:])
            o_lo = p0 + r_lo
            o_hi = o_lo + ODD_SUB
            tots = [psum_ref[0, o_lo:o_hi, 0:POOL_DG],
                    psum_ref[1, o_lo:o_hi, POOL_DG:2 * POOL_DG],
                    psum_ref[2, o_lo:o_hi, 2 * POOL_DG:3 * POOL_DG],
                    psum_ref[2, o_lo:o_hi, 3 * POOL_DG:] + psum_ref[2, o_lo - 8:o_hi - 8, 3 * POOL_DG:]]
            t_glob = j * tt + r_lo + lax.broadcasted_iota(jnp.int32, (ODD_SUB, 1), 0)
            for gi, win_len in enumerate(POOL_WINDOWS):
                lanes = slice(gi * POOL_DG, (gi + 1) * POOL_DG)
                cnt = jnp.minimum(win_len, t_glob + 1).astype(F32)
                pooled = tots[gi] / cnt - pbuf_ref[o_lo:o_hi, lanes]
                pooled_ref[r_lo:r_lo + ODD_SUB, lanes] = pooled.astype(BF16)

        return ([shift_copies] + [conv_block(r0) for r0 in range(r_lo, r_lo + ODD_SUB, CONV_ROWS)]
                + [norm_act, pooling])

    def output(k):
        rows = slice(k * ODD_SUB, (k + 1) * ODD_SUB)

        def pool_matmuls():
            outs = [_dot(pooled_ref[rows, gi * POOL_DG:(gi + 1) * POOL_DG], pw_ref[gi])
                    for gi in range(len(POOL_WINDOWS))]
            mixin_ref[rows, CONV_W:] = (jnp.concatenate(outs, axis=1) * ps_ref[...]).astype(BF16)

        def out_piece(c0):
            def run():
                xo_ref[rows, c0:c0 + ODD_COLS] = x_ref[rows, c0:c0 + ODD_COLS] + _dot(
                    mixin_ref[rows, :], wout_ref[:, c0:c0 + ODD_COLS])
            return run

        return [pool_matmuls] + [out_piece(c0) for c0 in range(0, xo_ref.shape[1], ODD_COLS)]

    pbuf_ref[p0:p0 + tt, :] = _dot(h_ref[...], win_ref[:, 2 * CONV_W:])
    for task in project(0):
        task()
    glu(0)
    for k in range(n_sub):
        valu_tasks = mix(k)
        mxu_tasks = output(k - 1) if k >= 1 else []
        if k + 1 < n_sub:
            mxu_tasks = project(k + 1) + mxu_tasks
            valu_tasks = valu_tasks + [functools.partial(glu, k + 1)]
        for idx, task in enumerate(valu_tasks):
            lo_m = idx * len(mxu_tasks) // len(valu_tasks)
            hi_m = (idx + 1) * len(mxu_tasks) // len(valu_tasks)
            for m_task in mxu_tasks[lo_m:hi_m]:
                m_task()
            task()
    for task in output(n_sub - 1):
        task()

    tail_g = gbuf_ref[tt:tt + CONV_PAD, :]
    tail_p = pbuf_ref[SUBLANES + tt:p0 + tt, :]
    gbuf_ref[0:CONV_PAD, :] = tail_g
    pbuf_ref[SUBLANES:p0, :] = tail_p

    @pl.when(j == pl.num_programs(1) - 1)
    def _():
        conv_ref[0] = tail_g[CONV_PAD - CONV_BUF:, :]
        pool_ref[0] = tail_p[POOL_PAD - POOL_BUF:, :]


def _odd_prompt(x, batch, g, od, *, block_rows):
    rows, d = x.shape
    t = rows // batch
    tt = min(block_rows, t)
    nj = t // tt
    full = lambda a: pl.BlockSpec(a.shape, (lambda b, j: (0,) * a.ndim), pipeline_mode=pl.Buffered(1))
    operands = (g, od["w_in"], od["conv_w"], od["conv_b"], od["ln_g"], od["ln_b"], od["pool_w"],
                od["pool_scale"], od["w_out"])
    return pl.pallas_call(
        _odd_prompt_body,
        grid=(batch, nj),
        in_specs=[pl.BlockSpec((tt, d), lambda b, j: (b * nj + j, 0))] + [full(a) for a in operands],
        out_specs=[
            pl.BlockSpec((tt, d), lambda b, j: (b * nj + j, 0)),
            pl.BlockSpec((1, CONV_BUF, CONV_W), lambda b, j: (b, 0, 0)),
            pl.BlockSpec((1, POOL_BUF, POOL_W), lambda b, j: (b, 0, 0)),
        ],
        out_shape=[
            jax.ShapeDtypeStruct((rows, d), F32),
            jax.ShapeDtypeStruct((batch, CONV_BUF, CONV_W), F32),
            jax.ShapeDtypeStruct((batch, POOL_BUF, POOL_W), F32),
        ],
        scratch_shapes=[
            pltpu.VMEM((CONV_PAD + tt + SUBLANES, CONV_W), F32),
            pltpu.VMEM((SUBLANES + POOL_PAD + tt, POOL_W), F32),
            pltpu.VMEM((SUBLANES - 1, tt + CONV_PAD - SUBLANES, CONV_W), F32),
            pltpu.VMEM((CONV_K, SUBLANES, CONV_W), F32),
            pltpu.VMEM((3, SUBLANES + POOL_PAD + tt, POOL_W), F32),
            pltpu.VMEM((tt, d), BF16),
            pltpu.VMEM((tt, CONV_W + POOL_W), BF16),
            pltpu.VMEM((tt, POOL_W), BF16),
            pltpu.VMEM((tt, 2 * CONV_W), F32),
            pltpu.VMEM((tt, CONV_W), F32),
        ],
        compiler_params=pltpu.CompilerParams(
            dimension_semantics=("arbitrary", "arbitrary"), vmem_limit_bytes=VMEM_LIMIT_BYTES),
        name="odd_prompt",
    )(x, *operands)


def _odd_sample_body(x_ref, cbuf_ref, pbuf_ref, g_ref, win32_ref, cw_ref, cb_ref, lng_ref, lnb_ref, pw32_ref,
                     ps_ref, wout32_ref, xo_ref, conv_ref, pool_ref, win_ref, pw_ref, wout_ref):
    @pl.when(pl.program_id(0) == 0)
    def _():
        win_ref[...] = win32_ref[...].astype(BF16)
        pw_ref[...] = pw32_ref[...].astype(BF16)
        wout_ref[...] = wout32_ref[...].astype(BF16)

    x = x_ref[...]
    h = _rms(x, g_ref[...]).astype(BF16)
    a = _dot(h, win_ref[:, 0:CONV_W])
    gt = _dot(h, win_ref[:, CONV_W:2 * CONV_W])
    xp = _dot(h, win_ref[:, 2 * CONV_W:])
    glu = a * jax.nn.sigmoid(gt)
    conv = glu * cw_ref[CONV_BUF:CONV_K, :] + cb_ref[...]
    for jj in range(CONV_BUF):
        conv = conv + cbuf_ref[jj] * cw_ref[jj:jj + 1, :]
    out_c = _layernorm(conv, lng_ref[...], lnb_ref[...])
    out_c = out_c * jax.nn.sigmoid(out_c)
    outs = []
    for gi, win_len in enumerate(POOL_WINDOWS):
        lanes = slice(gi * POOL_DG, (gi + 1) * POOL_DG)
        tot = xp[:, lanes]
        for jj in range(POOL_BUF - (win_len - 1), POOL_BUF):
            tot = tot + pbuf_ref[jj, :, lanes]
        pooled = tot / float(win_len) - xp[:, lanes]
        outs.append(_dot(pooled.astype(BF16), pw_ref[gi]))
    out_d = jnp.concatenate(outs, axis=1) * ps_ref[...]
    y = _dot(jnp.concatenate([out_c, out_d], axis=1).astype(BF16), wout_ref[...])
    xo_ref[...] = x + y
    conv_ref[0:CONV_BUF - 1] = cbuf_ref[1:CONV_BUF]
    conv_ref[CONV_BUF - 1] = glu
    pool_ref[0:POOL_BUF - 1] = pbuf_ref[1:POOL_BUF]
    pool_ref[POOL_BUF - 1] = xp


def _odd_sample(x, conv_buf, pool_buf, g, od, *, block_rows=32):
    rows, d = x.shape
    bb = min(block_rows, rows)
    full = lambda a: pl.BlockSpec(a.shape, (lambda i: (0,) * a.ndim), pipeline_mode=pl.Buffered(1))
    whole = lambda shape: pl.BlockSpec(shape, (lambda i: (0,) * len(shape)))
    operands = (g, od["w_in"], od["conv_w"], od["conv_b"], od["ln_g"], od["ln_b"], od["pool_w"],
                od["pool_scale"], od["w_out"])
    w_shapes = [od["w_in"].shape, od["pool_w"].shape, od["w_out"].shape]
    return pl.pallas_call(
        _odd_sample_body,
        grid=(rows // bb,),
        in_specs=[pl.BlockSpec((bb, d), lambda i: (i, 0)),
                  pl.BlockSpec((CONV_BUF, bb, CONV_W), lambda i: (0, i, 0)),
                  pl.BlockSpec((POOL_BUF, bb, POOL_W), lambda i: (0, i, 0))]
                 + [full(a) for a in operands],
        out_specs=[
            pl.BlockSpec((bb, d), lambda i: (i, 0)),
            pl.BlockSpec((CONV_BUF, bb, CONV_W), lambda i: (0, i, 0)),
            pl.BlockSpec((POOL_BUF, bb, POOL_W), lambda i: (0, i, 0)),
        ] + [whole(s) for s in w_shapes],
        out_shape=[
            jax.ShapeDtypeStruct((rows, d), F32),
            jax.ShapeDtypeStruct(conv_buf.shape, F32),
            jax.ShapeDtypeStruct(pool_buf.shape, F32),
        ] + [jax.ShapeDtypeStruct(s, BF16) for s in w_shapes],
        compiler_params=pltpu.CompilerParams(
            dimension_semantics=("arbitrary",), vmem_limit_bytes=VMEM_LIMIT_BYTES),
        name="odd_sample",
    )(x, conv_buf, pool_buf, *operands)


def _prep_odd(w_in, conv_w, conv_b, ln_g, ln_b, pool_w, pool_scale, w_out):
    return dict(w_in=w_in, conv_w=conv_w, conv_b=conv_b.reshape(1, -1), ln_g=ln_g.reshape(1, -1),
                ln_b=ln_b.reshape(1, -1), pool_w=pool_w, pool_scale=pool_scale.reshape(1, -1), w_out=w_out)


def kernel(x_prompt, x_sample, state_gla, state_conv, state_pool, norm_g, ff_in, ff_out, ev_w_in, ev_w_gate, ev_b_gate, ev_gla_g, ev_sg_ln_g, ev_sg_ln_b, ev_sg_w, ev_sg_b, ev_w_out, od_w_in, od_conv_w, od_conv_b, od_ln_g, od_ln_b, od_pool_w, od_pool_scale, od_w_out, norm_f):
    bp, t, d = x_prompt.shape
    bs = x_sample.shape[0]
    depth = norm_g.shape[0]
    xp = x_prompt.reshape(bp * t, d)
    xs = x_sample.reshape(bs, d)
    gla_p, gla_s, sgv_p, sgv_s, conv_p, conv_s, pool_p, pool_s = [], [], [], [], [], [], [], []
    for layer in range(depth):
        i = layer // 2
        last = layer == depth - 1
        xs, w_a, w_b, w_o = _ffn_stream(xs, norm_g[layer, 0], ff_in, ff_out, layer, 0, norm_f, final_norm=False)
        xp = _ffn(xp, norm_g[layer, 0], w_a, w_b, w_o, norm_f, final_norm=False, block_rows=FFN_ROWS)
        g_mix = norm_g[layer, 1].reshape(1, d)
        if layer % 2 == 0:
            ev = _prep_even(ev_b_gate[i], ev_gla_g[i], ev_sg_ln_g[i], ev_sg_ln_b[i], ev_sg_w[i], ev_sg_b[i])
            xs, s_s, v_s, w_in_b, w_gate_b, w_out_b, sg_w_b = _even_sample(
                xs, state_gla[i], g_mix, ev_w_in[i], ev_w_gate[i], ev_w_out[i], ev_sg_w[i], ev)
            xp, s_p, v_p = _even_prompt(xp, bp, g_mix, w_in_b, w_gate_b, ev["b_gate"], ev["gla_g"],
                                        ev["ln_g"], ev["ln_b"], sg_w_b, ev["sg_bias"], w_out_b,
                                        block_rows=512)
            gla_p.append(s_p); gla_s.append(s_s); sgv_p.append(v_p); sgv_s.append(v_s.reshape(bs, 1, SG_W))
        else:
            od = _prep_odd(od_w_in[i], od_conv_w[i], od_conv_b[i], od_ln_g[i], od_ln_b[i], od_pool_w[i],
                           od_pool_scale[i], od_w_out[i])
            xs, c_s, p_s, w_in_b, pool_w_b, w_out_b = _odd_sample(
                xs, jnp.transpose(state_conv[i], (1, 0, 2)), jnp.transpose(state_pool[i], (1, 0, 2)), g_mix, od)
            c_s, p_s = jnp.transpose(c_s, (1, 0, 2)), jnp.transpose(p_s, (1, 0, 2))
            xp, c_p, p_p = _odd_prompt(xp, bp, g_mix, dict(od, w_in=w_in_b, pool_w=pool_w_b, w_out=w_out_b),
                                       block_rows=256)
            conv_p.append(c_p); conv_s.append(c_s); pool_p.append(p_p); pool_s.append(p_s)
        xs, w_a, w_b, w_o = _ffn_stream(xs, norm_g[layer, 2], ff_in, ff_out, layer, 1, norm_f, final_norm=last)
        xp = _ffn(xp, norm_g[layer, 2], w_a, w_b, w_o, norm_f, final_norm=last, block_rows=FFN_ROWS)
    return (xp.reshape(bp, t, d), xs.reshape(bs, 1, d), jnp.stack(gla_p), jnp.stack(gla_s),
            jnp.stack(sgv_p), jnp.stack(sgv_s), jnp.stack(conv_p), jnp.stack(conv_s),
            jnp.stack(pool_p), jnp.stack(pool_s))
```

```python
import functools

import jax
import jax.numpy as jnp
import numpy as np
from jax import lax
from jax.experimental import pallas as pl
from jax.experimental.pallas import tpu as pltpu

EPS = 1e-6
LOG2_E = 1.4426950408889634
BF16 = jnp.bfloat16
F32 = jnp.float32

LANES = 128
SUBLANES = 8
VMEM_LIMIT_BYTES = 56 * 1024 * 1024
FFN_ROWS = 512


def _rms(x, g):
    return x * lax.rsqrt(jnp.mean(x * x, axis=-1, keepdims=True) + EPS) * g


def _dot(a, b):
    return jnp.dot(a, b, preferred_element_type=F32)


FFN_CHUNK = 256


def _ffn_body(x_ref, g_ref, wa_ref, wb_ref, wout_ref, gf_ref, o_ref, act_ref, *, final_norm):
    d_ff = wout_ref.shape[0]
    x = x_ref[...]
    h = _rms(x, g_ref[...]).astype(BF16)
    for c in range(d_ff // FFN_CHUNK):
        cols = slice(c * FFN_CHUNK, (c + 1) * FFN_CHUNK)
        a = _dot(h, wa_ref[:, cols])
        b = _dot(h, wb_ref[:, cols])
        act_ref[:, cols] = (a * jax.nn.sigmoid(a) * b).astype(BF16)
    y = x + 0.5 * _dot(act_ref[...], wout_ref[...])
    if final_norm:
        y = _rms(y, gf_ref[...])
    o_ref[...] = y


def _ffn(x, g, w_a, w_b, w_out, g_final, *, final_norm, block_rows):
    rows, d = x.shape
    d_ff = w_out.shape[0]
    tm = min(block_rows, rows)
    body = functools.partial(_ffn_body, final_norm=final_norm)
    const = lambda i: (0, 0)
    return pl.pallas_call(
        body,
        grid=(rows // tm,),
        in_specs=[
            pl.BlockSpec((tm, d), lambda i: (i, 0)),
            pl.BlockSpec((1, d), const),
            pl.BlockSpec((d, d_ff), const, pipeline_mode=pl.Buffered(1)),
            pl.BlockSpec((d, d_ff), const, pipeline_mode=pl.Buffered(1)),
            pl.BlockSpec((d_ff, d), const, pipeline_mode=pl.Buffered(1)),
            pl.BlockSpec((1, d), const),
        ],
        out_specs=pl.BlockSpec((tm, d), lambda i: (i, 0)),
        out_shape=jax.ShapeDtypeStruct((rows, d), F32),
        scratch_shapes=[pltpu.VMEM((tm, d_ff), BF16)],
        compiler_params=pltpu.CompilerParams(
            dimension_semantics=("arbitrary",), vmem_limit_bytes=VMEM_LIMIT_BYTES),
        name="ffn",
    )(x, g.reshape(1, d), w_a, w_b, w_out, g_final.reshape(1, d))


def _ffn_stream_body(x_ref, g_ref, wa_ref, wb_ref, wout_ref, gf_ref,
                     o_ref, wa_o_ref, wb_o_ref, wout_o_ref, h_ref, acc_ref, *, final_norm):
    c = pl.program_id(0)

    @pl.when(c == 0)
    def _():
        h_ref[...] = _rms(x_ref[...], g_ref[...]).astype(BF16)
        acc_ref[...] = jnp.zeros_like(acc_ref)

    wa = wa_ref[...].astype(BF16)
    wb = wb_ref[...].astype(BF16)
    wout = wout_ref[...].astype(BF16)
    wa_o_ref[...] = wa
    wb_o_ref[...] = wb
    wout_o_ref[...] = wout
    h = h_ref[...]
    a = _dot(h, wa)
    b = _dot(h, wb)
    acc_ref[...] += _dot((a * jax.nn.sigmoid(a) * b).astype(BF16), wout)

    @pl.when(c == pl.num_programs(0) - 1)
    def _():
        y = x_ref[...] + 0.5 * acc_ref[...]
        if final_norm:
            y = _rms(y, gf_ref[...])
        o_ref[...] = y


def _ffn_stream(x, g, ff_in, ff_out, layer, slot, g_final, *, final_norm):
    rows, d = x.shape
    d_ff = ff_out.shape[2]
    n_chunks = d_ff // FFN_CHUNK
    const = lambda c: (0, 0)
    body = functools.partial(_ffn_stream_body, final_norm=final_norm)
    return pl.pallas_call(
        body,
        grid=(n_chunks,),
        in_specs=[
            pl.BlockSpec((rows, d), const),
            pl.BlockSpec((1, d), const),
            pl.BlockSpec((None, None, d, FFN_CHUNK), lambda c: (layer, slot, 0, c)),
            pl.BlockSpec((None, None, d, FFN_CHUNK), lambda c: (layer, slot, 0, n_chunks + c)),
            pl.BlockSpec((None, None, FFN_CHUNK, d), lambda c: (layer, slot, c, 0)),
            pl.BlockSpec((1, d), const),
        ],
        out_specs=[
            pl.BlockSpec((rows, d), const),
            pl.BlockSpec((d, FFN_CHUNK), lambda c: (0, c)),
            pl.BlockSpec((d, FFN_CHUNK), lambda c: (0, c)),
            pl.BlockSpec((FFN_CHUNK, d), lambda c: (c, 0)),
        ],
        out_shape=[
            jax.ShapeDtypeStruct((rows, d), F32),
            jax.ShapeDtypeStruct((d, d_ff), BF16),
            jax.ShapeDtypeStruct((d, d_ff), BF16),
            jax.ShapeDtypeStruct((d_ff, d), BF16),
        ],
        scratch_shapes=[pltpu.VMEM((rows, d), BF16), pltpu.VMEM((rows, d), F32)],
        compiler_params=pltpu.CompilerParams(
            dimension_semantics=("arbitrary",), vmem_limit_bytes=VMEM_LIMIT_BYTES),
        name="ffn_stream",
    )(x, g.reshape(1, d), ff_in, ff_in, ff_out, g_final.reshape(1, d))


def _ffn_fused_body(xs_ref, xp_ref, g_ref, wa32_ref, wb32_ref, wo32_ref, gf_ref,
                    ys_ref, yp_ref, wa_ref, wb_ref, wo_ref, hs_ref, accs_ref, act_ref, *, final_norm):
    s = pl.program_id(0)
    n_chunks = wa_ref.shape[0]

    def finish(x, acc):
        y = x + 0.5 * acc
        return _rms(y, gf_ref[...]) if final_norm else y

    @pl.when(s < n_chunks)
    def _():
        @pl.when(s == 0)
        def _():
            hs_ref[...] = _rms(xs_ref[...], g_ref[...]).astype(BF16)
            accs_ref[...] = jnp.zeros_like(accs_ref)

        wa = wa32_ref[...].astype(BF16)
        wb = wb32_ref[...].astype(BF16)
        wo = wo32_ref[...].astype(BF16)
        wa_ref[s] = wa
        wb_ref[s] = wb
        wo_ref[s] = wo
        h = hs_ref[...]
        a = _dot(h, wa)
        b = _dot(h, wb)
        accs_ref[...] += _dot((a * jax.nn.sigmoid(a) * b).astype(BF16), wo)

        @pl.when(s == n_chunks - 1)
        def _():
            ys_ref[...] = finish(xs_ref[...], accs_ref[...])

    @pl.when(s >= n_chunks)
    def _():
        x = xp_ref[...]
        h = _rms(x, g_ref[...]).astype(BF16)
        for c in range(n_chunks):
            a = _dot(h, wa_ref[c])
            b = _dot(h, wb_ref[c])
            act_ref[:, c * FFN_CHUNK:(c + 1) * FFN_CHUNK] = (a * jax.nn.sigmoid(a) * b).astype(BF16)
        w_out = wo_ref[...].reshape(n_chunks * FFN_CHUNK, wo_ref.shape[2])
        yp_ref[...] = finish(x, _dot(act_ref[...], w_out))


def _ffn_fused(x_sample, x_prompt, g, ff_in, ff_out, layer, slot, g_final, *, final_norm, block_rows):
    rows_s, d = x_sample.shape
    rows_p = x_prompt.shape[0]
    d_ff = ff_out.shape[2]
    n_chunks = d_ff // FFN_CHUNK
    tm = min(block_rows, rows_p)
    const = lambda s: (0, 0)
    chunk = lambda s: jnp.minimum(s, n_chunks - 1)
    tile = lambda s: (jnp.maximum(s - n_chunks, 0), 0)
    body = functools.partial(_ffn_fused_body, final_norm=final_norm)
    return pl.pallas_call(
        body,
        grid=(n_chunks + rows_p // tm,),
        in_specs=[
            pl.BlockSpec((rows_s, d), const),
            pl.BlockSpec((tm, d), tile),
            pl.BlockSpec((1, d), const),
            pl.BlockSpec((None, None, d, FFN_CHUNK), lambda s: (layer, slot, 0, chunk(s))),
            pl.BlockSpec((None, None, d, FFN_CHUNK), lambda s: (layer, slot, 0, n_chunks + chunk(s))),
            pl.BlockSpec((None, None, FFN_CHUNK, d), lambda s: (layer, slot, chunk(s), 0)),
            pl.BlockSpec((1, d), const),
        ],
        out_specs=[pl.BlockSpec((rows_s, d), const), pl.BlockSpec((tm, d), tile)],
        out_shape=[jax.ShapeDtypeStruct((rows_s, d), F32), jax.ShapeDtypeStruct((rows_p, d), F32)],
        scratch_shapes=[
            pltpu.VMEM((n_chunks, d, FFN_CHUNK), BF16),
            pltpu.VMEM((n_chunks, d, FFN_CHUNK), BF16),
            pltpu.VMEM((n_chunks, FFN_CHUNK, d), BF16),
            pltpu.VMEM((rows_s, d), BF16),
            pltpu.VMEM((rows_s, d), F32),
            pltpu.VMEM((tm, d_ff), BF16),
        ],
        compiler_params=pltpu.CompilerParams(
            dimension_semantics=("arbitrary",), vmem_limit_bytes=VMEM_LIMIT_BYTES),
        name="ffn_fused",
    )(x_sample, x_prompt, g.reshape(1, d), ff_in, ff_in, ff_out, g_final.reshape(1, d))


GLA_H = 4
GLA_DK = 64
GLA_DV = 128
GLA_QK = GLA_H * GLA_DK
GLA_V = GLA_H * GLA_DV
GLA_RANK_PAD = 128
GLA_INV_TAU = 1.0 / 16.0
GLA_CH = 16
GLA_SEL_ROWS = 256
SG_H = 4
SG_DH = 128
SG_W = SG_H * SG_DH
SG_CHUNK = 128
EV_Q, EV_V, EV_R, EV_U, EV_VS, EV_Z, EV_END = 0, 512, 1024, 1536, 2048, 2560, 2688


def _log_sigmoid(x):
    return jnp.minimum(x, 0.0) - jnp.log(1.0 + jnp.exp(-jnp.abs(x)))


def _split3(x):
    hi = x.astype(BF16)
    r1 = x - hi.astype(F32)
    mid = r1.astype(BF16)
    lo = (r1 - mid.astype(F32)).astype(BF16)
    return hi, mid, lo


def _dot3(sel, parts):
    return _dot(sel, parts[0]) + _dot(sel, parts[1]) + _dot(sel, parts[2])


def _head_masks(width, per_head, dtype):
    lane = lax.broadcasted_iota(jnp.int32, (1, width), 1)
    return [jnp.where(lane // per_head == h, 1.0, 0.0).astype(dtype) for h in range(width // per_head)]


def _even_prompt_body(x_ref, g_ref, win_ref, wgate_ref, bgate_ref, glag_ref, lng_ref, lnb_ref,
                      sgw_ref, sgb_ref, e_ref, wout_ref,
                      xo_ref, gla_ref, sgv_ref,
                      st_ref, q_ref, k_ref, cum2_ref, qin_ref, kd_ref, dec_ref, v_ref, o_ref,
                      pcat_ref, acat_ref, mix_ref, add_ref, stb_ref, ruv_ref):
    j = pl.program_id(1)
    tt = x_ref.shape[0]
    n_chunks = tt // GLA_CH

    @pl.when(j == 0)
    def _():
        st_ref[...] = jnp.zeros_like(st_ref)

    x = x_ref[...]
    h = _rms(x, g_ref[...]).astype(BF16)

    qk = _dot(h, win_ref[:, EV_Q:EV_V])
    q = qk[:, :GLA_QK] * (GLA_DK ** -0.5)
    k = qk[:, GLA_QK:]
    v_ref[...] = _dot(h, win_ref[:, EV_V:EV_R]).astype(BF16)
    z = _dot(h, win_ref[:, EV_Z:EV_END]).astype(BF16)
    log_a = _log_sigmoid(_dot(z, wgate_ref[...]) + bgate_ref[...]) * GLA_INV_TAU

    sel_rows = min(tt, GLA_SEL_ROWS)
    row = lax.broadcasted_iota(jnp.int32, (sel_rows, sel_rows), 0)
    col = lax.broadcasted_iota(jnp.int32, (sel_rows, sel_rows), 1)
    same = (row // GLA_CH) == (col // GLA_CH)
    sel_cum = jnp.where(same & (col <= row), 1.0, 0.0).astype(BF16)
    sel_tot = jnp.where(same, 1.0, 0.0).astype(BF16)
    parts = _split3(log_a)
    groups_of_rows = [slice(r0, r0 + sel_rows) for r0 in range(0, tt, sel_rows)]
    cum = jnp.concatenate([_dot3(sel_cum, [p[rs] for p in parts]) for rs in groups_of_rows], axis=0)
    tot = jnp.concatenate([_dot3(sel_tot, [p[rs] for p in parts]) for rs in groups_of_rows], axis=0)
    q_ref[...] = q
    k_ref[...] = k
    cum2_ref[...] = cum * LOG2_E
    qin_ref[...] = (q * jnp.exp(cum)).astype(BF16)
    kd_ref[...] = (k * jnp.exp(tot - cum)).astype(BF16)
    dec_ref[...] = jnp.exp(tot)

    half = GLA_CH // 2
    proj_block = 256
    proj_cols = list(range(EV_R, EV_Z, proj_block))
    proj_every = n_chunks // len(proj_cols)
    assert proj_every >= 1
    for n in range(n_chunks):
        if n % proj_every == proj_every - 1 and n // proj_every < len(proj_cols):
            c0 = proj_cols[n // proj_every]
            ruv_ref[:, c0 - EV_R:c0 - EV_R + proj_block] = _dot(h, win_ref[:, c0:c0 + proj_block])
        base = n * GLA_CH
        qb = q_ref[base:base + GLA_CH, :]
        cb = cum2_ref[base:base + GLA_CH, :]
        for s in range(GLA_CH):
            ks = k_ref[base + s:base + s + 1, :]
            cs = cum2_ref[base + s:base + s + 1, :]
            if s < half:
                p = qb * ks * jnp.exp2(jnp.minimum(cb - cs, 0.0))
            else:
                p_hi = qb[half:] * ks * jnp.exp2(jnp.minimum(cb[half:] - cs, 0.0))
                p = jnp.concatenate([jnp.zeros_like(p_hi), p_hi], axis=0)
            pcat_ref[base:base + GLA_CH, s * GLA_QK:(s + 1) * GLA_QK] = p.astype(BF16)
    scores = _dot(pcat_ref[...], e_ref[...])

    u = jax.nn.gelu(ruv_ref[:, EV_U - EV_R:EV_VS - EV_R])
    v_ln = _layernorm(jax.nn.gelu(ruv_ref[:, EV_VS - EV_R:EV_Z - EV_R]), lng_ref[...], lnb_ref[...])
    v_lb = v_ln.astype(BF16)

    t_loc = lax.broadcasted_iota(jnp.int32, scores.shape, 0) % GLA_CH
    c_idx = lax.broadcasted_iota(jnp.int32, scores.shape, 1)
    causal = (c_idx % GLA_CH <= t_loc) & (c_idx < GLA_H * GLA_CH)
    acat_ref[...] = jnp.where(causal, scores, 0.0).astype(BF16)

    qk_masks = _head_masks(GLA_QK, GLA_DK, BF16)
    v_masks = _head_masks(GLA_V, GLA_DV, BF16)

    for n in range(n_chunks):
        rows = slice(n * GLA_CH, (n + 1) * GLA_CH)
        kn = kd_ref[rows, :]
        vn = v_ref[rows, :]
        lk = jnp.concatenate([kn * m for m in qk_masks], axis=0)
        vs = jnp.concatenate([vn[:, hh * GLA_DV:(hh + 1) * GLA_DV] for hh in range(GLA_H)], axis=0)
        add_ref[n] = lax.dot_general(vs, lk, (((0,), (0,)), ((), ())), preferred_element_type=F32)
    r = ruv_ref[:, 0:EV_U - EV_R]
    gate_r = r * jax.nn.sigmoid(r)
    sg_tasks = [(c, hh) for c in range(tt // SG_CHUNK) for hh in range(SG_H)]
    st = st_ref[...]
    for n in range(n_chunks):
        stb_ref[n] = st.astype(BF16)
        st = st * dec_ref[n * GLA_CH:n * GLA_CH + 1, :] + add_ref[n]
        if n < len(sg_tasks):
            c, hh = sg_tasks[n]
            mix_ref[c * SG_CHUNK:(c + 1) * SG_CHUNK, hh * SG_DH:(hh + 1) * SG_DH] = _dot(
                sgw_ref[hh], v_lb[c * SG_CHUNK:(c + 1) * SG_CHUNK, hh * SG_DH:(hh + 1) * SG_DH])
    st_ref[...] = st
    assert len(sg_tasks) <= n_chunks
    for n in range(n_chunks):
        rows = slice(n * GLA_CH, (n + 1) * GLA_CH)
        qn = qin_ref[rows, :]
        vn = v_ref[rows, :]
        lq = jnp.concatenate([qn * m for m in qk_masks], axis=0)
        oi = lax.dot_general(lq, stb_ref[n], (((1,), (1,)), ((), ())),
                             preferred_element_type=F32)
        o_inter = jnp.concatenate([oi[hh * GLA_CH:(hh + 1) * GLA_CH] for hh in range(GLA_H)], axis=1)
        vbd = jnp.concatenate([vn * m for m in v_masks]
                              + [jnp.zeros((acat_ref.shape[1] - GLA_H * GLA_CH, GLA_V), BF16)], axis=0)
        o_ref[rows, :] = o_inter + _dot(acat_ref[rows, :], vbd)

    o = o_ref[...]
    o_n = jnp.concatenate(
        [_rms(o[:, hh * GLA_DV:(hh + 1) * GLA_DV], 1.0) for hh in range(GLA_H)], axis=1) * glag_ref[...]
    out_a = o_n * gate_r
    bias = jnp.concatenate([sgb_ref[...]] * (tt // SG_CHUNK), axis=0)
    out_b = u * (mix_ref[...] + bias)

    y = _dot(jnp.concatenate([out_a, out_b], axis=1).astype(BF16), wout_ref[...])
    xo_ref[...] = x + y

    @pl.when(j == pl.num_programs(1) - 1)
    def _():
        sgv_ref[0] = v_ln[tt - SG_CHUNK:, :]
        gla_ref[0] = st_ref[...].T.reshape(GLA_H, GLA_DK, GLA_DV)


def _score_sum_matrix():
    r = np.arange(GLA_CH * GLA_QK)
    c = np.arange(LANES)
    s, hh = r // GLA_QK, (r % GLA_QK) // GLA_DK
    return jnp.asarray((c[None, :] == (hh * GLA_CH + s)[:, None]).astype(np.float32), dtype=BF16)


def _even_prompt(x, batch, g, w_in, w_gate, b_gate, gla_g, ln_g, ln_b, sg_w, sg_bias, w_out, *, block_rows):
    rows, d = x.shape
    t = rows // batch
    tt = min(block_rows, t)
    nj = t // tt
    const2 = lambda b, j: (0, 0)
    full = lambda a: pl.BlockSpec(a.shape, (lambda b, j: (0,) * a.ndim), pipeline_mode=pl.Buffered(1))
    e = _score_sum_matrix()
    operands = (g, w_in, w_gate, b_gate, gla_g, ln_g, ln_b, sg_w, sg_bias, e, w_out)
    return pl.pallas_call(
        _even_prompt_body,
        grid=(batch, nj),
        in_specs=[pl.BlockSpec((tt, d), lambda b, j: (b * nj + j, 0))] + [full(a) for a in operands],
        out_specs=[
            pl.BlockSpec((tt, d), lambda b, j: (b * nj + j, 0)),
            pl.BlockSpec((1, GLA_H, GLA_DK, GLA_DV), lambda b, j: (b, 0, 0, 0)),
            pl.BlockSpec((1, SG_CHUNK, SG_W), lambda b, j: (b, 0, 0)),
        ],
        out_shape=[
            jax.ShapeDtypeStruct((rows, d), F32),
            jax.ShapeDtypeStruct((batch, GLA_H, GLA_DK, GLA_DV), F32),
            jax.ShapeDtypeStruct((batch, SG_CHUNK, SG_W), F32),
        ],
        scratch_shapes=[
            pltpu.VMEM((GLA_DV, GLA_QK), F32),
            pltpu.VMEM((tt, GLA_QK), F32),
            pltpu.VMEM((tt, GLA_QK), F32),
            pltpu.VMEM((tt, GLA_QK), F32),
            pltpu.VMEM((tt, GLA_QK), BF16),
            pltpu.VMEM((tt, GLA_QK), BF16),
            pltpu.VMEM((tt, GLA_QK), F32),
            pltpu.VMEM((tt, GLA_V), BF16),
            pltpu.VMEM((tt, GLA_V), F32),
            pltpu.VMEM((tt, GLA_CH * GLA_QK), BF16),
            pltpu.VMEM((tt, LANES), BF16),
            pltpu.VMEM((tt, SG_W), F32),
            pltpu.VMEM((tt // GLA_CH, GLA_DV, GLA_QK), F32),
            pltpu.VMEM((tt // GLA_CH, GLA_DV, GLA_QK), BF16),
            pltpu.VMEM((tt, EV_Z - EV_R), F32),
        ],
        compiler_params=pltpu.CompilerParams(
            dimension_semantics=("arbitrary", "arbitrary"), vmem_limit_bytes=VMEM_LIMIT_BYTES),
        name="even_prompt",
    )(x, *operands)


def _prep_even(b_gate, gla_g, ln_g, ln_b, sg_w, sg_b):
    return dict(
        b_gate=b_gate.reshape(1, -1), b_gate_col=b_gate.reshape(-1, 1), gla_g=gla_g.reshape(1, -1),
        ln_g=ln_g.reshape(1, -1), ln_b=ln_b.reshape(1, -1),
        sg_bias=jnp.repeat(jnp.transpose(sg_b), SG_DH, axis=1),
        sg_w0=jnp.repeat(sg_w[:, 0, 0], SG_DH).reshape(1, -1), sg_b0=jnp.repeat(sg_b[:, 0], SG_DH).reshape(1, -1))


def _even_sample_body(x_ref, s_ref, g_ref, win32_ref, wgate32_ref, wout32_ref, sgw32_ref, bgate_ref,
                      bgatec_ref, glag_ref, lng_ref, lnb_ref, sgw0_ref, sgb0_ref,
                      xo_ref, so_ref, sgv_ref, win_ref, wgate_ref, wout_ref, sgw_ref,
                      o_ref, wint_ref, wgatet_ref):
    bb = x_ref.shape[0]

    @pl.when(pl.program_id(0) == 0)
    def _():
        z0 = 2 * GLA_QK + 2 * GLA_V
        rank = wgate32_ref.shape[0]
        wint_ref[0:z0] = win32_ref[0:z0].astype(BF16)
        wint_ref[z0:EV_Z] = win32_ref[z0 + rank:].astype(BF16)
        wint_ref[EV_Z:EV_Z + rank] = win32_ref[z0:z0 + rank].astype(BF16)
        wint_ref[EV_Z + rank:EV_END] = jnp.zeros((EV_END - EV_Z - rank, wint_ref.shape[1]), BF16)
        for c0 in range(0, EV_END, LANES):
            win_ref[:, c0:c0 + LANES] = wint_ref[c0:c0 + LANES, :].T
        wgate_ref[...] = jnp.concatenate(
            [wgate32_ref[...].astype(BF16), jnp.zeros((GLA_RANK_PAD - rank, GLA_QK), BF16)], axis=0)
        wgatet_ref[...] = wgate_ref[...].T
        wout_ref[...] = wout32_ref[...].astype(BF16)
        row = lax.broadcasted_iota(jnp.int32, (SG_CHUNK, SG_CHUNK), 0)
        col = lax.broadcasted_iota(jnp.int32, (SG_CHUNK, SG_CHUNK), 1)
        for hh in range(SG_H):
            sgw_ref[hh] = jnp.where(col <= row, sgw32_ref[hh], 0.0).astype(BF16)

    x = x_ref[...]
    h = _rms(x, g_ref[...]).astype(BF16)
    nt = (((1,), (1,)), ((), ()))
    qk_t = lax.dot_general(wint_ref[EV_Q:EV_V], h, nt, preferred_element_type=F32)
    q_t = qk_t[:GLA_QK] * (GLA_DK ** -0.5)
    k_t = qk_t[GLA_QK:].astype(BF16).astype(F32)
    z_t = lax.dot_general(wint_ref[EV_Z:EV_END], h, nt, preferred_element_type=F32).astype(BF16)
    a_t = jnp.exp(_log_sigmoid(_dot(wgatet_ref[...], z_t) + bgatec_ref[...]) * GLA_INV_TAU)
    v = _dot(h, win_ref[:, EV_V:EV_R])
    v_r = v.astype(BF16).astype(F32)
    for b in range(bb):
        s_old = s_ref[b].reshape(GLA_QK, GLA_DV)
        a_c = jnp.broadcast_to(a_t[:, b:b + 1], (GLA_QK, GLA_DV))
        k_c = jnp.broadcast_to(k_t[:, b:b + 1], (GLA_QK, GLA_DV))
        q_c = jnp.broadcast_to(q_t[:, b:b + 1], (GLA_QK, GLA_DV))
        v_rows = jnp.concatenate(
            [jnp.broadcast_to(v_r[b:b + 1, hh * GLA_DV:(hh + 1) * GLA_DV], (GLA_DK, GLA_DV))
             for hh in range(GLA_H)], axis=0)
        s_new = a_c * s_old + k_c * v_rows
        so_ref[b] = s_new.reshape(GLA_H, GLA_DK, GLA_DV)
        ob = jnp.sum((q_c * s_new).reshape(GLA_H, GLA_DK, GLA_DV), axis=1)
        o_ref[b:b + 1, :] = jnp.concatenate([ob[hh:hh + 1] for hh in range(GLA_H)], axis=1)
    o = o_ref[...]
    o_n = jnp.concatenate(
        [_rms(o[:, hh * GLA_DV:(hh + 1) * GLA_DV], 1.0) for hh in range(GLA_H)], axis=1) * glag_ref[...]
    r = _dot(h, win_ref[:, EV_R:EV_U])
    out_a = o_n * (r * jax.nn.sigmoid(r))
    u = jax.nn.gelu(_dot(h, win_ref[:, EV_U:EV_VS]))
    vv = jax.nn.gelu(_dot(h, win_ref[:, EV_VS:EV_Z]))
    mu = jnp.mean(vv, axis=-1, keepdims=True)
    xc = vv - mu
    var = jnp.mean(xc * xc, axis=-1, keepdims=True)
    v_ln = xc * lax.rsqrt(var + EPS) * lng_ref[...] + lnb_ref[...]
    sgv_ref[...] = v_ln
    out_b = u * (sgw0_ref[...] * v_ln + sgb0_ref[...])
    y = _dot(jnp.concatenate([out_a, out_b], axis=1).astype(BF16), wout_ref[...])
    xo_ref[...] = x + y


def _even_sample(x, state, g, w_in_t, w_gate, w_out, sg_w, ev, *, block_rows=32):
    rows, d = x.shape
    bb = min(block_rows, rows)
    full = lambda a: pl.BlockSpec(a.shape, (lambda i: (0,) * a.ndim), pipeline_mode=pl.Buffered(1))
    whole = lambda shape: pl.BlockSpec(shape, (lambda i: (0,) * len(shape)))
    operands = (g, w_in_t, w_gate, w_out, sg_w, ev["b_gate"], ev["b_gate_col"], ev["gla_g"], ev["ln_g"],
                ev["ln_b"], ev["sg_w0"], ev["sg_b0"])
    w_shapes = [(d, EV_END), (GLA_RANK_PAD, GLA_QK), w_out.shape, sg_w.shape]
    return pl.pallas_call(
        _even_sample_body,
        grid=(rows // bb,),
        in_specs=[pl.BlockSpec((bb, d), lambda i: (i, 0)),
                  pl.BlockSpec((bb, GLA_H, GLA_DK, GLA_DV), lambda i: (i, 0, 0, 0))]
                 + [full(a) for a in operands],
        out_specs=[
            pl.BlockSpec((bb, d), lambda i: (i, 0)),
            pl.BlockSpec((bb, GLA_H, GLA_DK, GLA_DV), lambda i: (i, 0, 0, 0)),
            pl.BlockSpec((bb, SG_W), lambda i: (i, 0)),
        ] + [whole(s) for s in w_shapes],
        out_shape=[
            jax.ShapeDtypeStruct((rows, d), F32),
            jax.ShapeDtypeStruct(state.shape, F32),
            jax.ShapeDtypeStruct((rows, SG_W), F32),
        ] + [jax.ShapeDtypeStruct(s, BF16) for s in w_shapes],
        scratch_shapes=[
            pltpu.VMEM((bb, GLA_V), F32),
            pltpu.VMEM((EV_END, d), BF16),
            pltpu.VMEM((GLA_QK, GLA_RANK_PAD), BF16),
        ],
        compiler_params=pltpu.CompilerParams(
            dimension_semantics=("arbitrary",), vmem_limit_bytes=VMEM_LIMIT_BYTES),
        name="even_sample",
    )(x, state, *operands)


CONV_W = 512
CONV_K = 31
CONV_BUF = CONV_K - 1
CONV_PAD = 32
POOL_W = 512
POOL_WINDOWS = (2, 4, 8, 16)
POOL_DG = POOL_W // len(POOL_WINDOWS)
POOL_BUF = 15
POOL_PAD = 16
CONV_ROWS = 32
ODD_SUB = 128
ODD_COLS = 256


def _layernorm(x, g, b):
    mu = jnp.mean(x, axis=-1, keepdims=True)
    xc = x - mu
    var = jnp.mean(xc * xc, axis=-1, keepdims=True)
    return xc * lax.rsqrt(var + EPS) * g + b


def _odd_prompt_body(x_ref, g_ref, win_ref, cw_ref, cb_ref, lng_ref, lnb_ref, pw_ref, ps_ref, wout_ref,
                     xo_ref, conv_ref, pool_ref, gbuf_ref, pbuf_ref, shift_ref, cwb_ref, psum_ref,
                     h_ref, mixin_ref, pooled_ref, raw_ref, convo_ref):
    j = pl.program_id(1)
    tt = x_ref.shape[0]
    assert POOL_WINDOWS == (2, 4, 8, 16)

    p0 = SUBLANES + POOL_PAD

    @pl.when(j == 0)
    def _():
        gbuf_ref[0:CONV_PAD, :] = jnp.zeros((CONV_PAD, CONV_W), F32)
        pbuf_ref[0:p0, :] = jnp.zeros((p0, POOL_W), F32)
        psum_ref[:, 0:SUBLANES, :] = jnp.zeros((psum_ref.shape[0], SUBLANES, POOL_W), F32)
        for o in range(CONV_K):
            cwb_ref[o] = jnp.broadcast_to(cw_ref[o:o + 1, :], (SUBLANES, CONV_W))

    h_ref[...] = _rms(x_ref[...], g_ref[...]).astype(BF16)
    first = CONV_PAD - CONV_BUF
    groups = CONV_ROWS // SUBLANES
    n_sub = tt // ODD_SUB

    def project(k):
        rows = slice(k * ODD_SUB, (k + 1) * ODD_SUB)

        def piece(c0):
            def run():
                raw_ref[rows, c0:c0 + ODD_COLS] = _dot(h_ref[rows, :], win_ref[:, c0:c0 + ODD_COLS])
            return run

        return [piece(c0) for c0 in range(0, 2 * CONV_W, ODD_COLS)]

    def glu(k):
        rows = slice(k * ODD_SUB, (k + 1) * ODD_SUB)
        gbuf_ref[CONV_PAD + k * ODD_SUB:CONV_PAD + (k + 1) * ODD_SUB, :] = (
            raw_ref[rows, 0:CONV_W] * jax.nn.sigmoid(raw_ref[rows, CONV_W:2 * CONV_W]))

    def mix(k):
        r_lo = k * ODD_SUB

        def shift_copies():
            lo = 0 if k == 0 else r_lo + CONV_PAD - SUBLANES
            hi = r_lo + ODD_SUB + CONV_PAD - SUBLANES
            for rr in range(1, SUBLANES):
                shift_ref[rr - 1, lo:hi, :] = gbuf_ref[rr + lo:rr + hi, :]

        def conv_block(r0):
            def run():
                accs = [jnp.zeros((SUBLANES, CONV_W), F32) + cb_ref[...] for _ in range(groups)]
                for o in range(first, first + CONV_K):
                    rr = o % SUBLANES
                    w8 = cwb_ref[o - first]
                    for gq in range(groups):
                        lo = o - rr + r0 + gq * SUBLANES
                        src = (gbuf_ref[lo:lo + SUBLANES, :] if rr == 0
                               else shift_ref[rr - 1, lo:lo + SUBLANES, :])
                        accs[gq] = accs[gq] + src * w8
                for gq in range(groups):
                    convo_ref[r0 + gq * SUBLANES:r0 + (gq + 1) * SUBLANES, :] = accs[gq]
            return run

        def norm_act():
            out_c = _layernorm(convo_ref[r_lo:r_lo + ODD_SUB, :], lng_ref[...], lnb_ref[...])
            mixin_ref[r_lo:r_lo + ODD_SUB, 0:CONV_W] = (out_c * jax.nn.sigmoid(out_c)).astype(BF16)

        def pooling():
            lo = SUBLANES if k == 0 else p0 + r_lo
            hi = p0 + r_lo + ODD_SUB
            psum_ref[0, lo:hi, :] = pbuf_ref[lo:hi, :] + pbuf_ref[lo - 1:hi - 1, :]
            psum_ref[1, lo:hi, POOL_DG:] = psum_ref[0, lo:hi, POOL_DG:] + psum_ref[0, lo - 2:hi - 2, POOL_DG:]
            psum_ref[2, lo:hi, 2 * POOL_DG:] = (psum_ref[1, lo:hi, 2 * POOL_DG:]
                                                + psum_ref[1, lo - 4:hi - 4, 2 * POOL_DG:])
            o_lo = p0 + r_lo
            o_hi = o_lo + ODD_SUB
            tots = [psum_ref[0, o_lo:o_hi, 0:POOL_DG],
                    psum_ref[1, o_lo:o_hi, POOL_DG:2 * POOL_DG],
                    psum_ref[2, o_lo:o_hi, 2 * POOL_DG:3 * POOL_DG],
                    psum_ref[2, o_lo:o_hi, 3 * POOL_DG:] + psum_ref[2, o_lo - 8:o_hi - 8, 3 * POOL_DG:]]
            t_glob = j * tt + r_lo + lax.broadcasted_iota(jnp.int32, (ODD_SUB, 1), 0)
            for gi, win_len in enumerate(POOL_WINDOWS):
                lanes = slice(gi * POOL_DG, (gi + 1) * POOL_DG)
                cnt = jnp.minimum(win_len, t_glob + 1).astype(F32)
                pooled = tots[gi] / cnt - pbuf_ref[o_lo:o_hi, lanes]
                pooled_ref[r_lo:r_lo + ODD_SUB, lanes] = pooled.astype(BF16)

        return ([shift_copies] + [conv_block(r0) for r0 in range(r_lo, r_lo + ODD_SUB, CONV_ROWS)]
                + [norm_act, pooling])

    def output(k):
        rows = slice(k * ODD_SUB, (k + 1) * ODD_SUB)

        def pool_matmuls():
            outs = [_dot(pooled_ref[rows, gi * POOL_DG:(gi + 1) * POOL_DG], pw_ref[gi])
                    for gi in range(len(POOL_WINDOWS))]
            mixin_ref[rows, CONV_W:] = (jnp.concatenate(outs, axis=1) * ps_ref[...]).astype(BF16)

        def out_piece(c0):
            def run():
                xo_ref[rows, c0:c0 + ODD_COLS] = x_ref[rows, c0:c0 + ODD_COLS] + _dot(
                    mixin_ref[rows, :], wout_ref[:, c0:c0 + ODD_COLS])
            return run

        return [pool_matmuls] + [out_piece(c0) for c0 in range(0, xo_ref.shape[1], ODD_COLS)]

    pbuf_ref[p0:p0 + tt, :] = _dot(h_ref[...], win_ref[:, 2 * CONV_W:])
    for task in project(0):
        task()
    glu(0)
    for k in range(n_sub):
        valu_tasks = mix(k)
        mxu_tasks = output(k - 1) if k >= 1 else []
        if k + 1 < n_sub:
            mxu_tasks = project(k + 1) + mxu_tasks
            valu_tasks = valu_tasks + [functools.partial(glu, k + 1)]
        for idx, task in enumerate(valu_tasks):
            lo_m = idx * len(mxu_tasks) // len(valu_tasks)
            hi_m = (idx + 1) * len(mxu_tasks) // len(valu_tasks)
            for m_task in mxu_tasks[lo_m:hi_m]:
                m_task()
            task()
    for task in output(n_sub - 1):
        task()

    tail_g = gbuf_ref[tt:tt + CONV_PAD, :]
    tail_p = pbuf_ref[SUBLANES + tt:p0 + tt, :]
    gbuf_ref[0:CONV_PAD, :] = tail_g
    pbuf_ref[SUBLANES:p0, :] = tail_p

    @pl.when(j == pl.num_programs(1) - 1)
    def _():
        conv_ref[0] = tail_g[CONV_PAD - CONV_BUF:, :]
        pool_ref[0] = tail_p[POOL_PAD - POOL_BUF:, :]


def _odd_prompt(x, batch, g, od, *, block_rows):
    rows, d = x.shape
    t = rows // batch
    tt = min(block_rows, t)
    nj = t // tt
    full = lambda a: pl.BlockSpec(a.shape, (lambda b, j: (0,) * a.ndim), pipeline_mode=pl.Buffered(1))
    operands = (g, od["w_in"], od["conv_w"], od["conv_b"], od["ln_g"], od["ln_b"], od["pool_w"],
                od["pool_scale"], od["w_out"])
    return pl.pallas_call(
        _odd_prompt_body,
        grid=(batch, nj),
        in_specs=[pl.BlockSpec((tt, d), lambda b, j: (b * nj + j, 0))] + [full(a) for a in operands],
        out_specs=[
            pl.BlockSpec((tt, d), lambda b, j: (b * nj + j, 0)),
            pl.BlockSpec((1, CONV_BUF, CONV_W), lambda b, j: (b, 0, 0)),
            pl.BlockSpec((1, POOL_BUF, POOL_W), lambda b, j: (b, 0, 0)),
        ],
        out_shape=[
            jax.ShapeDtypeStruct((rows, d), F32),
            jax.ShapeDtypeStruct((batch, CONV_BUF, CONV_W), F32),
            jax.ShapeDtypeStruct((batch, POOL_BUF, POOL_W), F32),
        ],
        scratch_shapes=[
            pltpu.VMEM((CONV_PAD + tt + SUBLANES, CONV_W), F32),
            pltpu.VMEM((SUBLANES + POOL_PAD + tt, POOL_W), F32),
            pltpu.VMEM((SUBLANES - 1, tt + CONV_PAD - SUBLANES, CONV_W), F32),
            pltpu.VMEM((CONV_K, SUBLANES, CONV_W), F32),
            pltpu.VMEM((3, SUBLANES + POOL_PAD + tt, POOL_W), F32),
            pltpu.VMEM((tt, d), BF16),
            pltpu.VMEM((tt, CONV_W + POOL_W), BF16),
            pltpu.VMEM((tt, POOL_W), BF16),
            pltpu.VMEM((tt, 2 * CONV_W), F32),
            pltpu.VMEM((tt, CONV_W), F32),
        ],
        compiler_params=pltpu.CompilerParams(
            dimension_semantics=("arbitrary", "arbitrary"), vmem_limit_bytes=VMEM_LIMIT_BYTES),
        name="odd_prompt",
    )(x, *operands)


def _odd_sample_body(x_ref, cbuf_ref, pbuf_ref, g_ref, win32_ref, cw_ref, cb_ref, lng_ref, lnb_ref, pw32_ref,
                     ps_ref, wout32_ref, xo_ref, conv_ref, pool_ref, win_ref, pw_ref, wout_ref):
    @pl.when(pl.program_id(0) == 0)
    def _():
        win_ref[...] = win32_ref[...].astype(BF16)
        pw_ref[...] = pw32_ref[...].astype(BF16)
        wout_ref[...] = wout32_ref[...].astype(BF16)

    x = x_ref[...]
    h = _rms(x, g_ref[...]).astype(BF16)
    a = _dot(h, win_ref[:, 0:CONV_W])
    gt = _dot(h, win_ref[:, CONV_W:2 * CONV_W])
    xp = _dot(h, win_ref[:, 2 * CONV_W:])
    glu = a * jax.nn.sigmoid(gt)
    conv = glu * cw_ref[CONV_BUF:CONV_K, :] + cb_ref[...]
    for jj in range(CONV_BUF):
        conv = conv + cbuf_ref[jj] * cw_ref[jj:jj + 1, :]
    out_c = _layernorm(conv, lng_ref[...], lnb_ref[...])
    out_c = out_c * jax.nn.sigmoid(out_c)
    outs = []
    for gi, win_len in enumerate(POOL_WINDOWS):
        lanes = slice(gi * POOL_DG, (gi + 1) * POOL_DG)
        tot = xp[:, lanes]
        for jj in range(POOL_BUF - (win_len - 1), POOL_BUF):
            tot = tot + pbuf_ref[jj, :, lanes]
        pooled = tot / float(win_len) - xp[:, lanes]
        outs.append(_dot(pooled.astype(BF16), pw_ref[gi]))
    out_d = jnp.concatenate(outs, axis=1) * ps_ref[...]
    y = _dot(jnp.concatenate([out_c, out_d], axis=1).astype(BF16), wout_ref[...])
    xo_ref[...] = x + y
    conv_ref[0:CONV_BUF - 1] = cbuf_ref[1:CONV_BUF]
    conv_ref[CONV_BUF - 1] = glu
    pool_ref[0:POOL_BUF - 1] = pbuf_ref[1:POOL_BUF]
    pool_ref[POOL_BUF - 1] = xp


def _odd_sample(x, conv_buf, pool_buf, g, od, *, block_rows=32):
    rows, d = x.shape
    bb = min(block_rows, rows)
    full = lambda a: pl.BlockSpec(a.shape, (lambda i: (0,) * a.ndim), pipeline_mode=pl.Buffered(1))
    whole = lambda shape: pl.BlockSpec(shape, (lambda i: (0,) * len(shape)))
    operands = (g, od["w_in"], od["conv_w"], od["conv_b"], od["ln_g"], od["ln_b"], od["pool_w"],
                od["pool_scale"], od["w_out"])
    w_shapes = [od["w_in"].shape, od["pool_w"].shape, od["w_out"].shape]
    return pl.pallas_call(
        _odd_sample_body,
        grid=(rows // bb,),
        in_specs=[pl.BlockSpec((bb, d), lambda i: (i, 0)),
                  pl.BlockSpec((CONV_BUF, bb, CONV_W), lambda i: (0, i, 0)),
                  pl.BlockSpec((POOL_BUF, bb, POOL_W), lambda i: (0, i, 0))]
                 + [full(a) for a in operands],
        out_specs=[
            pl.BlockSpec((bb, d), lambda i: (i, 0)),
            pl.BlockSpec((CONV_BUF, bb, CONV_W), lambda i: (0, i, 0)),
            pl.BlockSpec((POOL_BUF, bb, POOL_W), lambda i: (0, i, 0)),
        ] + [whole(s) for s in w_shapes],
        out_shape=[
            jax.ShapeDtypeStruct((rows, d), F32),
            jax.ShapeDtypeStruct(conv_buf.shape, F32),
            jax.ShapeDtypeStruct(pool_buf.shape, F32),
        ] + [jax.ShapeDtypeStruct(s, BF16) for s in w_shapes],
        compiler_params=pltpu.CompilerParams(
            dimension_semantics=("arbitrary",), vmem_limit_bytes=VMEM_LIMIT_BYTES),
        name="odd_sample",
    )(x, conv_buf, pool_buf, *operands)


def _prep_odd(w_in, conv_w, conv_b, ln_g, ln_b, pool_w, pool_scale, w_out):
    return dict(w_in=w_in, conv_w=conv_w, conv_b=conv_b.reshape(1, -1), ln_g=ln_g.reshape(1, -1),
                ln_b=ln_b.reshape(1, -1), pool_w=pool_w, pool_scale=pool_scale.reshape(1, -1), w_out=w_out)


def kernel(x_prompt, x_sample, state_gla, state_conv, state_pool, norm_g, ff_in, ff_out, ev_w_in, ev_w_gate, ev_b_gate, ev_gla_g, ev_sg_ln_g, ev_sg_ln_b, ev_sg_w, ev_sg_b, ev_w_out, od_w_in, od_conv_w, od_conv_b, od_ln_g, od_ln_b, od_pool_w, od_pool_scale, od_w_out, norm_f):
    bp, t, d = x_prompt.shape
    bs = x_sample.shape[0]
    depth = norm_g.shape[0]
    xp = x_prompt.reshape(bp * t, d)
    xs = x_sample.reshape(bs, d)
    gla_p, gla_s, sgv_p, sgv_s, conv_p, conv_s, pool_p, pool_s = [], [], [], [], [], [], [], []
    for layer in range(depth):
        i = layer // 2
        last = layer == depth - 1
        xs, xp = _ffn_fused(xs, xp, norm_g[layer, 0], ff_in, ff_out, layer, 0, norm_f, final_norm=False,
                            block_rows=FFN_ROWS)
        g_mix = norm_g[layer, 1].reshape(1, d)
        if layer % 2 == 0:
            ev = _prep_even(ev_b_gate[i], ev_gla_g[i], ev_sg_ln_g[i], ev_sg_ln_b[i], ev_sg_w[i], ev_sg_b[i])
            xs, s_s, v_s, w_in_b, w_gate_b, w_out_b, sg_w_b = _even_sample(
                xs, state_gla[i], g_mix, jnp.transpose(ev_w_in[i]), ev_w_gate[i], ev_w_out[i], ev_sg_w[i], ev)
            xp, s_p, v_p = _even_prompt(xp, bp, g_mix, w_in_b, w_gate_b, ev["b_gate"], ev["gla_g"],
                                        ev["ln_g"], ev["ln_b"], sg_w_b, ev["sg_bias"], w_out_b,
                                        block_rows=512)
            gla_p.append(s_p); gla_s.append(s_s); sgv_p.append(v_p); sgv_s.append(v_s.reshape(bs, 1, SG_W))
        else:
            od = _prep_odd(od_w_in[i], od_conv_w[i], od_conv_b[i], od_ln_g[i], od_ln_b[i], od_pool_w[i],
                           od_pool_scale[i], od_w_out[i])
            xs, c_s, p_s, w_in_b, pool_w_b, w_out_b = _odd_sample(
                xs, jnp.transpose(state_conv[i], (1, 0, 2)), jnp.transpose(state_pool[i], (1, 0, 2)), g_mix, od)
            c_s, p_s = jnp.transpose(c_s, (1, 0, 2)), jnp.transpose(p_s, (1, 0, 2))
            xp, c_p, p_p = _odd_prompt(xp, bp, g_mix, dict(od, w_in=w_in_b, pool_w=pool_w_b, w_out=w_out_b),
                                       block_rows=256)
            conv_p.append(c_p); conv_s.append(c_s); pool_p.append(p_p); pool_s.append(p_s)
        xs, xp = _ffn_fused(xs, xp, norm_g[layer, 2], ff_in, ff_out, layer, 1, norm_f, final_norm=last,
                            block_rows=FFN_ROWS)
    return (xp.reshape(bp, t, d), xs.reshape(bs, 1, d), jnp.stack(gla_p), jnp.stack(gla_s),
            jnp.stack(sgv_p), jnp.stack(sgv_s), jnp.stack(conv_p), jnp.stack(conv_s),
            jnp.stack(pool_p), jnp.stack(pool_s))
```

```python
import functools

import jax
import jax.numpy as jnp
import numpy as np
from jax import lax
from jax.experimental import pallas as pl
from jax.experimental.pallas import tpu as pltpu

EPS = 1e-6
LOG2_E = 1.4426950408889634
BF16 = jnp.bfloat16
F32 = jnp.float32

LANES = 128
SUBLANES = 8
VMEM_LIMIT_BYTES = 56 * 1024 * 1024
FFN_ROWS = 1024


def _rms(x, g):
    return x * lax.rsqrt(jnp.mean(x * x, axis=-1, keepdims=True) + EPS) * g


def _dot(a, b):
    return jnp.dot(a, b, preferred_element_type=F32)


FFN_CHUNK = 256


def _ffn_fused_body(xs_ref, xp_ref, g_ref, wa32_ref, wb32_ref, wo32_ref, gf_ref,
                    ys_ref, yp_ref, wa_ref, wb_ref, wo_ref, hs_ref, accs_ref, act_ref, *, final_norm):
    s = pl.program_id(0)
    n_chunks = wa_ref.shape[0]

    def finish(x, acc):
        y = x + 0.5 * acc
        return _rms(y, gf_ref[...]) if final_norm else y

    @pl.when(s < n_chunks)
    def _():
        @pl.when(s == 0)
        def _():
            hs_ref[...] = _rms(xs_ref[...], g_ref[...]).astype(BF16)
            accs_ref[...] = jnp.zeros_like(accs_ref)

        wa = wa32_ref[...].astype(BF16)
        wb = wb32_ref[...].astype(BF16)
        wo = wo32_ref[...].astype(BF16)
        wa_ref[s] = wa
        wb_ref[s] = wb
        wo_ref[s] = wo
        h = hs_ref[...]
        a = _dot(h, wa)
        b = _dot(h, wb)
        accs_ref[...] += _dot((a * jax.nn.sigmoid(a) * b).astype(BF16), wo)

        @pl.when(s == n_chunks - 1)
        def _():
            ys_ref[...] = finish(xs_ref[...], accs_ref[...])

    @pl.when(s >= n_chunks)
    def _():
        x = xp_ref[...]
        h = _rms(x, g_ref[...]).astype(BF16)
        for c in range(n_chunks):
            a = _dot(h, wa_ref[c])
            b = _dot(h, wb_ref[c])
            act_ref[:, c * FFN_CHUNK:(c + 1) * FFN_CHUNK] = (a * jax.nn.sigmoid(a) * b).astype(BF16)
        w_out = wo_ref[...].reshape(n_chunks * FFN_CHUNK, wo_ref.shape[2])
        yp_ref[...] = finish(x, _dot(act_ref[...], w_out))


def _ffn_fused(x_sample, x_prompt, g, ff_in, ff_out, layer, slot, g_final, *, final_norm, block_rows):
    rows_s, d = x_sample.shape
    rows_p = x_prompt.shape[0]
    d_ff = ff_out.shape[2]
    n_chunks = d_ff // FFN_CHUNK
    tm = min(block_rows, rows_p)
    const = lambda s: (0, 0)
    chunk = lambda s: jnp.minimum(s, n_chunks - 1)
    tile = lambda s: (jnp.maximum(s - n_chunks, 0), 0)
    body = functools.partial(_ffn_fused_body, final_norm=final_norm)
    return pl.pallas_call(
        body,
        grid=(n_chunks + rows_p // tm,),
        in_specs=[
            pl.BlockSpec((rows_s, d), const),
            pl.BlockSpec((tm, d), tile),
            pl.BlockSpec((1, d), const),
            pl.BlockSpec((None, None, d, FFN_CHUNK), lambda s: (layer, slot, 0, chunk(s))),
            pl.BlockSpec((None, None, d, FFN_CHUNK), lambda s: (layer, slot, 0, n_chunks + chunk(s))),
            pl.BlockSpec((None, None, FFN_CHUNK, d), lambda s: (layer, slot, chunk(s), 0)),
            pl.BlockSpec((1, d), const),
        ],
        out_specs=[pl.BlockSpec((rows_s, d), const), pl.BlockSpec((tm, d), tile)],
        out_shape=[jax.ShapeDtypeStruct((rows_s, d), F32), jax.ShapeDtypeStruct((rows_p, d), F32)],
        scratch_shapes=[
            pltpu.VMEM((n_chunks, d, FFN_CHUNK), BF16),
            pltpu.VMEM((n_chunks, d, FFN_CHUNK), BF16),
            pltpu.VMEM((n_chunks, FFN_CHUNK, d), BF16),
            pltpu.VMEM((rows_s, d), BF16),
            pltpu.VMEM((rows_s, d), F32),
            pltpu.VMEM((tm, d_ff), BF16),
        ],
        compiler_params=pltpu.CompilerParams(
            dimension_semantics=("arbitrary",), vmem_limit_bytes=VMEM_LIMIT_BYTES),
        name="ffn_fused",
    )(x_sample, x_prompt, g.reshape(1, d), ff_in, ff_in, ff_out, g_final.reshape(1, d))


GLA_H = 4
GLA_DK = 64
GLA_DV = 128
GLA_QK = GLA_H * GLA_DK
GLA_V = GLA_H * GLA_DV
GLA_RANK_PAD = 128
GLA_INV_TAU = 1.0 / 16.0
GLA_CH = 16
GLA_SEL_ROWS = 256
SG_H = 4
SG_DH = 128
SG_W = SG_H * SG_DH
SG_CHUNK = 128
EV_Q, EV_V, EV_R, EV_U, EV_VS, EV_Z, EV_END = 0, 512, 1024, 1536, 2048, 2560, 2688


def _log_sigmoid(x):
    return jnp.minimum(x, 0.0) - jnp.log(1.0 + jnp.exp(-jnp.abs(x)))


def _split3(x):
    hi = x.astype(BF16)
    r1 = x - hi.astype(F32)
    mid = r1.astype(BF16)
    lo = (r1 - mid.astype(F32)).astype(BF16)
    return hi, mid, lo


def _dot3(sel, parts):
    return _dot(sel, parts[0]) + _dot(sel, parts[1]) + _dot(sel, parts[2])


def _head_masks(width, per_head, dtype):
    lane = lax.broadcasted_iota(jnp.int32, (1, width), 1)
    return [jnp.where(lane // per_head == h, 1.0, 0.0).astype(dtype) for h in range(width // per_head)]


def _layernorm(x, g, b):
    mu = jnp.mean(x, axis=-1, keepdims=True)
    xc = x - mu
    var = jnp.mean(xc * xc, axis=-1, keepdims=True)
    return xc * lax.rsqrt(var + EPS) * g + b


def _even_prompt_body(x_ref, g_ref, win_ref, wgate_ref, bgate_ref, glag_ref, lng_ref, lnb_ref,
                      sgw_ref, sgb_ref, e_ref, wout_ref,
                      xo_ref, gla_ref, sgv_ref,
                      st_ref, q_ref, k_ref, cum2_ref, qin_ref, kd_ref, dec_ref, v_ref, o_ref,
                      pcat_ref, acat_ref, mix_ref, add_ref, stb_ref, ruv_ref):
    j = pl.program_id(1)
    tt = x_ref.shape[0]
    n_chunks = tt // GLA_CH

    @pl.when(j == 0)
    def _():
        st_ref[...] = jnp.zeros_like(st_ref)

    x = x_ref[...]
    h = _rms(x, g_ref[...]).astype(BF16)

    qk = _dot(h, win_ref[:, EV_Q:EV_V])
    q = qk[:, :GLA_QK] * (GLA_DK ** -0.5)
    k = qk[:, GLA_QK:]
    v_ref[...] = _dot(h, win_ref[:, EV_V:EV_R]).astype(BF16)
    z = _dot(h, win_ref[:, EV_Z:EV_END]).astype(BF16)
    log_a = _log_sigmoid(_dot(z, wgate_ref[...]) + bgate_ref[...]) * GLA_INV_TAU

    sel_rows = min(tt, GLA_SEL_ROWS)
    row = lax.broadcasted_iota(jnp.int32, (sel_rows, sel_rows), 0)
    col = lax.broadcasted_iota(jnp.int32, (sel_rows, sel_rows), 1)
    same = (row // GLA_CH) == (col // GLA_CH)
    sel_cum = jnp.where(same & (col <= row), 1.0, 0.0).astype(BF16)
    sel_tot = jnp.where(same, 1.0, 0.0).astype(BF16)
    parts = _split3(log_a)
    groups_of_rows = [slice(r0, r0 + sel_rows) for r0 in range(0, tt, sel_rows)]
    cum = jnp.concatenate([_dot3(sel_cum, [p[rs] for p in parts]) for rs in groups_of_rows], axis=0)
    tot = jnp.concatenate([_dot3(sel_tot, [p[rs] for p in parts]) for rs in groups_of_rows], axis=0)
    q_ref[...] = q
    k_ref[...] = k
    cum2_ref[...] = cum * LOG2_E
    qin_ref[...] = (q * jnp.exp(cum)).astype(BF16)
    kd_ref[...] = (k * jnp.exp(tot - cum)).astype(BF16)
    dec_ref[...] = jnp.exp(tot)

    half = GLA_CH // 2
    proj_block = 256
    proj_cols = list(range(EV_R, EV_Z, proj_block))
    proj_every = n_chunks // len(proj_cols)
    assert proj_every >= 1
    for n in range(n_chunks):
        if n % proj_every == proj_every - 1 and n // proj_every < len(proj_cols):
            c0 = proj_cols[n // proj_every]
            ruv_ref[:, c0 - EV_R:c0 - EV_R + proj_block] = _dot(h, win_ref[:, c0:c0 + proj_block])
        base = n * GLA_CH
        qb = q_ref[base:base + GLA_CH, :]
        cb = cum2_ref[base:base + GLA_CH, :]
        for s in range(GLA_CH):
            ks = k_ref[base + s:base + s + 1, :]
            cs = cum2_ref[base + s:base + s + 1, :]
            if s < half:
                p = qb * ks * jnp.exp2(jnp.minimum(cb - cs, 0.0))
            else:
                p_hi = qb[half:] * ks * jnp.exp2(jnp.minimum(cb[half:] - cs, 0.0))
                p = jnp.concatenate([jnp.zeros_like(p_hi), p_hi], axis=0)
            pcat_ref[base:base + GLA_CH, s * GLA_QK:(s + 1) * GLA_QK] = p.astype(BF16)
    scores = _dot(pcat_ref[...], e_ref[...])

    u = jax.nn.gelu(ruv_ref[:, EV_U - EV_R:EV_VS - EV_R])
    v_ln = _layernorm(jax.nn.gelu(ruv_ref[:, EV_VS - EV_R:EV_Z - EV_R]), lng_ref[...], lnb_ref[...])
    v_lb = v_ln.astype(BF16)

    t_loc = lax.broadcasted_iota(jnp.int32, scores.shape, 0) % GLA_CH
    c_idx = lax.broadcasted_iota(jnp.int32, scores.shape, 1)
    causal = (c_idx % GLA_CH <= t_loc) & (c_idx < GLA_H * GLA_CH)
    acat_ref[...] = jnp.where(causal, scores, 0.0).astype(BF16)

    qk_masks = _head_masks(GLA_QK, GLA_DK, BF16)
    v_masks = _head_masks(GLA_V, GLA_DV, BF16)

    for n in range(n_chunks):
        rows = slice(n * GLA_CH, (n + 1) * GLA_CH)
        kn = kd_ref[rows, :]
        vn = v_ref[rows, :]
        lk = jnp.concatenate([kn * m for m in qk_masks], axis=0)
        vs = jnp.concatenate([vn[:, hh * GLA_DV:(hh + 1) * GLA_DV] for hh in range(GLA_H)], axis=0)
        add_ref[n] = lax.dot_general(vs, lk, (((0,), (0,)), ((), ())), preferred_element_type=F32)
    r = ruv_ref[:, 0:EV_U - EV_R]
    gate_r = r * jax.nn.sigmoid(r)
    sg_tasks = [(c, hh) for c in range(tt // SG_CHUNK) for hh in range(SG_H)]
    st = st_ref[...]
    for n in range(n_chunks):
        stb_ref[n] = st.astype(BF16)
        st = st * dec_ref[n * GLA_CH:n * GLA_CH + 1, :] + add_ref[n]
        if n < len(sg_tasks):
            c, hh = sg_tasks[n]
            mix_ref[c * SG_CHUNK:(c + 1) * SG_CHUNK, hh * SG_DH:(hh + 1) * SG_DH] = _dot(
                sgw_ref[hh], v_lb[c * SG_CHUNK:(c + 1) * SG_CHUNK, hh * SG_DH:(hh + 1) * SG_DH])
    st_ref[...] = st
    assert len(sg_tasks) <= n_chunks
    for n in range(n_chunks):
        rows = slice(n * GLA_CH, (n + 1) * GLA_CH)
        qn = qin_ref[rows, :]
        vn = v_ref[rows, :]
        lq = jnp.concatenate([qn * m for m in qk_masks], axis=0)
        oi = lax.dot_general(lq, stb_ref[n], (((1,), (1,)), ((), ())),
                             preferred_element_type=F32)
        o_inter = jnp.concatenate([oi[hh * GLA_CH:(hh + 1) * GLA_CH] for hh in range(GLA_H)], axis=1)
        vbd = jnp.concatenate([vn * m for m in v_masks]
                              + [jnp.zeros((acat_ref.shape[1] - GLA_H * GLA_CH, GLA_V), BF16)], axis=0)
        o_ref[rows, :] = o_inter + _dot(acat_ref[rows, :], vbd)

    o = o_ref[...]
    o_n = jnp.concatenate(
        [_rms(o[:, hh * GLA_DV:(hh + 1) * GLA_DV], 1.0) for hh in range(GLA_H)], axis=1) * glag_ref[...]
    out_a = o_n * gate_r
    bias = jnp.concatenate([sgb_ref[...]] * (tt // SG_CHUNK), axis=0)
    out_b = u * (mix_ref[...] + bias)

    y = _dot(jnp.concatenate([out_a, out_b], axis=1).astype(BF16), wout_ref[...])
    xo_ref[...] = x + y

    @pl.when(j == pl.num_programs(1) - 1)
    def _():
        sgv_ref[0] = v_ln[tt - SG_CHUNK:, :]
        gla_ref[0] = st_ref[...].T.reshape(GLA_H, GLA_DK, GLA_DV)


def _score_sum_matrix():
    r = np.arange(GLA_CH * GLA_QK)
    c = np.arange(LANES)
    s, hh = r // GLA_QK, (r % GLA_QK) // GLA_DK
    return jnp.asarray((c[None, :] == (hh * GLA_CH + s)[:, None]).astype(np.float32), dtype=BF16)


def _even_prompt(x, batch, g, w_in, w_gate, b_gate, gla_g, ln_g, ln_b, sg_w, sg_bias, w_out, *, block_rows):
    rows, d = x.shape
    t = rows // batch
    tt = min(block_rows, t)
    nj = t // tt
    full = lambda a: pl.BlockSpec(a.shape, (lambda b, j: (0,) * a.ndim), pipeline_mode=pl.Buffered(1))
    e = _score_sum_matrix()
    operands = (g, w_in, w_gate, b_gate, gla_g, ln_g, ln_b, sg_w, sg_bias, e, w_out)
    return pl.pallas_call(
        _even_prompt_body,
        grid=(batch, nj),
        in_specs=[pl.BlockSpec((tt, d), lambda b, j: (b * nj + j, 0))] + [full(a) for a in operands],
        out_specs=[
            pl.BlockSpec((tt, d), lambda b, j: (b * nj + j, 0)),
            pl.BlockSpec((1, GLA_H, GLA_DK, GLA_DV), lambda b, j: (b, 0, 0, 0)),
            pl.BlockSpec((1, SG_CHUNK, SG_W), lambda b, j: (b, 0, 0)),
        ],
        out_shape=[
            jax.ShapeDtypeStruct((rows, d), F32),
            jax.ShapeDtypeStruct((batch, GLA_H, GLA_DK, GLA_DV), F32),
            jax.ShapeDtypeStruct((batch, SG_CHUNK, SG_W), F32),
        ],
        scratch_shapes=[
            pltpu.VMEM((GLA_DV, GLA_QK), F32),
            pltpu.VMEM((tt, GLA_QK), F32),
            pltpu.VMEM((tt, GLA_QK), F32),
            pltpu.VMEM((tt, GLA_QK), F32),
            pltpu.VMEM((tt, GLA_QK), BF16),
            pltpu.VMEM((tt, GLA_QK), BF16),
            pltpu.VMEM((tt, GLA_QK), F32),
            pltpu.VMEM((tt, GLA_V), BF16),
            pltpu.VMEM((tt, GLA_V), F32),
            pltpu.VMEM((tt, GLA_CH * GLA_QK), BF16),
            pltpu.VMEM((tt, LANES), BF16),
            pltpu.VMEM((tt, SG_W), F32),
            pltpu.VMEM((tt // GLA_CH, GLA_DV, GLA_QK), F32),
            pltpu.VMEM((tt // GLA_CH, GLA_DV, GLA_QK), BF16),
            pltpu.VMEM((tt, EV_Z - EV_R), F32),
        ],
        compiler_params=pltpu.CompilerParams(
            dimension_semantics=("arbitrary", "arbitrary"), vmem_limit_bytes=VMEM_LIMIT_BYTES),
        name="even_prompt",
    )(x, *operands)


def _prep_even(b_gate, gla_g, ln_g, ln_b, sg_w, sg_b):
    return dict(
        b_gate=b_gate.reshape(1, -1), b_gate_col=b_gate.reshape(-1, 1), gla_g=gla_g.reshape(1, -1),
        ln_g=ln_g.reshape(1, -1), ln_b=ln_b.reshape(1, -1),
        sg_bias=jnp.repeat(jnp.transpose(sg_b), SG_DH, axis=1),
        sg_w0=jnp.repeat(sg_w[:, 0, 0], SG_DH).reshape(1, -1), sg_b0=jnp.repeat(sg_b[:, 0], SG_DH).reshape(1, -1))


def _even_sample_body(x_ref, s_ref, g_ref, win32_ref, wgate32_ref, wout32_ref, sgw32_ref, bgate_ref,
                      bgatec_ref, glag_ref, lng_ref, lnb_ref, sgw0_ref, sgb0_ref,
                      xo_ref, so_ref, sgv_ref, win_ref, wgate_ref, wout_ref, sgw_ref,
                      o_ref, wint_ref, wgatet_ref):
    bb = x_ref.shape[0]

    @pl.when(pl.program_id(0) == 0)
    def _():
        z0 = 2 * GLA_QK + 2 * GLA_V
        rank = wgate32_ref.shape[0]
        wint_ref[0:z0] = win32_ref[0:z0].astype(BF16)
        wint_ref[z0:EV_Z] = win32_ref[z0 + rank:].astype(BF16)
        wint_ref[EV_Z:EV_Z + rank] = win32_ref[z0:z0 + rank].astype(BF16)
        wint_ref[EV_Z + rank:EV_END] = jnp.zeros((EV_END - EV_Z - rank, wint_ref.shape[1]), BF16)
        for c0 in range(0, EV_END, LANES):
            win_ref[:, c0:c0 + LANES] = wint_ref[c0:c0 + LANES, :].T
        wgate_ref[...] = jnp.concatenate(
            [wgate32_ref[...].astype(BF16), jnp.zeros((GLA_RANK_PAD - rank, GLA_QK), BF16)], axis=0)
        wgatet_ref[...] = wgate_ref[...].T
        wout_ref[...] = wout32_ref[...].astype(BF16)
        row = lax.broadcasted_iota(jnp.int32, (SG_CHUNK, SG_CHUNK), 0)
        col = lax.broadcasted_iota(jnp.int32, (SG_CHUNK, SG_CHUNK), 1)
        for hh in range(SG_H):
            sgw_ref[hh] = jnp.where(col <= row, sgw32_ref[hh], 0.0).astype(BF16)

    x = x_ref[...]
    h = _rms(x, g_ref[...]).astype(BF16)
    nt = (((1,), (1,)), ((), ()))
    qk_t = lax.dot_general(wint_ref[EV_Q:EV_V], h, nt, preferred_element_type=F32)
    q_t = qk_t[:GLA_QK] * (GLA_DK ** -0.5)
    k_t = qk_t[GLA_QK:].astype(BF16).astype(F32)
    z_t = lax.dot_general(wint_ref[EV_Z:EV_END], h, nt, preferred_element_type=F32).astype(BF16)
    a_t = jnp.exp(_log_sigmoid(_dot(wgatet_ref[...], z_t) + bgatec_ref[...]) * GLA_INV_TAU)
    v = _dot(h, win_ref[:, EV_V:EV_R])
    v_r = v.astype(BF16).astype(F32)
    for b in range(bb):
        s_old = s_ref[b].reshape(GLA_QK, GLA_DV)
        a_c = jnp.broadcast_to(a_t[:, b:b + 1], (GLA_QK, GLA_DV))
        k_c = jnp.broadcast_to(k_t[:, b:b + 1], (GLA_QK, GLA_DV))
        q_c = jnp.broadcast_to(q_t[:, b:b + 1], (GLA_QK, GLA_DV))
        v_rows = jnp.concatenate(
            [jnp.broadcast_to(v_r[b:b + 1, hh * GLA_DV:(hh + 1) * GLA_DV], (GLA_DK, GLA_DV))
             for hh in range(GLA_H)], axis=0)
        s_new = a_c * s_old + k_c * v_rows
        so_ref[b] = s_new.reshape(GLA_H, GLA_DK, GLA_DV)
        ob = jnp.sum((q_c * s_new).reshape(GLA_H, GLA_DK, GLA_DV), axis=1)
        o_ref[b:b + 1, :] = jnp.concatenate([ob[hh:hh + 1] for hh in range(GLA_H)], axis=1)
    o = o_ref[...]
    o_n = jnp.concatenate(
        [_rms(o[:, hh * GLA_DV:(hh + 1) * GLA_DV], 1.0) for hh in range(GLA_H)], axis=1) * glag_ref[...]
    r = _dot(h, win_ref[:, EV_R:EV_U])
    out_a = o_n * (r * jax.nn.sigmoid(r))
    u = jax.nn.gelu(_dot(h, win_ref[:, EV_U:EV_VS]))
    v_ln = _layernorm(jax.nn.gelu(_dot(h, win_ref[:, EV_VS:EV_Z])), lng_ref[...], lnb_ref[...])
    sgv_ref[...] = v_ln
    out_b = u * (sgw0_ref[...] * v_ln + sgb0_ref[...])
    y = _dot(jnp.concatenate([out_a, out_b], axis=1).astype(BF16), wout_ref[...])
    xo_ref[...] = x + y


def _even_sample(x, state, g, w_in_t, w_gate, w_out, sg_w, ev, *, block_rows=32):
    rows, d = x.shape
    bb = min(block_rows, rows)
    full = lambda a: pl.BlockSpec(a.shape, (lambda i: (0,) * a.ndim), pipeline_mode=pl.Buffered(1))
    whole = lambda shape: pl.BlockSpec(shape, (lambda i: (0,) * len(shape)))
    operands = (g, w_in_t, w_gate, w_out, sg_w, ev["b_gate"], ev["b_gate_col"], ev["gla_g"], ev["ln_g"],
                ev["ln_b"], ev["sg_w0"], ev["sg_b0"])
    w_shapes = [(d, EV_END), (GLA_RANK_PAD, GLA_QK), w_out.shape, sg_w.shape]
    return pl.pallas_call(
        _even_sample_body,
        grid=(rows // bb,),
        in_specs=[pl.BlockSpec((bb, d), lambda i: (i, 0)),
                  pl.BlockSpec((bb, GLA_H, GLA_DK, GLA_DV), lambda i: (i, 0, 0, 0))]
                 + [full(a) for a in operands],
        out_specs=[
            pl.BlockSpec((bb, d), lambda i: (i, 0)),
            pl.BlockSpec((bb, GLA_H, GLA_DK, GLA_DV), lambda i: (i, 0, 0, 0)),
            pl.BlockSpec((bb, SG_W), lambda i: (i, 0)),
        ] + [whole(s) for s in w_shapes],
        out_shape=[
            jax.ShapeDtypeStruct((rows, d), F32),
            jax.ShapeDtypeStruct(state.shape, F32),
            jax.ShapeDtypeStruct((rows, SG_W), F32),
        ] + [jax.ShapeDtypeStruct(s, BF16) for s in w_shapes],
        scratch_shapes=[
            pltpu.VMEM((bb, GLA_V), F32),
            pltpu.VMEM((EV_END, d), BF16),
            pltpu.VMEM((GLA_QK, GLA_RANK_PAD), BF16),
        ],
        compiler_params=pltpu.CompilerParams(
            dimension_semantics=("arbitrary",), vmem_limit_bytes=VMEM_LIMIT_BYTES),
        name="even_sample",
    )(x, state, *operands)


CONV_W = 512
CONV_K = 31
CONV_BUF = CONV_K - 1
CONV_PAD = 32
POOL_W = 512
POOL_WINDOWS = (2, 4, 8, 16)
POOL_DG = POOL_W // len(POOL_WINDOWS)
POOL_BUF = 15
POOL_PAD = 16
CONV_ROWS = 32
ODD_SUB = 128
ODD_COLS = 256


def _odd_prompt_body(x_ref, g_ref, win_ref, cw_ref, cb_ref, lng_ref, lnb_ref, pw_ref, ps_ref, wout_ref,
                     xo_ref, conv_ref, pool_ref, gbuf_ref, pbuf_ref, shift_ref, cwb_ref, psum_ref,
                     h_ref, mixin_ref, pooled_ref, raw_ref, convo_ref):
    j = pl.program_id(1)
    tt = x_ref.shape[0]
    assert POOL_WINDOWS == (2, 4, 8, 16)

    p0 = SUBLANES + POOL_PAD

    @pl.when(j == 0)
    def _():
        gbuf_ref[0:CONV_PAD, :] = jnp.zeros((CONV_PAD, CONV_W), F32)
        pbuf_ref[0:p0, :] = jnp.zeros((p0, POOL_W), F32)
        psum_ref[:, 0:SUBLANES, :] = jnp.zeros((psum_ref.shape[0], SUBLANES, POOL_W), F32)
        for o in range(CONV_K):
            cwb_ref[o] = jnp.broadcast_to(cw_ref[o:o + 1, :], (SUBLANES, CONV_W))

    h_ref[...] = _rms(x_ref[...], g_ref[...]).astype(BF16)
    first = CONV_PAD - CONV_BUF
    groups = CONV_ROWS // SUBLANES
    n_sub = tt // ODD_SUB

    def project(k):
        rows = slice(k * ODD_SUB, (k + 1) * ODD_SUB)

        def piece(c0):
            def run():
                raw_ref[rows, c0:c0 + ODD_COLS] = _dot(h_ref[rows, :], win_ref[:, c0:c0 + ODD_COLS])
            return run

        return [piece(c0) for c0 in range(0, 2 * CONV_W, ODD_COLS)]

    def glu(k):
        rows = slice(k * ODD_SUB, (k + 1) * ODD_SUB)
        gbuf_ref[CONV_PAD + k * ODD_SUB:CONV_PAD + (k + 1) * ODD_SUB, :] = (
            raw_ref[rows, 0:CONV_W] * jax.nn.sigmoid(raw_ref[rows, CONV_W:2 * CONV_W]))

    def mix(k):
        r_lo = k * ODD_SUB

        def shift_copies():
            lo = 0 if k == 0 else r_lo + CONV_PAD - SUBLANES
            hi = r_lo + ODD_SUB + CONV_PAD - SUBLANES
            for rr in range(1, SUBLANES):
                shift_ref[rr - 1, lo:hi, :] = gbuf_ref[rr + lo:rr + hi, :]

        def conv_block(r0):
            def run():
                accs = [jnp.zeros((SUBLANES, CONV_W), F32) + cb_ref[...] for _ in range(groups)]
                for o in range(first, first + CONV_K):
                    rr = o % SUBLANES
                    w8 = cwb_ref[o - first]
                    for gq in range(groups):
                        lo = o - rr + r0 + gq * SUBLANES
                        src = (gbuf_ref[lo:lo + SUBLANES, :] if rr == 0
                               else shift_ref[rr - 1, lo:lo + SUBLANES, :])
                        accs[gq] = accs[gq] + src * w8
                for gq in range(groups):
                    convo_ref[r0 + gq * SUBLANES:r0 + (gq + 1) * SUBLANES, :] = accs[gq]
            return run

        def norm_act():
            out_c = _layernorm(convo_ref[r_lo:r_lo + ODD_SUB, :], lng_ref[...], lnb_ref[...])
            mixin_ref[r_lo:r_lo + ODD_SUB, 0:CONV_W] = (out_c * jax.nn.sigmoid(out_c)).astype(BF16)

        def pooling():
            lo = SUBLANES if k == 0 else p0 + r_lo
            hi = p0 + r_lo + ODD_SUB
            psum_ref[0, lo:hi, :] = pbuf_ref[lo:hi, :] + pbuf_ref[lo - 1:hi - 1, :]
            psum_ref[1, lo:hi, POOL_DG:] = psum_ref[0, lo:hi, POOL_DG:] + psum_ref[0, lo - 2:hi - 2, POOL_DG:]
            psum_ref[2, lo:hi, 2 * POOL_DG:] = (psum_ref[1, lo:hi, 2 * POOL_DG:]
                                                + psum_ref[1, lo - 4:hi - 4, 2 * POOL_DG:])
            o_lo = p0 + r_lo
            o_hi = o_lo + ODD_SUB
            tots = [psum_ref[0, o_lo:o_hi, 0:POOL_DG],
                    psum_ref[1, o_lo:o_hi, POOL_DG:2 * POOL_DG],
                    psum_ref[2, o_lo:o_hi, 2 * POOL_DG:3 * POOL_DG],
                    psum_ref[2, o_lo:o_hi, 3 * POOL_DG:] + psum_ref[2, o_lo - 8:o_hi - 8, 3 * POOL_DG:]]
            t_glob = j * tt + r_lo + lax.broadcasted_iota(jnp.int32, (ODD_SUB, 1), 0)
            for gi, win_len in enumerate(POOL_WINDOWS):
                lanes = slice(gi * POOL_DG, (gi + 1) * POOL_DG)
                cnt = jnp.minimum(win_len, t_glob + 1).astype(F32)
                pooled = tots[gi] / cnt - pbuf_ref[o_lo:o_hi, lanes]
                pooled_ref[r_lo:r_lo + ODD_SUB, lanes] = pooled.astype(BF16)

        return ([shift_copies] + [conv_block(r0) for r0 in range(r_lo, r_lo + ODD_SUB, CONV_ROWS)]
                + [norm_act, pooling])

    def output(k):
        rows = slice(k * ODD_SUB, (k + 1) * ODD_SUB)

        def pool_matmuls():
            outs = [_dot(pooled_ref[rows, gi * POOL_DG:(gi + 1) * POOL_DG], pw_ref[gi])
                    for gi in range(len(POOL_WINDOWS))]
            mixin_ref[rows, CONV_W:] = (jnp.concatenate(outs, axis=1) * ps_ref[...]).astype(BF16)

        def out_piece(c0):
            def run():
                xo_ref[rows, c0:c0 + ODD_COLS] = x_ref[rows, c0:c0 + ODD_COLS] + _dot(
                    mixin_ref[rows, :], wout_ref[:, c0:c0 + ODD_COLS])
            return run

        return [pool_matmuls] + [out_piece(c0) for c0 in range(0, xo_ref.shape[1], ODD_COLS)]

    pbuf_ref[p0:p0 + tt, :] = _dot(h_ref[...], win_ref[:, 2 * CONV_W:])
    for task in project(0):
        task()
    glu(0)
    for k in range(n_sub):
        valu_tasks = mix(k)
        mxu_tasks = output(k - 1) if k >= 1 else []
        if k + 1 < n_sub:
            mxu_tasks = project(k + 1) + mxu_tasks
            valu_tasks = valu_tasks + [functools.partial(glu, k + 1)]
        for idx, task in enumerate(valu_tasks):
            lo_m = idx * len(mxu_tasks) // len(valu_tasks)
            hi_m = (idx + 1) * len(mxu_tasks) // len(valu_tasks)
            for m_task in mxu_tasks[lo_m:hi_m]:
                m_task()
            task()
    for task in output(n_sub - 1):
        task()

    tail_g = gbuf_ref[tt:tt + CONV_PAD, :]
    tail_p = pbuf_ref[SUBLANES + tt:p0 + tt, :]
    gbuf_ref[0:CONV_PAD, :] = tail_g
    pbuf_ref[SUBLANES:p0, :] = tail_p

    @pl.when(j == pl.num_programs(1) - 1)
    def _():
        conv_ref[0] = tail_g[CONV_PAD - CONV_BUF:, :]
        pool_ref[0] = tail_p[POOL_PAD - POOL_BUF:, :]


def _odd_prompt(x, batch, g, od, *, block_rows):
    rows, d = x.shape
    t = rows // batch
    tt = min(block_rows, t)
    nj = t // tt
    full = lambda a: pl.BlockSpec(a.shape, (lambda b, j: (0,) * a.ndim), pipeline_mode=pl.Buffered(1))
    operands = (g, od["w_in"], od["conv_w"], od["conv_b"], od["ln_g"], od["ln_b"], od["pool_w"],
                od["pool_scale"], od["w_out"])
    return pl.pallas_call(
        _odd_prompt_body,
        grid=(batch, nj),
        in_specs=[pl.BlockSpec((tt, d), lambda b, j: (b * nj + j, 0))] + [full(a) for a in operands],
        out_specs=[
            pl.BlockSpec((tt, d), lambda b, j: (b * nj + j, 0)),
            pl.BlockSpec((1, CONV_BUF, CONV_W), lambda b, j: (b, 0, 0)),
            pl.BlockSpec((1, POOL_BUF, POOL_W), lambda b, j: (b, 0, 0)),
        ],
        out_shape=[
            jax.ShapeDtypeStruct((rows, d), F32),
            jax.ShapeDtypeStruct((batch, CONV_BUF, CONV_W), F32),
            jax.ShapeDtypeStruct((batch, POOL_BUF, POOL_W), F32),
        ],
        scratch_shapes=[
            pltpu.VMEM((CONV_PAD + tt + SUBLANES, CONV_W), F32),
            pltpu.VMEM((SUBLANES + POOL_PAD + tt, POOL_W), F32),
            pltpu.VMEM((SUBLANES - 1, tt + CONV_PAD - SUBLANES, CONV_W), F32),
            pltpu.VMEM((CONV_K, SUBLANES, CONV_W), F32),
            pltpu.VMEM((3, SUBLANES + POOL_PAD + tt, POOL_W), F32),
            pltpu.VMEM((tt, d), BF16),
            pltpu.VMEM((tt, CONV_W + POOL_W), BF16),
            pltpu.VMEM((tt, POOL_W), BF16),
            pltpu.VMEM((tt, 2 * CONV_W), F32),
            pltpu.VMEM((tt, CONV_W), F32),
        ],
        compiler_params=pltpu.CompilerParams(
            dimension_semantics=("arbitrary", "arbitrary"), vmem_limit_bytes=VMEM_LIMIT_BYTES),
        name="odd_prompt",
    )(x, *operands)


def _odd_sample_body(x_ref, cbuf_ref, pbuf_ref, g_ref, win32_ref, cw_ref, cb_ref, lng_ref, lnb_ref, pw32_ref,
                     ps_ref, wout32_ref, xo_ref, conv_ref, pool_ref, win_ref, pw_ref, wout_ref):
    @pl.when(pl.program_id(0) == 0)
    def _():
        win_ref[...] = win32_ref[...].astype(BF16)
        pw_ref[...] = pw32_ref[...].astype(BF16)
        wout_ref[...] = wout32_ref[...].astype(BF16)

    x = x_ref[...]
    h = _rms(x, g_ref[...]).astype(BF16)
    a = _dot(h, win_ref[:, 0:CONV_W])
    gt = _dot(h, win_ref[:, CONV_W:2 * CONV_W])
    xp = _dot(h, win_ref[:, 2 * CONV_W:])
    glu = a * jax.nn.sigmoid(gt)
    conv = glu * cw_ref[CONV_BUF:CONV_K, :] + cb_ref[...]
    for jj in range(CONV_BUF):
        conv = conv + cbuf_ref[jj] * cw_ref[jj:jj + 1, :]
    out_c = _layernorm(conv, lng_ref[...], lnb_ref[...])
    out_c = out_c * jax.nn.sigmoid(out_c)
    outs = []
    for gi, win_len in enumerate(POOL_WINDOWS):
        lanes = slice(gi * POOL_DG, (gi + 1) * POOL_DG)
        tot = xp[:, lanes]
        for jj in range(POOL_BUF - (win_len - 1), POOL_BUF):
            tot = tot + pbuf_ref[jj, :, lanes]
        pooled = tot / float(win_len) - xp[:, lanes]
        outs.append(_dot(pooled.astype(BF16), pw_ref[gi]))
    out_d = jnp.concatenate(outs, axis=1) * ps_ref[...]
    y = _dot(jnp.concatenate([out_c, out_d], axis=1).astype(BF16), wout_ref[...])
    xo_ref[...] = x + y
    conv_ref[0:CONV_BUF - 1] = cbuf_ref[1:CONV_BUF]
    conv_ref[CONV_BUF - 1] = glu
    pool_ref[0:POOL_BUF - 1] = pbuf_ref[1:POOL_BUF]
    pool_ref[POOL_BUF - 1] = xp


def _odd_sample(x, conv_buf, pool_buf, g, od, *, block_rows=32):
    rows, d = x.shape
    bb = min(block_rows, rows)
    full = lambda a: pl.BlockSpec(a.shape, (lambda i: (0,) * a.ndim), pipeline_mode=pl.Buffered(1))
    whole = lambda shape: pl.BlockSpec(shape, (lambda i: (0,) * len(shape)))
    operands = (g, od["w_in"], od["conv_w"], od["conv_b"], od["ln_g"], od["ln_b"], od["pool_w"],
                od["pool_scale"], od["w_out"])
    w_shapes = [od["w_in"].shape, od["pool_w"].shape, od["w_out"].shape]
    return pl.pallas_call(
        _odd_sample_body,
        grid=(rows // bb,),
        in_specs=[pl.BlockSpec((bb, d), lambda i: (i, 0)),
                  pl.BlockSpec((CONV_BUF, bb, CONV_W), lambda i: (0, i, 0)),
                  pl.BlockSpec((POOL_BUF, bb, POOL_W), lambda i: (0, i, 0))]
                 + [full(a) for a in operands],
        out_specs=[
            pl.BlockSpec((bb, d), lambda i: (i, 0)),
            pl.BlockSpec((CONV_BUF, bb, CONV_W), lambda i: (0, i, 0)),
            pl.BlockSpec((POOL_BUF, bb, POOL_W), lambda i: (0, i, 0)),
        ] + [whole(s) for s in w_shapes],
        out_shape=[
            jax.ShapeDtypeStruct((rows, d), F32),
            jax.ShapeDtypeStruct(conv_buf.shape, F32),
            jax.ShapeDtypeStruct(pool_buf.shape, F32),
        ] + [jax.ShapeDtypeStruct(s, BF16) for s in w_shapes],
        compiler_params=pltpu.CompilerParams(
            dimension_semantics=("arbitrary",), vmem_limit_bytes=VMEM_LIMIT_BYTES),
        name="odd_sample",
    )(x, conv_buf, pool_buf, *operands)


def _prep_odd(w_in, conv_w, conv_b, ln_g, ln_b, pool_w, pool_scale, w_out):
    return dict(w_in=w_in, conv_w=conv_w, conv_b=conv_b.reshape(1, -1), ln_g=ln_g.reshape(1, -1),
                ln_b=ln_b.reshape(1, -1), pool_w=pool_w, pool_scale=pool_scale.reshape(1, -1), w_out=w_out)


def kernel(x_prompt, x_sample, state_gla, state_conv, state_pool, norm_g, ff_in, ff_out, ev_w_in, ev_w_gate, ev_b_gate, ev_gla_g, ev_sg_ln_g, ev_sg_ln_b, ev_sg_w, ev_sg_b, ev_w_out, od_w_in, od_conv_w, od_conv_b, od_ln_g, od_ln_b, od_pool_w, od_pool_scale, od_w_out, norm_f):
    bp, t, d = x_prompt.shape
    bs = x_sample.shape[0]
    depth = norm_g.shape[0]
    xp = x_prompt.reshape(bp * t, d)
    xs = x_sample.reshape(bs, d)
    gla_p, gla_s, sgv_p, sgv_s, conv_p, conv_s, pool_p, pool_s = [], [], [], [], [], [], [], []
    for layer in range(depth):
        i = layer // 2
        last = layer == depth - 1
        xs, xp = _ffn_fused(xs, xp, norm_g[layer, 0], ff_in, ff_out, layer, 0, norm_f, final_norm=False,
                            block_rows=FFN_ROWS)
        g_mix = norm_g[layer, 1].reshape(1, d)
        if layer % 2 == 0:
            ev = _prep_even(ev_b_gate[i], ev_gla_g[i], ev_sg_ln_g[i], ev_sg_ln_b[i], ev_sg_w[i], ev_sg_b[i])
            xs, s_s, v_s, w_in_b, w_gate_b, w_out_b, sg_w_b = _even_sample(
                xs, state_gla[i], g_mix, jnp.transpose(ev_w_in[i]), ev_w_gate[i], ev_w_out[i], ev_sg_w[i], ev)
            xp, s_p, v_p = _even_prompt(xp, bp, g_mix, w_in_b, w_gate_b, ev["b_gate"], ev["gla_g"],
                                        ev["ln_g"], ev["ln_b"], sg_w_b, ev["sg_bias"], w_out_b,
                                        block_rows=512)
            gla_p.append(s_p); gla_s.append(s_s); sgv_p.append(v_p); sgv_s.append(v_s.reshape(bs, 1, SG_W))
        else:
            od = _prep_odd(od_w_in[i], od_conv_w[i], od_conv_b[i], od_ln_g[i], od_ln_b[i], od_pool_w[i],
                           od_pool_scale[i], od_w_out[i])
            xs, c_s, p_s, w_in_b, pool_w_b, w_out_b = _odd_sample(
                xs, jnp.transpose(state_conv[i], (1, 0, 2)), jnp.transpose(state_pool[i], (1, 0, 2)), g_mix, od)
            c_s, p_s = jnp.transpose(c_s, (1, 0, 2)), jnp.transpose(p_s, (1, 0, 2))
            xp, c_p, p_p = _odd_prompt(xp, bp, g_mix, dict(od, w_in=w_in_b, pool_w=pool_w_b, w_out=w_out_b),
                                       block_rows=512)
            conv_p.append(c_p); conv_s.append(c_s); pool_p.append(p_p); pool_s.append(p_s)
        xs, xp = _ffn_fused(xs, xp, norm_g[layer, 2], ff_in, ff_out, layer, 1, norm_f, final_norm=last,
                            block_rows=FFN_ROWS)
    return (xp.reshape(bp, t, d), xs.reshape(bs, 1, d), jnp.stack(gla_p), jnp.stack(gla_s),
            jnp.stack(sgv_p), jnp.stack(sgv_s), jnp.stack(conv_p), jnp.stack(conv_s),
            jnp.stack(pool_p), jnp.stack(pool_s))
```

```python
import functools

import jax
import jax.numpy as jnp
import numpy as np
from jax import lax
from jax.experimental import pallas as pl
from jax.experimental.pallas import tpu as pltpu

EPS = 1e-6
LOG2_E = 1.4426950408889634
BF16 = jnp.bfloat16
F32 = jnp.float32

LANES = 128
SUBLANES = 8
VMEM_LIMIT_BYTES = 56 * 1024 * 1024
FFN_ROWS = 1024


def _rms(x, g):
    return x * lax.rsqrt(jnp.mean(x * x, axis=-1, keepdims=True) + EPS) * g


def _dot(a, b):
    return jnp.dot(a, b, preferred_element_type=F32)


FFN_CHUNK = 256


def _ffn_fused_body(xs_ref, xp_ref, g_ref, wa32_ref, wb32_ref, wo32_ref, gf_ref,
                    ys_ref, yp_ref, wa_ref, wb_ref, wo_ref, hs_ref, accs_ref, act_ref, hp_ref, accp_ref,
                    *, final_norm):
    s = pl.program_id(0)
    n_chunks = wa_ref.shape[0]

    def finish(x, acc):
        y = x + 0.5 * acc
        return _rms(y, gf_ref[...]) if final_norm else y

    @pl.when(s < n_chunks)
    def _():
        @pl.when(s == 0)
        def _():
            hs_ref[...] = _rms(xs_ref[...], g_ref[...]).astype(BF16)
            accs_ref[...] = jnp.zeros_like(accs_ref)
            hp_ref[...] = _rms(xp_ref[...], g_ref[...]).astype(BF16)
            accp_ref[...] = jnp.zeros_like(accp_ref)

        wa = wa32_ref[...].astype(BF16)
        wb = wb32_ref[...].astype(BF16)
        wo = wo32_ref[...].astype(BF16)
        wa_ref[s] = wa
        wb_ref[s] = wb
        wo_ref[s] = wo
        for h_ref, acc_ref in ((hs_ref, accs_ref), (hp_ref, accp_ref)):
            h = h_ref[...]
            a = _dot(h, wa)
            b = _dot(h, wb)
            acc_ref[...] += _dot((a * jax.nn.sigmoid(a) * b).astype(BF16), wo)

        @pl.when(s == n_chunks - 1)
        def _():
            ys_ref[...] = finish(xs_ref[...], accs_ref[...])
            yp_ref[...] = finish(xp_ref[...], accp_ref[...])

    @pl.when(s >= n_chunks)
    def _():
        x = xp_ref[...]
        h = _rms(x, g_ref[...]).astype(BF16)
        for c in range(n_chunks):
            a = _dot(h, wa_ref[c])
            b = _dot(h, wb_ref[c])
            act_ref[:, c * FFN_CHUNK:(c + 1) * FFN_CHUNK] = (a * jax.nn.sigmoid(a) * b).astype(BF16)
        w_out = wo_ref[...].reshape(n_chunks * FFN_CHUNK, wo_ref.shape[2])
        yp_ref[...] = finish(x, _dot(act_ref[...], w_out))


def _ffn_fused(x_sample, x_prompt, g, ff_in, ff_out, layer, slot, g_final, *, final_norm, block_rows):
    rows_s, d = x_sample.shape
    rows_p = x_prompt.shape[0]
    d_ff = ff_out.shape[2]
    n_chunks = d_ff // FFN_CHUNK
    tm = min(block_rows, rows_p)
    const = lambda s: (0, 0)
    chunk = lambda s: jnp.minimum(s, n_chunks - 1)
    tile = lambda s: (jnp.maximum(s - (n_chunks - 1), 0), 0)
    body = functools.partial(_ffn_fused_body, final_norm=final_norm)
    return pl.pallas_call(
        body,
        grid=(n_chunks - 1 + rows_p // tm,),
        in_specs=[
            pl.BlockSpec((rows_s, d), const),
            pl.BlockSpec((tm, d), tile),
            pl.BlockSpec((1, d), const),
            pl.BlockSpec((None, None, d, FFN_CHUNK), lambda s: (layer, slot, 0, chunk(s))),
            pl.BlockSpec((None, None, d, FFN_CHUNK), lambda s: (layer, slot, 0, n_chunks + chunk(s))),
            pl.BlockSpec((None, None, FFN_CHUNK, d), lambda s: (layer, slot, chunk(s), 0)),
            pl.BlockSpec((1, d), const),
        ],
        out_specs=[pl.BlockSpec((rows_s, d), const), pl.BlockSpec((tm, d), tile)],
        out_shape=[jax.ShapeDtypeStruct((rows_s, d), F32), jax.ShapeDtypeStruct((rows_p, d), F32)],
        scratch_shapes=[
            pltpu.VMEM((n_chunks, d, FFN_CHUNK), BF16),
            pltpu.VMEM((n_chunks, d, FFN_CHUNK), BF16),
            pltpu.VMEM((n_chunks, FFN_CHUNK, d), BF16),
            pltpu.VMEM((rows_s, d), BF16),
            pltpu.VMEM((rows_s, d), F32),
            pltpu.VMEM((tm, d_ff), BF16),
            pltpu.VMEM((tm, d), BF16),
            pltpu.VMEM((tm, d), F32),
        ],
        compiler_params=pltpu.CompilerParams(
            dimension_semantics=("arbitrary",), vmem_limit_bytes=VMEM_LIMIT_BYTES),
        name="ffn_fused",
    )(x_sample, x_prompt, g.reshape(1, d), ff_in, ff_in, ff_out, g_final.reshape(1, d))


GLA_H = 4
GLA_DK = 64
GLA_DV = 128
GLA_QK = GLA_H * GLA_DK
GLA_V = GLA_H * GLA_DV
GLA_RANK_PAD = 128
GLA_INV_TAU = 1.0 / 16.0
GLA_CH = 16
GLA_SEL_ROWS = 256
SG_H = 4
SG_DH = 128
SG_W = SG_H * SG_DH
SG_CHUNK = 128
EV_Q, EV_V, EV_R, EV_U, EV_VS, EV_Z, EV_END = 0, 512, 1024, 1536, 2048, 2560, 2688


def _log_sigmoid(x):
    return jnp.minimum(x, 0.0) - jnp.log(1.0 + jnp.exp(-jnp.abs(x)))


def _split3(x):
    hi = x.astype(BF16)
    r1 = x - hi.astype(F32)
    mid = r1.astype(BF16)
    lo = (r1 - mid.astype(F32)).astype(BF16)
    return hi, mid, lo


def _dot3(sel, parts):
    return _dot(sel, parts[0]) + _dot(sel, parts[1]) + _dot(sel, parts[2])


def _head_masks(width, per_head, dtype):
    lane = lax.broadcasted_iota(jnp.int32, (1, width), 1)
    return [jnp.where(lane // per_head == h, 1.0, 0.0).astype(dtype) for h in range(width // per_head)]


def _layernorm(x, g, b):
    mu = jnp.mean(x, axis=-1, keepdims=True)
    xc = x - mu
    var = jnp.mean(xc * xc, axis=-1, keepdims=True)
    return xc * lax.rsqrt(var + EPS) * g + b


def _even_prompt_body(x_ref, g_ref, win_ref, wgate_ref, bgate_ref, glag_ref, lng_ref, lnb_ref,
                      sgw_ref, sgb_ref, e_ref, wout_ref,
                      xo_ref, gla_ref, sgv_ref,
                      st_ref, q_ref, k_ref, cum2_ref, qin_ref, kd_ref, dec_ref, v_ref, o_ref,
                      pcat_ref, acat_ref, mix_ref, add_ref, stb_ref, ruv_ref):
    j = pl.program_id(1)
    tt = x_ref.shape[0]
    n_chunks = tt // GLA_CH

    @pl.when(j == 0)
    def _():
        st_ref[...] = jnp.zeros_like(st_ref)

    x = x_ref[...]
    h = _rms(x, g_ref[...]).astype(BF16)

    qk = _dot(h, win_ref[:, EV_Q:EV_V])
    q = qk[:, :GLA_QK] * (GLA_DK ** -0.5)
    k = qk[:, GLA_QK:]
    v_ref[...] = _dot(h, win_ref[:, EV_V:EV_R]).astype(BF16)
    z = _dot(h, win_ref[:, EV_Z:EV_END]).astype(BF16)
    log_a = _log_sigmoid(_dot(z, wgate_ref[...]) + bgate_ref[...]) * GLA_INV_TAU

    sel_rows = min(tt, GLA_SEL_ROWS)
    row = lax.broadcasted_iota(jnp.int32, (sel_rows, sel_rows), 0)
    col = lax.broadcasted_iota(jnp.int32, (sel_rows, sel_rows), 1)
    same = (row // GLA_CH) == (col // GLA_CH)
    sel_cum = jnp.where(same & (col <= row), 1.0, 0.0).astype(BF16)
    parts = _split3(log_a)
    groups_of_rows = [slice(r0, r0 + sel_rows) for r0 in range(0, tt, sel_rows)]
    cum = jnp.concatenate([_dot3(sel_cum, [p[rs] for p in parts]) for rs in groups_of_rows], axis=0)
    tot = jnp.concatenate(
        [jnp.broadcast_to(cum[(n + 1) * GLA_CH - 1:(n + 1) * GLA_CH, :], (GLA_CH, GLA_QK)) for n in range(n_chunks)],
        axis=0)
    q_ref[...] = q
    k_ref[...] = k
    cum2_ref[...] = cum * LOG2_E
    qin_ref[...] = (q * jnp.exp(cum)).astype(BF16)
    kd_ref[...] = (k * jnp.exp(tot - cum)).astype(BF16)
    dec_ref[...] = jnp.exp(tot)

    half = GLA_CH // 2
    proj_block = 256
    proj_cols = list(range(EV_R, EV_Z, proj_block))
    proj_every = n_chunks // len(proj_cols)
    assert proj_every >= 1
    for n in range(n_chunks):
        if n % proj_every == proj_every - 1 and n // proj_every < len(proj_cols):
            c0 = proj_cols[n // proj_every]
            ruv_ref[:, c0 - EV_R:c0 - EV_R + proj_block] = _dot(h, win_ref[:, c0:c0 + proj_block])
        base = n * GLA_CH
        qb = q_ref[base:base + GLA_CH, :]
        cb = cum2_ref[base:base + GLA_CH, :]
        for s in range(GLA_CH):
            ks = k_ref[base + s:base + s + 1, :]
            cs = cum2_ref[base + s:base + s + 1, :]
            if s < half:
                p = qb * ks * jnp.exp2(jnp.minimum(cb - cs, 0.0))
            else:
                p_hi = qb[half:] * ks * jnp.exp2(jnp.minimum(cb[half:] - cs, 0.0))
                p = jnp.concatenate([jnp.zeros_like(p_hi), p_hi], axis=0)
            pcat_ref[base:base + GLA_CH, s * GLA_QK:(s + 1) * GLA_QK] = p.astype(BF16)
    scores = _dot(pcat_ref[...], e_ref[...])

    u = jax.nn.gelu(ruv_ref[:, EV_U - EV_R:EV_VS - EV_R])
    v_ln = _layernorm(jax.nn.gelu(ruv_ref[:, EV_VS - EV_R:EV_Z - EV_R]), lng_ref[...], lnb_ref[...])
    v_lb = v_ln.astype(BF16)

    t_loc = lax.broadcasted_iota(jnp.int32, scores.shape, 0) % GLA_CH
    c_idx = lax.broadcasted_iota(jnp.int32, scores.shape, 1)
    causal = (c_idx % GLA_CH <= t_loc) & (c_idx < GLA_H * GLA_CH)
    acat_ref[...] = jnp.where(causal, scores, 0.0).astype(BF16)

    qk_masks = _head_masks(GLA_QK, GLA_DK, BF16)
    v_masks = _head_masks(GLA_V, GLA_DV, BF16)

    for n in range(n_chunks):
        rows = slice(n * GLA_CH, (n + 1) * GLA_CH)
        kn = kd_ref[rows, :]
        vn = v_ref[rows, :]
        lk = jnp.concatenate([kn * m for m in qk_masks], axis=0)
        vs = jnp.concatenate([vn[:, hh * GLA_DV:(hh + 1) * GLA_DV] for hh in range(GLA_H)], axis=0)
        add_ref[n] = lax.dot_general(vs, lk, (((0,), (0,)), ((), ())), preferred_element_type=F32)
    r = ruv_ref[:, 0:EV_U - EV_R]
    gate_r = r * jax.nn.sigmoid(r)
    sg_chunks = tt // SG_CHUNK
    sg_every = n_chunks // SG_H
    st = st_ref[...]
    for n in range(n_chunks):
        stb_ref[n] = st.astype(BF16).T
        st = st * dec_ref[n * GLA_CH:n * GLA_CH + 1, :] + add_ref[n]
        if n % sg_every == 0 and n // sg_every < SG_H:
            hh = n // sg_every
            cols = slice(hh * SG_DH, (hh + 1) * SG_DH)
            mixed = _dot(sgw_ref[hh], jnp.concatenate(
                [v_lb[c * SG_CHUNK:(c + 1) * SG_CHUNK, cols] for c in range(sg_chunks)], axis=1))
            for c in range(sg_chunks):
                mix_ref[c * SG_CHUNK:(c + 1) * SG_CHUNK, cols] = mixed[:, c * SG_DH:(c + 1) * SG_DH]
    st_ref[...] = st
    assert sg_every >= 1
    for n in range(n_chunks):
        rows = slice(n * GLA_CH, (n + 1) * GLA_CH)
        qn = qin_ref[rows, :]
        vn = v_ref[rows, :]
        lq = jnp.concatenate([qn * m for m in qk_masks], axis=0)
        oi = _dot(lq, stb_ref[n])
        o_inter = jnp.concatenate([oi[hh * GLA_CH:(hh + 1) * GLA_CH] for hh in range(GLA_H)], axis=1)
        vbd = jnp.concatenate([vn * m for m in v_masks], axis=0)
        o_ref[rows, :] = o_inter + _dot(acat_ref[rows, 0:GLA_H * GLA_CH], vbd)

    o = o_ref[...]
    o_n = jnp.concatenate(
        [_rms(o[:, hh * GLA_DV:(hh + 1) * GLA_DV], 1.0) for hh in range(GLA_H)], axis=1) * glag_ref[...]
    out_a = o_n * gate_r
    bias = jnp.concatenate([sgb_ref[...]] * (tt // SG_CHUNK), axis=0)
    out_b = u * (mix_ref[...] + bias)

    y = _dot(jnp.concatenate([out_a, out_b], axis=1).astype(BF16), wout_ref[...])
    xo_ref[...] = x + y

    @pl.when(j == pl.num_programs(1) - 1)
    def _():
        sgv_ref[0] = v_ln[tt - SG_CHUNK:, :]
        gla_ref[0] = st_ref[...].T.reshape(GLA_H, GLA_DK, GLA_DV)


def _score_sum_matrix():
    r = np.arange(GLA_CH * GLA_QK)
    c = np.arange(LANES)
    s, hh = r // GLA_QK, (r % GLA_QK) // GLA_DK
    return jnp.asarray((c[None, :] == (hh * GLA_CH + s)[:, None]).astype(np.float32), dtype=BF16)


def _even_prompt(x, batch, g, w_in, w_gate, b_gate, gla_g, ln_g, ln_b, sg_w, sg_bias, w_out, *, block_rows):
    rows, d = x.shape
    t = rows // batch
    tt = min(block_rows, t)
    nj = t // tt
    full = lambda a: pl.BlockSpec(a.shape, (lambda b, j: (0,) * a.ndim), pipeline_mode=pl.Buffered(1))
    e = _score_sum_matrix()
    operands = (g, w_in, w_gate, b_gate, gla_g, ln_g, ln_b, sg_w, sg_bias, e, w_out)
    return pl.pallas_call(
        _even_prompt_body,
        grid=(batch, nj),
        in_specs=[pl.BlockSpec((tt, d), lambda b, j: (b * nj + j, 0))] + [full(a) for a in operands],
        out_specs=[
            pl.BlockSpec((tt, d), lambda b, j: (b * nj + j, 0)),
            pl.BlockSpec((1, GLA_H, GLA_DK, GLA_DV), lambda b, j: (b, 0, 0, 0)),
            pl.BlockSpec((1, SG_CHUNK, SG_W), lambda b, j: (b, 0, 0)),
        ],
        out_shape=[
            jax.ShapeDtypeStruct((rows, d), F32),
            jax.ShapeDtypeStruct((batch, GLA_H, GLA_DK, GLA_DV), F32),
            jax.ShapeDtypeStruct((batch, SG_CHUNK, SG_W), F32),
        ],
        scratch_shapes=[
            pltpu.VMEM((GLA_DV, GLA_QK), F32),
            pltpu.VMEM((tt, GLA_QK), F32),
            pltpu.VMEM((tt, GLA_QK), F32),
            pltpu.VMEM((tt, GLA_QK), F32),
            pltpu.VMEM((tt, GLA_QK), BF16),
            pltpu.VMEM((tt, GLA_QK), BF16),
            pltpu.VMEM((tt, GLA_QK), F32),
            pltpu.VMEM((tt, GLA_V), BF16),
            pltpu.VMEM((tt, GLA_V), F32),
            pltpu.VMEM((tt, GLA_CH * GLA_QK), BF16),
            pltpu.VMEM((tt, LANES), BF16),
            pltpu.VMEM((tt, SG_W), F32),
            pltpu.VMEM((tt // GLA_CH, GLA_DV, GLA_QK), F32),
            pltpu.VMEM((tt // GLA_CH, GLA_QK, GLA_DV), BF16),
            pltpu.VMEM((tt, EV_Z - EV_R), F32),
        ],
        compiler_params=pltpu.CompilerParams(
            dimension_semantics=("arbitrary", "arbitrary"), vmem_limit_bytes=VMEM_LIMIT_BYTES),
        name="even_prompt",
    )(x, *operands)


def _prep_even(b_gate, gla_g, ln_g, ln_b, sg_w, sg_b):
    return dict(
        b_gate=b_gate.reshape(1, -1), b_gate_col=b_gate.reshape(-1, 1), gla_g=gla_g.reshape(1, -1),
        ln_g=ln_g.reshape(1, -1), ln_b=ln_b.reshape(1, -1),
        sg_bias=jnp.repeat(jnp.transpose(sg_b), SG_DH, axis=1),
        sg_w0=jnp.repeat(sg_w[:, 0, 0], SG_DH).reshape(1, -1), sg_b0=jnp.repeat(sg_b[:, 0], SG_DH).reshape(1, -1))


def _even_sample_body(x_ref, s_ref, g_ref, win32_ref, wgate32_ref, wout32_ref, sgw32_ref, bgate_ref,
                      bgatec_ref, glag_ref, lng_ref, lnb_ref, sgw0_ref, sgb0_ref,
                      xo_ref, so_ref, sgv_ref, win_ref, wgate_ref, wout_ref, sgw_ref,
                      o_ref, wint_ref, wgatet_ref):
    bb = x_ref.shape[0]

    @pl.when(pl.program_id(0) == 0)
    def _():
        z0 = 2 * GLA_QK + 2 * GLA_V
        rank = wgate32_ref.shape[0]
        wint_ref[0:z0] = win32_ref[0:z0].astype(BF16)
        wint_ref[z0:EV_Z] = win32_ref[z0 + rank:].astype(BF16)
        wint_ref[EV_Z:EV_Z + rank] = win32_ref[z0:z0 + rank].astype(BF16)
        wint_ref[EV_Z + rank:EV_END] = jnp.zeros((EV_END - EV_Z - rank, wint_ref.shape[1]), BF16)
        for c0 in range(0, EV_END, LANES):
            win_ref[:, c0:c0 + LANES] = wint_ref[c0:c0 + LANES, :].T
        wgate_ref[...] = jnp.concatenate(
            [wgate32_ref[...].astype(BF16), jnp.zeros((GLA_RANK_PAD - rank, GLA_QK), BF16)], axis=0)
        wgatet_ref[...] = wgate_ref[...].T
        wout_ref[...] = wout32_ref[...].astype(BF16)
        row = lax.broadcasted_iota(jnp.int32, (SG_CHUNK, SG_CHUNK), 0)
        col = lax.broadcasted_iota(jnp.int32, (SG_CHUNK, SG_CHUNK), 1)
        for hh in range(SG_H):
            sgw_ref[hh] = jnp.where(col <= row, sgw32_ref[hh], 0.0).astype(BF16)

    x = x_ref[...]
    h = _rms(x, g_ref[...]).astype(BF16)
    nt = (((1,), (1,)), ((), ()))
    qk_t = lax.dot_general(wint_ref[EV_Q:EV_V], h, nt, preferred_element_type=F32)
    q_t = qk_t[:GLA_QK] * (GLA_DK ** -0.5)
    k_t = qk_t[GLA_QK:].astype(BF16).astype(F32)
    z_t = lax.dot_general(wint_ref[EV_Z:EV_END], h, nt, preferred_element_type=F32).astype(BF16)
    a_t = jnp.exp(_log_sigmoid(_dot(wgatet_ref[...], z_t) + bgatec_ref[...]) * GLA_INV_TAU)
    v = _dot(h, win_ref[:, EV_V:EV_R])
    v_r = v.astype(BF16).astype(F32)
    for b in range(bb):
        s_old = s_ref[b].reshape(GLA_QK, GLA_DV)
        a_c = jnp.broadcast_to(a_t[:, b:b + 1], (GLA_QK, GLA_DV))
        k_c = jnp.broadcast_to(k_t[:, b:b + 1], (GLA_QK, GLA_DV))
        q_c = jnp.broadcast_to(q_t[:, b:b + 1], (GLA_QK, GLA_DV))
        v_rows = jnp.concatenate(
            [jnp.broadcast_to(v_r[b:b + 1, hh * GLA_DV:(hh + 1) * GLA_DV], (GLA_DK, GLA_DV))
             for hh in range(GLA_H)], axis=0)
        s_new = a_c * s_old + k_c * v_rows
        so_ref[b] = s_new.reshape(GLA_H, GLA_DK, GLA_DV)
        ob = jnp.sum((q_c * s_new).reshape(GLA_H, GLA_DK, GLA_DV), axis=1)
        o_ref[b:b + 1, :] = jnp.concatenate([ob[hh:hh + 1] for hh in range(GLA_H)], axis=1)
    o = o_ref[...]
    o_n = jnp.concatenate(
        [_rms(o[:, hh * GLA_DV:(hh + 1) * GLA_DV], 1.0) for hh in range(GLA_H)], axis=1) * glag_ref[...]
    r = _dot(h, win_ref[:, EV_R:EV_U])
    out_a = o_n * (r * jax.nn.sigmoid(r))
    u = jax.nn.gelu(_dot(h, win_ref[:, EV_U:EV_VS]))
    v_ln = _layernorm(jax.nn.gelu(_dot(h, win_ref[:, EV_VS:EV_Z])), lng_ref[...], lnb_ref[...])
    sgv_ref[...] = v_ln
    out_b = u * (sgw0_ref[...] * v_ln + sgb0_ref[...])
    y = _dot(jnp.concatenate([out_a, out_b], axis=1).astype(BF16), wout_ref[...])
    xo_ref[...] = x + y


def _even_sample(x, state, g, w_in_t, w_gate, w_out, sg_w, ev, *, block_rows=32):
    rows, d = x.shape
    bb = min(block_rows, rows)
    full = lambda a: pl.BlockSpec(a.shape, (lambda i: (0,) * a.ndim), pipeline_mode=pl.Buffered(1))
    whole = lambda shape: pl.BlockSpec(shape, (lambda i: (0,) * len(shape)))
    operands = (g, w_in_t, w_gate, w_out, sg_w, ev["b_gate"], ev["b_gate_col"], ev["gla_g"], ev["ln_g"],
                ev["ln_b"], ev["sg_w0"], ev["sg_b0"])
    w_shapes = [(d, EV_END), (GLA_RANK_PAD, GLA_QK), w_out.shape, sg_w.shape]
    return pl.pallas_call(
        _even_sample_body,
        grid=(rows // bb,),
        in_specs=[pl.BlockSpec((bb, d), lambda i: (i, 0)),
                  pl.BlockSpec((bb, GLA_H, GLA_DK, GLA_DV), lambda i: (i, 0, 0, 0))]
                 + [full(a) for a in operands],
        out_specs=[
            pl.BlockSpec((bb, d), lambda i: (i, 0)),
            pl.BlockSpec((bb, GLA_H, GLA_DK, GLA_DV), lambda i: (i, 0, 0, 0)),
            pl.BlockSpec((bb, SG_W), lambda i: (i, 0)),
        ] + [whole(s) for s in w_shapes],
        out_shape=[
            jax.ShapeDtypeStruct((rows, d), F32),
            jax.ShapeDtypeStruct(state.shape, F32),
            jax.ShapeDtypeStruct((rows, SG_W), F32),
        ] + [jax.ShapeDtypeStruct(s, BF16) for s in w_shapes],
        scratch_shapes=[
            pltpu.VMEM((bb, GLA_V), F32),
            pltpu.VMEM((EV_END, d), BF16),
            pltpu.VMEM((GLA_QK, GLA_RANK_PAD), BF16),
        ],
        compiler_params=pltpu.CompilerParams(
            dimension_semantics=("arbitrary",), vmem_limit_bytes=VMEM_LIMIT_BYTES),
        name="even_sample",
    )(x, state, *operands)


CONV_W = 512
CONV_K = 31
CONV_BUF = CONV_K - 1
CONV_PAD = 32
POOL_W = 512
POOL_WINDOWS = (2, 4, 8, 16)
POOL_DG = POOL_W // len(POOL_WINDOWS)
POOL_BUF = 15
POOL_PAD = 16
CONV_ROWS = 32
ODD_SUB = 128
ODD_COLS = 256


def _odd_prompt_body(x_ref, g_ref, win_ref, cw_ref, cb_ref, lng_ref, lnb_ref, pw_ref, ps_ref, wout_ref,
                     xo_ref, conv_ref, pool_ref, gbuf_ref, pbuf_ref, shift_ref, cwb_ref, psum_ref,
                     h_ref, mixin_ref, pooled_ref, raw_ref, convo_ref):
    j = pl.program_id(1)
    tt = x_ref.shape[0]
    assert POOL_WINDOWS == (2, 4, 8, 16)

    p0 = SUBLANES + POOL_PAD

    @pl.when(j == 0)
    def _():
        gbuf_ref[0:CONV_PAD, :] = jnp.zeros((CONV_PAD, CONV_W), F32)
        pbuf_ref[0:p0, :] = jnp.zeros((p0, POOL_W), F32)
        psum_ref[:, 0:SUBLANES, :] = jnp.zeros((psum_ref.shape[0], SUBLANES, POOL_W), F32)
        for o in range(CONV_K):
            cwb_ref[o] = jnp.broadcast_to(cw_ref[o:o + 1, :], (SUBLANES, CONV_W))

    h_ref[...] = _rms(x_ref[...], g_ref[...]).astype(BF16)
    first = CONV_PAD - CONV_BUF
    groups = CONV_ROWS // SUBLANES
    n_sub = tt // ODD_SUB

    def project(k):
        rows = slice(k * ODD_SUB, (k + 1) * ODD_SUB)

        def piece(c0):
            def run():
                raw_ref[rows, c0:c0 + ODD_COLS] = _dot(h_ref[rows, :], win_ref[:, c0:c0 + ODD_COLS])
            return run

        return [piece(c0) for c0 in range(0, 2 * CONV_W, ODD_COLS)]

    def glu(k):
        rows = slice(k * ODD_SUB, (k + 1) * ODD_SUB)
        gbuf_ref[CONV_PAD + k * ODD_SUB:CONV_PAD + (k + 1) * ODD_SUB, :] = (
            raw_ref[rows, 0:CONV_W] * jax.nn.sigmoid(raw_ref[rows, CONV_W:2 * CONV_W]))

    def mix(k):
        r_lo = k * ODD_SUB

        def shift_copies():
            lo = 0 if k == 0 else r_lo + CONV_PAD - SUBLANES
            hi = r_lo + ODD_SUB + CONV_PAD - SUBLANES
            for rr in range(1, SUBLANES):
                shift_ref[rr - 1, lo:hi, :] = gbuf_ref[rr + lo:rr + hi, :]

        def conv_block(r0):
            def run():
                accs = [jnp.zeros((SUBLANES, CONV_W), F32) + cb_ref[...] for _ in range(groups)]
                for o in range(first, first + CONV_K):
                    rr = o % SUBLANES
                    w8 = cwb_ref[o - first]
                    for gq in range(groups):
                        lo = o - rr + r0 + gq * SUBLANES
                        src = (gbuf_ref[lo:lo + SUBLANES, :] if rr == 0
                               else shift_ref[rr - 1, lo:lo + SUBLANES, :])
                        accs[gq] = accs[gq] + src * w8
                for gq in range(groups):
                    convo_ref[r0 + gq * SUBLANES:r0 + (gq + 1) * SUBLANES, :] = accs[gq]
            return run

        def norm_act():
            out_c = _layernorm(convo_ref[r_lo:r_lo + ODD_SUB, :], lng_ref[...], lnb_ref[...])
            mixin_ref[r_lo:r_lo + ODD_SUB, 0:CONV_W] = (out_c * jax.nn.sigmoid(out_c)).astype(BF16)

        def pooling():
            lo = SUBLANES if k == 0 else p0 + r_lo
            hi = p0 + r_lo + ODD_SUB
            psum_ref[0, lo:hi, :] = pbuf_ref[lo:hi, :] + pbuf_ref[lo - 1:hi - 1, :]
            psum_ref[1, lo:hi, POOL_DG:] = psum_ref[0, lo:hi, POOL_DG:] + psum_ref[0, lo - 2:hi - 2, POOL_DG:]
            psum_ref[2, lo:hi, 2 * POOL_DG:] = (psum_ref[1, lo:hi, 2 * POOL_DG:]
                                                + psum_ref[1, lo - 4:hi - 4, 2 * POOL_DG:])
            o_lo = p0 + r_lo
            o_hi = o_lo + ODD_SUB
            tots = [psum_ref[0, o_lo:o_hi, 0:POOL_DG],
                    psum_ref[1, o_lo:o_hi, POOL_DG:2 * POOL_DG],
                    psum_ref[2, o_lo:o_hi, 2 * POOL_DG:3 * POOL_DG],
                    psum_ref[2, o_lo:o_hi, 3 * POOL_DG:] + psum_ref[2, o_lo - 8:o_hi - 8, 3 * POOL_DG:]]
            t_glob = j * tt + r_lo + lax.broadcasted_iota(jnp.int32, (ODD_SUB, 1), 0)
            for gi, win_len in enumerate(POOL_WINDOWS):
                lanes = slice(gi * POOL_DG, (gi + 1) * POOL_DG)
                cnt = jnp.minimum(win_len, t_glob + 1).astype(F32)
                pooled = tots[gi] / cnt - pbuf_ref[o_lo:o_hi, lanes]
                pooled_ref[r_lo:r_lo + ODD_SUB, lanes] = pooled.astype(BF16)

        return ([shift_copies] + [conv_block(r0) for r0 in range(r_lo, r_lo + ODD_SUB, CONV_ROWS)]
                + [norm_act, pooling])

    def output(k):
        rows = slice(k * ODD_SUB, (k + 1) * ODD_SUB)

        def pool_matmuls():
            outs = [_dot(pooled_ref[rows, gi * POOL_DG:(gi + 1) * POOL_DG], pw_ref[gi])
                    for gi in range(len(POOL_WINDOWS))]
            mixin_ref[rows, CONV_W:] = (jnp.concatenate(outs, axis=1) * ps_ref[...]).astype(BF16)

        def out_piece(c0):
            def run():
                xo_ref[rows, c0:c0 + ODD_COLS] = x_ref[rows, c0:c0 + ODD_COLS] + _dot(
                    mixin_ref[rows, :], wout_ref[:, c0:c0 + ODD_COLS])
            return run

        return [pool_matmuls] + [out_piece(c0) for c0 in range(0, xo_ref.shape[1], ODD_COLS)]

    pbuf_ref[p0:p0 + tt, :] = _dot(h_ref[...], win_ref[:, 2 * CONV_W:])
    for task in project(0):
        task()
    glu(0)
    for k in range(n_sub):
        valu_tasks = mix(k)
        mxu_tasks = output(k - 1) if k >= 1 else []
        if k + 1 < n_sub:
            mxu_tasks = project(k + 1) + mxu_tasks
            valu_tasks = valu_tasks + [functools.partial(glu, k + 1)]
        for idx, task in enumerate(valu_tasks):
            lo_m = idx * len(mxu_tasks) // len(valu_tasks)
            hi_m = (idx + 1) * len(mxu_tasks) // len(valu_tasks)
            for m_task in mxu_tasks[lo_m:hi_m]:
                m_task()
            task()
    for task in output(n_sub - 1):
        task()

    tail_g = gbuf_ref[tt:tt + CONV_PAD, :]
    tail_p = pbuf_ref[SUBLANES + tt:p0 + tt, :]
    gbuf_ref[0:CONV_PAD, :] = tail_g
    pbuf_ref[SUBLANES:p0, :] = tail_p

    @pl.when(j == pl.num_programs(1) - 1)
    def _():
        conv_ref[0] = tail_g[CONV_PAD - CONV_BUF:, :]
        pool_ref[0] = tail_p[POOL_PAD - POOL_BUF:, :]


def _odd_prompt(x, batch, g, od, *, block_rows):
    rows, d = x.shape
    t = rows // batch
    tt = min(block_rows, t)
    nj = t // tt
    full = lambda a: pl.BlockSpec(a.shape, (lambda b, j: (0,) * a.ndim), pipeline_mode=pl.Buffered(1))
    operands = (g, od["w_in"], od["conv_w"], od["conv_b"], od["ln_g"], od["ln_b"], od["pool_w"],
                od["pool_scale"], od["w_out"])
    return pl.pallas_call(
        _odd_prompt_body,
        grid=(batch, nj),
        in_specs=[pl.BlockSpec((tt, d), lambda b, j: (b * nj + j, 0))] + [full(a) for a in operands],
        out_specs=[
            pl.BlockSpec((tt, d), lambda b, j: (b * nj + j, 0)),
            pl.BlockSpec((1, CONV_BUF, CONV_W), lambda b, j: (b, 0, 0)),
            pl.BlockSpec((1, POOL_BUF, POOL_W), lambda b, j: (b, 0, 0)),
        ],
        out_shape=[
            jax.ShapeDtypeStruct((rows, d), F32),
            jax.ShapeDtypeStruct((batch, CONV_BUF, CONV_W), F32),
            jax.ShapeDtypeStruct((batch, POOL_BUF, POOL_W), F32),
        ],
        scratch_shapes=[
            pltpu.VMEM((CONV_PAD + tt + SUBLANES, CONV_W), F32),
            pltpu.VMEM((SUBLANES + POOL_PAD + tt, POOL_W), F32),
            pltpu.VMEM((SUBLANES - 1, tt + CONV_PAD - SUBLANES, CONV_W), F32),
            pltpu.VMEM((CONV_K, SUBLANES, CONV_W), F32),
            pltpu.VMEM((3, SUBLANES + POOL_PAD + tt, POOL_W), F32),
            pltpu.VMEM((tt, d), BF16),
            pltpu.VMEM((tt, CONV_W + POOL_W), BF16),
            pltpu.VMEM((tt, POOL_W), BF16),
            pltpu.VMEM((tt, 2 * CONV_W), F32),
            pltpu.VMEM((tt, CONV_W), F32),
        ],
        compiler_params=pltpu.CompilerParams(
            dimension_semantics=("arbitrary", "arbitrary"), vmem_limit_bytes=VMEM_LIMIT_BYTES),
        name="odd_prompt",
    )(x, *operands)


def _odd_sample_body(x_ref, cbuf_ref, pbuf_ref, g_ref, win32_ref, cw_ref, cb_ref, lng_ref, lnb_ref, pw32_ref,
                     ps_ref, wout32_ref, xo_ref, conv_ref, pool_ref, win_ref, pw_ref, wout_ref):
    @pl.when(pl.program_id(0) == 0)
    def _():
        win_ref[...] = win32_ref[...].astype(BF16)
        pw_ref[...] = pw32_ref[...].astype(BF16)
        wout_ref[...] = wout32_ref[...].astype(BF16)

    x = x_ref[...]
    h = _rms(x, g_ref[...]).astype(BF16)
    a = _dot(h, win_ref[:, 0:CONV_W])
    gt = _dot(h, win_ref[:, CONV_W:2 * CONV_W])
    xp = _dot(h, win_ref[:, 2 * CONV_W:])
    glu = a * jax.nn.sigmoid(gt)
    conv = glu * cw_ref[CONV_BUF:CONV_K, :] + cb_ref[...]
    for jj in range(CONV_BUF):
        conv = conv + cbuf_ref[jj] * cw_ref[jj:jj + 1, :]
    out_c = _layernorm(conv, lng_ref[...], lnb_ref[...])
    out_c = out_c * jax.nn.sigmoid(out_c)
    outs = []
    for gi, win_len in enumerate(POOL_WINDOWS):
        lanes = slice(gi * POOL_DG, (gi + 1) * POOL_DG)
        tot = xp[:, lanes]
        for jj in range(POOL_BUF - (win_len - 1), POOL_BUF):
            tot = tot + pbuf_ref[jj, :, lanes]
        pooled = tot / float(win_len) - xp[:, lanes]
        outs.append(_dot(pooled.astype(BF16), pw_ref[gi]))
    out_d = jnp.concatenate(outs, axis=1) * ps_ref[...]
    y = _dot(jnp.concatenate([out_c, out_d], axis=1).astype(BF16), wout_ref[...])
    xo_ref[...] = x + y
    conv_ref[0:CONV_BUF - 1] = cbuf_ref[1:CONV_BUF]
    conv_ref[CONV_BUF - 1] = glu
    pool_ref[0:POOL_BUF - 1] = pbuf_ref[1:POOL_BUF]
    pool_ref[POOL_BUF - 1] = xp


def _odd_sample(x, conv_buf, pool_buf, g, od, *, block_rows=32):
    rows, d = x.shape
    bb = min(block_rows, rows)
    full = lambda a: pl.BlockSpec(a.shape, (lambda i: (0,) * a.ndim), pipeline_mode=pl.Buffered(1))
    whole = lambda shape: pl.BlockSpec(shape, (lambda i: (0,) * len(shape)))
    operands = (g, od["w_in"], od["conv_w"], od["conv_b"], od["ln_g"], od["ln_b"], od["pool_w"],
                od["pool_scale"], od["w_out"])
    w_shapes = [od["w_in"].shape, od["pool_w"].shape, od["w_out"].shape]
    return pl.pallas_call(
        _odd_sample_body,
        grid=(rows // bb,),
        in_specs=[pl.BlockSpec((bb, d), lambda i: (i, 0)),
                  pl.BlockSpec((CONV_BUF, bb, CONV_W), lambda i: (0, i, 0)),
                  pl.BlockSpec((POOL_BUF, bb, POOL_W), lambda i: (0, i, 0))]
                 + [full(a) for a in operands],
        out_specs=[
            pl.BlockSpec((bb, d), lambda i: (i, 0)),
            pl.BlockSpec((CONV_BUF, bb, CONV_W), lambda i: (0, i, 0)),
            pl.BlockSpec((POOL_BUF, bb, POOL_W), lambda i: (0, i, 0)),
        ] + [whole(s) for s in w_shapes],
        out_shape=[
            jax.ShapeDtypeStruct((rows, d), F32),
            jax.ShapeDtypeStruct(conv_buf.shape, F32),
            jax.ShapeDtypeStruct(pool_buf.shape, F32),
        ] + [jax.ShapeDtypeStruct(s, BF16) for s in w_shapes],
        compiler_params=pltpu.CompilerParams(
            dimension_semantics=("arbitrary",), vmem_limit_bytes=VMEM_LIMIT_BYTES),
        name="odd_sample",
    )(x, conv_buf, pool_buf, *operands)


def _prep_odd(w_in, conv_w, conv_b, ln_g, ln_b, pool_w, pool_scale, w_out):
    return dict(w_in=w_in, conv_w=conv_w, conv_b=conv_b.reshape(1, -1), ln_g=ln_g.reshape(1, -1),
                ln_b=ln_b.reshape(1, -1), pool_w=pool_w, pool_scale=pool_scale.reshape(1, -1), w_out=w_out)


def kernel(x_prompt, x_sample, state_gla, state_conv, state_pool, norm_g, ff_in, ff_out, ev_w_in, ev_w_gate, ev_b_gate, ev_gla_g, ev_sg_ln_g, ev_sg_ln_b, ev_sg_w, ev_sg_b, ev_w_out, od_w_in, od_conv_w, od_conv_b, od_ln_g, od_ln_b, od_pool_w, od_pool_scale, od_w_out, norm_f):
    bp, t, d = x_prompt.shape
    bs = x_sample.shape[0]
    depth = norm_g.shape[0]
    xp = x_prompt.reshape(bp * t, d)
    xs = x_sample.reshape(bs, d)
    gla_p, gla_s, sgv_p, sgv_s, conv_p, conv_s, pool_p, pool_s = [], [], [], [], [], [], [], []
    for layer in range(depth):
        i = layer // 2
        last = layer == depth - 1
        xs, xp = _ffn_fused(xs, xp, norm_g[layer, 0], ff_in, ff_out, layer, 0, norm_f, final_norm=False,
                            block_rows=FFN_ROWS)
        g_mix = norm_g[layer, 1].reshape(1, d)
        if layer % 2 == 0:
            ev = _prep_even(ev_b_gate[i], ev_gla_g[i], ev_sg_ln_g[i], ev_sg_ln_b[i], ev_sg_w[i], ev_sg_b[i])
            xs, s_s, v_s, w_in_b, w_gate_b, w_out_b, sg_w_b = _even_sample(
                xs, state_gla[i], g_mix, jnp.transpose(ev_w_in[i]), ev_w_gate[i], ev_w_out[i], ev_sg_w[i], ev)
            xp, s_p, v_p = _even_prompt(xp, bp, g_mix, w_in_b, w_gate_b, ev["b_gate"], ev["gla_g"],
                                        ev["ln_g"], ev["ln_b"], sg_w_b, ev["sg_bias"], w_out_b,
                                        block_rows=512)
            gla_p.append(s_p); gla_s.append(s_s); sgv_p.append(v_p); sgv_s.append(v_s.reshape(bs, 1, SG_W))
        else:
            od = _prep_odd(od_w_in[i], od_conv_w[i], od_conv_b[i], od_ln_g[i], od_ln_b[i], od_pool_w[i],
                           od_pool_scale[i], od_w_out[i])
            xs, c_s, p_s, w_in_b, pool_w_b, w_out_b = _odd_sample(
                xs, jnp.transpose(state_conv[i], (1, 0, 2)), jnp.transpose(state_pool[i], (1, 0, 2)), g_mix, od)
            c_s, p_s = jnp.transpose(c_s, (1, 0, 2)), jnp.transpose(p_s, (1, 0, 2))
            xp, c_p, p_p = _odd_prompt(xp, bp, g_mix, dict(od, w_in=w_in_b, pool_w=pool_w_b, w_out=w_out_b),
                                       block_rows=512)
            conv_p.append(c_p); conv_s.append(c_s); pool_p.append(p_p); pool_s.append(p_s)
        xs, xp = _ffn_fused(xs, xp, norm_g[layer, 2], ff_in, ff_out, layer, 1, norm_f, final_norm=last,
                            block_rows=FFN_ROWS)
    return (xp.reshape(bp, t, d), xs.reshape(bs, 1, d), jnp.stack(gla_p), jnp.stack(gla_s),
            jnp.stack(sgv_p), jnp.stack(sgv_s), jnp.stack(conv_p), jnp.stack(conv_s),
            jnp.stack(pool_p), jnp.stack(pool_s))
```

```python
import functools

import jax
import jax.numpy as jnp
import numpy as np
from jax import lax
from jax.experimental import pallas as pl
from jax.experimental.pallas import tpu as pltpu

EPS = 1e-6
LOG2_E = 1.4426950408889634
BF16 = jnp.bfloat16
F32 = jnp.float32

LANES = 128
SUBLANES = 8
VMEM_LIMIT_BYTES = 56 * 1024 * 1024
FFN_ROWS = 1024


def _rms(x, g):
    return x * lax.rsqrt(jnp.mean(x * x, axis=-1, keepdims=True) + EPS) * g


def _dot(a, b):
    return jnp.dot(a, b, preferred_element_type=F32)


FFN_CHUNK = 256


def _ffn_fused_body(xs_ref, xp_ref, g_ref, wa32_ref, wb32_ref, wo32_ref, gf_ref,
                    ys_ref, yp_ref, wa_ref, wb_ref, wo_ref, hs_ref, accs_ref, act_ref, hp_ref, accp_ref,
                    *, final_norm):
    s = pl.program_id(0)
    n_chunks = wa_ref.shape[0]

    def finish(x, acc):
        y = x + 0.5 * acc
        return _rms(y, gf_ref[...]) if final_norm else y

    @pl.when(s < n_chunks)
    def _():
        @pl.when(s == 0)
        def _():
            hs_ref[...] = _rms(xs_ref[...], g_ref[...]).astype(BF16)
            accs_ref[...] = jnp.zeros_like(accs_ref)
            hp_ref[...] = _rms(xp_ref[...], g_ref[...]).astype(BF16)
            accp_ref[...] = jnp.zeros_like(accp_ref)

        wa = wa32_ref[...].astype(BF16)
        wb = wb32_ref[...].astype(BF16)
        wo = wo32_ref[...].astype(BF16)
        wa_ref[s] = wa
        wb_ref[s] = wb
        wo_ref[s] = wo
        for h_ref, acc_ref in ((hs_ref, accs_ref), (hp_ref, accp_ref)):
            h = h_ref[...]
            a = _dot(h, wa)
            b = _dot(h, wb)
            acc_ref[...] += _dot((a * jax.nn.sigmoid(a) * b).astype(BF16), wo)

        @pl.when(s == n_chunks - 1)
        def _():
            ys_ref[...] = finish(xs_ref[...], accs_ref[...])
            yp_ref[...] = finish(xp_ref[...], accp_ref[...])

    @pl.when(s >= n_chunks)
    def _():
        x = xp_ref[...]
        h = _rms(x, g_ref[...]).astype(BF16)
        for c in range(n_chunks):
            a = _dot(h, wa_ref[c])
            b = _dot(h, wb_ref[c])
            act_ref[:, c * FFN_CHUNK:(c + 1) * FFN_CHUNK] = (a * jax.nn.sigmoid(a) * b).astype(BF16)
        w_out = wo_ref[...].reshape(n_chunks * FFN_CHUNK, wo_ref.shape[2])
        yp_ref[...] = finish(x, _dot(act_ref[...], w_out))


def _ffn_fused(x_sample, x_prompt, g, ff_in, ff_out, layer, slot, g_final, *, final_norm, block_rows):
    rows_s, d = x_sample.shape
    rows_p = x_prompt.shape[0]
    d_ff = ff_out.shape[2]
    n_chunks = d_ff // FFN_CHUNK
    tm = min(block_rows, rows_p)
    const = lambda s: (0, 0)
    chunk = lambda s: jnp.minimum(s, n_chunks - 1)
    tile = lambda s: (jnp.maximum(s - (n_chunks - 1), 0), 0)
    body = functools.partial(_ffn_fused_body, final_norm=final_norm)
    return pl.pallas_call(
        body,
        grid=(n_chunks - 1 + rows_p // tm,),
        in_specs=[
            pl.BlockSpec((rows_s, d), const),
            pl.BlockSpec((tm, d), tile),
            pl.BlockSpec((1, d), const),
            pl.BlockSpec((None, None, d, FFN_CHUNK), lambda s: (layer, slot, 0, chunk(s))),
            pl.BlockSpec((None, None, d, FFN_CHUNK), lambda s: (layer, slot, 0, n_chunks + chunk(s))),
            pl.BlockSpec((None, None, FFN_CHUNK, d), lambda s: (layer, slot, chunk(s), 0)),
            pl.BlockSpec((1, d), const),
        ],
        out_specs=[pl.BlockSpec((rows_s, d), const), pl.BlockSpec((tm, d), tile)],
        out_shape=[jax.ShapeDtypeStruct((rows_s, d), F32), jax.ShapeDtypeStruct((rows_p, d), F32)],
        scratch_shapes=[
            pltpu.VMEM((n_chunks, d, FFN_CHUNK), BF16),
            pltpu.VMEM((n_chunks, d, FFN_CHUNK), BF16),
            pltpu.VMEM((n_chunks, FFN_CHUNK, d), BF16),
            pltpu.VMEM((rows_s, d), BF16),
            pltpu.VMEM((rows_s, d), F32),
            pltpu.VMEM((tm, d_ff), BF16),
            pltpu.VMEM((tm, d), BF16),
            pltpu.VMEM((tm, d), F32),
        ],
        compiler_params=pltpu.CompilerParams(
            dimension_semantics=("arbitrary",), vmem_limit_bytes=VMEM_LIMIT_BYTES),
        name="ffn_fused",
    )(x_sample, x_prompt, g.reshape(1, d), ff_in, ff_in, ff_out, g_final.reshape(1, d))


GLA_H = 4
GLA_DK = 64
GLA_DV = 128
GLA_QK = GLA_H * GLA_DK
GLA_V = GLA_H * GLA_DV
GLA_RANK_PAD = 128
GLA_INV_TAU = 1.0 / 16.0
GLA_CH = 16
GLA_GROUP = 16
GLA_PROJ_COLS = 256
SG_H = 4
SG_DH = 128
SG_W = SG_H * SG_DH
SG_CHUNK = 128
EV_Q, EV_V, EV_R, EV_U, EV_VS, EV_Z, EV_END = 0, 512, 1024, 1536, 2048, 2560, 2688


def _log_sigmoid(x):
    return jnp.minimum(x, 0.0) - jnp.log(1.0 + jnp.exp(-jnp.abs(x)))


def _split3(x):
    hi = x.astype(BF16)
    r1 = x - hi.astype(F32)
    mid = r1.astype(BF16)
    lo = (r1 - mid.astype(F32)).astype(BF16)
    return hi, mid, lo


def _dot3(sel, parts):
    return _dot(sel, parts[0]) + _dot(sel, parts[1]) + _dot(sel, parts[2])


def _head_masks(width, per_head, dtype):
    lane = lax.broadcasted_iota(jnp.int32, (1, width), 1)
    return [jnp.where(lane // per_head == h, 1.0, 0.0).astype(dtype) for h in range(width // per_head)]


def _layernorm(x, g, b):
    mu = jnp.mean(x, axis=-1, keepdims=True)
    xc = x - mu
    var = jnp.mean(xc * xc, axis=-1, keepdims=True)
    return xc * lax.rsqrt(var + EPS) * g + b


def _even_prompt_body(x_ref, g_ref, win_ref, wgate_ref, bgate_ref, glag_ref, lng_ref, lnb_ref,
                      sgw_ref, sgb_ref, e_ref, wout_ref,
                      xo_ref, gla_ref, sgv_ref,
                      st_ref, q_ref, k_ref, cum2_ref, qin_ref, kd_ref, dec_ref, v_ref, o_ref,
                      pcat_ref, acat_ref, mix_ref, add_ref, stb_ref, ruv_ref):
    j = pl.program_id(1)
    tt = x_ref.shape[0]
    n_chunks = tt // GLA_CH

    @pl.when(j == 0)
    def _():
        st_ref[...] = jnp.zeros_like(st_ref)

    x = x_ref[...]
    h = _rms(x, g_ref[...]).astype(BF16)

    group = min(n_chunks, GLA_GROUP)
    g_rows = group * GLA_CH
    row = lax.broadcasted_iota(jnp.int32, (g_rows, g_rows), 0)
    col = lax.broadcasted_iota(jnp.int32, (g_rows, g_rows), 1)
    sel_cum = jnp.where(((row // GLA_CH) == (col // GLA_CH)) & (col <= row), 1.0, 0.0).astype(BF16)

    def stage1(g0):
        def run():
            rs = slice(g0 * GLA_CH, g0 * GLA_CH + g_rows)
            hg = h[rs]
            qk = _dot(hg, win_ref[:, EV_Q:EV_V])
            q = qk[:, :GLA_QK] * (GLA_DK ** -0.5)
            k = qk[:, GLA_QK:]
            v_ref[rs, :] = _dot(hg, win_ref[:, EV_V:EV_R]).astype(BF16)
            z = _dot(hg, win_ref[:, EV_Z:EV_END]).astype(BF16)
            log_a = _log_sigmoid(_dot(z, wgate_ref[...]) + bgate_ref[...]) * GLA_INV_TAU
            cum = _dot3(sel_cum, _split3(log_a))
            tot = jnp.concatenate(
                [jnp.broadcast_to(cum[(n + 1) * GLA_CH - 1:(n + 1) * GLA_CH, :], (GLA_CH, GLA_QK))
                 for n in range(group)], axis=0)
            q_ref[rs, :] = q
            k_ref[rs, :] = k
            cum2_ref[rs, :] = cum * LOG2_E
            qin_ref[rs, :] = (q * jnp.exp(cum)).astype(BF16)
            kd_ref[rs, :] = (k * jnp.exp(tot - cum)).astype(BF16)
            dec_ref[rs, :] = jnp.exp(tot)
        return run

    stage1(0)()

    half = GLA_CH // 2
    qk_masks = _head_masks(GLA_QK, GLA_DK, BF16)
    v_masks = _head_masks(GLA_V, GLA_DV, BF16)
    state = [st_ref[...]]

    def score_products(n):
        base = n * GLA_CH
        qb = q_ref[base:base + GLA_CH, :]
        cb = cum2_ref[base:base + GLA_CH, :]
        for s in range(GLA_CH):
            ks = k_ref[base + s:base + s + 1, :]
            cs = cum2_ref[base + s:base + s + 1, :]
            if s < half:
                p = qb * ks * jnp.exp2(jnp.minimum(cb - cs, 0.0))
            else:
                p_hi = qb[half:] * ks * jnp.exp2(jnp.minimum(cb[half:] - cs, 0.0))
                p = jnp.concatenate([jnp.zeros_like(p_hi), p_hi], axis=0)
            pcat_ref[base:base + GLA_CH, s * GLA_QK:(s + 1) * GLA_QK] = p.astype(BF16)

    def score_sum(rows):
        scores = _dot(pcat_ref[rows, :], e_ref[...])
        t_loc = lax.broadcasted_iota(jnp.int32, scores.shape, 0) % GLA_CH
        c_idx = lax.broadcasted_iota(jnp.int32, scores.shape, 1)
        causal = (c_idx % GLA_CH <= t_loc) & (c_idx < GLA_H * GLA_CH)
        acat_ref[rows, :] = jnp.where(causal, scores, 0.0).astype(BF16)

    def projection_piece(c0):
        def run():
            ruv_ref[:, c0 - EV_R:c0 - EV_R + GLA_PROJ_COLS] = _dot(h, win_ref[:, c0:c0 + GLA_PROJ_COLS])
        return run

    def phase1(n):
        def run():
            rows = slice(n * GLA_CH, (n + 1) * GLA_CH)
            kn = kd_ref[rows, :]
            vn = v_ref[rows, :]
            lk = jnp.concatenate([kn * m for m in qk_masks], axis=0)
            vs = jnp.concatenate([vn[:, hh * GLA_DV:(hh + 1) * GLA_DV] for hh in range(GLA_H)], axis=0)
            add_ref[n] = lax.dot_general(vs, lk, (((0,), (0,)), ((), ())), preferred_element_type=F32)
        return run

    def phase2(n):
        def run():
            stb_ref[n] = state[0].astype(BF16).T
            state[0] = state[0] * dec_ref[n * GLA_CH:n * GLA_CH + 1, :] + add_ref[n]
        return run

    def phase3(n):
        def run():
            rows = slice(n * GLA_CH, (n + 1) * GLA_CH)
            qn = qin_ref[rows, :]
            vn = v_ref[rows, :]
            lq = jnp.concatenate([qn * m for m in qk_masks], axis=0)
            oi = _dot(lq, stb_ref[n])
            o_inter = jnp.concatenate([oi[hh * GLA_CH:(hh + 1) * GLA_CH] for hh in range(GLA_H)], axis=1)
            vbd = jnp.concatenate([vn * m for m in v_masks], axis=0)
            o_ref[rows, :] = o_inter + _dot(acat_ref[rows, 0:GLA_H * GLA_CH], vbd)
        return run

    side = [projection_piece(c0) for c0 in range(EV_R, EV_Z, GLA_PROJ_COLS)]
    u = v_ln = v_lb = None
    for g0 in range(0, n_chunks, group):
        chunks = range(g0, g0 + group)
        if g0 + group < n_chunks:
            side = [stage1(g0 + group)] + side
        for idx, n in enumerate(chunks):
            score_products(n)
            for task in side[idx * len(side) // group:(idx + 1) * len(side) // group]:
                task()
        score_sum(slice(g0 * GLA_CH, (g0 + group) * GLA_CH))
        if g0 == 0:
            u = jax.nn.gelu(ruv_ref[:, EV_U - EV_R:EV_VS - EV_R])
            v_ln = _layernorm(jax.nn.gelu(ruv_ref[:, EV_VS - EV_R:EV_Z - EV_R]), lng_ref[...], lnb_ref[...])
            v_lb = v_ln.astype(BF16)
        side = [phase(n) for phase in (phase1, phase2, phase3) for n in chunks]
    r = ruv_ref[:, 0:EV_U - EV_R]
    gate_r = r * jax.nn.sigmoid(r)
    sg_chunks = tt // SG_CHUNK
    sg_every = len(side) // SG_H
    for idx, task in enumerate(side):
        task()
        if idx % sg_every == 0 and idx // sg_every < SG_H:
            hh = idx // sg_every
            cols = slice(hh * SG_DH, (hh + 1) * SG_DH)
            mixed = _dot(sgw_ref[hh], jnp.concatenate(
                [v_lb[c * SG_CHUNK:(c + 1) * SG_CHUNK, cols] for c in range(sg_chunks)], axis=1))
            for c in range(sg_chunks):
                mix_ref[c * SG_CHUNK:(c + 1) * SG_CHUNK, cols] = mixed[:, c * SG_DH:(c + 1) * SG_DH]
    st_ref[...] = state[0]

    o = o_ref[...]
    o_n = jnp.concatenate(
        [_rms(o[:, hh * GLA_DV:(hh + 1) * GLA_DV], 1.0) for hh in range(GLA_H)], axis=1) * glag_ref[...]
    out_a = o_n * gate_r
    bias = jnp.concatenate([sgb_ref[...]] * (tt // SG_CHUNK), axis=0)
    out_b = u * (mix_ref[...] + bias)

    y = _dot(jnp.concatenate([out_a, out_b], axis=1).astype(BF16), wout_ref[...])
    xo_ref[...] = x + y

    @pl.when(j == pl.num_programs(1) - 1)
    def _():
        sgv_ref[0] = v_ln[tt - SG_CHUNK:, :]
        gla_ref[0] = st_ref[...].T.reshape(GLA_H, GLA_DK, GLA_DV)


def _score_sum_matrix():
    r = np.arange(GLA_CH * GLA_QK)
    c = np.arange(LANES)
    s, hh = r // GLA_QK, (r % GLA_QK) // GLA_DK
    return jnp.asarray((c[None, :] == (hh * GLA_CH + s)[:, None]).astype(np.float32), dtype=BF16)


def _even_prompt(x, batch, g, w_in, w_gate, b_gate, gla_g, ln_g, ln_b, sg_w, sg_bias, w_out, *, block_rows):
    rows, d = x.shape
    t = rows // batch
    tt = min(block_rows, t)
    nj = t // tt
    full = lambda a: pl.BlockSpec(a.shape, (lambda b, j: (0,) * a.ndim), pipeline_mode=pl.Buffered(1))
    e = _score_sum_matrix()
    operands = (g, w_in, w_gate, b_gate, gla_g, ln_g, ln_b, sg_w, sg_bias, e, w_out)
    return pl.pallas_call(
        _even_prompt_body,
        grid=(batch, nj),
        in_specs=[pl.BlockSpec((tt, d), lambda b, j: (b * nj + j, 0))] + [full(a) for a in operands],
        out_specs=[
            pl.BlockSpec((tt, d), lambda b, j: (b * nj + j, 0)),
            pl.BlockSpec((1, GLA_H, GLA_DK, GLA_DV), lambda b, j: (b, 0, 0, 0)),
            pl.BlockSpec((1, SG_CHUNK, SG_W), lambda b, j: (b, 0, 0)),
        ],
        out_shape=[
            jax.ShapeDtypeStruct((rows, d), F32),
            jax.ShapeDtypeStruct((batch, GLA_H, GLA_DK, GLA_DV), F32),
            jax.ShapeDtypeStruct((batch, SG_CHUNK, SG_W), F32),
        ],
        scratch_shapes=[
            pltpu.VMEM((GLA_DV, GLA_QK), F32),
            pltpu.VMEM((tt, GLA_QK), F32),
            pltpu.VMEM((tt, GLA_QK), F32),
            pltpu.VMEM((tt, GLA_QK), F32),
            pltpu.VMEM((tt, GLA_QK), BF16),
            pltpu.VMEM((tt, GLA_QK), BF16),
            pltpu.VMEM((tt, GLA_QK), F32),
            pltpu.VMEM((tt, GLA_V), BF16),
            pltpu.VMEM((tt, GLA_V), F32),
            pltpu.VMEM((tt, GLA_CH * GLA_QK), BF16),
            pltpu.VMEM((tt, LANES), BF16),
            pltpu.VMEM((tt, SG_W), F32),
            pltpu.VMEM((tt // GLA_CH, GLA_DV, GLA_QK), F32),
            pltpu.VMEM((tt // GLA_CH, GLA_QK, GLA_DV), BF16),
            pltpu.VMEM((tt, EV_Z - EV_R), F32),
        ],
        compiler_params=pltpu.CompilerParams(
            dimension_semantics=("arbitrary", "arbitrary"), vmem_limit_bytes=VMEM_LIMIT_BYTES),
        name="even_prompt",
    )(x, *operands)


def _prep_even(b_gate, gla_g, ln_g, ln_b, sg_w, sg_b):
    return dict(
        b_gate=b_gate.reshape(1, -1), b_gate_col=b_gate.reshape(-1, 1), gla_g=gla_g.reshape(1, -1),
        ln_g=ln_g.reshape(1, -1), ln_b=ln_b.reshape(1, -1),
        sg_bias=jnp.repeat(jnp.transpose(sg_b), SG_DH, axis=1),
        sg_w0=jnp.repeat(sg_w[:, 0, 0], SG_DH).reshape(1, -1), sg_b0=jnp.repeat(sg_b[:, 0], SG_DH).reshape(1, -1))


def _even_sample_body(x_ref, s_ref, g_ref, win32_ref, wgate32_ref, wout32_ref, sgw32_ref, bgate_ref,
                      bgatec_ref, glag_ref, lng_ref, lnb_ref, sgw0_ref, sgb0_ref,
                      xo_ref, so_ref, sgv_ref, win_ref, wgate_ref, wout_ref, sgw_ref,
                      o_ref, wint_ref, wgatet_ref):
    bb = x_ref.shape[0]

    @pl.when(pl.program_id(0) == 0)
    def _():
        z0 = 2 * GLA_QK + 2 * GLA_V
        rank = wgate32_ref.shape[0]
        wint_ref[0:z0] = win32_ref[0:z0].astype(BF16)
        wint_ref[z0:EV_Z] = win32_ref[z0 + rank:].astype(BF16)
        wint_ref[EV_Z:EV_Z + rank] = win32_ref[z0:z0 + rank].astype(BF16)
        wint_ref[EV_Z + rank:EV_END] = jnp.zeros((EV_END - EV_Z - rank, wint_ref.shape[1]), BF16)
        for c0 in range(0, EV_END, LANES):
            win_ref[:, c0:c0 + LANES] = wint_ref[c0:c0 + LANES, :].T
        wgate_ref[...] = jnp.concatenate(
            [wgate32_ref[...].astype(BF16), jnp.zeros((GLA_RANK_PAD - rank, GLA_QK), BF16)], axis=0)
        wgatet_ref[...] = wgate_ref[...].T
        wout_ref[...] = wout32_ref[...].astype(BF16)
        row = lax.broadcasted_iota(jnp.int32, (SG_CHUNK, SG_CHUNK), 0)
        col = lax.broadcasted_iota(jnp.int32, (SG_CHUNK, SG_CHUNK), 1)
        for hh in range(SG_H):
            sgw_ref[hh] = jnp.where(col <= row, sgw32_ref[hh], 0.0).astype(BF16)

    x = x_ref[...]
    h = _rms(x, g_ref[...]).astype(BF16)
    nt = (((1,), (1,)), ((), ()))
    k_t = lax.dot_general(wint_ref[EV_Q + GLA_QK:EV_V], h, nt, preferred_element_type=F32)
    k_t = k_t.astype(BF16).astype(F32)
    z_t = lax.dot_general(wint_ref[EV_Z:EV_END], h, nt, preferred_element_type=F32).astype(BF16)
    a_t = jnp.exp(_log_sigmoid(_dot(wgatet_ref[...], z_t) + bgatec_ref[...]) * GLA_INV_TAU)
    q = _dot(h, win_ref[:, EV_Q:EV_Q + GLA_QK]) * (GLA_DK ** -0.5)
    v = _dot(h, win_ref[:, EV_V:EV_R])
    v_r = v.astype(BF16).astype(F32)
    head_rows = jnp.concatenate(_head_masks(GLA_QK, GLA_DK, F32), axis=0)
    for b in range(bb):
        s_old = s_ref[b].reshape(GLA_QK, GLA_DV)
        a_c = jnp.broadcast_to(a_t[:, b:b + 1], (GLA_QK, GLA_DV))
        k_c = jnp.broadcast_to(k_t[:, b:b + 1], (GLA_QK, GLA_DV))
        v_rows = jnp.concatenate(
            [jnp.broadcast_to(v_r[b:b + 1, hh * GLA_DV:(hh + 1) * GLA_DV], (GLA_DK, GLA_DV))
             for hh in range(GLA_H)], axis=0)
        s_new = a_c * s_old + k_c * v_rows
        so_ref[b] = s_new.reshape(GLA_H, GLA_DK, GLA_DV)
        ob = _dot((q[b:b + 1, :] * head_rows).astype(BF16), s_new.astype(BF16))
        o_ref[b:b + 1, :] = jnp.concatenate([ob[hh:hh + 1] for hh in range(GLA_H)], axis=1)
    o = o_ref[...]
    o_n = jnp.concatenate(
        [_rms(o[:, hh * GLA_DV:(hh + 1) * GLA_DV], 1.0) for hh in range(GLA_H)], axis=1) * glag_ref[...]
    r = _dot(h, win_ref[:, EV_R:EV_U])
    out_a = o_n * (r * jax.nn.sigmoid(r))
    u = jax.nn.gelu(_dot(h, win_ref[:, EV_U:EV_VS]))
    v_ln = _layernorm(jax.nn.gelu(_dot(h, win_ref[:, EV_VS:EV_Z])), lng_ref[...], lnb_ref[...])
    sgv_ref[...] = v_ln
    out_b = u * (sgw0_ref[...] * v_ln + sgb0_ref[...])
    y = _dot(jnp.concatenate([out_a, out_b], axis=1).astype(BF16), wout_ref[...])
    xo_ref[...] = x + y


def _even_sample(x, state, g, w_in_t, w_gate, w_out, sg_w, ev, *, block_rows=32):
    rows, d = x.shape
    bb = min(block_rows, rows)
    full = lambda a: pl.BlockSpec(a.shape, (lambda i: (0,) * a.ndim), pipeline_mode=pl.Buffered(1))
    whole = lambda shape: pl.BlockSpec(shape, (lambda i: (0,) * len(shape)))
    operands = (g, w_in_t, w_gate, w_out, sg_w, ev["b_gate"], ev["b_gate_col"], ev["gla_g"], ev["ln_g"],
                ev["ln_b"], ev["sg_w0"], ev["sg_b0"])
    w_shapes = [(d, EV_END), (GLA_RANK_PAD, GLA_QK), w_out.shape, sg_w.shape]
    return pl.pallas_call(
        _even_sample_body,
        grid=(rows // bb,),
        in_specs=[pl.BlockSpec((bb, d), lambda i: (i, 0)),
                  pl.BlockSpec((bb, GLA_H, GLA_DK, GLA_DV), lambda i: (i, 0, 0, 0))]
                 + [full(a) for a in operands],
        out_specs=[
            pl.BlockSpec((bb, d), lambda i: (i, 0)),
            pl.BlockSpec((bb, GLA_H, GLA_DK, GLA_DV), lambda i: (i, 0, 0, 0)),
            pl.BlockSpec((bb, SG_W), lambda i: (i, 0)),
        ] + [whole(s) for s in w_shapes],
        out_shape=[
            jax.ShapeDtypeStruct((rows, d), F32),
            jax.ShapeDtypeStruct(state.shape, F32),
            jax.ShapeDtypeStruct((rows, SG_W), F32),
        ] + [jax.ShapeDtypeStruct(s, BF16) for s in w_shapes],
        scratch_shapes=[
            pltpu.VMEM((bb, GLA_V), F32),
            pltpu.VMEM((EV_END, d), BF16),
            pltpu.VMEM((GLA_QK, GLA_RANK_PAD), BF16),
        ],
        compiler_params=pltpu.CompilerParams(
            dimension_semantics=("arbitrary",), vmem_limit_bytes=VMEM_LIMIT_BYTES),
        name="even_sample",
    )(x, state, *operands)


CONV_W = 512
CONV_K = 31
CONV_BUF = CONV_K - 1
CONV_PAD = 32
POOL_W = 512
POOL_WINDOWS = (2, 4, 8, 16)
POOL_DG = POOL_W // len(POOL_WINDOWS)
POOL_BUF = 15
POOL_PAD = 16
CONV_ROWS = 32
ODD_SUB = 128
ODD_COLS = 256


def _odd_prompt_body(x_ref, g_ref, win_ref, cw_ref, cb_ref, lng_ref, lnb_ref, pw_ref, ps_ref, wout_ref,
                     xo_ref, conv_ref, pool_ref, gbuf_ref, pbuf_ref, shift_ref, cwb_ref, psum_ref,
                     h_ref, mixin_ref, pooled_ref, raw_ref, convo_ref):
    j = pl.program_id(1)
    tt = x_ref.shape[0]
    assert POOL_WINDOWS == (2, 4, 8, 16)

    p0 = SUBLANES + POOL_PAD

    @pl.when(j == 0)
    def _():
        gbuf_ref[0:CONV_PAD, :] = jnp.zeros((CONV_PAD, CONV_W), F32)
        pbuf_ref[0:p0, :] = jnp.zeros((p0, POOL_W), F32)
        psum_ref[:, 0:SUBLANES, :] = jnp.zeros((psum_ref.shape[0], SUBLANES, POOL_W), F32)
        for o in range(CONV_K):
            cwb_ref[o] = jnp.broadcast_to(cw_ref[o:o + 1, :], (SUBLANES, CONV_W))

    h_ref[...] = _rms(x_ref[...], g_ref[...]).astype(BF16)
    first = CONV_PAD - CONV_BUF
    groups = CONV_ROWS // SUBLANES
    n_sub = tt // ODD_SUB

    def project(k):
        rows = slice(k * ODD_SUB, (k + 1) * ODD_SUB)

        def piece(c0):
            def run():
                raw_ref[rows, c0:c0 + ODD_COLS] = _dot(h_ref[rows, :], win_ref[:, c0:c0 + ODD_COLS])
            return run

        return [piece(c0) for c0 in range(0, 2 * CONV_W, ODD_COLS)]

    def glu(k):
        rows = slice(k * ODD_SUB, (k + 1) * ODD_SUB)
        gbuf_ref[CONV_PAD + k * ODD_SUB:CONV_PAD + (k + 1) * ODD_SUB, :] = (
            raw_ref[rows, 0:CONV_W] * jax.nn.sigmoid(raw_ref[rows, CONV_W:2 * CONV_W]))

    def mix(k):
        r_lo = k * ODD_SUB

        def shift_copies():
            lo = 0 if k == 0 else r_lo + CONV_PAD - SUBLANES
            hi = r_lo + ODD_SUB + CONV_PAD - SUBLANES
            for rr in range(1, SUBLANES):
                shift_ref[rr - 1, lo:hi, :] = gbuf_ref[rr + lo:rr + hi, :]

        def conv_block(r0):
            def run():
                accs = [jnp.zeros((SUBLANES, CONV_W), F32) + cb_ref[...] for _ in range(groups)]
                for o in range(first, first + CONV_K):
                    rr = o % SUBLANES
                    w8 = cwb_ref[o - first]
                    for gq in range(groups):
                        lo = o - rr + r0 + gq * SUBLANES
                        src = (gbuf_ref[lo:lo + SUBLANES, :] if rr == 0
                               else shift_ref[rr - 1, lo:lo + SUBLANES, :])
                        accs[gq] = accs[gq] + src * w8
                for gq in range(groups):
                    convo_ref[r0 + gq * SUBLANES:r0 + (gq + 1) * SUBLANES, :] = accs[gq]
            return run

        def norm_act():
            out_c = _layernorm(convo_ref[r_lo:r_lo + ODD_SUB, :], lng_ref[...], lnb_ref[...])
            mixin_ref[r_lo:r_lo + ODD_SUB, 0:CONV_W] = (out_c * jax.nn.sigmoid(out_c)).astype(BF16)

        def pooling():
            lo = SUBLANES if k == 0 else p0 + r_lo
            hi = p0 + r_lo + ODD_SUB
            psum_ref[0, lo:hi, :] = pbuf_ref[lo:hi, :] + pbuf_ref[lo - 1:hi - 1, :]
            psum_ref[1, lo:hi, POOL_DG:] = psum_ref[0, lo:hi, POOL_DG:] + psum_ref[0, lo - 2:hi - 2, POOL_DG:]
            psum_ref[2, lo:hi, 2 * POOL_DG:] = (psum_ref[1, lo:hi, 2 * POOL_DG:]
                                                + psum_ref[1, lo - 4:hi - 4, 2 * POOL_DG:])
            o_lo = p0 + r_lo
            o_hi = o_lo + ODD_SUB
            tots = [psum_ref[0, o_lo:o_hi, 0:POOL_DG],
                    psum_ref[1, o_lo:o_hi, POOL_DG:2 * POOL_DG],
                    psum_ref[2, o_lo:o_hi, 2 * POOL_DG:3 * POOL_DG],
                    psum_ref[2, o_lo:o_hi, 3 * POOL_DG:] + psum_ref[2, o_lo - 8:o_hi - 8, 3 * POOL_DG:]]
            t_glob = j * tt + r_lo + lax.broadcasted_iota(jnp.int32, (ODD_SUB, 1), 0)
            for gi, win_len in enumerate(POOL_WINDOWS):
                lanes = slice(gi * POOL_DG, (gi + 1) * POOL_DG)
                cnt = jnp.minimum(win_len, t_glob + 1).astype(F32)
                pooled = tots[gi] / cnt - pbuf_ref[o_lo:o_hi, lanes]
                pooled_ref[r_lo:r_lo + ODD_SUB, lanes] = pooled.astype(BF16)

        return ([shift_copies] + [conv_block(r0) for r0 in range(r_lo, r_lo + ODD_SUB, CONV_ROWS)]
                + [norm_act, pooling])

    def output(k):
        rows = slice(k * ODD_SUB, (k + 1) * ODD_SUB)

        def pool_matmuls():
            outs = [_dot(pooled_ref[rows, gi * POOL_DG:(gi + 1) * POOL_DG], pw_ref[gi])
                    for gi in range(len(POOL_WINDOWS))]
            mixin_ref[rows, CONV_W:] = (jnp.concatenate(outs, axis=1) * ps_ref[...]).astype(BF16)

        def out_piece(c0):
            def run():
                xo_ref[rows, c0:c0 + ODD_COLS] = x_ref[rows, c0:c0 + ODD_COLS] + _dot(
                    mixin_ref[rows, :], wout_ref[:, c0:c0 + ODD_COLS])
            return run

        return [pool_matmuls] + [out_piece(c0) for c0 in range(0, xo_ref.shape[1], ODD_COLS)]

    pbuf_ref[p0:p0 + tt, :] = _dot(h_ref[...], win_ref[:, 2 * CONV_W:])
    for task in project(0):
        task()
    glu(0)
    for k in range(n_sub):
        valu_tasks = mix(k)
        mxu_tasks = output(k - 1) if k >= 1 else []
        if k + 1 < n_sub:
            mxu_tasks = project(k + 1) + mxu_tasks
            valu_tasks = valu_tasks + [functools.partial(glu, k + 1)]
        for idx, task in enumerate(valu_tasks):
            lo_m = idx * len(mxu_tasks) // len(valu_tasks)
            hi_m = (idx + 1) * len(mxu_tasks) // len(valu_tasks)
            for m_task in mxu_tasks[lo_m:hi_m]:
                m_task()
            task()
    for task in output(n_sub - 1):
        task()

    tail_g = gbuf_ref[tt:tt + CONV_PAD, :]
    tail_p = pbuf_ref[SUBLANES + tt:p0 + tt, :]
    gbuf_ref[0:CONV_PAD, :] = tail_g
    pbuf_ref[SUBLANES:p0, :] = tail_p

    @pl.when(j == pl.num_programs(1) - 1)
    def _():
        conv_ref[0] = tail_g[CONV_PAD - CONV_BUF:, :]
        pool_ref[0] = tail_p[POOL_PAD - POOL_BUF:, :]


def _odd_prompt(x, batch, g, od, *, block_rows):
    rows, d = x.shape
    t = rows // batch
    tt = min(block_rows, t)
    nj = t // tt
    full = lambda a: pl.BlockSpec(a.shape, (lambda b, j: (0,) * a.ndim), pipeline_mode=pl.Buffered(1))
    operands = (g, od["w_in"], od["conv_w"], od["conv_b"], od["ln_g"], od["ln_b"], od["pool_w"],
                od["pool_scale"], od["w_out"])
    return pl.pallas_call(
        _odd_prompt_body,
        grid=(batch, nj),
        in_specs=[pl.BlockSpec((tt, d), lambda b, j: (b * nj + j, 0))] + [full(a) for a in operands],
        out_specs=[
            pl.BlockSpec((tt, d), lambda b, j: (b * nj + j, 0)),
            pl.BlockSpec((1, CONV_BUF, CONV_W), lambda b, j: (b, 0, 0)),
            pl.BlockSpec((1, POOL_BUF, POOL_W), lambda b, j: (b, 0, 0)),
        ],
        out_shape=[
            jax.ShapeDtypeStruct((rows, d), F32),
            jax.ShapeDtypeStruct((batch, CONV_BUF, CONV_W), F32),
            jax.ShapeDtypeStruct((batch, POOL_BUF, POOL_W), F32),
        ],
        scratch_shapes=[
            pltpu.VMEM((CONV_PAD + tt + SUBLANES, CONV_W), F32),
            pltpu.VMEM((SUBLANES + POOL_PAD + tt, POOL_W), F32),
            pltpu.VMEM((SUBLANES - 1, tt + CONV_PAD - SUBLANES, CONV_W), F32),
            pltpu.VMEM((CONV_K, SUBLANES, CONV_W), F32),
            pltpu.VMEM((3, SUBLANES + POOL_PAD + tt, POOL_W), F32),
            pltpu.VMEM((tt, d), BF16),
            pltpu.VMEM((tt, CONV_W + POOL_W), BF16),
            pltpu.VMEM((tt, POOL_W), BF16),
            pltpu.VMEM((tt, 2 * CONV_W), F32),
            pltpu.VMEM((tt, CONV_W), F32),
        ],
        compiler_params=pltpu.CompilerParams(
            dimension_semantics=("arbitrary", "arbitrary"), vmem_limit_bytes=VMEM_LIMIT_BYTES),
        name="odd_prompt",
    )(x, *operands)


def _odd_sample_body(x_ref, cbuf_ref, pbuf_ref, g_ref, win32_ref, cw_ref, cb_ref, lng_ref, lnb_ref, pw32_ref,
                     ps_ref, wout32_ref, xo_ref, conv_ref, pool_ref, win_ref, pw_ref, wout_ref):
    @pl.when(pl.program_id(0) == 0)
    def _():
        win_ref[...] = win32_ref[...].astype(BF16)
        pw_ref[...] = pw32_ref[...].astype(BF16)
        wout_ref[...] = wout32_ref[...].astype(BF16)

    x = x_ref[...]
    h = _rms(x, g_ref[...]).astype(BF16)
    a = _dot(h, win_ref[:, 0:CONV_W])
    gt = _dot(h, win_ref[:, CONV_W:2 * CONV_W])
    xp = _dot(h, win_ref[:, 2 * CONV_W:])
    glu = a * jax.nn.sigmoid(gt)
    conv = glu * cw_ref[CONV_BUF:CONV_K, :] + cb_ref[...]
    for jj in range(CONV_BUF):
        conv = conv + cbuf_ref[jj] * cw_ref[jj:jj + 1, :]
    out_c = _layernorm(conv, lng_ref[...], lnb_ref[...])
    out_c = out_c * jax.nn.sigmoid(out_c)
    outs = []
    for gi, win_len in enumerate(POOL_WINDOWS):
        lanes = slice(gi * POOL_DG, (gi + 1) * POOL_DG)
        tot = xp[:, lanes]
        for jj in range(POOL_BUF - (win_len - 1), POOL_BUF):
            tot = tot + pbuf_ref[jj, :, lanes]
        pooled = tot / float(win_len) - xp[:, lanes]
        outs.append(_dot(pooled.astype(BF16), pw_ref[gi]))
    out_d = jnp.concatenate(outs, axis=1) * ps_ref[...]
    y = _dot(jnp.concatenate([out_c, out_d], axis=1).astype(BF16), wout_ref[...])
    xo_ref[...] = x + y
    conv_ref[0:CONV_BUF - 1] = cbuf_ref[1:CONV_BUF]
    conv_ref[CONV_BUF - 1] = glu
    pool_ref[0:POOL_BUF - 1] = pbuf_ref[1:POOL_BUF]
    pool_ref[POOL_BUF - 1] = xp


def _odd_sample(x, conv_buf, pool_buf, g, od, *, block_rows=32):
    rows, d = x.shape
    bb = min(block_rows, rows)
    full = lambda a: pl.BlockSpec(a.shape, (lambda i: (0,) * a.ndim), pipeline_mode=pl.Buffered(1))
    whole = lambda shape: pl.BlockSpec(shape, (lambda i: (0,) * len(shape)))
    operands = (g, od["w_in"], od["conv_w"], od["conv_b"], od["ln_g"], od["ln_b"], od["pool_w"],
                od["pool_scale"], od["w_out"])
    w_shapes = [od["w_in"].shape, od["pool_w"].shape, od["w_out"].shape]
    return pl.pallas_call(
        _odd_sample_body,
        grid=(rows // bb,),
        in_specs=[pl.BlockSpec((bb, d), lambda i: (i, 0)),
                  pl.BlockSpec((CONV_BUF, bb, CONV_W), lambda i: (0, i, 0)),
                  pl.BlockSpec((POOL_BUF, bb, POOL_W), lambda i: (0, i, 0))]
                 + [full(a) for a in operands],
        out_specs=[
            pl.BlockSpec((bb, d), lambda i: (i, 0)),
            pl.BlockSpec((CONV_BUF, bb, CONV_W), lambda i: (0, i, 0)),
            pl.BlockSpec((POOL_BUF, bb, POOL_W), lambda i: (0, i, 0)),
        ] + [whole(s) for s in w_shapes],
        out_shape=[
            jax.ShapeDtypeStruct((rows, d), F32),
            jax.ShapeDtypeStruct(conv_buf.shape, F32),
            jax.ShapeDtypeStruct(pool_buf.shape, F32),
        ] + [jax.ShapeDtypeStruct(s, BF16) for s in w_shapes],
        compiler_params=pltpu.CompilerParams(
            dimension_semantics=("arbitrary",), vmem_limit_bytes=VMEM_LIMIT_BYTES),
        name="odd_sample",
    )(x, conv_buf, pool_buf, *operands)


def _prep_odd(w_in, conv_w, conv_b, ln_g, ln_b, pool_w, pool_scale, w_out):
    return dict(w_in=w_in, conv_w=conv_w, conv_b=conv_b.reshape(1, -1), ln_g=ln_g.reshape(1, -1),
                ln_b=ln_b.reshape(1, -1), pool_w=pool_w, pool_scale=pool_scale.reshape(1, -1), w_out=w_out)


def kernel(x_prompt, x_sample, state_gla, state_conv, state_pool, norm_g, ff_in, ff_out, ev_w_in, ev_w_gate, ev_b_gate, ev_gla_g, ev_sg_ln_g, ev_sg_ln_b, ev_sg_w, ev_sg_b, ev_w_out, od_w_in, od_conv_w, od_conv_b, od_ln_g, od_ln_b, od_pool_w, od_pool_scale, od_w_out, norm_f):
    bp, t, d = x_prompt.shape
    bs = x_sample.shape[0]
    depth = norm_g.shape[0]
    xp = x_prompt.reshape(bp * t, d)
    xs = x_sample.reshape(bs, d)
    gla_p, gla_s, sgv_p, sgv_s, conv_p, conv_s, pool_p, pool_s = [], [], [], [], [], [], [], []
    for layer in range(depth):
        i = layer // 2
        last = layer == depth - 1
        xs, xp = _ffn_fused(xs, xp, norm_g[layer, 0], ff_in, ff_out, layer, 0, norm_f, final_norm=False,
                            block_rows=FFN_ROWS)
        g_mix = norm_g[layer, 1].reshape(1, d)
        if layer % 2 == 0:
            ev = _prep_even(ev_b_gate[i], ev_gla_g[i], ev_sg_ln_g[i], ev_sg_ln_b[i], ev_sg_w[i], ev_sg_b[i])
            xs, s_s, v_s, w_in_b, w_gate_b, w_out_b, sg_w_b = _even_sample(
                xs, state_gla[i], g_mix, jnp.transpose(ev_w_in[i]), ev_w_gate[i], ev_w_out[i], ev_sg_w[i], ev)
            xp, s_p, v_p = _even_prompt(xp, bp, g_mix, w_in_b, w_gate_b, ev["b_gate"], ev["gla_g"],
                                        ev["ln_g"], ev["ln_b"], sg_w_b, ev["sg_bias"], w_out_b,
                                        block_rows=512)
            gla_p.append(s_p); gla_s.append(s_s); sgv_p.append(v_p); sgv_s.append(v_s.reshape(bs, 1, SG_W))
        else:
            od = _prep_odd(od_w_in[i], od_conv_w[i], od_conv_b[i], od_ln_g[i], od_ln_b[i], od_pool_w[i],
                           od_pool_scale[i], od_w_out[i])
            xs, c_s, p_s, w_in_b, pool_w_b, w_out_b = _odd_sample(
                xs, jnp.transpose(state_conv[i], (1, 0, 2)), jnp.transpose(state_pool[i], (1, 0, 2)), g_mix, od)
            c_s, p_s = jnp.transpose(c_s, (1, 0, 2)), jnp.transpose(p_s, (1, 0, 2))
            xp, c_p, p_p = _odd_prompt(xp, bp, g_mix, dict(od, w_in=w_in_b, pool_w=pool_w_b, w_out=w_out_b),
                                       block_rows=512)
            conv_p.append(c_p); conv_s.append(c_s); pool_p.append(p_p); pool_s.append(p_s)
        xs, xp = _ffn_fused(xs, xp, norm_g[layer, 2], ff_in, ff_out, layer, 1, norm_f, final_norm=last,
                            block_rows=FFN_ROWS)
    return (xp.reshape(bp, t, d), xs.reshape(bs, 1, d), jnp.stack(gla_p), jnp.stack(gla_s),
            jnp.stack(sgv_p), jnp.stack(sgv_s), jnp.stack(conv_p), jnp.stack(conv_s),
            jnp.stack(pool_p), jnp.stack(pool_s))
```

```python
import functools

import jax
import jax.numpy as jnp
import numpy as np
from jax import lax
from jax.experimental import pallas as pl
from jax.experimental.pallas import tpu as pltpu

EPS = 1e-6
LOG2_E = 1.4426950408889634
BF16 = jnp.bfloat16
F32 = jnp.float32

LANES = 128
SUBLANES = 8
VMEM_LIMIT_BYTES = 56 * 1024 * 1024
FFN_ROWS = 1024


def _rms(x, g):
    return x * lax.rsqrt(jnp.mean(x * x, axis=-1, keepdims=True) + EPS) * g


def _dot(a, b):
    return jnp.dot(a, b, preferred_element_type=F32)


FFN_CHUNK = 256


def _ffn_fused_body(xs_ref, xp_ref, g_ref, wa32_ref, wb32_ref, wo32_ref, gf_ref,
                    ys_ref, yp_ref, wa_ref, wb_ref, wo_ref, hs_ref, accs_ref, act_ref, hp_ref, accp_ref,
                    *, final_norm):
    s = pl.program_id(0)
    n_chunks = wa_ref.shape[0]

    def finish(x, acc):
        y = x + 0.5 * acc
        return _rms(y, gf_ref[...]) if final_norm else y

    @pl.when(s < n_chunks)
    def _():
        @pl.when(s == 0)
        def _():
            hs_ref[...] = _rms(xs_ref[...], g_ref[...]).astype(BF16)
            accs_ref[...] = jnp.zeros_like(accs_ref)
            hp_ref[...] = _rms(xp_ref[...], g_ref[...]).astype(BF16)
            accp_ref[...] = jnp.zeros_like(accp_ref)

        wa = wa32_ref[...].astype(BF16)
        wb = wb32_ref[...].astype(BF16)
        wo = wo32_ref[...].astype(BF16)
        wa_ref[s] = wa
        wb_ref[s] = wb
        wo_ref[s] = wo
        for h_ref, acc_ref in ((hs_ref, accs_ref), (hp_ref, accp_ref)):
            h = h_ref[...]
            a = _dot(h, wa)
            b = _dot(h, wb)
            acc_ref[...] += _dot((a * jax.nn.sigmoid(a) * b).astype(BF16), wo)

        @pl.when(s == n_chunks - 1)
        def _():
            ys_ref[...] = finish(xs_ref[...], accs_ref[...])
            yp_ref[...] = finish(xp_ref[...], accp_ref[...])

    @pl.when(s >= n_chunks)
    def _():
        x = xp_ref[...]
        h = _rms(x, g_ref[...]).astype(BF16)
        for c in range(n_chunks):
            a = _dot(h, wa_ref[c])
            b = _dot(h, wb_ref[c])
            act_ref[:, c * FFN_CHUNK:(c + 1) * FFN_CHUNK] = (a * jax.nn.sigmoid(a) * b).astype(BF16)
        w_out = wo_ref[...].reshape(n_chunks * FFN_CHUNK, wo_ref.shape[2])
        yp_ref[...] = finish(x, _dot(act_ref[...], w_out))


def _ffn_fused(x_sample, x_prompt, g, ff_in, ff_out, layer, slot, g_final, *, final_norm, block_rows):
    rows_s, d = x_sample.shape
    rows_p = x_prompt.shape[0]
    d_ff = ff_out.shape[2]
    n_chunks = d_ff // FFN_CHUNK
    tm = min(block_rows, rows_p)
    const = lambda s: (0, 0)
    chunk = lambda s: jnp.minimum(s, n_chunks - 1)
    tile = lambda s: (jnp.maximum(s - (n_chunks - 1), 0), 0)
    body = functools.partial(_ffn_fused_body, final_norm=final_norm)
    return pl.pallas_call(
        body,
        grid=(n_chunks - 1 + rows_p // tm,),
        in_specs=[
            pl.BlockSpec((rows_s, d), const),
            pl.BlockSpec((tm, d), tile),
            pl.BlockSpec((1, d), const),
            pl.BlockSpec((None, None, d, FFN_CHUNK), lambda s: (layer, slot, 0, chunk(s))),
            pl.BlockSpec((None, None, d, FFN_CHUNK), lambda s: (layer, slot, 0, n_chunks + chunk(s))),
            pl.BlockSpec((None, None, FFN_CHUNK, d), lambda s: (layer, slot, chunk(s), 0)),
            pl.BlockSpec((1, d), const),
        ],
        out_specs=[pl.BlockSpec((rows_s, d), const), pl.BlockSpec((tm, d), tile)],
        out_shape=[jax.ShapeDtypeStruct((rows_s, d), F32), jax.ShapeDtypeStruct((rows_p, d), F32)],
        scratch_shapes=[
            pltpu.VMEM((n_chunks, d, FFN_CHUNK), BF16),
            pltpu.VMEM((n_chunks, d, FFN_CHUNK), BF16),
            pltpu.VMEM((n_chunks, FFN_CHUNK, d), BF16),
            pltpu.VMEM((rows_s, d), BF16),
            pltpu.VMEM((rows_s, d), F32),
            pltpu.VMEM((tm, d_ff), BF16),
            pltpu.VMEM((tm, d), BF16),
            pltpu.VMEM((tm, d), F32),
        ],
        compiler_params=pltpu.CompilerParams(
            dimension_semantics=("arbitrary",), vmem_limit_bytes=VMEM_LIMIT_BYTES),
        name="ffn_fused",
    )(x_sample, x_prompt, g.reshape(1, d), ff_in, ff_in, ff_out, g_final.reshape(1, d))


GLA_H = 4
GLA_DK = 64
GLA_DV = 128
GLA_QK = GLA_H * GLA_DK
GLA_V = GLA_H * GLA_DV
GLA_RANK_PAD = 128
GLA_INV_TAU = 1.0 / 16.0
GLA_CH = 16
GLA_GROUP = 16
GLA_PROJ_COLS = 256
SG_H = 4
SG_DH = 128
SG_W = SG_H * SG_DH
SG_CHUNK = 128
EV_Q, EV_V, EV_R, EV_U, EV_VS, EV_Z, EV_END = 0, 512, 1024, 1536, 2048, 2560, 2688


def _log_sigmoid(x):
    return jnp.minimum(x, 0.0) - jnp.log(1.0 + jnp.exp(-jnp.abs(x)))


def _split3(x):
    hi = x.astype(BF16)
    r1 = x - hi.astype(F32)
    mid = r1.astype(BF16)
    lo = (r1 - mid.astype(F32)).astype(BF16)
    return hi, mid, lo


def _dot3(sel, parts):
    return _dot(sel, parts[0]) + _dot(sel, parts[1]) + _dot(sel, parts[2])


def _head_masks(width, per_head, dtype):
    lane = lax.broadcasted_iota(jnp.int32, (1, width), 1)
    return [jnp.where(lane // per_head == h, 1.0, 0.0).astype(dtype) for h in range(width // per_head)]


def _layernorm(x, g, b):
    mu = jnp.mean(x, axis=-1, keepdims=True)
    xc = x - mu
    var = jnp.mean(xc * xc, axis=-1, keepdims=True)
    return xc * lax.rsqrt(var + EPS) * g + b


def _even_prompt_body(x_ref, g_ref, win_ref, wgate_ref, bgate_ref, glag_ref, lng_ref, lnb_ref,
                      sgw_ref, sgb_ref, e_ref, wout_ref,
                      xo_ref, gla_ref, sgv_ref,
                      st_ref, q_ref, k_ref, cum2_ref, qin_ref, kd_ref, dec_ref, v_ref, o_ref,
                      pcat_ref, acat_ref, mix_ref, add_ref, stb_ref, ruv_ref):
    j = pl.program_id(1)
    tt = x_ref.shape[0]
    n_chunks = tt // GLA_CH

    @pl.when(j == 0)
    def _():
        st_ref[...] = jnp.zeros_like(st_ref)

    x = x_ref[...]
    h = _rms(x, g_ref[...]).astype(BF16)

    group = min(n_chunks, GLA_GROUP)
    g_rows = group * GLA_CH
    row = lax.broadcasted_iota(jnp.int32, (g_rows, g_rows), 0)
    col = lax.broadcasted_iota(jnp.int32, (g_rows, g_rows), 1)
    sel_cum = jnp.where(((row // GLA_CH) == (col // GLA_CH)) & (col <= row), 1.0, 0.0).astype(BF16)

    def stage1(g0):
        def run():
            rs = slice(g0 * GLA_CH, g0 * GLA_CH + g_rows)
            hg = h[rs]
            qk = _dot(hg, win_ref[:, EV_Q:EV_V])
            q = qk[:, :GLA_QK] * (GLA_DK ** -0.5)
            k = qk[:, GLA_QK:]
            v_ref[rs, :] = _dot(hg, win_ref[:, EV_V:EV_R]).astype(BF16)
            z = _dot(hg, win_ref[:, EV_Z:EV_END]).astype(BF16)
            log_a = _log_sigmoid(_dot(z, wgate_ref[...]) + bgate_ref[...]) * GLA_INV_TAU
            cum = _dot3(sel_cum, _split3(log_a))
            tot = jnp.concatenate(
                [jnp.broadcast_to(cum[(n + 1) * GLA_CH - 1:(n + 1) * GLA_CH, :], (GLA_CH, GLA_QK))
                 for n in range(group)], axis=0)
            q_ref[rs, :] = q
            k_ref[rs, :] = k
            cum2_ref[rs, :] = cum * LOG2_E
            qin_ref[rs, :] = (q * jnp.exp(cum)).astype(BF16)
            kd_ref[rs, :] = (k * jnp.exp(tot - cum)).astype(BF16)
            dec_ref[rs, :] = jnp.exp(tot)
        return run

    stage1(0)()

    half = GLA_CH // 2
    qk_masks = _head_masks(GLA_QK, GLA_DK, BF16)
    v_masks = _head_masks(GLA_V, GLA_DV, BF16)
    state = [st_ref[...]]

    def score_products(n):
        base = n * GLA_CH
        qb = q_ref[base:base + GLA_CH, :]
        cb = cum2_ref[base:base + GLA_CH, :]
        for s in range(GLA_CH):
            ks = k_ref[base + s:base + s + 1, :]
            cs = cum2_ref[base + s:base + s + 1, :]
            if s < half:
                p = qb * ks * jnp.exp2(jnp.minimum(cb - cs, 0.0))
            else:
                p_hi = qb[half:] * ks * jnp.exp2(jnp.minimum(cb[half:] - cs, 0.0))
                p = jnp.concatenate([jnp.zeros_like(p_hi), p_hi], axis=0)
            pcat_ref[base:base + GLA_CH, s * GLA_QK:(s + 1) * GLA_QK] = p.astype(BF16)

    def score_sum(rows):
        scores = _dot(pcat_ref[rows, :], e_ref[...])
        t_loc = lax.broadcasted_iota(jnp.int32, scores.shape, 0) % GLA_CH
        c_idx = lax.broadcasted_iota(jnp.int32, scores.shape, 1)
        causal = (c_idx % GLA_CH <= t_loc) & (c_idx < GLA_H * GLA_CH)
        acat_ref[rows, :] = jnp.where(causal, scores, 0.0).astype(BF16)

    def projection_piece(c0):
        def run():
            ruv_ref[:, c0 - EV_R:c0 - EV_R + GLA_PROJ_COLS] = _dot(h, win_ref[:, c0:c0 + GLA_PROJ_COLS])
        return run

    def phase1(n):
        def run():
            rows = slice(n * GLA_CH, (n + 1) * GLA_CH)
            kn = kd_ref[rows, :]
            vn = v_ref[rows, :]
            lk = jnp.concatenate([kn * m for m in qk_masks], axis=0)
            vs = jnp.concatenate([vn[:, hh * GLA_DV:(hh + 1) * GLA_DV] for hh in range(GLA_H)], axis=0)
            add_ref[n] = lax.dot_general(vs, lk, (((0,), (0,)), ((), ())), preferred_element_type=F32)
        return run

    def phase2(n):
        def run():
            stb_ref[n] = state[0].astype(BF16).T
            state[0] = state[0] * dec_ref[n * GLA_CH:n * GLA_CH + 1, :] + add_ref[n]
        return run

    def phase3(n):
        def run():
            rows = slice(n * GLA_CH, (n + 1) * GLA_CH)
            qn = qin_ref[rows, :]
            vn = v_ref[rows, :]
            lq = jnp.concatenate([qn * m for m in qk_masks], axis=0)
            oi = _dot(lq, stb_ref[n])
            o_inter = jnp.concatenate([oi[hh * GLA_CH:(hh + 1) * GLA_CH] for hh in range(GLA_H)], axis=1)
            vbd = jnp.concatenate([vn * m for m in v_masks], axis=0)
            o_ref[rows, :] = o_inter + _dot(acat_ref[rows, 0:GLA_H * GLA_CH], vbd)
        return run

    side = [projection_piece(c0) for c0 in range(EV_R, EV_Z, GLA_PROJ_COLS)]
    u = v_ln = v_lb = None
    for g0 in range(0, n_chunks, group):
        chunks = range(g0, g0 + group)
        if g0 + group < n_chunks:
            side = [stage1(g0 + group)] + side
        for idx, n in enumerate(chunks):
            score_products(n)
            for task in side[idx * len(side) // group:(idx + 1) * len(side) // group]:
                task()
        score_sum(slice(g0 * GLA_CH, (g0 + group) * GLA_CH))
        if g0 == 0:
            u = jax.nn.gelu(ruv_ref[:, EV_U - EV_R:EV_VS - EV_R])
            v_ln = _layernorm(jax.nn.gelu(ruv_ref[:, EV_VS - EV_R:EV_Z - EV_R]), lng_ref[...], lnb_ref[...])
            v_lb = v_ln.astype(BF16)
        side = [phase(n) for phase in (phase1, phase2, phase3) for n in chunks]
    r = ruv_ref[:, 0:EV_U - EV_R]
    gate_r = r * jax.nn.sigmoid(r)
    sg_chunks = tt // SG_CHUNK
    sg_every = len(side) // SG_H
    for idx, task in enumerate(side):
        task()
        if idx % sg_every == 0 and idx // sg_every < SG_H:
            hh = idx // sg_every
            cols = slice(hh * SG_DH, (hh + 1) * SG_DH)
            mixed = _dot(sgw_ref[hh], jnp.concatenate(
                [v_lb[c * SG_CHUNK:(c + 1) * SG_CHUNK, cols] for c in range(sg_chunks)], axis=1))
            for c in range(sg_chunks):
                mix_ref[c * SG_CHUNK:(c + 1) * SG_CHUNK, cols] = mixed[:, c * SG_DH:(c + 1) * SG_DH]
    st_ref[...] = state[0]

    o = o_ref[...]
    o_n = jnp.concatenate(
        [_rms(o[:, hh * GLA_DV:(hh + 1) * GLA_DV], 1.0) for hh in range(GLA_H)], axis=1) * glag_ref[...]
    out_a = o_n * gate_r
    bias = jnp.concatenate([sgb_ref[...]] * (tt // SG_CHUNK), axis=0)
    out_b = u * (mix_ref[...] + bias)

    y = _dot(jnp.concatenate([out_a, out_b], axis=1).astype(BF16), wout_ref[...])
    xo_ref[...] = x + y

    @pl.when(j == pl.num_programs(1) - 1)
    def _():
        sgv_ref[0] = v_ln[tt - SG_CHUNK:, :]
        gla_ref[0] = st_ref[...].T.reshape(GLA_H, GLA_DK, GLA_DV)


def _score_sum_matrix():
    r = np.arange(GLA_CH * GLA_QK)
    c = np.arange(LANES)
    s, hh = r // GLA_QK, (r % GLA_QK) // GLA_DK
    return jnp.asarray((c[None, :] == (hh * GLA_CH + s)[:, None]).astype(np.float32), dtype=BF16)


def _even_prompt(x, batch, g, w_in, w_gate, b_gate, gla_g, ln_g, ln_b, sg_w, sg_bias, w_out, *, block_rows):
    rows, d = x.shape
    t = rows // batch
    tt = min(block_rows, t)
    nj = t // tt
    full = lambda a: pl.BlockSpec(a.shape, (lambda b, j: (0,) * a.ndim), pipeline_mode=pl.Buffered(1))
    e = _score_sum_matrix()
    operands = (g, w_in, w_gate, b_gate, gla_g, ln_g, ln_b, sg_w, sg_bias, e, w_out)
    return pl.pallas_call(
        _even_prompt_body,
        grid=(batch, nj),
        in_specs=[pl.BlockSpec((tt, d), lambda b, j: (b * nj + j, 0))] + [full(a) for a in operands],
        out_specs=[
            pl.BlockSpec((tt, d), lambda b, j: (b * nj + j, 0)),
            pl.BlockSpec((1, GLA_H, GLA_DK, GLA_DV), lambda b, j: (b, 0, 0, 0)),
            pl.BlockSpec((1, SG_CHUNK, SG_W), lambda b, j: (b, 0, 0)),
        ],
        out_shape=[
            jax.ShapeDtypeStruct((rows, d), F32),
            jax.ShapeDtypeStruct((batch, GLA_H, GLA_DK, GLA_DV), F32),
            jax.ShapeDtypeStruct((batch, SG_CHUNK, SG_W), F32),
        ],
        scratch_shapes=[
            pltpu.VMEM((GLA_DV, GLA_QK), F32),
            pltpu.VMEM((tt, GLA_QK), F32),
            pltpu.VMEM((tt, GLA_QK), F32),
            pltpu.VMEM((tt, GLA_QK), F32),
            pltpu.VMEM((tt, GLA_QK), BF16),
            pltpu.VMEM((tt, GLA_QK), BF16),
            pltpu.VMEM((tt, GLA_QK), F32),
            pltpu.VMEM((tt, GLA_V), BF16),
            pltpu.VMEM((tt, GLA_V), F32),
            pltpu.VMEM((tt, GLA_CH * GLA_QK), BF16),
            pltpu.VMEM((tt, LANES), BF16),
            pltpu.VMEM((tt, SG_W), F32),
            pltpu.VMEM((tt // GLA_CH, GLA_DV, GLA_QK), F32),
            pltpu.VMEM((tt // GLA_CH, GLA_QK, GLA_DV), BF16),
            pltpu.VMEM((tt, EV_Z - EV_R), F32),
        ],
        compiler_params=pltpu.CompilerParams(
            dimension_semantics=("arbitrary", "arbitrary"), vmem_limit_bytes=VMEM_LIMIT_BYTES),
        name="even_prompt",
    )(x, *operands)


def _prep_even(b_gate, gla_g, ln_g, ln_b, sg_w, sg_b):
    return dict(
        b_gate=b_gate.reshape(1, -1), b_gate_col=b_gate.reshape(-1, 1), gla_g=gla_g.reshape(1, -1),
        ln_g=ln_g.reshape(1, -1), ln_b=ln_b.reshape(1, -1),
        sg_bias=jnp.repeat(jnp.transpose(sg_b), SG_DH, axis=1),
        sg_w0=jnp.repeat(sg_w[:, 0, 0], SG_DH).reshape(1, -1), sg_b0=jnp.repeat(sg_b[:, 0], SG_DH).reshape(1, -1))


def _even_sample_body(x_ref, s_ref, g_ref, win32_ref, wgate32_ref, wout32_ref, sgw32_ref, bgate_ref,
                      bgatec_ref, glag_ref, lng_ref, lnb_ref, sgw0_ref, sgb0_ref,
                      xo_ref, so_ref, sgv_ref, win_ref, wgate_ref, wout_ref, sgw_ref,
                      o_ref, wint_ref, wgatet_ref):
    bb = x_ref.shape[0]

    @pl.when(pl.program_id(0) == 0)
    def _():
        z0 = 2 * GLA_QK + 2 * GLA_V
        rank = wgate32_ref.shape[0]
        wint_ref[0:z0] = win32_ref[0:z0].astype(BF16)
        wint_ref[z0:EV_Z] = win32_ref[z0 + rank:].astype(BF16)
        wint_ref[EV_Z:EV_Z + rank] = win32_ref[z0:z0 + rank].astype(BF16)
        wint_ref[EV_Z + rank:EV_END] = jnp.zeros((EV_END - EV_Z - rank, wint_ref.shape[1]), BF16)
        for c0 in range(0, EV_END, LANES):
            win_ref[:, c0:c0 + LANES] = wint_ref[c0:c0 + LANES, :].T
        wgate_ref[...] = jnp.concatenate(
            [wgate32_ref[...].astype(BF16), jnp.zeros((GLA_RANK_PAD - rank, GLA_QK), BF16)], axis=0)
        wgatet_ref[...] = wgate_ref[...].T
        wout_ref[...] = wout32_ref[...].astype(BF16)
        row = lax.broadcasted_iota(jnp.int32, (SG_CHUNK, SG_CHUNK), 0)
        col = lax.broadcasted_iota(jnp.int32, (SG_CHUNK, SG_CHUNK), 1)
        for hh in range(SG_H):
            sgw_ref[hh] = jnp.where(col <= row, sgw32_ref[hh], 0.0).astype(BF16)

    x = x_ref[...]
    h = _rms(x, g_ref[...]).astype(BF16)
    nt = (((1,), (1,)), ((), ()))
    k_t = lax.dot_general(wint_ref[EV_Q + GLA_QK:EV_V], h, nt, preferred_element_type=F32)
    k_t = k_t.astype(BF16).astype(F32)
    z_t = lax.dot_general(wint_ref[EV_Z:EV_END], h, nt, preferred_element_type=F32).astype(BF16)
    a_t = jnp.exp(_log_sigmoid(_dot(wgatet_ref[...], z_t) + bgatec_ref[...]) * GLA_INV_TAU)
    q = _dot(h, win_ref[:, EV_Q:EV_Q + GLA_QK]) * (GLA_DK ** -0.5)
    v = _dot(h, win_ref[:, EV_V:EV_R])
    v_r = v.astype(BF16).astype(F32)
    head_rows = jnp.concatenate(_head_masks(GLA_QK, GLA_DK, F32), axis=0)
    for b in range(bb):
        s_old = s_ref[b].reshape(GLA_QK, GLA_DV)
        a_c = jnp.broadcast_to(a_t[:, b:b + 1], (GLA_QK, GLA_DV))
        k_c = jnp.broadcast_to(k_t[:, b:b + 1], (GLA_QK, GLA_DV))
        v_rows = jnp.concatenate(
            [jnp.broadcast_to(v_r[b:b + 1, hh * GLA_DV:(hh + 1) * GLA_DV], (GLA_DK, GLA_DV))
             for hh in range(GLA_H)], axis=0)
        s_new = a_c * s_old + k_c * v_rows
        so_ref[b] = s_new.reshape(GLA_H, GLA_DK, GLA_DV)
        ob = _dot((q[b:b + 1, :] * head_rows).astype(BF16), s_new.astype(BF16))
        o_ref[b:b + 1, :] = jnp.concatenate([ob[hh:hh + 1] for hh in range(GLA_H)], axis=1)
    o = o_ref[...]
    o_n = jnp.concatenate(
        [_rms(o[:, hh * GLA_DV:(hh + 1) * GLA_DV], 1.0) for hh in range(GLA_H)], axis=1) * glag_ref[...]
    r = _dot(h, win_ref[:, EV_R:EV_U])
    out_a = o_n * (r * jax.nn.sigmoid(r))
    u = jax.nn.gelu(_dot(h, win_ref[:, EV_U:EV_VS]))
    v_ln = _layernorm(jax.nn.gelu(_dot(h, win_ref[:, EV_VS:EV_Z])), lng_ref[...], lnb_ref[...])
    sgv_ref[...] = v_ln
    out_b = u * (sgw0_ref[...] * v_ln + sgb0_ref[...])
    y = _dot(jnp.concatenate([out_a, out_b], axis=1).astype(BF16), wout_ref[...])
    xo_ref[...] = x + y


def _even_sample(x, state, g, w_in_t, w_gate, w_out, sg_w, ev, *, block_rows=32):
    rows, d = x.shape
    bb = min(block_rows, rows)
    full = lambda a: pl.BlockSpec(a.shape, (lambda i: (0,) * a.ndim), pipeline_mode=pl.Buffered(1))
    whole = lambda shape: pl.BlockSpec(shape, (lambda i: (0,) * len(shape)))
    operands = (g, w_in_t, w_gate, w_out, sg_w, ev["b_gate"], ev["b_gate_col"], ev["gla_g"], ev["ln_g"],
                ev["ln_b"], ev["sg_w0"], ev["sg_b0"])
    w_shapes = [(d, EV_END), (GLA_RANK_PAD, GLA_QK), w_out.shape, sg_w.shape]
    return pl.pallas_call(
        _even_sample_body,
        grid=(rows // bb,),
        in_specs=[pl.BlockSpec((bb, d), lambda i: (i, 0)),
                  pl.BlockSpec((bb, GLA_H, GLA_DK, GLA_DV), lambda i: (i, 0, 0, 0))]
                 + [full(a) for a in operands],
        out_specs=[
            pl.BlockSpec((bb, d), lambda i: (i, 0)),
            pl.BlockSpec((bb, GLA_H, GLA_DK, GLA_DV), lambda i: (i, 0, 0, 0)),
            pl.BlockSpec((bb, SG_W), lambda i: (i, 0)),
        ] + [whole(s) for s in w_shapes],
        out_shape=[
            jax.ShapeDtypeStruct((rows, d), F32),
            jax.ShapeDtypeStruct(state.shape, F32),
            jax.ShapeDtypeStruct((rows, SG_W), F32),
        ] + [jax.ShapeDtypeStruct(s, BF16) for s in w_shapes],
        scratch_shapes=[
            pltpu.VMEM((bb, GLA_V), F32),
            pltpu.VMEM((EV_END, d), BF16),
            pltpu.VMEM((GLA_QK, GLA_RANK_PAD), BF16),
        ],
        compiler_params=pltpu.CompilerParams(
            dimension_semantics=("arbitrary",), vmem_limit_bytes=VMEM_LIMIT_BYTES),
        name="even_sample",
    )(x, state, *operands)


CONV_W = 512
CONV_K = 31
CONV_BUF = CONV_K - 1
CONV_PAD = 32
POOL_W = 512
POOL_WINDOWS = (2, 4, 8, 16)
POOL_DG = POOL_W // len(POOL_WINDOWS)
POOL_BUF = 15
POOL_PAD = 16
CONV_ROWS = 32
ODD_SUB = 128
ODD_COLS = 256


def _odd_prompt_body(x_ref, g_ref, win_ref, cw_ref, cb_ref, lng_ref, lnb_ref, pw_ref, ps_ref, wout_ref,
                     xo_ref, conv_ref, pool_ref, gbuf_ref, pbuf_ref, shift_ref, cwb_ref, psum_ref,
                     h_ref, mixin_ref, pooled_ref, raw_ref, convo_ref):
    j = pl.program_id(1)
    tt = x_ref.shape[0]
    assert POOL_WINDOWS == (2, 4, 8, 16)

    p0 = SUBLANES + POOL_PAD

    @pl.when(j == 0)
    def _():
        gbuf_ref[0:CONV_PAD, :] = jnp.zeros((CONV_PAD, CONV_W), F32)
        pbuf_ref[0:p0, :] = jnp.zeros((p0, POOL_W), F32)
        psum_ref[:, 0:SUBLANES, :] = jnp.zeros((psum_ref.shape[0], SUBLANES, POOL_W), F32)
        for o in range(CONV_K):
            cwb_ref[o] = jnp.broadcast_to(cw_ref[o:o + 1, :], (SUBLANES, CONV_W))

    h_ref[...] = _rms(x_ref[...], g_ref[...]).astype(BF16)
    first = CONV_PAD - CONV_BUF
    groups = CONV_ROWS // SUBLANES
    n_sub = tt // ODD_SUB

    def project(k):
        rows = slice(k * ODD_SUB, (k + 1) * ODD_SUB)

        def piece(c0):
            def run():
                raw_ref[rows, c0:c0 + ODD_COLS] = _dot(h_ref[rows, :], win_ref[:, c0:c0 + ODD_COLS])
            return run

        return [piece(c0) for c0 in range(0, 2 * CONV_W, ODD_COLS)]

    def glu(k):
        rows = slice(k * ODD_SUB, (k + 1) * ODD_SUB)
        gbuf_ref[CONV_PAD + k * ODD_SUB:CONV_PAD + (k + 1) * ODD_SUB, :] = (
            raw_ref[rows, 0:CONV_W] * jax.nn.sigmoid(raw_ref[rows, CONV_W:2 * CONV_W]))

    def mix(k):
        r_lo = k * ODD_SUB

        def shift_copies():
            lo = 0 if k == 0 else r_lo + CONV_PAD - SUBLANES
            hi = r_lo + ODD_SUB + CONV_PAD - SUBLANES
            for rr in range(1, SUBLANES):
                shift_ref[rr - 1, lo:hi, :] = gbuf_ref[rr + lo:rr + hi, :]

        def conv_block(r0):
            def run():
                accs = [jnp.zeros((SUBLANES, CONV_W), F32) + cb_ref[...] for _ in range(groups)]
                for o in range(first, first + CONV_K):
                    rr = o % SUBLANES
                    w8 = cwb_ref[o - first]
                    for gq in range(groups):
                        lo = o - rr + r0 + gq * SUBLANES
                        src = (gbuf_ref[lo:lo + SUBLANES, :] if rr == 0
                               else shift_ref[rr - 1, lo:lo + SUBLANES, :])
                        accs[gq] = accs[gq] + src * w8
                for gq in range(groups):
                    convo_ref[r0 + gq * SUBLANES:r0 + (gq + 1) * SUBLANES, :] = accs[gq]
            return run

        def norm_act():
            out_c = _layernorm(convo_ref[r_lo:r_lo + ODD_SUB, :], lng_ref[...], lnb_ref[...])
            mixin_ref[r_lo:r_lo + ODD_SUB, 0:CONV_W] = (out_c * jax.nn.sigmoid(out_c)).astype(BF16)

        def pooling():
            lo = SUBLANES if k == 0 else p0 + r_lo
            hi = p0 + r_lo + ODD_SUB
            psum_ref[0, lo:hi, :] = pbuf_ref[lo:hi, :] + pbuf_ref[lo - 1:hi - 1, :]
            psum_ref[1, lo:hi, POOL_DG:] = psum_ref[0, lo:hi, POOL_DG:] + psum_ref[0, lo - 2:hi - 2, POOL_DG:]
            psum_ref[2, lo:hi, 2 * POOL_DG:] = (psum_ref[1, lo:hi, 2 * POOL_DG:]
                                                + psum_ref[1, lo - 4:hi - 4, 2 * POOL_DG:])
            o_lo = p0 + r_lo
            o_hi = o_lo + ODD_SUB
            tots = [psum_ref[0, o_lo:o_hi, 0:POOL_DG],
                    psum_ref[1, o_lo:o_hi, POOL_DG:2 * POOL_DG],
                    psum_ref[2, o_lo:o_hi, 2 * POOL_DG:3 * POOL_DG],
                    psum_ref[2, o_lo:o_hi, 3 * POOL_DG:] + psum_ref[2, o_lo - 8:o_hi - 8, 3 * POOL_DG:]]
            t_glob = j * tt + r_lo + lax.broadcasted_iota(jnp.int32, (ODD_SUB, 1), 0)
            for gi, win_len in enumerate(POOL_WINDOWS):
                lanes = slice(gi * POOL_DG, (gi + 1) * POOL_DG)
                cnt = jnp.minimum(win_len, t_glob + 1).astype(F32)
                pooled = tots[gi] / cnt - pbuf_ref[o_lo:o_hi, lanes]
                pooled_ref[r_lo:r_lo + ODD_SUB, lanes] = pooled.astype(BF16)

        return ([shift_copies] + [conv_block(r0) for r0 in range(r_lo, r_lo + ODD_SUB, CONV_ROWS)]
                + [norm_act, pooling])

    def output(k):
        rows = slice(k * ODD_SUB, (k + 1) * ODD_SUB)

        def pool_matmuls():
            outs = [_dot(pooled_ref[rows, gi * POOL_DG:(gi + 1) * POOL_DG], pw_ref[gi])
                    for gi in range(len(POOL_WINDOWS))]
            mixin_ref[rows, CONV_W:] = (jnp.concatenate(outs, axis=1) * ps_ref[...]).astype(BF16)

        def out_piece(c0):
            def run():
                xo_ref[rows, c0:c0 + ODD_COLS] = x_ref[rows, c0:c0 + ODD_COLS] + _dot(
                    mixin_ref[rows, :], wout_ref[:, c0:c0 + ODD_COLS])
            return run

        return [pool_matmuls] + [out_piece(c0) for c0 in range(0, xo_ref.shape[1], ODD_COLS)]

    pbuf_ref[p0:p0 + tt, :] = _dot(h_ref[...], win_ref[:, 2 * CONV_W:])
    for task in project(0):
        task()
    glu(0)
    for k in range(n_sub):
        valu_tasks = mix(k)
        mxu_tasks = output(k - 1) if k >= 1 else []
        if k + 1 < n_sub:
            mxu_tasks = project(k + 1) + mxu_tasks
            valu_tasks = valu_tasks + [functools.partial(glu, k + 1)]
        for idx, task in enumerate(valu_tasks):
            lo_m = idx * len(mxu_tasks) // len(valu_tasks)
            hi_m = (idx + 1) * len(mxu_tasks) // len(valu_tasks)
            for m_task in mxu_tasks[lo_m:hi_m]:
                m_task()
            task()
    for task in output(n_sub - 1):
        task()

    tail_g = gbuf_ref[tt:tt + CONV_PAD, :]
    tail_p = pbuf_ref[SUBLANES + tt:p0 + tt, :]
    gbuf_ref[0:CONV_PAD, :] = tail_g
    pbuf_ref[SUBLANES:p0, :] = tail_p

    @pl.when(j == pl.num_programs(1) - 1)
    def _():
        conv_ref[0] = tail_g[CONV_PAD - CONV_BUF:, :]
        pool_ref[0] = tail_p[POOL_PAD - POOL_BUF:, :]


def _odd_prompt(x, batch, g, od, *, block_rows):
    rows, d = x.shape
    t = rows // batch
    tt = min(block_rows, t)
    nj = t // tt
    full = lambda a: pl.BlockSpec(a.shape, (lambda b, j: (0,) * a.ndim), pipeline_mode=pl.Buffered(1))
    operands = (g, od["w_in"], od["conv_w"], od["conv_b"], od["ln_g"], od["ln_b"], od["pool_w"],
                od["pool_scale"], od["w_out"])
    return pl.pallas_call(
        _odd_prompt_body,
        grid=(batch, nj),
        in_specs=[pl.BlockSpec((tt, d), lambda b, j: (b * nj + j, 0))] + [full(a) for a in operands],
        out_specs=[
            pl.BlockSpec((tt, d), lambda b, j: (b * nj + j, 0)),
            pl.BlockSpec((1, CONV_BUF, CONV_W), lambda b, j: (b, 0, 0)),
            pl.BlockSpec((1, POOL_BUF, POOL_W), lambda b, j: (b, 0, 0)),
        ],
        out_shape=[
            jax.ShapeDtypeStruct((rows, d), F32),
            jax.ShapeDtypeStruct((batch, CONV_BUF, CONV_W), F32),
            jax.ShapeDtypeStruct((batch, POOL_BUF, POOL_W), F32),
        ],
        scratch_shapes=[
            pltpu.VMEM((CONV_PAD + tt + SUBLANES, CONV_W), F32),
            pltpu.VMEM((SUBLANES + POOL_PAD + tt, POOL_W), F32),
            pltpu.VMEM((SUBLANES - 1, tt + CONV_PAD - SUBLANES, CONV_W), F32),
            pltpu.VMEM((CONV_K, SUBLANES, CONV_W), F32),
            pltpu.VMEM((3, SUBLANES + POOL_PAD + tt, POOL_W), F32),
            pltpu.VMEM((tt, d), BF16),
            pltpu.VMEM((tt, CONV_W + POOL_W), BF16),
            pltpu.VMEM((tt, POOL_W), BF16),
            pltpu.VMEM((tt, 2 * CONV_W), F32),
            pltpu.VMEM((tt, CONV_W), F32),
        ],
        compiler_params=pltpu.CompilerParams(
            dimension_semantics=("arbitrary", "arbitrary"), vmem_limit_bytes=VMEM_LIMIT_BYTES),
        name="odd_prompt",
    )(x, *operands)


def _odd_sample_body(x_ref, cbuf_ref, pbuf_ref, g_ref, win32_ref, cw_ref, cb_ref, lng_ref, lnb_ref, pw32_ref,
                     ps_ref, wout32_ref, xo_ref, conv_ref, pool_ref, win_ref, pw_ref, wout_ref):
    @pl.when(pl.program_id(0) == 0)
    def _():
        win_ref[...] = win32_ref[...].astype(BF16)
        pw_ref[...] = pw32_ref[...].astype(BF16)
        wout_ref[...] = wout32_ref[...].astype(BF16)

    x = x_ref[...]
    h = _rms(x, g_ref[...]).astype(BF16)
    a = _dot(h, win_ref[:, 0:CONV_W])
    gt = _dot(h, win_ref[:, CONV_W:2 * CONV_W])
    xp = _dot(h, win_ref[:, 2 * CONV_W:])
    glu = a * jax.nn.sigmoid(gt)
    conv = glu * cw_ref[CONV_BUF:CONV_K, :] + cb_ref[...]
    for jj in range(CONV_BUF):
        conv = conv + cbuf_ref[jj] * cw_ref[jj:jj + 1, :]
    out_c = _layernorm(conv, lng_ref[...], lnb_ref[...])
    out_c = out_c * jax.nn.sigmoid(out_c)
    outs = []
    for gi, win_len in enumerate(POOL_WINDOWS):
        lanes = slice(gi * POOL_DG, (gi + 1) * POOL_DG)
        tot = xp[:, lanes]
        for jj in range(POOL_BUF - (win_len - 1), POOL_BUF):
            tot = tot + pbuf_ref[jj, :, lanes]
        pooled = tot / float(win_len) - xp[:, lanes]
        outs.append(_dot(pooled.astype(BF16), pw_ref[gi]))
    out_d = jnp.concatenate(outs, axis=1) * ps_ref[...]
    y = _dot(jnp.concatenate([out_c, out_d], axis=1).astype(BF16), wout_ref[...])
    xo_ref[...] = x + y
    conv_ref[0:CONV_BUF - 1] = cbuf_ref[1:CONV_BUF]
    conv_ref[CONV_BUF - 1] = glu
    pool_ref[0:POOL_BUF - 1] = pbuf_ref[1:POOL_BUF]
    pool_ref[POOL_BUF - 1] = xp


def _odd_sample(x, conv_buf, pool_buf, g, od, *, block_rows=32):
    rows, d = x.shape
    bb = min(block_rows, rows)
    full = lambda a: pl.BlockSpec(a.shape, (lambda i: (0,) * a.ndim), pipeline_mode=pl.Buffered(1))
    whole = lambda shape: pl.BlockSpec(shape, (lambda i: (0,) * len(shape)))
    operands = (g, od["w_in"], od["conv_w"], od["conv_b"], od["ln_g"], od["ln_b"], od["pool_w"],
                od["pool_scale"], od["w_out"])
    w_shapes = [od["w_in"].shape, od["pool_w"].shape, od["w_out"].shape]
    return pl.pallas_call(
        _odd_sample_body,
        grid=(rows // bb,),
        in_specs=[pl.BlockSpec((bb, d), lambda i: (i, 0)),
                  pl.BlockSpec((CONV_BUF, bb, CONV_W), lambda i: (0, i, 0)),
                  pl.BlockSpec((POOL_BUF, bb, POOL_W), lambda i: (0, i, 0))]
                 + [full(a) for a in operands],
        out_specs=[
            pl.BlockSpec((bb, d), lambda i: (i, 0)),
            pl.BlockSpec((CONV_BUF, bb, CONV_W), lambda i: (0, i, 0)),
            pl.BlockSpec((POOL_BUF, bb, POOL_W), lambda i: (0, i, 0)),
        ] + [whole(s) for s in w_shapes],
        out_shape=[
            jax.ShapeDtypeStruct((rows, d), F32),
            jax.ShapeDtypeStruct(conv_buf.shape, F32),
            jax.ShapeDtypeStruct(pool_buf.shape, F32),
        ] + [jax.ShapeDtypeStruct(s, BF16) for s in w_shapes],
        compiler_params=pltpu.CompilerParams(
            dimension_semantics=("arbitrary",), vmem_limit_bytes=VMEM_LIMIT_BYTES),
        name="odd_sample",
    )(x, conv_buf, pool_buf, *operands)


def _prep_odd(w_in, conv_w, conv_b, ln_g, ln_b, pool_w, pool_scale, w_out):
    return dict(w_in=w_in, conv_w=conv_w, conv_b=conv_b.reshape(1, -1), ln_g=ln_g.reshape(1, -1),
                ln_b=ln_b.reshape(1, -1), pool_w=pool_w, pool_scale=pool_scale.reshape(1, -1), w_out=w_out)


def kernel(x_prompt, x_sample, state_gla, state_conv, state_pool, norm_g, ff_in, ff_out, ev_w_in, ev_w_gate, ev_b_gate, ev_gla_g, ev_sg_ln_g, ev_sg_ln_b, ev_sg_w, ev_sg_b, ev_w_out, od_w_in, od_conv_w, od_conv_b, od_ln_g, od_ln_b, od_pool_w, od_pool_scale, od_w_out, norm_f):
    bp, t, d = x_prompt.shape
    bs = x_sample.shape[0]
    depth = norm_g.shape[0]
    xp = x_prompt.reshape(bp * t, d)
    xs = x_sample.reshape(bs, d)
    gla_p, gla_s, sgv_p, sgv_s, conv_p, conv_s, pool_p, pool_s = [], [], [], [], [], [], [], []
    for layer in range(depth):
        i = layer // 2
        last = layer == depth - 1
        xs, xp = _ffn_fused(xs, xp, norm_g[layer, 0], ff_in, ff_out, layer, 0, norm_f, final_norm=False,
                            block_rows=FFN_ROWS)
        g_mix = norm_g[layer, 1].reshape(1, d)
        if layer % 2 == 0:
            ev = _prep_even(ev_b_gate[i], ev_gla_g[i], ev_sg_ln_g[i], ev_sg_ln_b[i], ev_sg_w[i], ev_sg_b[i])
            xs, s_s, v_s, w_in_b, w_gate_b, w_out_b, sg_w_b = _even_sample(
                xs, state_gla[i], g_mix, jnp.transpose(ev_w_in[i]), ev_w_gate[i], ev_w_out[i], ev_sg_w[i], ev)
            xp, s_p, v_p = _even_prompt(xp, bp, g_mix, w_in_b, w_gate_b, ev["b_gate"], ev["gla_g"],
                                        ev["ln_g"], ev["ln_b"], sg_w_b, ev["sg_bias"], w_out_b,
                                        block_rows=512)
            gla_p.append(s_p); gla_s.append(s_s); sgv_p.append(v_p); sgv_s.append(v_s.reshape(bs, 1, SG_W))
        else:
            od = _prep_odd(od_w_in[i], od_conv_w[i], od_conv_b[i], od_ln_g[i], od_ln_b[i], od_pool_w[i],
                           od_pool_scale[i], od_w_out[i])
            xs, c_s, p_s, w_in_b, pool_w_b, w_out_b = _odd_sample(
                xs, jnp.transpose(state_conv[i], (1, 0, 2)), jnp.transpose(state_pool[i], (1, 0, 2)), g_mix, od)
            c_s, p_s = jnp.transpose(c_s, (1, 0, 2)), jnp.transpose(p_s, (1, 0, 2))
            xp, c_p, p_p = _odd_prompt(xp, bp, g_mix, dict(od, w_in=w_in_b, pool_w=pool_w_b, w_out=w_out_b),
                                       block_rows=1024)
            conv_p.append(c_p); conv_s.append(c_s); pool_p.append(p_p); pool_s.append(p_s)
        xs, xp = _ffn_fused(xs, xp, norm_g[layer, 2], ff_in, ff_out, layer, 1, norm_f, final_norm=last,
                            block_rows=FFN_ROWS)
    return (xp.reshape(bp, t, d), xs.reshape(bs, 1, d), jnp.stack(gla_p), jnp.stack(gla_s),
            jnp.stack(sgv_p), jnp.stack(sgv_s), jnp.stack(conv_p), jnp.stack(conv_s),
            jnp.stack(pool_p), jnp.stack(pool_s))
```

```python
import functools

import jax
import jax.numpy as jnp
import numpy as np
from jax import lax
from jax.experimental import pallas as pl
from jax.experimental.pallas import tpu as pltpu

EPS = 1e-6
LOG2_E = 1.4426950408889634
BF16 = jnp.bfloat16
F32 = jnp.float32

LANES = 128
SUBLANES = 8
VMEM_LIMIT_BYTES = 56 * 1024 * 1024
FFN_ROWS = 1024


def _rms(x, g):
    return x * lax.rsqrt(jnp.mean(x * x, axis=-1, keepdims=True) + EPS) * g


def _dot(a, b):
    return jnp.dot(a, b, preferred_element_type=F32)


FFN_CHUNK = 256


def _ffn_fused_body(xs_ref, xp_ref, g_ref, wa32_ref, wb32_ref, wo32_ref, gf_ref,
                    ys_ref, yp_ref, wa_ref, wb_ref, wo_ref, hs_ref, accs_ref, act_ref, hp_ref, accp_ref,
                    *, final_norm):
    s = pl.program_id(0)
    n_chunks = wa_ref.shape[0]

    def finish(x, acc):
        y = x + 0.5 * acc
        return _rms(y, gf_ref[...]) if final_norm else y

    @pl.when(s < n_chunks)
    def _():
        @pl.when(s == 0)
        def _():
            hs_ref[...] = _rms(xs_ref[...], g_ref[...]).astype(BF16)
            accs_ref[...] = jnp.zeros_like(accs_ref)
            hp_ref[...] = _rms(xp_ref[...], g_ref[...]).astype(BF16)
            accp_ref[...] = jnp.zeros_like(accp_ref)

        wa = wa32_ref[...].astype(BF16)
        wb = wb32_ref[...].astype(BF16)
        wo = wo32_ref[...].astype(BF16)
        wa_ref[s] = wa
        wb_ref[s] = wb
        wo_ref[s] = wo
        for h_ref, acc_ref in ((hs_ref, accs_ref), (hp_ref, accp_ref)):
            h = h_ref[...]
            a = _dot(h, wa)
            b = _dot(h, wb)
            acc_ref[...] += _dot((a * jax.nn.sigmoid(a) * b).astype(BF16), wo)

        @pl.when(s == n_chunks - 1)
        def _():
            ys_ref[...] = finish(xs_ref[...], accs_ref[...])
            yp_ref[...] = finish(xp_ref[...], accp_ref[...])

    @pl.when(s >= n_chunks)
    def _():
        x = xp_ref[...]
        h = _rms(x, g_ref[...]).astype(BF16)
        for c in range(n_chunks):
            a = _dot(h, wa_ref[c])
            b = _dot(h, wb_ref[c])
            act_ref[:, c * FFN_CHUNK:(c + 1) * FFN_CHUNK] = (a * jax.nn.sigmoid(a) * b).astype(BF16)
        w_out = wo_ref[...].reshape(n_chunks * FFN_CHUNK, wo_ref.shape[2])
        yp_ref[...] = finish(x, _dot(act_ref[...], w_out))


def _ffn_fused(x_sample, x_prompt, g, ff_in, ff_out, layer, slot, g_final, *, final_norm, block_rows):
    rows_s, d = x_sample.shape
    rows_p = x_prompt.shape[0]
    d_ff = ff_out.shape[2]
    n_chunks = d_ff // FFN_CHUNK
    tm = min(block_rows, rows_p)
    const = lambda s: (0, 0)
    chunk = lambda s: jnp.minimum(s, n_chunks - 1)
    tile = lambda s: (jnp.maximum(s - (n_chunks - 1), 0), 0)
    body = functools.partial(_ffn_fused_body, final_norm=final_norm)
    return pl.pallas_call(
        body,
        grid=(n_chunks - 1 + rows_p // tm,),
        in_specs=[
            pl.BlockSpec((rows_s, d), const),
            pl.BlockSpec((tm, d), tile),
            pl.BlockSpec((1, d), const),
            pl.BlockSpec((None, None, d, FFN_CHUNK), lambda s: (layer, slot, 0, chunk(s))),
            pl.BlockSpec((None, None, d, FFN_CHUNK), lambda s: (layer, slot, 0, n_chunks + chunk(s))),
            pl.BlockSpec((None, None, FFN_CHUNK, d), lambda s: (layer, slot, chunk(s), 0)),
            pl.BlockSpec((1, d), const),
        ],
        out_specs=[pl.BlockSpec((rows_s, d), const), pl.BlockSpec((tm, d), tile)],
        out_shape=[jax.ShapeDtypeStruct((rows_s, d), F32), jax.ShapeDtypeStruct((rows_p, d), F32)],
        scratch_shapes=[
            pltpu.VMEM((n_chunks, d, FFN_CHUNK), BF16),
            pltpu.VMEM((n_chunks, d, FFN_CHUNK), BF16),
            pltpu.VMEM((n_chunks, FFN_CHUNK, d), BF16),
            pltpu.VMEM((rows_s, d), BF16),
            pltpu.VMEM((rows_s, d), F32),
            pltpu.VMEM((tm, d_ff), BF16),
            pltpu.VMEM((tm, d), BF16),
            pltpu.VMEM((tm, d), F32),
        ],
        compiler_params=pltpu.CompilerParams(
            dimension_semantics=("arbitrary",), vmem_limit_bytes=VMEM_LIMIT_BYTES),
        name="ffn_fused",
    )(x_sample, x_prompt, g.reshape(1, d), ff_in, ff_in, ff_out, g_final.reshape(1, d))


GLA_H = 4
GLA_DK = 64
GLA_DV = 128
GLA_QK = GLA_H * GLA_DK
GLA_V = GLA_H * GLA_DV
GLA_RANK_PAD = 128
GLA_INV_TAU = 1.0 / 16.0
GLA_CH = 16
GLA_GROUP = 16
GLA_PROJ_COLS = 256
SG_H = 4
SG_DH = 128
SG_W = SG_H * SG_DH
SG_CHUNK = 128
EV_Q, EV_V, EV_R, EV_U, EV_VS, EV_Z, EV_END = 0, 512, 1024, 1536, 2048, 2560, 2688


def _log_sigmoid(x):
    return jnp.minimum(x, 0.0) - jnp.log(1.0 + jnp.exp(-jnp.abs(x)))


def _split3(x):
    hi = x.astype(BF16)
    r1 = x - hi.astype(F32)
    mid = r1.astype(BF16)
    lo = (r1 - mid.astype(F32)).astype(BF16)
    return hi, mid, lo


def _dot3(sel, parts):
    return _dot(sel, parts[0]) + _dot(sel, parts[1]) + _dot(sel, parts[2])


def _head_masks(width, per_head, dtype):
    lane = lax.broadcasted_iota(jnp.int32, (1, width), 1)
    return [jnp.where(lane // per_head == h, 1.0, 0.0).astype(dtype) for h in range(width // per_head)]


def _layernorm(x, g, b):
    mu = jnp.mean(x, axis=-1, keepdims=True)
    xc = x - mu
    var = jnp.mean(xc * xc, axis=-1, keepdims=True)
    return xc * lax.rsqrt(var + EPS) * g + b


def _even_prompt_body(x_ref, g_ref, win_ref, wgate_ref, bgate_ref, glag_ref, lng_ref, lnb_ref,
                      sgw_ref, sgb_ref, e_ref, wout_ref,
                      xo_ref, gla_ref, sgv_ref,
                      st_ref, q_ref, k_ref, cum2_ref, qin_ref, kd_ref, dec_ref, v_ref, o_ref,
                      pcat_ref, acat_ref, mix_ref, add_ref, stb_ref, ruv_ref):
    j = pl.program_id(1)
    tt = x_ref.shape[0]
    n_chunks = tt // GLA_CH

    @pl.when(j == 0)
    def _():
        st_ref[...] = jnp.zeros_like(st_ref)

    x = x_ref[...]
    h = _rms(x, g_ref[...]).astype(BF16)

    group = min(n_chunks, GLA_GROUP)
    g_rows = group * GLA_CH
    row = lax.broadcasted_iota(jnp.int32, (g_rows, g_rows), 0)
    col = lax.broadcasted_iota(jnp.int32, (g_rows, g_rows), 1)
    sel_cum = jnp.where(((row // GLA_CH) == (col // GLA_CH)) & (col <= row), 1.0, 0.0).astype(BF16)

    def stage1(g0):
        def run():
            rs = slice(g0 * GLA_CH, g0 * GLA_CH + g_rows)
            hg = h[rs]
            qk = _dot(hg, win_ref[:, EV_Q:EV_V])
            q = qk[:, :GLA_QK] * (GLA_DK ** -0.5)
            k = qk[:, GLA_QK:]
            v_ref[rs, :] = _dot(hg, win_ref[:, EV_V:EV_R]).astype(BF16)
            z = _dot(hg, win_ref[:, EV_Z:EV_END]).astype(BF16)
            log_a = _log_sigmoid(_dot(z, wgate_ref[...]) + bgate_ref[...]) * GLA_INV_TAU
            cum = _dot3(sel_cum, _split3(log_a))
            tot = jnp.concatenate(
                [jnp.broadcast_to(cum[(n + 1) * GLA_CH - 1:(n + 1) * GLA_CH, :], (GLA_CH, GLA_QK))
                 for n in range(group)], axis=0)
            q_ref[rs, :] = q
            k_ref[rs, :] = k
            cum2_ref[rs, :] = cum * LOG2_E
            qin_ref[rs, :] = (q * jnp.exp(cum)).astype(BF16)
            kd_ref[rs, :] = (k * jnp.exp(tot - cum)).astype(BF16)
            dec_ref[rs, :] = jnp.exp(tot)
        return run

    stage1(0)()

    half = GLA_CH // 2
    qk_masks = _head_masks(GLA_QK, GLA_DK, BF16)
    v_masks = _head_masks(GLA_V, GLA_DV, BF16)
    state = [st_ref[...]]

    def score_products(n):
        base = n * GLA_CH
        qb = q_ref[base:base + GLA_CH, :]
        cb = cum2_ref[base:base + GLA_CH, :]
        for s in range(GLA_CH):
            ks = k_ref[base + s:base + s + 1, :]
            cs = cum2_ref[base + s:base + s + 1, :]
            if s < half:
                p = qb * ks * jnp.exp2(jnp.minimum(cb - cs, 0.0))
            else:
                p_hi = qb[half:] * ks * jnp.exp2(jnp.minimum(cb[half:] - cs, 0.0))
                p = jnp.concatenate([jnp.zeros_like(p_hi), p_hi], axis=0)
            pcat_ref[base:base + GLA_CH, s * GLA_QK:(s + 1) * GLA_QK] = p.astype(BF16)

    def score_sum(rows):
        scores = _dot(pcat_ref[rows, :], e_ref[...])
        t_loc = lax.broadcasted_iota(jnp.int32, scores.shape, 0) % GLA_CH
        c_idx = lax.broadcasted_iota(jnp.int32, scores.shape, 1)
        causal = (c_idx % GLA_CH <= t_loc) & (c_idx < GLA_H * GLA_CH)
        acat_ref[rows, :] = jnp.where(causal, scores, 0.0).astype(BF16)

    def projection_piece(c0):
        def run():
            ruv_ref[:, c0 - EV_R:c0 - EV_R + GLA_PROJ_COLS] = _dot(h, win_ref[:, c0:c0 + GLA_PROJ_COLS])
        return run

    def phase1(n):
        def run():
            rows = slice(n * GLA_CH, (n + 1) * GLA_CH)
            kn = kd_ref[rows, :]
            vn = v_ref[rows, :]
            lk = jnp.concatenate([kn * m for m in qk_masks], axis=0)
            vs = jnp.concatenate([vn[:, hh * GLA_DV:(hh + 1) * GLA_DV] for hh in range(GLA_H)], axis=0)
            add_ref[n] = lax.dot_general(vs, lk, (((0,), (0,)), ((), ())), preferred_element_type=F32)
        return run

    def phase2(n):
        def run():
            stb_ref[n] = state[0].astype(BF16).T
            state[0] = state[0] * dec_ref[n * GLA_CH:n * GLA_CH + 1, :] + add_ref[n]
        return run

    def phase3(n):
        def run():
            rows = slice(n * GLA_CH, (n + 1) * GLA_CH)
            qn = qin_ref[rows, :]
            vn = v_ref[rows, :]
            lq = jnp.concatenate([qn * m for m in qk_masks], axis=0)
            oi = _dot(lq, stb_ref[n])
            o_inter = jnp.concatenate([oi[hh * GLA_CH:(hh + 1) * GLA_CH] for hh in range(GLA_H)], axis=1)
            vbd = jnp.concatenate([vn * m for m in v_masks], axis=0)
            o_ref[rows, :] = o_inter + _dot(acat_ref[rows, 0:GLA_H * GLA_CH], vbd)
        return run

    side = [projection_piece(c0) for c0 in range(EV_R, EV_Z, GLA_PROJ_COLS)]
    u = v_ln = v_lb = None
    for g0 in range(0, n_chunks, group):
        chunks = range(g0, g0 + group)
        if g0 + group < n_chunks:
            side = [stage1(g0 + group)] + side
        for idx, n in enumerate(chunks):
            score_products(n)
            for task in side[idx * len(side) // group:(idx + 1) * len(side) // group]:
                task()
        score_sum(slice(g0 * GLA_CH, (g0 + group) * GLA_CH))
        if g0 == 0:
            u = jax.nn.gelu(ruv_ref[:, EV_U - EV_R:EV_VS - EV_R])
            v_ln = _layernorm(jax.nn.gelu(ruv_ref[:, EV_VS - EV_R:EV_Z - EV_R]), lng_ref[...], lnb_ref[...])
            v_lb = v_ln.astype(BF16)
        side = [phase(n) for phase in (phase1, phase2, phase3) for n in chunks]
    r = ruv_ref[:, 0:EV_U - EV_R]
    gate_r = r * jax.nn.sigmoid(r)
    sg_chunks = tt // SG_CHUNK
    sg_every = len(side) // SG_H
    for idx, task in enumerate(side):
        task()
        if idx % sg_every == 0 and idx // sg_every < SG_H:
            hh = idx // sg_every
            cols = slice(hh * SG_DH, (hh + 1) * SG_DH)
            mixed = _dot(sgw_ref[hh], jnp.concatenate(
                [v_lb[c * SG_CHUNK:(c + 1) * SG_CHUNK, cols] for c in range(sg_chunks)], axis=1))
            for c in range(sg_chunks):
                mix_ref[c * SG_CHUNK:(c + 1) * SG_CHUNK, cols] = mixed[:, c * SG_DH:(c + 1) * SG_DH]
    st_ref[...] = state[0]

    o = o_ref[...]
    o_n = jnp.concatenate(
        [_rms(o[:, hh * GLA_DV:(hh + 1) * GLA_DV], 1.0) for hh in range(GLA_H)], axis=1) * glag_ref[...]
    out_a = o_n * gate_r
    bias = jnp.concatenate([sgb_ref[...]] * (tt // SG_CHUNK), axis=0)
    out_b = u * (mix_ref[...] + bias)

    y = _dot(jnp.concatenate([out_a, out_b], axis=1).astype(BF16), wout_ref[...])
    xo_ref[...] = x + y

    @pl.when(j == pl.num_programs(1) - 1)
    def _():
        sgv_ref[0] = v_ln[tt - SG_CHUNK:, :]
        gla_ref[0] = st_ref[...].T.reshape(GLA_H, GLA_DK, GLA_DV)


def _score_sum_matrix():
    r = np.arange(GLA_CH * GLA_QK)
    c = np.arange(LANES)
    s, hh = r // GLA_QK, (r % GLA_QK) // GLA_DK
    return jnp.asarray((c[None, :] == (hh * GLA_CH + s)[:, None]).astype(np.float32), dtype=BF16)


def _even_prompt(x, batch, g, w_in, w_gate, b_gate, gla_g, ln_g, ln_b, sg_w, sg_bias, w_out, *, block_rows):
    rows, d = x.shape
    t = rows // batch
    tt = min(block_rows, t)
    nj = t // tt
    full = lambda a: pl.BlockSpec(a.shape, (lambda b, j: (0,) * a.ndim), pipeline_mode=pl.Buffered(1))
    e = _score_sum_matrix()
    operands = (g, w_in, w_gate, b_gate, gla_g, ln_g, ln_b, sg_w, sg_bias, e, w_out)
    return pl.pallas_call(
        _even_prompt_body,
        grid=(batch, nj),
        in_specs=[pl.BlockSpec((tt, d), lambda b, j: (b * nj + j, 0))] + [full(a) for a in operands],
        out_specs=[
            pl.BlockSpec((tt, d), lambda b, j: (b * nj + j, 0)),
            pl.BlockSpec((1, GLA_H, GLA_DK, GLA_DV), lambda b, j: (b, 0, 0, 0)),
            pl.BlockSpec((1, SG_CHUNK, SG_W), lambda b, j: (b, 0, 0)),
        ],
        out_shape=[
            jax.ShapeDtypeStruct((rows, d), F32),
            jax.ShapeDtypeStruct((batch, GLA_H, GLA_DK, GLA_DV), F32),
            jax.ShapeDtypeStruct((batch, SG_CHUNK, SG_W), F32),
        ],
        scratch_shapes=[
            pltpu.VMEM((GLA_DV, GLA_QK), F32),
            pltpu.VMEM((tt, GLA_QK), F32),
            pltpu.VMEM((tt, GLA_QK), F32),
            pltpu.VMEM((tt, GLA_QK), F32),
            pltpu.VMEM((tt, GLA_QK), BF16),
            pltpu.VMEM((tt, GLA_QK), BF16),
            pltpu.VMEM((tt, GLA_QK), F32),
            pltpu.VMEM((tt, GLA_V), BF16),
            pltpu.VMEM((tt, GLA_V), F32),
            pltpu.VMEM((tt, GLA_CH * GLA_QK), BF16),
            pltpu.VMEM((tt, LANES), BF16),
            pltpu.VMEM((tt, SG_W), F32),
            pltpu.VMEM((tt // GLA_CH, GLA_DV, GLA_QK), F32),
            pltpu.VMEM((tt // GLA_CH, GLA_QK, GLA_DV), BF16),
            pltpu.VMEM((tt, EV_Z - EV_R), F32),
        ],
        compiler_params=pltpu.CompilerParams(
            dimension_semantics=("arbitrary", "arbitrary"), vmem_limit_bytes=VMEM_LIMIT_BYTES),
        name="even_prompt",
    )(x, *operands)


def _prep_even(b_gate, gla_g, ln_g, ln_b, sg_w, sg_b):
    return dict(
        b_gate=b_gate.reshape(1, -1), b_gate_col=b_gate.reshape(-1, 1), gla_g=gla_g.reshape(1, -1),
        ln_g=ln_g.reshape(1, -1), ln_b=ln_b.reshape(1, -1),
        sg_bias=jnp.repeat(jnp.transpose(sg_b), SG_DH, axis=1),
        sg_w0=jnp.repeat(sg_w[:, 0, 0], SG_DH).reshape(1, -1), sg_b0=jnp.repeat(sg_b[:, 0], SG_DH).reshape(1, -1))


def _even_sample_body(x_ref, s_ref, g_ref, win32_ref, wgate32_ref, wout32_ref, sgw32_ref, bgate_ref,
                      bgatec_ref, glag_ref, lng_ref, lnb_ref, sgw0_ref, sgb0_ref,
                      xo_ref, so_ref, sgv_ref, win_ref, wgate_ref, wout_ref, sgw_ref,
                      o_ref, wint_ref, wgatet_ref):
    bb = x_ref.shape[0]

    @pl.when(pl.program_id(0) == 0)
    def _():
        z0 = 2 * GLA_QK + 2 * GLA_V
        rank = wgate32_ref.shape[0]
        wint_ref[0:z0] = win32_ref[0:z0].astype(BF16)
        wint_ref[z0:EV_Z] = win32_ref[z0 + rank:].astype(BF16)
        wint_ref[EV_Z:EV_Z + rank] = win32_ref[z0:z0 + rank].astype(BF16)
        wint_ref[EV_Z + rank:EV_END] = jnp.zeros((EV_END - EV_Z - rank, wint_ref.shape[1]), BF16)
        for c0 in range(0, EV_END, LANES):
            win_ref[:, c0:c0 + LANES] = wint_ref[c0:c0 + LANES, :].T
        wgate_ref[...] = jnp.concatenate(
            [wgate32_ref[...].astype(BF16), jnp.zeros((GLA_RANK_PAD - rank, GLA_QK), BF16)], axis=0)
        wgatet_ref[...] = wgate_ref[...].T
        wout_ref[...] = wout32_ref[...].astype(BF16)
        row = lax.broadcasted_iota(jnp.int32, (SG_CHUNK, SG_CHUNK), 0)
        col = lax.broadcasted_iota(jnp.int32, (SG_CHUNK, SG_CHUNK), 1)
        for hh in range(SG_H):
            sgw_ref[hh] = jnp.where(col <= row, sgw32_ref[hh], 0.0).astype(BF16)

    x = x_ref[...]
    h = _rms(x, g_ref[...]).astype(BF16)
    nt = (((1,), (1,)), ((), ()))
    k_t = lax.dot_general(wint_ref[EV_Q + GLA_QK:EV_V], h, nt, preferred_element_type=F32)
    k_t = k_t.astype(BF16).astype(F32)
    z_t = lax.dot_general(wint_ref[EV_Z:EV_END], h, nt, preferred_element_type=F32).astype(BF16)
    a_t = jnp.exp(_log_sigmoid(_dot(wgatet_ref[...], z_t) + bgatec_ref[...]) * GLA_INV_TAU)
    q = _dot(h, win_ref[:, EV_Q:EV_Q + GLA_QK]) * (GLA_DK ** -0.5)
    v = _dot(h, win_ref[:, EV_V:EV_R])
    v_r = v.astype(BF16).astype(F32)
    head_rows = jnp.concatenate(_head_masks(GLA_QK, GLA_DK, F32), axis=0)
    for b in range(bb):
        s_old = s_ref[b].reshape(GLA_QK, GLA_DV)
        a_c = jnp.broadcast_to(a_t[:, b:b + 1], (GLA_QK, GLA_DV))
        k_c = jnp.broadcast_to(k_t[:, b:b + 1], (GLA_QK, GLA_DV))
        v_rows = jnp.concatenate(
            [jnp.broadcast_to(v_r[b:b + 1, hh * GLA_DV:(hh + 1) * GLA_DV], (GLA_DK, GLA_DV))
             for hh in range(GLA_H)], axis=0)
        s_new = a_c * s_old + k_c * v_rows
        so_ref[b] = s_new.reshape(GLA_H, GLA_DK, GLA_DV)
        ob = _dot((q[b:b + 1, :] * head_rows).astype(BF16), s_new.astype(BF16))
        o_ref[b:b + 1, :] = jnp.concatenate([ob[hh:hh + 1] for hh in range(GLA_H)], axis=1)
    o = o_ref[...]
    o_n = jnp.concatenate(
        [_rms(o[:, hh * GLA_DV:(hh + 1) * GLA_DV], 1.0) for hh in range(GLA_H)], axis=1) * glag_ref[...]
    r = _dot(h, win_ref[:, EV_R:EV_U])
    out_a = o_n * (r * jax.nn.sigmoid(r))
    u = jax.nn.gelu(_dot(h, win_ref[:, EV_U:EV_VS]))
    v_ln = _layernorm(jax.nn.gelu(_dot(h, win_ref[:, EV_VS:EV_Z])), lng_ref[...], lnb_ref[...])
    sgv_ref[...] = v_ln
    out_b = u * (sgw0_ref[...] * v_ln + sgb0_ref[...])
    y = _dot(jnp.concatenate([out_a, out_b], axis=1).astype(BF16), wout_ref[...])
    xo_ref[...] = x + y


def _even_sample(x, state, g, w_in_t, w_gate, w_out, sg_w, ev, *, block_rows=32):
    rows, d = x.shape
    bb = min(block_rows, rows)
    full = lambda a: pl.BlockSpec(a.shape, (lambda i: (0,) * a.ndim), pipeline_mode=pl.Buffered(1))
    whole = lambda shape: pl.BlockSpec(shape, (lambda i: (0,) * len(shape)))
    operands = (g, w_in_t, w_gate, w_out, sg_w, ev["b_gate"], ev["b_gate_col"], ev["gla_g"], ev["ln_g"],
                ev["ln_b"], ev["sg_w0"], ev["sg_b0"])
    w_shapes = [(d, EV_END), (GLA_RANK_PAD, GLA_QK), w_out.shape, sg_w.shape]
    return pl.pallas_call(
        _even_sample_body,
        grid=(rows // bb,),
        in_specs=[pl.BlockSpec((bb, d), lambda i: (i, 0)),
                  pl.BlockSpec((bb, GLA_H, GLA_DK, GLA_DV), lambda i: (i, 0, 0, 0))]
                 + [full(a) for a in operands],
        out_specs=[
            pl.BlockSpec((bb, d), lambda i: (i, 0)),
            pl.BlockSpec((bb, GLA_H, GLA_DK, GLA_DV), lambda i: (i, 0, 0, 0)),
            pl.BlockSpec((bb, SG_W), lambda i: (i, 0)),
        ] + [whole(s) for s in w_shapes],
        out_shape=[
            jax.ShapeDtypeStruct((rows, d), F32),
            jax.ShapeDtypeStruct(state.shape, F32),
            jax.ShapeDtypeStruct((rows, SG_W), F32),
        ] + [jax.ShapeDtypeStruct(s, BF16) for s in w_shapes],
        scratch_shapes=[
            pltpu.VMEM((bb, GLA_V), F32),
            pltpu.VMEM((EV_END, d), BF16),
            pltpu.VMEM((GLA_QK, GLA_RANK_PAD), BF16),
        ],
        compiler_params=pltpu.CompilerParams(
            dimension_semantics=("arbitrary",), vmem_limit_bytes=VMEM_LIMIT_BYTES),
        name="even_sample",
    )(x, state, *operands)


CONV_W = 512
CONV_K = 31
CONV_BUF = CONV_K - 1
CONV_PAD = 32
POOL_W = 512
POOL_WINDOWS = (2, 4, 8, 16)
POOL_DG = POOL_W // len(POOL_WINDOWS)
POOL_BUF = 15
POOL_PAD = 16
CONV_ROWS = 32
ODD_SUB = 128
ODD_COLS = 256


def _odd_prompt_body(x_ref, g_ref, win_ref, cw_ref, cb_ref, lng_ref, lnb_ref, pw_ref, ps_ref, wout_ref,
                     xo_ref, conv_ref, pool_ref, gbuf_ref, pbuf_ref, shift_ref, cwb_ref, psum_ref,
                     h_ref, mixin_ref, pooled_ref, raw_ref, convo_ref):
    j = pl.program_id(1)
    tt = x_ref.shape[0]
    assert POOL_WINDOWS == (2, 4, 8, 16)

    p0 = SUBLANES + POOL_PAD

    @pl.when(j == 0)
    def _():
        gbuf_ref[0:CONV_PAD, :] = jnp.zeros((CONV_PAD, CONV_W), F32)
        pbuf_ref[0:p0, :] = jnp.zeros((p0, POOL_W), F32)
        psum_ref[:, 0:SUBLANES, :] = jnp.zeros((psum_ref.shape[0], SUBLANES, POOL_W), F32)
        for o in range(CONV_K):
            cwb_ref[o, :, 0:CONV_W] = jnp.broadcast_to(cw_ref[o:o + 1, :], (SUBLANES, CONV_W))

    h_ref[...] = _rms(x_ref[...], g_ref[...]).astype(BF16)
    first = CONV_PAD - CONV_BUF
    groups = CONV_ROWS // SUBLANES
    n_sub = tt // ODD_SUB

    def project(k):
        rows = slice(k * ODD_SUB, (k + 1) * ODD_SUB)

        def piece(c0):
            def run():
                raw_ref[rows, c0:c0 + ODD_COLS] = _dot(h_ref[rows, :], win_ref[:, c0:c0 + ODD_COLS])
            return run

        return [piece(c0) for c0 in range(0, 2 * CONV_W, ODD_COLS)]

    def glu(k):
        rows = slice(k * ODD_SUB, (k + 1) * ODD_SUB)
        gbuf_ref[CONV_PAD + k * ODD_SUB:CONV_PAD + (k + 1) * ODD_SUB, :] = (
            raw_ref[rows, 0:CONV_W] * jax.nn.sigmoid(raw_ref[rows, CONV_W:2 * CONV_W]))

    def mix(k):
        r_lo = k * ODD_SUB

        def shift_copies():
            lo = 0 if k == 0 else r_lo + CONV_PAD - SUBLANES
            hi = r_lo + ODD_SUB + CONV_PAD - SUBLANES
            for rr in range(1, SUBLANES):
                shift_ref[rr - 1, lo:hi, 0:CONV_W] = gbuf_ref[rr + lo:rr + hi, :]

        def conv_block(r0):
            def run():
                accs = [jnp.zeros((SUBLANES, CONV_W), F32) + cb_ref[...] for _ in range(groups)]
                for o in range(first, first + CONV_K):
                    rr = o % SUBLANES
                    w8 = cwb_ref[o - first, :, 0:CONV_W]
                    for gq in range(groups):
                        lo = o - rr + r0 + gq * SUBLANES
                        src = (gbuf_ref[lo:lo + SUBLANES, :] if rr == 0
                               else shift_ref[rr - 1, lo:lo + SUBLANES, 0:CONV_W])
                        accs[gq] = accs[gq] + src * w8
                for gq in range(groups):
                    convo_ref[r0 + gq * SUBLANES:r0 + (gq + 1) * SUBLANES, :] = accs[gq]
            return run

        def norm_act():
            out_c = _layernorm(convo_ref[r_lo:r_lo + ODD_SUB, :], lng_ref[...], lnb_ref[...])
            mixin_ref[r_lo:r_lo + ODD_SUB, 0:CONV_W] = (out_c * jax.nn.sigmoid(out_c)).astype(BF16)

        def pooling():
            lo = SUBLANES if k == 0 else p0 + r_lo
            hi = p0 + r_lo + ODD_SUB
            psum_ref[0, lo:hi, :] = pbuf_ref[lo:hi, :] + pbuf_ref[lo - 1:hi - 1, :]
            psum_ref[1, lo:hi, POOL_DG:] = psum_ref[0, lo:hi, POOL_DG:] + psum_ref[0, lo - 2:hi - 2, POOL_DG:]
            psum_ref[2, lo:hi, 2 * POOL_DG:] = (psum_ref[1, lo:hi, 2 * POOL_DG:]
                                                + psum_ref[1, lo - 4:hi - 4, 2 * POOL_DG:])
            o_lo = p0 + r_lo
            o_hi = o_lo + ODD_SUB
            tots = [psum_ref[0, o_lo:o_hi, 0:POOL_DG],
                    psum_ref[1, o_lo:o_hi, POOL_DG:2 * POOL_DG],
                    psum_ref[2, o_lo:o_hi, 2 * POOL_DG:3 * POOL_DG],
                    psum_ref[2, o_lo:o_hi, 3 * POOL_DG:] + psum_ref[2, o_lo - 8:o_hi - 8, 3 * POOL_DG:]]
            t_glob = j * tt + r_lo + lax.broadcasted_iota(jnp.int32, (ODD_SUB, 1), 0)
            for gi, win_len in enumerate(POOL_WINDOWS):
                lanes = slice(gi * POOL_DG, (gi + 1) * POOL_DG)
                cnt = jnp.minimum(win_len, t_glob + 1).astype(F32)
                pooled = tots[gi] / cnt - pbuf_ref[o_lo:o_hi, lanes]
                pooled_ref[r_lo:r_lo + ODD_SUB, lanes] = pooled.astype(BF16)

        return ([shift_copies] + [conv_block(r0) for r0 in range(r_lo, r_lo + ODD_SUB, CONV_ROWS)]
                + [norm_act, pooling])

    def output(k):
        rows = slice(k * ODD_SUB, (k + 1) * ODD_SUB)

        def pool_matmuls():
            outs = [_dot(pooled_ref[rows, gi * POOL_DG:(gi + 1) * POOL_DG], pw_ref[gi])
                    for gi in range(len(POOL_WINDOWS))]
            mixin_ref[rows, CONV_W:] = (jnp.concatenate(outs, axis=1) * ps_ref[...]).astype(BF16)

        def out_piece(c0):
            def run():
                xo_ref[rows, c0:c0 + ODD_COLS] = x_ref[rows, c0:c0 + ODD_COLS] + _dot(
                    mixin_ref[rows, :], wout_ref[:, c0:c0 + ODD_COLS])
            return run

        return [pool_matmuls] + [out_piece(c0) for c0 in range(0, xo_ref.shape[1], ODD_COLS)]

    pbuf_ref[p0:p0 + tt, :] = _dot(h_ref[...], win_ref[:, 2 * CONV_W:])
    for task in project(0):
        task()
    glu(0)
    for k in range(n_sub):
        valu_tasks = mix(k)
        mxu_tasks = output(k - 1) if k >= 1 else []
        if k + 1 < n_sub:
            mxu_tasks = project(k + 1) + mxu_tasks
            valu_tasks = valu_tasks + [functools.partial(glu, k + 1)]
        for idx, task in enumerate(valu_tasks):
            lo_m = idx * len(mxu_tasks) // len(valu_tasks)
            hi_m = (idx + 1) * len(mxu_tasks) // len(valu_tasks)
            for m_task in mxu_tasks[lo_m:hi_m]:
                m_task()
            task()
    for task in output(n_sub - 1):
        task()

    tail_g = gbuf_ref[tt:tt + CONV_PAD, :]
    tail_p = pbuf_ref[SUBLANES + tt:p0 + tt, :]
    gbuf_ref[0:CONV_PAD, :] = tail_g
    pbuf_ref[SUBLANES:p0, :] = tail_p

    @pl.when(j == pl.num_programs(1) - 1)
    def _():
        conv_ref[0] = tail_g[CONV_PAD - CONV_BUF:, :]
        pool_ref[0] = tail_p[POOL_PAD - POOL_BUF:, :]


def _odd_prompt(x, batch, g, od, *, block_rows):
    rows, d = x.shape
    t = rows // batch
    tt = min(block_rows, t)
    nj = t // tt
    full = lambda a: pl.BlockSpec(a.shape, (lambda b, j: (0,) * a.ndim), pipeline_mode=pl.Buffered(1))
    operands = (g, od["w_in"], od["conv_w"], od["conv_b"], od["ln_g"], od["ln_b"], od["pool_w"],
                od["pool_scale"], od["w_out"])
    return pl.pallas_call(
        _odd_prompt_body,
        grid=(batch, nj),
        in_specs=[pl.BlockSpec((tt, d), lambda b, j: (b * nj + j, 0))] + [full(a) for a in operands],
        out_specs=[
            pl.BlockSpec((tt, d), lambda b, j: (b * nj + j, 0)),
            pl.BlockSpec((1, CONV_BUF, CONV_W), lambda b, j: (b, 0, 0)),
            pl.BlockSpec((1, POOL_BUF, POOL_W), lambda b, j: (b, 0, 0)),
        ],
        out_shape=[
            jax.ShapeDtypeStruct((rows, d), F32),
            jax.ShapeDtypeStruct((batch, CONV_BUF, CONV_W), F32),
            jax.ShapeDtypeStruct((batch, POOL_BUF, POOL_W), F32),
        ],
        scratch_shapes=[
            pltpu.VMEM((CONV_PAD + tt + SUBLANES, CONV_W), F32),
            pltpu.VMEM((SUBLANES + POOL_PAD + tt, POOL_W), F32),
            pltpu.VMEM((SUBLANES - 1, tt + CONV_PAD - SUBLANES, CONV_W + LANES), F32),
            pltpu.VMEM((CONV_K, SUBLANES, CONV_W + LANES), F32),
            pltpu.VMEM((3, SUBLANES + POOL_PAD + tt, POOL_W), F32),
            pltpu.VMEM((tt, d), BF16),
            pltpu.VMEM((tt, CONV_W + POOL_W), BF16),
            pltpu.VMEM((tt, POOL_W), BF16),
            pltpu.VMEM((tt, 2 * CONV_W), F32),
            pltpu.VMEM((tt, CONV_W), F32),
        ],
        compiler_params=pltpu.CompilerParams(
            dimension_semantics=("arbitrary", "arbitrary"), vmem_limit_bytes=VMEM_LIMIT_BYTES),
        name="odd_prompt",
    )(x, *operands)


def _odd_sample_body(x_ref, cbuf_ref, pbuf_ref, g_ref, win32_ref, cw_ref, cb_ref, lng_ref, lnb_ref, pw32_ref,
                     ps_ref, wout32_ref, xo_ref, conv_ref, pool_ref, win_ref, pw_ref, wout_ref):
    @pl.when(pl.program_id(0) == 0)
    def _():
        win_ref[...] = win32_ref[...].astype(BF16)
        pw_ref[...] = pw32_ref[...].astype(BF16)
        wout_ref[...] = wout32_ref[...].astype(BF16)

    x = x_ref[...]
    h = _rms(x, g_ref[...]).astype(BF16)
    a = _dot(h, win_ref[:, 0:CONV_W])
    gt = _dot(h, win_ref[:, CONV_W:2 * CONV_W])
    xp = _dot(h, win_ref[:, 2 * CONV_W:])
    glu = a * jax.nn.sigmoid(gt)
    conv = glu * cw_ref[CONV_BUF:CONV_K, :] + cb_ref[...]
    for jj in range(CONV_BUF):
        conv = conv + cbuf_ref[jj] * cw_ref[jj:jj + 1, :]
    out_c = _layernorm(conv, lng_ref[...], lnb_ref[...])
    out_c = out_c * jax.nn.sigmoid(out_c)
    outs = []
    for gi, win_len in enumerate(POOL_WINDOWS):
        lanes = slice(gi * POOL_DG, (gi + 1) * POOL_DG)
        tot = xp[:, lanes]
        for jj in range(POOL_BUF - (win_len - 1), POOL_BUF):
            tot = tot + pbuf_ref[jj, :, lanes]
        pooled = tot / float(win_len) - xp[:, lanes]
        outs.append(_dot(pooled.astype(BF16), pw_ref[gi]))
    out_d = jnp.concatenate(outs, axis=1) * ps_ref[...]
    y = _dot(jnp.concatenate([out_c, out_d], axis=1).astype(BF16), wout_ref[...])
    xo_ref[...] = x + y
    conv_ref[0:CONV_BUF - 1] = cbuf_ref[1:CONV_BUF]
    conv_ref[CONV_BUF - 1] = glu
    pool_ref[0:POOL_BUF - 1] = pbuf_ref[1:POOL_BUF]
    pool_ref[POOL_BUF - 1] = xp


def _odd_sample(x, conv_buf, pool_buf, g, od, *, block_rows=32):
    rows, d = x.shape
    bb = min(block_rows, rows)
    full = lambda a: pl.BlockSpec(a.shape, (lambda i: (0,) * a.ndim), pipeline_mode=pl.Buffered(1))
    whole = lambda shape: pl.BlockSpec(shape, (lambda i: (0,) * len(shape)))
    operands = (g, od["w_in"], od["conv_w"], od["conv_b"], od["ln_g"], od["ln_b"], od["pool_w"],
                od["pool_scale"], od["w_out"])
    w_shapes = [od["w_in"].shape, od["pool_w"].shape, od["w_out"].shape]
    return pl.pallas_call(
        _odd_sample_body,
        grid=(rows // bb,),
        in_specs=[pl.BlockSpec((bb, d), lambda i: (i, 0)),
                  pl.BlockSpec((CONV_BUF, bb, CONV_W), lambda i: (0, i, 0)),
                  pl.BlockSpec((POOL_BUF, bb, POOL_W), lambda i: (0, i, 0))]
                 + [full(a) for a in operands],
        out_specs=[
            pl.BlockSpec((bb, d), lambda i: (i, 0)),
            pl.BlockSpec((CONV_BUF, bb, CONV_W), lambda i: (0, i, 0)),
            pl.BlockSpec((POOL_BUF, bb, POOL_W), lambda i: (0, i, 0)),
        ] + [whole(s) for s in w_shapes],
        out_shape=[
            jax.ShapeDtypeStruct((rows, d), F32),
            jax.ShapeDtypeStruct(conv_buf.shape, F32),
            jax.ShapeDtypeStruct(pool_buf.shape, F32),
        ] + [jax.ShapeDtypeStruct(s, BF16) for s in w_shapes],
        compiler_params=pltpu.CompilerParams(
            dimension_semantics=("arbitrary",), vmem_limit_bytes=VMEM_LIMIT_BYTES),
        name="odd_sample",
    )(x, conv_buf, pool_buf, *operands)


def _prep_odd(w_in, conv_w, conv_b, ln_g, ln_b, pool_w, pool_scale, w_out):
    return dict(w_in=w_in, conv_w=conv_w, conv_b=conv_b.reshape(1, -1), ln_g=ln_g.reshape(1, -1),
                ln_b=ln_b.reshape(1, -1), pool_w=pool_w, pool_scale=pool_scale.reshape(1, -1), w_out=w_out)


def kernel(x_prompt, x_sample, state_gla, state_conv, state_pool, norm_g, ff_in, ff_out, ev_w_in, ev_w_gate, ev_b_gate, ev_gla_g, ev_sg_ln_g, ev_sg_ln_b, ev_sg_w, ev_sg_b, ev_w_out, od_w_in, od_conv_w, od_conv_b, od_ln_g, od_ln_b, od_pool_w, od_pool_scale, od_w_out, norm_f):
    bp, t, d = x_prompt.shape
    bs = x_sample.shape[0]
    depth = norm_g.shape[0]
    xp = x_prompt.reshape(bp * t, d)
    xs = x_sample.reshape(bs, d)
    gla_p, gla_s, sgv_p, sgv_s, conv_p, conv_s, pool_p, pool_s = [], [], [], [], [], [], [], []
    for layer in range(depth):
        i = layer // 2
        last = layer == depth - 1
        xs, xp = _ffn_fused(xs, xp, norm_g[layer, 0], ff_in, ff_out, layer, 0, norm_f, final_norm=False,
                            block_rows=FFN_ROWS)
        g_mix = norm_g[layer, 1].reshape(1, d)
        if layer % 2 == 0:
            ev = _prep_even(ev_b_gate[i], ev_gla_g[i], ev_sg_ln_g[i], ev_sg_ln_b[i], ev_sg_w[i], ev_sg_b[i])
            xs, s_s, v_s, w_in_b, w_gate_b, w_out_b, sg_w_b = _even_sample(
                xs, state_gla[i], g_mix, jnp.transpose(ev_w_in[i]), ev_w_gate[i], ev_w_out[i], ev_sg_w[i], ev)
            xp, s_p, v_p = _even_prompt(xp, bp, g_mix, w_in_b, w_gate_b, ev["b_gate"], ev["gla_g"],
                                        ev["ln_g"], ev["ln_b"], sg_w_b, ev["sg_bias"], w_out_b,
                                        block_rows=512)
            gla_p.append(s_p); gla_s.append(s_s); sgv_p.append(v_p); sgv_s.append(v_s.reshape(bs, 1, SG_W))
        else:
            od = _prep_odd(od_w_in[i], od_conv_w[i], od_conv_b[i], od_ln_g[i], od_ln_b[i], od_pool_w[i],
                           od_pool_scale[i], od_w_out[i])
            xs, c_s, p_s, w_in_b, pool_w_b, w_out_b = _odd_sample(
                xs, jnp.transpose(state_conv[i], (1, 0, 2)), jnp.transpose(state_pool[i], (1, 0, 2)), g_mix, od)
            c_s, p_s = jnp.transpose(c_s, (1, 0, 2)), jnp.transpose(p_s, (1, 0, 2))
            xp, c_p, p_p = _odd_prompt(xp, bp, g_mix, dict(od, w_in=w_in_b, pool_w=pool_w_b, w_out=w_out_b),
                                       block_rows=1024)
            conv_p.append(c_p); conv_s.append(c_s); pool_p.append(p_p); pool_s.append(p_s)
        xs, xp = _ffn_fused(xs, xp, norm_g[layer, 2], ff_in, ff_out, layer, 1, norm_f, final_norm=last,
                            block_rows=FFN_ROWS)
    return (xp.reshape(bp, t, d), xs.reshape(bs, 1, d), jnp.stack(gla_p), jnp.stack(gla_s),
            jnp.stack(sgv_p), jnp.stack(sgv_s), jnp.stack(conv_p), jnp.stack(conv_s),
            jnp.stack(pool_p), jnp.stack(pool_s))
```

```python
import functools

import jax
import jax.numpy as jnp
import numpy as np
from jax import lax
from jax.experimental import pallas as pl
from jax.experimental.pallas import tpu as pltpu

EPS = 1e-6
LOG2_E = 1.4426950408889634
BF16 = jnp.bfloat16
F32 = jnp.float32

LANES = 128
SUBLANES = 8
VMEM_LIMIT_BYTES = 56 * 1024 * 1024
FFN_ROWS = 1024


def _rms(x, g):
    return x * lax.rsqrt(jnp.mean(x * x, axis=-1, keepdims=True) + EPS) * g


def _dot(a, b):
    return jnp.dot(a, b, preferred_element_type=F32)


FFN_CHUNK = 256


def _ffn_fused_body(xs_ref, xp_ref, g_ref, wa32_ref, wb32_ref, wo32_ref, gf_ref,
                    ys_ref, yp_ref, wa_ref, wb_ref, wo_ref, hs_ref, accs_ref, act_ref, hp_ref, accp_ref,
                    *, final_norm):
    s = pl.program_id(0)
    n_chunks = wa_ref.shape[0]

    def finish(x, acc):
        y = x + 0.5 * acc
        return _rms(y, gf_ref[...]) if final_norm else y

    @pl.when(s < n_chunks)
    def _():
        @pl.when(s == 0)
        def _():
            hs_ref[...] = _rms(xs_ref[...], g_ref[...]).astype(BF16)
            accs_ref[...] = jnp.zeros_like(accs_ref)
            hp_ref[...] = _rms(xp_ref[...], g_ref[...]).astype(BF16)
            accp_ref[...] = jnp.zeros_like(accp_ref)

        wa = wa32_ref[...].astype(BF16)
        wb = wb32_ref[...].astype(BF16)
        wo = wo32_ref[...].astype(BF16)
        wa_ref[s] = wa
        wb_ref[s] = wb
        wo_ref[s] = wo
        for h_ref, acc_ref in ((hs_ref, accs_ref), (hp_ref, accp_ref)):
            h = h_ref[...]
            a = _dot(h, wa)
            b = _dot(h, wb)
            acc_ref[...] += _dot((a * jax.nn.sigmoid(a) * b).astype(BF16), wo)

        @pl.when(s == n_chunks - 1)
        def _():
            ys_ref[...] = finish(xs_ref[...], accs_ref[...])
            yp_ref[...] = finish(xp_ref[...], accp_ref[...])

    @pl.when(s >= n_chunks)
    def _():
        x = xp_ref[...]
        h = _rms(x, g_ref[...]).astype(BF16)
        for c in range(n_chunks):
            a = _dot(h, wa_ref[c])
            b = _dot(h, wb_ref[c])
            act_ref[:, c * FFN_CHUNK:(c + 1) * FFN_CHUNK] = (a * jax.nn.sigmoid(a) * b).astype(BF16)
        w_out = wo_ref[...].reshape(n_chunks * FFN_CHUNK, wo_ref.shape[2])
        yp_ref[...] = finish(x, _dot(act_ref[...], w_out))


def _ffn_fused(x_sample, x_prompt, g, ff_in, ff_out, layer, slot, g_final, *, final_norm, block_rows):
    rows_s, d = x_sample.shape
    rows_p = x_prompt.shape[0]
    d_ff = ff_out.shape[2]
    n_chunks = d_ff // FFN_CHUNK
    tm = min(block_rows, rows_p)
    const = lambda s: (0, 0)
    chunk = lambda s: jnp.minimum(s, n_chunks - 1)
    tile = lambda s: (jnp.maximum(s - (n_chunks - 1), 0), 0)
    body = functools.partial(_ffn_fused_body, final_norm=final_norm)
    return pl.pallas_call(
        body,
        grid=(n_chunks - 1 + rows_p // tm,),
        in_specs=[
            pl.BlockSpec((rows_s, d), const),
            pl.BlockSpec((tm, d), tile),
            pl.BlockSpec((1, d), const),
            pl.BlockSpec((None, None, d, FFN_CHUNK), lambda s: (layer, slot, 0, chunk(s))),
            pl.BlockSpec((None, None, d, FFN_CHUNK), lambda s: (layer, slot, 0, n_chunks + chunk(s))),
            pl.BlockSpec((None, None, FFN_CHUNK, d), lambda s: (layer, slot, chunk(s), 0)),
            pl.BlockSpec((1, d), const),
        ],
        out_specs=[pl.BlockSpec((rows_s, d), const), pl.BlockSpec((tm, d), tile)],
        out_shape=[jax.ShapeDtypeStruct((rows_s, d), F32), jax.ShapeDtypeStruct((rows_p, d), F32)],
        scratch_shapes=[
            pltpu.VMEM((n_chunks, d, FFN_CHUNK), BF16),
            pltpu.VMEM((n_chunks, d, FFN_CHUNK), BF16),
            pltpu.VMEM((n_chunks, FFN_CHUNK, d), BF16),
            pltpu.VMEM((rows_s, d), BF16),
            pltpu.VMEM((rows_s, d), F32),
            pltpu.VMEM((tm, d_ff), BF16),
            pltpu.VMEM((tm, d), BF16),
            pltpu.VMEM((tm, d), F32),
        ],
        compiler_params=pltpu.CompilerParams(
            dimension_semantics=("arbitrary",), vmem_limit_bytes=VMEM_LIMIT_BYTES),
        name="ffn_fused",
    )(x_sample, x_prompt, g.reshape(1, d), ff_in, ff_in, ff_out, g_final.reshape(1, d))


GLA_H = 4
GLA_DK = 64
GLA_DV = 128
GLA_QK = GLA_H * GLA_DK
GLA_V = GLA_H * GLA_DV
GLA_RANK_PAD = 128
GLA_INV_TAU = 1.0 / 16.0
GLA_CH = 16
GLA_GROUP = 16
GLA_PROJ_COLS = 256
SG_H = 4
SG_DH = 128
SG_W = SG_H * SG_DH
SG_CHUNK = 128
EV_Q, EV_V, EV_R, EV_U, EV_VS, EV_Z, EV_END = 0, 512, 1024, 1536, 2048, 2560, 2688


def _log_sigmoid(x):
    return jnp.minimum(x, 0.0) - jnp.log(1.0 + jnp.exp(-jnp.abs(x)))


def _split3(x):
    hi = x.astype(BF16)
    r1 = x - hi.astype(F32)
    mid = r1.astype(BF16)
    lo = (r1 - mid.astype(F32)).astype(BF16)
    return hi, mid, lo


def _dot3(sel, parts):
    return _dot(sel, parts[0]) + _dot(sel, parts[1]) + _dot(sel, parts[2])


def _head_masks(width, per_head, dtype):
    lane = lax.broadcasted_iota(jnp.int32, (1, width), 1)
    return [jnp.where(lane // per_head == h, 1.0, 0.0).astype(dtype) for h in range(width // per_head)]


def _layernorm(x, g, b):
    mu = jnp.mean(x, axis=-1, keepdims=True)
    xc = x - mu
    var = jnp.mean(xc * xc, axis=-1, keepdims=True)
    return xc * lax.rsqrt(var + EPS) * g + b


def _even_prompt_body(x_ref, g_ref, win_ref, wgate_ref, bgate_ref, glag_ref, lng_ref, lnb_ref,
                      sgw_ref, sgb_ref, e_ref, wout_ref,
                      xo_ref, gla_ref, sgv_ref,
                      st_ref, q_ref, k_ref, cum2_ref, qin_ref, kd_ref, dec_ref, v_ref, o_ref,
                      pcat_ref, acat_ref, mix_ref, add_ref, stb_ref, ruv_ref):
    j = pl.program_id(1)
    tt = x_ref.shape[0]
    n_chunks = tt // GLA_CH

    @pl.when(j == 0)
    def _():
        st_ref[...] = jnp.zeros_like(st_ref)

    x = x_ref[...]
    h = _rms(x, g_ref[...]).astype(BF16)

    group = min(n_chunks, GLA_GROUP)
    g_rows = group * GLA_CH
    row = lax.broadcasted_iota(jnp.int32, (g_rows, g_rows), 0)
    col = lax.broadcasted_iota(jnp.int32, (g_rows, g_rows), 1)
    sel_cum = jnp.where(((row // GLA_CH) == (col // GLA_CH)) & (col <= row), 1.0, 0.0).astype(BF16)

    def stage1(g0):
        def run():
            rs = slice(g0 * GLA_CH, g0 * GLA_CH + g_rows)
            hg = h[rs]
            qk = _dot(hg, win_ref[:, EV_Q:EV_V])
            q = qk[:, :GLA_QK] * (GLA_DK ** -0.5)
            k = qk[:, GLA_QK:]
            v_ref[rs, :] = _dot(hg, win_ref[:, EV_V:EV_R]).astype(BF16)
            z = _dot(hg, win_ref[:, EV_Z:EV_END]).astype(BF16)
            log_a = _log_sigmoid(_dot(z, wgate_ref[...]) + bgate_ref[...]) * GLA_INV_TAU
            cum = _dot3(sel_cum, _split3(log_a))
            tot = jnp.concatenate(
                [jnp.broadcast_to(cum[(n + 1) * GLA_CH - 1:(n + 1) * GLA_CH, :], (GLA_CH, GLA_QK))
                 for n in range(group)], axis=0)
            q_ref[rs, :] = q
            k_ref[rs, :] = k
            cum2_ref[rs, :] = cum * LOG2_E
            qin_ref[rs, :] = (q * jnp.exp(cum)).astype(BF16)
            kd_ref[rs, :] = (k * jnp.exp(tot - cum)).astype(BF16)
            dec_ref[rs, :] = jnp.exp(tot)
        return run

    stage1(0)()

    half = GLA_CH // 2
    qk_masks = _head_masks(GLA_QK, GLA_DK, BF16)
    v_masks = _head_masks(GLA_V, GLA_DV, BF16)
    state = [st_ref[...]]

    def score_products(n):
        base = n * GLA_CH
        qb = q_ref[base:base + GLA_CH, :]
        cb = cum2_ref[base:base + GLA_CH, :]
        for s in range(GLA_CH):
            ks = k_ref[base + s:base + s + 1, :]
            cs = cum2_ref[base + s:base + s + 1, :]
            if s < half:
                p = qb * ks * jnp.exp2(jnp.minimum(cb - cs, 0.0))
            else:
                p_hi = qb[half:] * ks * jnp.exp2(jnp.minimum(cb[half:] - cs, 0.0))
                p = jnp.concatenate([jnp.zeros_like(p_hi), p_hi], axis=0)
            pcat_ref[base:base + GLA_CH, s * GLA_QK:(s + 1) * GLA_QK] = p.astype(BF16)

    def score_sum(rows):
        scores = _dot(pcat_ref[rows, :], e_ref[...])
        t_loc = lax.broadcasted_iota(jnp.int32, scores.shape, 0) % GLA_CH
        c_idx = lax.broadcasted_iota(jnp.int32, scores.shape, 1)
        causal = (c_idx % GLA_CH <= t_loc) & (c_idx < GLA_H * GLA_CH)
        acat_ref[rows, :] = jnp.where(causal, scores, 0.0).astype(BF16)

    def projection_piece(c0):
        def run():
            ruv_ref[:, c0 - EV_R:c0 - EV_R + GLA_PROJ_COLS] = _dot(h, win_ref[:, c0:c0 + GLA_PROJ_COLS])
        return run

    def phase1(n):
        def run():
            rows = slice(n * GLA_CH, (n + 1) * GLA_CH)
            kn = kd_ref[rows, :]
            vn = v_ref[rows, :]
            lk = jnp.concatenate([kn * m for m in qk_masks], axis=0)
            vs = jnp.concatenate([vn[:, hh * GLA_DV:(hh + 1) * GLA_DV] for hh in range(GLA_H)], axis=0)
            add_ref[n] = lax.dot_general(vs, lk, (((0,), (0,)), ((), ())), preferred_element_type=F32)
        return run

    def phase2(n):
        def run():
            stb_ref[n] = state[0].astype(BF16).T
            state[0] = state[0] * dec_ref[n * GLA_CH:n * GLA_CH + 1, :] + add_ref[n]
        return run

    def phase3(n):
        def run():
            rows = slice(n * GLA_CH, (n + 1) * GLA_CH)
            qn = qin_ref[rows, :]
            vn = v_ref[rows, :]
            lq = jnp.concatenate([qn * m for m in qk_masks], axis=0)
            oi = _dot(lq, stb_ref[n])
            o_inter = jnp.concatenate([oi[hh * GLA_CH:(hh + 1) * GLA_CH] for hh in range(GLA_H)], axis=1)
            vbd = jnp.concatenate([vn * m for m in v_masks], axis=0)
            o_ref[rows, :] = o_inter + _dot(acat_ref[rows, 0:GLA_H * GLA_CH], vbd)
        return run

    side = [projection_piece(c0) for c0 in range(EV_R, EV_Z, GLA_PROJ_COLS)]
    u = v_ln = v_lb = None
    for g0 in range(0, n_chunks, group):
        chunks = range(g0, g0 + group)
        if g0 + group < n_chunks:
            side = [stage1(g0 + group)] + side
        for idx, n in enumerate(chunks):
            score_products(n)
            for task in side[idx * len(side) // group:(idx + 1) * len(side) // group]:
                task()
        score_sum(slice(g0 * GLA_CH, (g0 + group) * GLA_CH))
        if g0 == 0:
            u = jax.nn.gelu(ruv_ref[:, EV_U - EV_R:EV_VS - EV_R])
            v_ln = _layernorm(jax.nn.gelu(ruv_ref[:, EV_VS - EV_R:EV_Z - EV_R]), lng_ref[...], lnb_ref[...])
            v_lb = v_ln.astype(BF16)
        side = [phase(n) for phase in (phase1, phase2, phase3) for n in chunks]
    r = ruv_ref[:, 0:EV_U - EV_R]
    gate_r = r * jax.nn.sigmoid(r)
    sg_chunks = tt // SG_CHUNK
    sg_every = len(side) // SG_H
    for idx, task in enumerate(side):
        task()
        if idx % sg_every == 0 and idx // sg_every < SG_H:
            hh = idx // sg_every
            cols = slice(hh * SG_DH, (hh + 1) * SG_DH)
            mixed = _dot(sgw_ref[hh], jnp.concatenate(
                [v_lb[c * SG_CHUNK:(c + 1) * SG_CHUNK, cols] for c in range(sg_chunks)], axis=1))
            for c in range(sg_chunks):
                mix_ref[c * SG_CHUNK:(c + 1) * SG_CHUNK, cols] = mixed[:, c * SG_DH:(c + 1) * SG_DH]
    st_ref[...] = state[0]

    o = o_ref[...]
    o_n = jnp.concatenate(
        [_rms(o[:, hh * GLA_DV:(hh + 1) * GLA_DV], 1.0) for hh in range(GLA_H)], axis=1) * glag_ref[...]
    out_a = o_n * gate_r
    bias = jnp.concatenate([sgb_ref[...]] * (tt // SG_CHUNK), axis=0)
    out_b = u * (mix_ref[...] + bias)

    y = _dot(jnp.concatenate([out_a, out_b], axis=1).astype(BF16), wout_ref[...])
    xo_ref[...] = x + y

    @pl.when(j == pl.num_programs(1) - 1)
    def _():
        sgv_ref[0] = v_ln[tt - SG_CHUNK:, :]
        gla_ref[0] = st_ref[...].T.reshape(GLA_H, GLA_DK, GLA_DV)


def _score_sum_matrix():
    r = np.arange(GLA_CH * GLA_QK)
    c = np.arange(LANES)
    s, hh = r // GLA_QK, (r % GLA_QK) // GLA_DK
    return jnp.asarray((c[None, :] == (hh * GLA_CH + s)[:, None]).astype(np.float32), dtype=BF16)


def _even_prompt(x, batch, g, w_in, w_gate, b_gate, gla_g, ln_g, ln_b, sg_w, sg_bias, w_out, *, block_rows):
    rows, d = x.shape
    t = rows // batch
    tt = min(block_rows, t)
    nj = t // tt
    full = lambda a: pl.BlockSpec(a.shape, (lambda b, j: (0,) * a.ndim), pipeline_mode=pl.Buffered(1))
    e = _score_sum_matrix()
    operands = (g, w_in, w_gate, b_gate, gla_g, ln_g, ln_b, sg_w, sg_bias, e, w_out)
    return pl.pallas_call(
        _even_prompt_body,
        grid=(batch, nj),
        in_specs=[pl.BlockSpec((tt, d), lambda b, j: (b * nj + j, 0))] + [full(a) for a in operands],
        out_specs=[
            pl.BlockSpec((tt, d), lambda b, j: (b * nj + j, 0)),
            pl.BlockSpec((1, GLA_H, GLA_DK, GLA_DV), lambda b, j: (b, 0, 0, 0)),
            pl.BlockSpec((1, SG_CHUNK, SG_W), lambda b, j: (b, 0, 0)),
        ],
        out_shape=[
            jax.ShapeDtypeStruct((rows, d), F32),
            jax.ShapeDtypeStruct((batch, GLA_H, GLA_DK, GLA_DV), F32),
            jax.ShapeDtypeStruct((batch, SG_CHUNK, SG_W), F32),
        ],
        scratch_shapes=[
            pltpu.VMEM((GLA_DV, GLA_QK), F32),
            pltpu.VMEM((tt, GLA_QK), F32),
            pltpu.VMEM((tt, GLA_QK), F32),
            pltpu.VMEM((tt, GLA_QK), F32),
            pltpu.VMEM((tt, GLA_QK), BF16),
            pltpu.VMEM((tt, GLA_QK), BF16),
            pltpu.VMEM((tt, GLA_QK), F32),
            pltpu.VMEM((tt, GLA_V), BF16),
            pltpu.VMEM((tt, GLA_V), F32),
            pltpu.VMEM((tt, GLA_CH * GLA_QK), BF16),
            pltpu.VMEM((tt, LANES), BF16),
            pltpu.VMEM((tt, SG_W), F32),
            pltpu.VMEM((tt // GLA_CH, GLA_DV, GLA_QK), F32),
            pltpu.VMEM((tt // GLA_CH, GLA_QK, GLA_DV), BF16),
            pltpu.VMEM((tt, EV_Z - EV_R), F32),
        ],
        compiler_params=pltpu.CompilerParams(
            dimension_semantics=("arbitrary", "arbitrary"), vmem_limit_bytes=VMEM_LIMIT_BYTES),
        name="even_prompt",
    )(x, *operands)


def _prep_even(b_gate, gla_g, ln_g, ln_b, sg_w, sg_b):
    return dict(
        b_gate=b_gate.reshape(1, -1), b_gate_col=b_gate.reshape(-1, 1), gla_g=gla_g.reshape(1, -1),
        ln_g=ln_g.reshape(1, -1), ln_b=ln_b.reshape(1, -1),
        sg_bias=jnp.repeat(jnp.transpose(sg_b), SG_DH, axis=1),
        sg_w0=jnp.repeat(sg_w[:, 0, 0], SG_DH).reshape(1, -1), sg_b0=jnp.repeat(sg_b[:, 0], SG_DH).reshape(1, -1))


def _even_sample_body(x_ref, s_ref, g_ref, win32_ref, wgate32_ref, wout32_ref, sgw32_ref, bgate_ref,
                      bgatec_ref, glag_ref, lng_ref, lnb_ref, sgw0_ref, sgb0_ref,
                      xo_ref, so_ref, sgv_ref, win_ref, wgate_ref, wout_ref, sgw_ref,
                      o_ref, wint_ref, wgatet_ref):
    bb = x_ref.shape[0]

    @pl.when(pl.program_id(0) == 0)
    def _():
        z0 = 2 * GLA_QK + 2 * GLA_V
        rank = wgate32_ref.shape[0]
        wint_ref[0:z0] = win32_ref[0:z0].astype(BF16)
        wint_ref[z0:EV_Z] = win32_ref[z0 + rank:].astype(BF16)
        wint_ref[EV_Z:EV_Z + rank] = win32_ref[z0:z0 + rank].astype(BF16)
        wint_ref[EV_Z + rank:EV_END] = jnp.zeros((EV_END - EV_Z - rank, wint_ref.shape[1]), BF16)
        for c0 in range(0, EV_END, LANES):
            win_ref[:, c0:c0 + LANES] = wint_ref[c0:c0 + LANES, :].T
        wgate_ref[...] = jnp.concatenate(
            [wgate32_ref[...].astype(BF16), jnp.zeros((GLA_RANK_PAD - rank, GLA_QK), BF16)], axis=0)
        wgatet_ref[...] = wgate_ref[...].T
        wout_ref[...] = wout32_ref[...].astype(BF16)
        row = lax.broadcasted_iota(jnp.int32, (SG_CHUNK, SG_CHUNK), 0)
        col = lax.broadcasted_iota(jnp.int32, (SG_CHUNK, SG_CHUNK), 1)
        for hh in range(SG_H):
            sgw_ref[hh] = jnp.where(col <= row, sgw32_ref[hh], 0.0).astype(BF16)

    x = x_ref[...]
    h = _rms(x, g_ref[...]).astype(BF16)
    nt = (((1,), (1,)), ((), ()))
    k_t = lax.dot_general(wint_ref[EV_Q + GLA_QK:EV_V], h, nt, preferred_element_type=F32)
    k_t = k_t.astype(BF16).astype(F32)
    z_t = lax.dot_general(wint_ref[EV_Z:EV_END], h, nt, preferred_element_type=F32).astype(BF16)
    a_t = jnp.exp(_log_sigmoid(_dot(wgatet_ref[...], z_t) + bgatec_ref[...]) * GLA_INV_TAU)
    q = _dot(h, win_ref[:, EV_Q:EV_Q + GLA_QK]) * (GLA_DK ** -0.5)
    v = _dot(h, win_ref[:, EV_V:EV_R])
    v_r = v.astype(BF16).astype(F32)
    head_rows = jnp.concatenate(_head_masks(GLA_QK, GLA_DK, F32), axis=0)
    for b in range(bb):
        s_old = s_ref[b].reshape(GLA_QK, GLA_DV)
        a_c = jnp.broadcast_to(a_t[:, b:b + 1], (GLA_QK, GLA_DV))
        k_c = jnp.broadcast_to(k_t[:, b:b + 1], (GLA_QK, GLA_DV))
        v_rows = jnp.concatenate(
            [jnp.broadcast_to(v_r[b:b + 1, hh * GLA_DV:(hh + 1) * GLA_DV], (GLA_DK, GLA_DV))
             for hh in range(GLA_H)], axis=0)
        s_new = a_c * s_old + k_c * v_rows
        so_ref[b] = s_new.reshape(GLA_H, GLA_DK, GLA_DV)
        ob = _dot((q[b:b + 1, :] * head_rows).astype(BF16), s_new.astype(BF16))
        o_ref[b:b + 1, :] = jnp.concatenate([ob[hh:hh + 1] for hh in range(GLA_H)], axis=1)
    o = o_ref[...]
    o_n = jnp.concatenate(
        [_rms(o[:, hh * GLA_DV:(hh + 1) * GLA_DV], 1.0) for hh in range(GLA_H)], axis=1) * glag_ref[...]
    r = _dot(h, win_ref[:, EV_R:EV_U])
    out_a = o_n * (r * jax.nn.sigmoid(r))
    u = jax.nn.gelu(_dot(h, win_ref[:, EV_U:EV_VS]))
    v_ln = _layernorm(jax.nn.gelu(_dot(h, win_ref[:, EV_VS:EV_Z])), lng_ref[...], lnb_ref[...])
    sgv_ref[...] = v_ln
    out_b = u * (sgw0_ref[...] * v_ln + sgb0_ref[...])
    y = _dot(jnp.concatenate([out_a, out_b], axis=1).astype(BF16), wout_ref[...])
    xo_ref[...] = x + y


def _even_sample(x, state, g, w_in_t, w_gate, w_out, sg_w, ev, *, block_rows=32):
    rows, d = x.shape
    bb = min(block_rows, rows)
    full = lambda a: pl.BlockSpec(a.shape, (lambda i: (0,) * a.ndim), pipeline_mode=pl.Buffered(1))
    whole = lambda shape: pl.BlockSpec(shape, (lambda i: (0,) * len(shape)))
    operands = (g, w_in_t, w_gate, w_out, sg_w, ev["b_gate"], ev["b_gate_col"], ev["gla_g"], ev["ln_g"],
                ev["ln_b"], ev["sg_w0"], ev["sg_b0"])
    w_shapes = [(d, EV_END), (GLA_RANK_PAD, GLA_QK), w_out.shape, sg_w.shape]
    return pl.pallas_call(
        _even_sample_body,
        grid=(rows // bb,),
        in_specs=[pl.BlockSpec((bb, d), lambda i: (i, 0)),
                  pl.BlockSpec((bb, GLA_H, GLA_DK, GLA_DV), lambda i: (i, 0, 0, 0))]
                 + [full(a) for a in operands],
        out_specs=[
            pl.BlockSpec((bb, d), lambda i: (i, 0)),
            pl.BlockSpec((bb, GLA_H, GLA_DK, GLA_DV), lambda i: (i, 0, 0, 0)),
            pl.BlockSpec((bb, SG_W), lambda i: (i, 0)),
        ] + [whole(s) for s in w_shapes],
        out_shape=[
            jax.ShapeDtypeStruct((rows, d), F32),
            jax.ShapeDtypeStruct(state.shape, F32),
            jax.ShapeDtypeStruct((rows, SG_W), F32),
        ] + [jax.ShapeDtypeStruct(s, BF16) for s in w_shapes],
        scratch_shapes=[
            pltpu.VMEM((bb, GLA_V), F32),
            pltpu.VMEM((EV_END, d), BF16),
            pltpu.VMEM((GLA_QK, GLA_RANK_PAD), BF16),
        ],
        compiler_params=pltpu.CompilerParams(
            dimension_semantics=("arbitrary",), vmem_limit_bytes=VMEM_LIMIT_BYTES),
        name="even_sample",
    )(x, state, *operands)


CONV_W = 512
CONV_K = 31
CONV_BUF = CONV_K - 1
CONV_PAD = 32
POOL_W = 512
POOL_WINDOWS = (2, 4, 8, 16)
POOL_DG = POOL_W // len(POOL_WINDOWS)
POOL_BUF = 15
POOL_PAD = 16
CONV_ROWS = 64
ODD_SUB = 128
ODD_COLS = 256


def _odd_prompt_body(x_ref, g_ref, win_ref, cw_ref, cb_ref, lng_ref, lnb_ref, pw_ref, ps_ref, wout_ref,
                     xo_ref, conv_ref, pool_ref, gbuf_ref, pbuf_ref, shift_ref, cwb_ref, psum_ref,
                     h_ref, mixin_ref, pooled_ref, raw_ref, convo_ref):
    j = pl.program_id(1)
    tt = x_ref.shape[0]
    assert POOL_WINDOWS == (2, 4, 8, 16)

    p0 = SUBLANES + POOL_PAD

    @pl.when(j == 0)
    def _():
        gbuf_ref[0:CONV_PAD, :] = jnp.zeros((CONV_PAD, CONV_W), F32)
        pbuf_ref[0:p0, :] = jnp.zeros((p0, POOL_W), F32)
        psum_ref[:, 0:SUBLANES, :] = jnp.zeros((psum_ref.shape[0], SUBLANES, POOL_W), F32)
        for o in range(CONV_K):
            cwb_ref[o] = jnp.broadcast_to(cw_ref[o:o + 1, :], (SUBLANES, CONV_W))

    h_ref[...] = _rms(x_ref[...], g_ref[...]).astype(BF16)
    first = CONV_PAD - CONV_BUF
    groups = CONV_ROWS // SUBLANES
    n_sub = tt // ODD_SUB

    def project(k):
        rows = slice(k * ODD_SUB, (k + 1) * ODD_SUB)

        def piece(c0):
            def run():
                raw_ref[rows, c0:c0 + ODD_COLS] = _dot(h_ref[rows, :], win_ref[:, c0:c0 + ODD_COLS])
            return run

        return [piece(c0) for c0 in range(0, 2 * CONV_W, ODD_COLS)]

    def glu(k):
        rows = slice(k * ODD_SUB, (k + 1) * ODD_SUB)
        gbuf_ref[CONV_PAD + k * ODD_SUB:CONV_PAD + (k + 1) * ODD_SUB, :] = (
            raw_ref[rows, 0:CONV_W] * jax.nn.sigmoid(raw_ref[rows, CONV_W:2 * CONV_W]))

    def mix(k):
        r_lo = k * ODD_SUB

        def shift_copies():
            lo = 0 if k == 0 else r_lo + CONV_PAD - SUBLANES
            hi = r_lo + ODD_SUB + CONV_PAD - SUBLANES
            for rr in range(1, SUBLANES):
                shift_ref[rr - 1, lo:hi, :] = gbuf_ref[rr + lo:rr + hi, :]

        def conv_block(r0, c0):
            def run():
                lanes = slice(c0, c0 + LANES)
                accs = [jnp.zeros((SUBLANES, LANES), F32) + cb_ref[:, lanes] for _ in range(groups)]
                for o in range(first, first + CONV_K):
                    rr = o % SUBLANES
                    w8 = cwb_ref[o - first, :, lanes]
                    for gq in range(groups):
                        lo = o - rr + r0 + gq * SUBLANES
                        src = (gbuf_ref[lo:lo + SUBLANES, lanes] if rr == 0
                               else shift_ref[rr - 1, lo:lo + SUBLANES, lanes])
                        accs[gq] = accs[gq] + src * w8
                for gq in range(groups):
                    convo_ref[r0 + gq * SUBLANES:r0 + (gq + 1) * SUBLANES, lanes] = accs[gq]
            return run

        def norm_act():
            out_c = _layernorm(convo_ref[r_lo:r_lo + ODD_SUB, :], lng_ref[...], lnb_ref[...])
            mixin_ref[r_lo:r_lo + ODD_SUB, 0:CONV_W] = (out_c * jax.nn.sigmoid(out_c)).astype(BF16)

        def pooling():
            lo = SUBLANES if k == 0 else p0 + r_lo
            hi = p0 + r_lo + ODD_SUB
            psum_ref[0, lo:hi, :] = pbuf_ref[lo:hi, :] + pbuf_ref[lo - 1:hi - 1, :]
            psum_ref[1, lo:hi, POOL_DG:] = psum_ref[0, lo:hi, POOL_DG:] + psum_ref[0, lo - 2:hi - 2, POOL_DG:]
            psum_ref[2, lo:hi, 2 * POOL_DG:] = (psum_ref[1, lo:hi, 2 * POOL_DG:]
                                                + psum_ref[1, lo - 4:hi - 4, 2 * POOL_DG:])
            o_lo = p0 + r_lo
            o_hi = o_lo + ODD_SUB
            tots = [psum_ref[0, o_lo:o_hi, 0:POOL_DG],
                    psum_ref[1, o_lo:o_hi, POOL_DG:2 * POOL_DG],
                    psum_ref[2, o_lo:o_hi, 2 * POOL_DG:3 * POOL_DG],
                    psum_ref[2, o_lo:o_hi, 3 * POOL_DG:] + psum_ref[2, o_lo - 8:o_hi - 8, 3 * POOL_DG:]]
            t_glob = j * tt + r_lo + lax.broadcasted_iota(jnp.int32, (ODD_SUB, 1), 0)
            for gi, win_len in enumerate(POOL_WINDOWS):
                lanes = slice(gi * POOL_DG, (gi + 1) * POOL_DG)
                cnt = jnp.minimum(win_len, t_glob + 1).astype(F32)
                pooled = tots[gi] / cnt - pbuf_ref[o_lo:o_hi, lanes]
                pooled_ref[r_lo:r_lo + ODD_SUB, lanes] = pooled.astype(BF16)

        return ([shift_copies]
                + [conv_block(r0, c0) for r0 in range(r_lo, r_lo + ODD_SUB, CONV_ROWS)
                   for c0 in range(0, CONV_W, LANES)]
                + [norm_act, pooling])

    def output(k):
        rows = slice(k * ODD_SUB, (k + 1) * ODD_SUB)

        def pool_matmuls():
            outs = [_dot(pooled_ref[rows, gi * POOL_DG:(gi + 1) * POOL_DG], pw_ref[gi])
                    for gi in range(len(POOL_WINDOWS))]
            mixin_ref[rows, CONV_W:] = (jnp.concatenate(outs, axis=1) * ps_ref[...]).astype(BF16)

        def out_piece(c0):
            def run():
                xo_ref[rows, c0:c0 + ODD_COLS] = x_ref[rows, c0:c0 + ODD_COLS] + _dot(
                    mixin_ref[rows, :], wout_ref[:, c0:c0 + ODD_COLS])
            return run

        return [pool_matmuls] + [out_piece(c0) for c0 in range(0, xo_ref.shape[1], ODD_COLS)]

    pbuf_ref[p0:p0 + tt, :] = _dot(h_ref[...], win_ref[:, 2 * CONV_W:])
    for task in project(0):
        task()
    glu(0)
    for k in range(n_sub):
        valu_tasks = mix(k)
        mxu_tasks = output(k - 1) if k >= 1 else []
        if k + 1 < n_sub:
            mxu_tasks = project(k + 1) + mxu_tasks
            valu_tasks = valu_tasks + [functools.partial(glu, k + 1)]
        for idx, task in enumerate(valu_tasks):
            lo_m = idx * len(mxu_tasks) // len(valu_tasks)
            hi_m = (idx + 1) * len(mxu_tasks) // len(valu_tasks)
            for m_task in mxu_tasks[lo_m:hi_m]:
                m_task()
            task()
    for task in output(n_sub - 1):
        task()

    tail_g = gbuf_ref[tt:tt + CONV_PAD, :]
    tail_p = pbuf_ref[SUBLANES + tt:p0 + tt, :]
    gbuf_ref[0:CONV_PAD, :] = tail_g
    pbuf_ref[SUBLANES:p0, :] = tail_p

    @pl.when(j == pl.num_programs(1) - 1)
    def _():
        conv_ref[0] = tail_g[CONV_PAD - CONV_BUF:, :]
        pool_ref[0] = tail_p[POOL_PAD - POOL_BUF:, :]


def _odd_prompt(x, batch, g, od, *, block_rows):
    rows, d = x.shape
    t = rows // batch
    tt = min(block_rows, t)
    nj = t // tt
    full = lambda a: pl.BlockSpec(a.shape, (lambda b, j: (0,) * a.ndim), pipeline_mode=pl.Buffered(1))
    operands = (g, od["w_in"], od["conv_w"], od["conv_b"], od["ln_g"], od["ln_b"], od["pool_w"],
                od["pool_scale"], od["w_out"])
    return pl.pallas_call(
        _odd_prompt_body,
        grid=(batch, nj),
        in_specs=[pl.BlockSpec((tt, d), lambda b, j: (b * nj + j, 0))] + [full(a) for a in operands],
        out_specs=[
            pl.BlockSpec((tt, d), lambda b, j: (b * nj + j, 0)),
            pl.BlockSpec((1, CONV_BUF, CONV_W), lambda b, j: (b, 0, 0)),
            pl.BlockSpec((1, POOL_BUF, POOL_W), lambda b, j: (b, 0, 0)),
        ],
        out_shape=[
            jax.ShapeDtypeStruct((rows, d), F32),
            jax.ShapeDtypeStruct((batch, CONV_BUF, CONV_W), F32),
            jax.ShapeDtypeStruct((batch, POOL_BUF, POOL_W), F32),
        ],
        scratch_shapes=[
            pltpu.VMEM((CONV_PAD + tt + SUBLANES, CONV_W), F32),
            pltpu.VMEM((SUBLANES + POOL_PAD + tt, POOL_W), F32),
            pltpu.VMEM((SUBLANES - 1, tt + CONV_PAD - SUBLANES, CONV_W), F32),
            pltpu.VMEM((CONV_K, SUBLANES, CONV_W), F32),
            pltpu.VMEM((3, SUBLANES + POOL_PAD + tt, POOL_W), F32),
            pltpu.VMEM((tt, d), BF16),
            pltpu.VMEM((tt, CONV_W + POOL_W), BF16),
            pltpu.VMEM((tt, POOL_W), BF16),
            pltpu.VMEM((tt, 2 * CONV_W), F32),
            pltpu.VMEM((tt, CONV_W), F32),
        ],
        compiler_params=pltpu.CompilerParams(
            dimension_semantics=("arbitrary", "arbitrary"), vmem_limit_bytes=VMEM_LIMIT_BYTES),
        name="odd_prompt",
    )(x, *operands)


def _odd_sample_body(x_ref, cbuf_ref, pbuf_ref, g_ref, win32_ref, cw_ref, cb_ref, lng_ref, lnb_ref, pw32_ref,
                     ps_ref, wout32_ref, xo_ref, conv_ref, pool_ref, win_ref, pw_ref, wout_ref):
    @pl.when(pl.program_id(0) == 0)
    def _():
        win_ref[...] = win32_ref[...].astype(BF16)
        pw_ref[...] = pw32_ref[...].astype(BF16)
        wout_ref[...] = wout32_ref[...].astype(BF16)

    x = x_ref[...]
    h = _rms(x, g_ref[...]).astype(BF16)
    a = _dot(h, win_ref[:, 0:CONV_W])
    gt = _dot(h, win_ref[:, CONV_W:2 * CONV_W])
    xp = _dot(h, win_ref[:, 2 * CONV_W:])
    glu = a * jax.nn.sigmoid(gt)
    conv = glu * cw_ref[CONV_BUF:CONV_K, :] + cb_ref[...]
    for jj in range(CONV_BUF):
        conv = conv + cbuf_ref[jj] * cw_ref[jj:jj + 1, :]
    out_c = _layernorm(conv, lng_ref[...], lnb_ref[...])
    out_c = out_c * jax.nn.sigmoid(out_c)
    outs = []
    for gi, win_len in enumerate(POOL_WINDOWS):
        lanes = slice(gi * POOL_DG, (gi + 1) * POOL_DG)
        tot = xp[:, lanes]
        for jj in range(POOL_BUF - (win_len - 1), POOL_BUF):
            tot = tot + pbuf_ref[jj, :, lanes]
        pooled = tot / float(win_len) - xp[:, lanes]
        outs.append(_dot(pooled.astype(BF16), pw_ref[gi]))
    out_d = jnp.concatenate(outs, axis=1) * ps_ref[...]
    y = _dot(jnp.concatenate([out_c, out_d], axis=1).astype(BF16), wout_ref[...])
    xo_ref[...] = x + y
    conv_ref[0:CONV_BUF - 1] = cbuf_ref[1:CONV_BUF]
    conv_ref[CONV_BUF - 1] = glu
    pool_ref[0:POOL_BUF - 1] = pbuf_ref[1:POOL_BUF]
    pool_ref[POOL_BUF - 1] = xp


def _odd_sample(x, conv_buf, pool_buf, g, od, *, block_rows=32):
    rows, d = x.shape
    bb = min(block_rows, rows)
    full = lambda a: pl.BlockSpec(a.shape, (lambda i: (0,) * a.ndim), pipeline_mode=pl.Buffered(1))
    whole = lambda shape: pl.BlockSpec(shape, (lambda i: (0,) * len(shape)))
    operands = (g, od["w_in"], od["conv_w"], od["conv_b"], od["ln_g"], od["ln_b"], od["pool_w"],
                od["pool_scale"], od["w_out"])
    w_shapes = [od["w_in"].shape, od["pool_w"].shape, od["w_out"].shape]
    return pl.pallas_call(
        _odd_sample_body,
        grid=(rows // bb,),
        in_specs=[pl.BlockSpec((bb, d), lambda i: (i, 0)),
                  pl.BlockSpec((CONV_BUF, bb, CONV_W), lambda i: (0, i, 0)),
                  pl.BlockSpec((POOL_BUF, bb, POOL_W), lambda i: (0, i, 0))]
                 + [full(a) for a in operands],
        out_specs=[
            pl.BlockSpec((bb, d), lambda i: (i, 0)),
            pl.BlockSpec((CONV_BUF, bb, CONV_W), lambda i: (0, i, 0)),
            pl.BlockSpec((POOL_BUF, bb, POOL_W), lambda i: (0, i, 0)),
        ] + [whole(s) for s in w_shapes],
        out_shape=[
            jax.ShapeDtypeStruct((rows, d), F32),
            jax.ShapeDtypeStruct(conv_buf.shape, F32),
            jax.ShapeDtypeStruct(pool_buf.shape, F32),
        ] + [jax.ShapeDtypeStruct(s, BF16) for s in w_shapes],
        compiler_params=pltpu.CompilerParams(
            dimension_semantics=("arbitrary",), vmem_limit_bytes=VMEM_LIMIT_BYTES),
        name="odd_sample",
    )(x, conv_buf, pool_buf, *operands)


def _prep_odd(w_in, conv_w, conv_b, ln_g, ln_b, pool_w, pool_scale, w_out):
    return dict(w_in=w_in, conv_w=conv_w, conv_b=conv_b.reshape(1, -1), ln_g=ln_g.reshape(1, -1),
                ln_b=ln_b.reshape(1, -1), pool_w=pool_w, pool_scale=pool_scale.reshape(1, -1), w_out=w_out)


def kernel(x_prompt, x_sample, state_gla, state_conv, state_pool, norm_g, ff_in, ff_out, ev_w_in, ev_w_gate, ev_b_gate, ev_gla_g, ev_sg_ln_g, ev_sg_ln_b, ev_sg_w, ev_sg_b, ev_w_out, od_w_in, od_conv_w, od_conv_b, od_ln_g, od_ln_b, od_pool_w, od_pool_scale, od_w_out, norm_f):
    bp, t, d = x_prompt.shape
    bs = x_sample.shape[0]
    depth = norm_g.shape[0]
    xp = x_prompt.reshape(bp * t, d)
    xs = x_sample.reshape(bs, d)
    gla_p, gla_s, sgv_p, sgv_s, conv_p, conv_s, pool_p, pool_s = [], [], [], [], [], [], [], []
    for layer in range(depth):
        i = layer // 2
        last = layer == depth - 1
        xs, xp = _ffn_fused(xs, xp, norm_g[layer, 0], ff_in, ff_out, layer, 0, norm_f, final_norm=False,
                            block_rows=FFN_ROWS)
        g_mix = norm_g[layer, 1].reshape(1, d)
        if layer % 2 == 0:
            ev = _prep_even(ev_b_gate[i], ev_gla_g[i], ev_sg_ln_g[i], ev_sg_ln_b[i], ev_sg_w[i], ev_sg_b[i])
            xs, s_s, v_s, w_in_b, w_gate_b, w_out_b, sg_w_b = _even_sample(
                xs, state_gla[i], g_mix, jnp.transpose(ev_w_in[i]), ev_w_gate[i], ev_w_out[i], ev_sg_w[i], ev)
            xp, s_p, v_p = _even_prompt(xp, bp, g_mix, w_in_b, w_gate_b, ev["b_gate"], ev["gla_g"],
                                        ev["ln_g"], ev["ln_b"], sg_w_b, ev["sg_bias"], w_out_b,
                                        block_rows=512)
            gla_p.append(s_p); gla_s.append(s_s); sgv_p.append(v_p); sgv_s.append(v_s.reshape(bs, 1, SG_W))
        else:
            od = _prep_odd(od_w_in[i], od_conv_w[i], od_conv_b[i], od_ln_g[i], od_ln_b[i], od_pool_w[i],
                           od_pool_scale[i], od_w_out[i])
            xs, c_s, p_s, w_in_b, pool_w_b, w_out_b = _odd_sample(
                xs, jnp.transpose(state_conv[i], (1, 0, 2)), jnp.transpose(state_pool[i], (1, 0, 2)), g_mix, od)
            c_s, p_s = jnp.transpose(c_s, (1, 0, 2)), jnp.transpose(p_s, (1, 0, 2))
            xp, c_p, p_p = _odd_prompt(xp, bp, g_mix, dict(od, w_in=w_in_b, pool_w=pool_w_b, w_out=w_out_b),
                                       block_rows=1024)
            conv_p.append(c_p); conv_s.append(c_s); pool_p.append(p_p); pool_s.append(p_s)
        xs, xp = _ffn_fused(xs, xp, norm_g[layer, 2], ff_in, ff_out, layer, 1, norm_f, final_norm=last,
                            block_rows=FFN_ROWS)
    return (xp.reshape(bp, t, d), xs.reshape(bs, 1, d), jnp.stack(gla_p), jnp.stack(gla_s),
            jnp.stack(sgv_p), jnp.stack(sgv_s), jnp.stack(conv_p), jnp.stack(conv_s),
            jnp.stack(pool_p), jnp.stack(pool_s))
```

```python
import functools

import jax
import jax.numpy as jnp
import numpy as np
from jax import lax
from jax.experimental import pallas as pl
from jax.experimental.pallas import tpu as pltpu

EPS = 1e-6
LOG2_E = 1.4426950408889634
BF16 = jnp.bfloat16
F32 = jnp.float32

LANES = 128
SUBLANES = 8
VMEM_LIMIT_BYTES = 56 * 1024 * 1024
FFN_ROWS = 1024


def _rms(x, g):
    return x * lax.rsqrt(jnp.mean(x * x, axis=-1, keepdims=True) + EPS) * g


def _dot(a, b):
    return jnp.dot(a, b, preferred_element_type=F32)


FFN_CHUNK = 256


def _ffn_fused_body(xs_ref, xp_ref, g_ref, wa32_ref, wb32_ref, wo32_ref, gf_ref,
                    ys_ref, yp_ref, wa_ref, wb_ref, wo_ref, hs_ref, accs_ref, act_ref, hp_ref, accp_ref,
                    *, final_norm):
    s = pl.program_id(0)
    n_chunks = wa_ref.shape[0]

    def finish(x, acc):
        y = x + 0.5 * acc
        return _rms(y, gf_ref[...]) if final_norm else y

    @pl.when(s < n_chunks)
    def _():
        @pl.when(s == 0)
        def _():
            hs_ref[...] = _rms(xs_ref[...], g_ref[...]).astype(BF16)
            accs_ref[...] = jnp.zeros_like(accs_ref)
            hp_ref[...] = _rms(xp_ref[...], g_ref[...]).astype(BF16)
            accp_ref[...] = jnp.zeros_like(accp_ref)

        wa = wa32_ref[...].astype(BF16)
        wb = wb32_ref[...].astype(BF16)
        wo = wo32_ref[...].astype(BF16)
        wa_ref[s] = wa
        wb_ref[s] = wb
        wo_ref[s] = wo
        for h_ref, acc_ref in ((hs_ref, accs_ref), (hp_ref, accp_ref)):
            h = h_ref[...]
            a = _dot(h, wa)
            b = _dot(h, wb)
            acc_ref[...] += _dot((a * jax.nn.sigmoid(a) * b).astype(BF16), wo)

        @pl.when(s == n_chunks - 1)
        def _():
            ys_ref[...] = finish(xs_ref[...], accs_ref[...])
            yp_ref[...] = finish(xp_ref[...], accp_ref[...])

    @pl.when(s >= n_chunks)
    def _():
        x = xp_ref[...]
        h = _rms(x, g_ref[...]).astype(BF16)
        for c in range(n_chunks):
            a = _dot(h, wa_ref[c])
            b = _dot(h, wb_ref[c])
            act_ref[:, c * FFN_CHUNK:(c + 1) * FFN_CHUNK] = (a * jax.nn.sigmoid(a) * b).astype(BF16)
        w_out = wo_ref[...].reshape(n_chunks * FFN_CHUNK, wo_ref.shape[2])
        yp_ref[...] = finish(x, _dot(act_ref[...], w_out))


def _ffn_fused(x_sample, x_prompt, g, ff_in, ff_out, layer, slot, g_final, *, final_norm, block_rows):
    rows_s, d = x_sample.shape
    rows_p = x_prompt.shape[0]
    d_ff = ff_out.shape[2]
    n_chunks = d_ff // FFN_CHUNK
    tm = min(block_rows, rows_p)
    const = lambda s: (0, 0)
    chunk = lambda s: jnp.minimum(s, n_chunks - 1)
    tile = lambda s: (jnp.maximum(s - (n_chunks - 1), 0), 0)
    body = functools.partial(_ffn_fused_body, final_norm=final_norm)
    return pl.pallas_call(
        body,
        grid=(n_chunks - 1 + rows_p // tm,),
        in_specs=[
            pl.BlockSpec((rows_s, d), const),
            pl.BlockSpec((tm, d), tile),
            pl.BlockSpec((1, d), const),
            pl.BlockSpec((None, None, d, FFN_CHUNK), lambda s: (layer, slot, 0, chunk(s))),
            pl.BlockSpec((None, None, d, FFN_CHUNK), lambda s: (layer, slot, 0, n_chunks + chunk(s))),
            pl.BlockSpec((None, None, FFN_CHUNK, d), lambda s: (layer, slot, chunk(s), 0)),
            pl.BlockSpec((1, d), const),
        ],
        out_specs=[pl.BlockSpec((rows_s, d), const), pl.BlockSpec((tm, d), tile)],
        out_shape=[jax.ShapeDtypeStruct((rows_s, d), F32), jax.ShapeDtypeStruct((rows_p, d), F32)],
        scratch_shapes=[
            pltpu.VMEM((n_chunks, d, FFN_CHUNK), BF16),
            pltpu.VMEM((n_chunks, d, FFN_CHUNK), BF16),
            pltpu.VMEM((n_chunks, FFN_CHUNK, d), BF16),
            pltpu.VMEM((rows_s, d), BF16),
            pltpu.VMEM((rows_s, d), F32),
            pltpu.VMEM((tm, d_ff), BF16),
            pltpu.VMEM((tm, d), BF16),
            pltpu.VMEM((tm, d), F32),
        ],
        compiler_params=pltpu.CompilerParams(
            dimension_semantics=("arbitrary",), vmem_limit_bytes=VMEM_LIMIT_BYTES),
        name="ffn_fused",
    )(x_sample, x_prompt, g.reshape(1, d), ff_in, ff_in, ff_out, g_final.reshape(1, d))


GLA_H = 4
GLA_DK = 64
GLA_DV = 128
GLA_QK = GLA_H * GLA_DK
GLA_V = GLA_H * GLA_DV
GLA_RANK_PAD = 128
GLA_INV_TAU = 1.0 / 16.0
GLA_CH = 16
GLA_GROUP = 16
GLA_PROJ_COLS = 256
SG_H = 4
SG_DH = 128
SG_W = SG_H * SG_DH
SG_CHUNK = 128
EV_Q, EV_V, EV_R, EV_U, EV_VS, EV_Z, EV_END = 0, 512, 1024, 1536, 2048, 2560, 2688


def _log_sigmoid(x):
    return jnp.minimum(x, 0.0) - jnp.log(1.0 + jnp.exp(-jnp.abs(x)))


def _split3(x):
    hi = x.astype(BF16)
    r1 = x - hi.astype(F32)
    mid = r1.astype(BF16)
    lo = (r1 - mid.astype(F32)).astype(BF16)
    return hi, mid, lo


def _dot3(sel, parts):
    return _dot(sel, parts[0]) + _dot(sel, parts[1]) + _dot(sel, parts[2])


def _head_masks(width, per_head, dtype):
    lane = lax.broadcasted_iota(jnp.int32, (1, width), 1)
    return [jnp.where(lane // per_head == h, 1.0, 0.0).astype(dtype) for h in range(width // per_head)]


def _layernorm(x, g, b):
    mu = jnp.mean(x, axis=-1, keepdims=True)
    xc = x - mu
    var = jnp.mean(xc * xc, axis=-1, keepdims=True)
    return xc * lax.rsqrt(var + EPS) * g + b


def _even_prompt_body(x_ref, g_ref, win_ref, wgate_ref, bgate_ref, glag_ref, lng_ref, lnb_ref,
                      sgw_ref, sgb_ref, e_ref, wout_ref,
                      xo_ref, gla_ref, sgv_ref,
                      st_ref, q_ref, k_ref, cum2_ref, qin_ref, kd_ref, dec_ref, v_ref, o_ref,
                      pcat_ref, acat_ref, mix_ref, add_ref, stb_ref, ruv_ref):
    j = pl.program_id(1)
    tt = x_ref.shape[0]
    n_chunks = tt // GLA_CH

    @pl.when(j == 0)
    def _():
        st_ref[...] = jnp.zeros_like(st_ref)

    x = x_ref[...]
    h = _rms(x, g_ref[...]).astype(BF16)

    group = min(n_chunks, GLA_GROUP)
    g_rows = group * GLA_CH
    row = lax.broadcasted_iota(jnp.int32, (g_rows, g_rows), 0)
    col = lax.broadcasted_iota(jnp.int32, (g_rows, g_rows), 1)
    sel_cum = jnp.where(((row // GLA_CH) == (col // GLA_CH)) & (col <= row), 1.0, 0.0).astype(BF16)

    def stage1(g0):
        def run():
            rs = slice(g0 * GLA_CH, g0 * GLA_CH + g_rows)
            hg = h[rs]
            qk = _dot(hg, win_ref[:, EV_Q:EV_V])
            q = qk[:, :GLA_QK] * (GLA_DK ** -0.5)
            k = qk[:, GLA_QK:]
            v_ref[rs, :] = _dot(hg, win_ref[:, EV_V:EV_R]).astype(BF16)
            z = _dot(hg, win_ref[:, EV_Z:EV_END]).astype(BF16)
            log_a = _log_sigmoid(_dot(z, wgate_ref[...]) + bgate_ref[...]) * GLA_INV_TAU
            cum = _dot3(sel_cum, _split3(log_a))
            tot = jnp.concatenate(
                [jnp.broadcast_to(cum[(n + 1) * GLA_CH - 1:(n + 1) * GLA_CH, :], (GLA_CH, GLA_QK))
                 for n in range(group)], axis=0)
            q_ref[rs, :] = q
            k_ref[rs, :] = k
            cum2_ref[rs, :] = cum * LOG2_E
            qin_ref[rs, :] = (q * jnp.exp(cum)).astype(BF16)
            kd_ref[rs, :] = (k * jnp.exp(tot - cum)).astype(BF16)
            dec_ref[rs, :] = jnp.exp(tot)
        return run

    stage1(0)()

    half = GLA_CH // 2
    qk_masks = _head_masks(GLA_QK, GLA_DK, BF16)
    v_masks = _head_masks(GLA_V, GLA_DV, BF16)
    state = [st_ref[...]]

    def score_products(n):
        base = n * GLA_CH
        qb = q_ref[base:base + GLA_CH, :]
        cb = cum2_ref[base:base + GLA_CH, :]
        for s in range(GLA_CH):
            ks = k_ref[base + s:base + s + 1, :]
            cs = cum2_ref[base + s:base + s + 1, :]
            if s < half:
                p = qb * ks * jnp.exp2(jnp.minimum(cb - cs, 0.0))
            else:
                p_hi = qb[half:] * ks * jnp.exp2(jnp.minimum(cb[half:] - cs, 0.0))
                p = jnp.concatenate([jnp.zeros_like(p_hi), p_hi], axis=0)
            pcat_ref[base:base + GLA_CH, s * GLA_QK:(s + 1) * GLA_QK] = p.astype(BF16)

    def score_sum(rows):
        scores = _dot(pcat_ref[rows, :], e_ref[...])
        t_loc = lax.broadcasted_iota(jnp.int32, scores.shape, 0) % GLA_CH
        c_idx = lax.broadcasted_iota(jnp.int32, scores.shape, 1)
        causal = (c_idx % GLA_CH <= t_loc) & (c_idx < GLA_H * GLA_CH)
        acat_ref[rows, :] = jnp.where(causal, scores, 0.0).astype(BF16)

    def projection_piece(c0):
        def run():
            ruv_ref[:, c0 - EV_R:c0 - EV_R + GLA_PROJ_COLS] = _dot(h, win_ref[:, c0:c0 + GLA_PROJ_COLS])
        return run

    def phase1(n):
        def run():
            rows = slice(n * GLA_CH, (n + 1) * GLA_CH)
            kn = kd_ref[rows, :]
            vn = v_ref[rows, :]
            lk = jnp.concatenate([kn * m for m in qk_masks], axis=0)
            vs = jnp.concatenate([vn[:, hh * GLA_DV:(hh + 1) * GLA_DV] for hh in range(GLA_H)], axis=0)
            add_ref[n] = lax.dot_general(vs, lk, (((0,), (0,)), ((), ())), preferred_element_type=F32)
        return run

    def phase2(n):
        def run():
            stb_ref[n] = state[0].astype(BF16).T
            state[0] = state[0] * dec_ref[n * GLA_CH:n * GLA_CH + 1, :] + add_ref[n]
        return run

    def phase3(n):
        def run():
            rows = slice(n * GLA_CH, (n + 1) * GLA_CH)
            qn = qin_ref[rows, :]
            vn = v_ref[rows, :]
            lq = jnp.concatenate([qn * m for m in qk_masks], axis=0)
            oi = _dot(lq, stb_ref[n])
            o_inter = jnp.concatenate([oi[hh * GLA_CH:(hh + 1) * GLA_CH] for hh in range(GLA_H)], axis=1)
            vbd = jnp.concatenate([vn * m for m in v_masks], axis=0)
            o_ref[rows, :] = o_inter + _dot(acat_ref[rows, 0:GLA_H * GLA_CH], vbd)
        return run

    side = [projection_piece(c0) for c0 in range(EV_R, EV_Z, GLA_PROJ_COLS)]
    u = v_ln = v_lb = None
    for g0 in range(0, n_chunks, group):
        chunks = range(g0, g0 + group)
        if g0 + group < n_chunks:
            side = [stage1(g0 + group)] + side
        for idx, n in enumerate(chunks):
            score_products(n)
            for task in side[idx * len(side) // group:(idx + 1) * len(side) // group]:
                task()
        score_sum(slice(g0 * GLA_CH, (g0 + group) * GLA_CH))
        if g0 == 0:
            u = jax.nn.gelu(ruv_ref[:, EV_U - EV_R:EV_VS - EV_R])
            v_ln = _layernorm(jax.nn.gelu(ruv_ref[:, EV_VS - EV_R:EV_Z - EV_R]), lng_ref[...], lnb_ref[...])
            v_lb = v_ln.astype(BF16)
        side = [phase(n) for phase in (phase1, phase2, phase3) for n in chunks]
    r = ruv_ref[:, 0:EV_U - EV_R]
    gate_r = r * jax.nn.sigmoid(r)
    sg_chunks = tt // SG_CHUNK
    sg_every = len(side) // SG_H
    for idx, task in enumerate(side):
        task()
        if idx % sg_every == 0 and idx // sg_every < SG_H:
            hh = idx // sg_every
            cols = slice(hh * SG_DH, (hh + 1) * SG_DH)
            mixed = _dot(sgw_ref[hh], jnp.concatenate(
                [v_lb[c * SG_CHUNK:(c + 1) * SG_CHUNK, cols] for c in range(sg_chunks)], axis=1))
            for c in range(sg_chunks):
                mix_ref[c * SG_CHUNK:(c + 1) * SG_CHUNK, cols] = mixed[:, c * SG_DH:(c + 1) * SG_DH]
    st_ref[...] = state[0]

    o = o_ref[...]
    o_n = jnp.concatenate(
        [_rms(o[:, hh * GLA_DV:(hh + 1) * GLA_DV], 1.0) for hh in range(GLA_H)], axis=1) * glag_ref[...]
    out_a = o_n * gate_r
    bias = jnp.concatenate([sgb_ref[...]] * (tt // SG_CHUNK), axis=0)
    out_b = u * (mix_ref[...] + bias)

    y = _dot(jnp.concatenate([out_a, out_b], axis=1).astype(BF16), wout_ref[...])
    xo_ref[...] = x + y

    @pl.when(j == pl.num_programs(1) - 1)
    def _():
        sgv_ref[0] = v_ln[tt - SG_CHUNK:, :]
        gla_ref[0] = st_ref[...].T.reshape(GLA_H, GLA_DK, GLA_DV)


def _score_sum_matrix():
    r = np.arange(GLA_CH * GLA_QK)
    c = np.arange(LANES)
    s, hh = r // GLA_QK, (r % GLA_QK) // GLA_DK
    return jnp.asarray((c[None, :] == (hh * GLA_CH + s)[:, None]).astype(np.float32), dtype=BF16)


def _even_prompt(x, batch, g, w_in, w_gate, b_gate, gla_g, ln_g, ln_b, sg_w, sg_bias, w_out, *, block_rows):
    rows, d = x.shape
    t = rows // batch
    tt = min(block_rows, t)
    nj = t // tt
    full = lambda a: pl.BlockSpec(a.shape, (lambda b, j: (0,) * a.ndim), pipeline_mode=pl.Buffered(1))
    e = _score_sum_matrix()
    operands = (g, w_in, w_gate, b_gate, gla_g, ln_g, ln_b, sg_w, sg_bias, e, w_out)
    return pl.pallas_call(
        _even_prompt_body,
        grid=(batch, nj),
        in_specs=[pl.BlockSpec((tt, d), lambda b, j: (b * nj + j, 0))] + [full(a) for a in operands],
        out_specs=[
            pl.BlockSpec((tt, d), lambda b, j: (b * nj + j, 0)),
            pl.BlockSpec((1, GLA_H, GLA_DK, GLA_DV), lambda b, j: (b, 0, 0, 0)),
            pl.BlockSpec((1, SG_CHUNK, SG_W), lambda b, j: (b, 0, 0)),
        ],
        out_shape=[
            jax.ShapeDtypeStruct((rows, d), F32),
            jax.ShapeDtypeStruct((batch, GLA_H, GLA_DK, GLA_DV), F32),
            jax.ShapeDtypeStruct((batch, SG_CHUNK, SG_W), F32),
        ],
        scratch_shapes=[
            pltpu.VMEM((GLA_DV, GLA_QK), F32),
            pltpu.VMEM((tt, GLA_QK), F32),
            pltpu.VMEM((tt, GLA_QK), F32),
            pltpu.VMEM((tt, GLA_QK), F32),
            pltpu.VMEM((tt, GLA_QK), BF16),
            pltpu.VMEM((tt, GLA_QK), BF16),
            pltpu.VMEM((tt, GLA_QK), F32),
            pltpu.VMEM((tt, GLA_V), BF16),
            pltpu.VMEM((tt, GLA_V), F32),
            pltpu.VMEM((tt, GLA_CH * GLA_QK), BF16),
            pltpu.VMEM((tt, LANES), BF16),
            pltpu.VMEM((tt, SG_W), F32),
            pltpu.VMEM((tt // GLA_CH, GLA_DV, GLA_QK), F32),
            pltpu.VMEM((tt // GLA_CH, GLA_QK, GLA_DV), BF16),
            pltpu.VMEM((tt, EV_Z - EV_R), F32),
        ],
        compiler_params=pltpu.CompilerParams(
            dimension_semantics=("arbitrary", "arbitrary"), vmem_limit_bytes=VMEM_LIMIT_BYTES),
        name="even_prompt",
    )(x, *operands)


def _prep_even(b_gate, gla_g, ln_g, ln_b, sg_w, sg_b):
    return dict(
        b_gate=b_gate.reshape(1, -1), b_gate_col=b_gate.reshape(-1, 1), gla_g=gla_g.reshape(1, -1),
        ln_g=ln_g.reshape(1, -1), ln_b=ln_b.reshape(1, -1),
        sg_bias=jnp.repeat(jnp.transpose(sg_b), SG_DH, axis=1),
        sg_w0=jnp.repeat(sg_w[:, 0, 0], SG_DH).reshape(1, -1), sg_b0=jnp.repeat(sg_b[:, 0], SG_DH).reshape(1, -1))


def _even_sample_body(x_ref, s_ref, g_ref, win32_ref, wgate32_ref, wout32_ref, sgw32_ref, bgate_ref,
                      bgatec_ref, glag_ref, lng_ref, lnb_ref, sgw0_ref, sgb0_ref,
                      xo_ref, so_ref, sgv_ref, win_ref, wgate_ref, wout_ref, sgw_ref,
                      o_ref, wint_ref, wgatet_ref):
    bb = x_ref.shape[0]

    @pl.when(pl.program_id(0) == 0)
    def _():
        z0 = 2 * GLA_QK + 2 * GLA_V
        rank = wgate32_ref.shape[0]
        wint_ref[0:z0] = win32_ref[0:z0].astype(BF16)
        wint_ref[z0:EV_Z] = win32_ref[z0 + rank:].astype(BF16)
        wint_ref[EV_Z:EV_Z + rank] = win32_ref[z0:z0 + rank].astype(BF16)
        wint_ref[EV_Z + rank:EV_END] = jnp.zeros((EV_END - EV_Z - rank, wint_ref.shape[1]), BF16)
        for c0 in range(0, EV_END, LANES):
            win_ref[:, c0:c0 + LANES] = wint_ref[c0:c0 + LANES, :].T
        wgate_ref[...] = jnp.concatenate(
            [wgate32_ref[...].astype(BF16), jnp.zeros((GLA_RANK_PAD - rank, GLA_QK), BF16)], axis=0)
        wgatet_ref[...] = wgate_ref[...].T
        wout_ref[...] = wout32_ref[...].astype(BF16)
        row = lax.broadcasted_iota(jnp.int32, (SG_CHUNK, SG_CHUNK), 0)
        col = lax.broadcasted_iota(jnp.int32, (SG_CHUNK, SG_CHUNK), 1)
        for hh in range(SG_H):
            sgw_ref[hh] = jnp.where(col <= row, sgw32_ref[hh], 0.0).astype(BF16)

    x = x_ref[...]
    h = _rms(x, g_ref[...]).astype(BF16)
    nt = (((1,), (1,)), ((), ()))
    k_t = lax.dot_general(wint_ref[EV_Q + GLA_QK:EV_V], h, nt, preferred_element_type=F32)
    k_t = k_t.astype(BF16).astype(F32)
    z_t = lax.dot_general(wint_ref[EV_Z:EV_END], h, nt, preferred_element_type=F32).astype(BF16)
    a_t = jnp.exp(_log_sigmoid(_dot(wgatet_ref[...], z_t) + bgatec_ref[...]) * GLA_INV_TAU)
    q = _dot(h, win_ref[:, EV_Q:EV_Q + GLA_QK]) * (GLA_DK ** -0.5)
    v = _dot(h, win_ref[:, EV_V:EV_R])
    v_r = v.astype(BF16).astype(F32)
    head_rows = jnp.concatenate(_head_masks(GLA_QK, GLA_DK, F32), axis=0)
    for b in range(bb):
        s_old = s_ref[b].reshape(GLA_QK, GLA_DV)
        a_c = jnp.broadcast_to(a_t[:, b:b + 1], (GLA_QK, GLA_DV))
        k_c = jnp.broadcast_to(k_t[:, b:b + 1], (GLA_QK, GLA_DV))
        v_rows = jnp.concatenate(
            [jnp.broadcast_to(v_r[b:b + 1, hh * GLA_DV:(hh + 1) * GLA_DV], (GLA_DK, GLA_DV))
             for hh in range(GLA_H)], axis=0)
        s_new = a_c * s_old + k_c * v_rows
        so_ref[b] = s_new.reshape(GLA_H, GLA_DK, GLA_DV)
        ob = _dot((q[b:b + 1, :] * head_rows).astype(BF16), s_new.astype(BF16))
        o_ref[b:b + 1, :] = jnp.concatenate([ob[hh:hh + 1] for hh in range(GLA_H)], axis=1)
    o = o_ref[...]
    o_n = jnp.concatenate(
        [_rms(o[:, hh * GLA_DV:(hh + 1) * GLA_DV], 1.0) for hh in range(GLA_H)], axis=1) * glag_ref[...]
    r = _dot(h, win_ref[:, EV_R:EV_U])
    out_a = o_n * (r * jax.nn.sigmoid(r))
    u = jax.nn.gelu(_dot(h, win_ref[:, EV_U:EV_VS]))
    v_ln = _layernorm(jax.nn.gelu(_dot(h, win_ref[:, EV_VS:EV_Z])), lng_ref[...], lnb_ref[...])
    sgv_ref[...] = v_ln
    out_b = u * (sgw0_ref[...] * v_ln + sgb0_ref[...])
    y = _dot(jnp.concatenate([out_a, out_b], axis=1).astype(BF16), wout_ref[...])
    xo_ref[...] = x + y


def _even_sample(x, state, g, w_in_t, w_gate, w_out, sg_w, ev, *, block_rows=32):
    rows, d = x.shape
    bb = min(block_rows, rows)
    full = lambda a: pl.BlockSpec(a.shape, (lambda i: (0,) * a.ndim), pipeline_mode=pl.Buffered(1))
    whole = lambda shape: pl.BlockSpec(shape, (lambda i: (0,) * len(shape)))
    operands = (g, w_in_t, w_gate, w_out, sg_w, ev["b_gate"], ev["b_gate_col"], ev["gla_g"], ev["ln_g"],
                ev["ln_b"], ev["sg_w0"], ev["sg_b0"])
    w_shapes = [(d, EV_END), (GLA_RANK_PAD, GLA_QK), w_out.shape, sg_w.shape]
    return pl.pallas_call(
        _even_sample_body,
        grid=(rows // bb,),
        in_specs=[pl.BlockSpec((bb, d), lambda i: (i, 0)),
                  pl.BlockSpec((bb, GLA_H, GLA_DK, GLA_DV), lambda i: (i, 0, 0, 0))]
                 + [full(a) for a in operands],
        out_specs=[
            pl.BlockSpec((bb, d), lambda i: (i, 0)),
            pl.BlockSpec((bb, GLA_H, GLA_DK, GLA_DV), lambda i: (i, 0, 0, 0)),
            pl.BlockSpec((bb, SG_W), lambda i: (i, 0)),
        ] + [whole(s) for s in w_shapes],
        out_shape=[
            jax.ShapeDtypeStruct((rows, d), F32),
            jax.ShapeDtypeStruct(state.shape, F32),
            jax.ShapeDtypeStruct((rows, SG_W), F32),
        ] + [jax.ShapeDtypeStruct(s, BF16) for s in w_shapes],
        scratch_shapes=[
            pltpu.VMEM((bb, GLA_V), F32),
            pltpu.VMEM((EV_END, d), BF16),
            pltpu.VMEM((GLA_QK, GLA_RANK_PAD), BF16),
        ],
        compiler_params=pltpu.CompilerParams(
            dimension_semantics=("arbitrary",), vmem_limit_bytes=VMEM_LIMIT_BYTES),
        name="even_sample",
    )(x, state, *operands)


CONV_W = 512
CONV_K = 31
CONV_BUF = CONV_K - 1
CONV_PAD = 32
POOL_W = 512
POOL_WINDOWS = (2, 4, 8, 16)
POOL_DG = POOL_W // len(POOL_WINDOWS)
POOL_BUF = 15
POOL_PAD = 16
CONV_ROWS = 16
ODD_SUB = 128
ODD_COLS = 256


def _odd_prompt_body(x_ref, g_ref, win_ref, cw_ref, cb_ref, lng_ref, lnb_ref, pw_ref, ps_ref, wout_ref,
                     xo_ref, conv_ref, pool_ref, gbuf_ref, pbuf_ref, shift_ref, cwb_ref, psum_ref,
                     h_ref, mixin_ref, pooled_ref, raw_ref, convo_ref):
    j = pl.program_id(1)
    tt = x_ref.shape[0]
    assert POOL_WINDOWS == (2, 4, 8, 16)

    p0 = SUBLANES + POOL_PAD

    @pl.when(j == 0)
    def _():
        gbuf_ref[0:CONV_PAD, :] = jnp.zeros((CONV_PAD, CONV_W), F32)
        pbuf_ref[0:p0, :] = jnp.zeros((p0, POOL_W), F32)
        psum_ref[:, 0:SUBLANES, :] = jnp.zeros((psum_ref.shape[0], SUBLANES, POOL_W), F32)
        for o in range(CONV_K):
            cwb_ref[o] = jnp.broadcast_to(cw_ref[o:o + 1, :], (SUBLANES, CONV_W))

    h_ref[...] = _rms(x_ref[...], g_ref[...]).astype(BF16)
    first = CONV_PAD - CONV_BUF
    groups = CONV_ROWS // SUBLANES
    n_sub = tt // ODD_SUB

    def project(k):
        rows = slice(k * ODD_SUB, (k + 1) * ODD_SUB)

        def piece(c0):
            def run():
                raw_ref[rows, c0:c0 + ODD_COLS] = _dot(h_ref[rows, :], win_ref[:, c0:c0 + ODD_COLS])
            return run

        return [piece(c0) for c0 in range(0, 2 * CONV_W, ODD_COLS)]

    def glu(k):
        rows = slice(k * ODD_SUB, (k + 1) * ODD_SUB)
        gbuf_ref[CONV_PAD + k * ODD_SUB:CONV_PAD + (k + 1) * ODD_SUB, :] = (
            raw_ref[rows, 0:CONV_W] * jax.nn.sigmoid(raw_ref[rows, CONV_W:2 * CONV_W]))

    def mix(k):
        r_lo = k * ODD_SUB

        def shift_copies():
            lo = 0 if k == 0 else r_lo + CONV_PAD - SUBLANES
            hi = r_lo + ODD_SUB + CONV_PAD - SUBLANES
            for rr in range(1, SUBLANES):
                shift_ref[rr - 1, lo:hi, :] = gbuf_ref[rr + lo:rr + hi, :]

        def conv_block(r0):
            def run():
                accs = [jnp.zeros((SUBLANES, CONV_W), F32) + cb_ref[...] for _ in range(groups)]
                for o in range(first, first + CONV_K):
                    rr = o % SUBLANES
                    w8 = cwb_ref[o - first]
                    for gq in range(groups):
                        lo = o - rr + r0 + gq * SUBLANES
                        src = (gbuf_ref[lo:lo + SUBLANES, :] if rr == 0
                               else shift_ref[rr - 1, lo:lo + SUBLANES, :])
                        accs[gq] = accs[gq] + src * w8
                for gq in range(groups):
                    convo_ref[r0 + gq * SUBLANES:r0 + (gq + 1) * SUBLANES, :] = accs[gq]
            return run

        def norm_act():
            out_c = _layernorm(convo_ref[r_lo:r_lo + ODD_SUB, :], lng_ref[...], lnb_ref[...])
            mixin_ref[r_lo:r_lo + ODD_SUB, 0:CONV_W] = (out_c * jax.nn.sigmoid(out_c)).astype(BF16)

        def pooling():
            lo = SUBLANES if k == 0 else p0 + r_lo
            hi = p0 + r_lo + ODD_SUB
            psum_ref[0, lo:hi, :] = pbuf_ref[lo:hi, :] + pbuf_ref[lo - 1:hi - 1, :]
            psum_ref[1, lo:hi, POOL_DG:] = psum_ref[0, lo:hi, POOL_DG:] + psum_ref[0, lo - 2:hi - 2, POOL_DG:]
            psum_ref[2, lo:hi, 2 * POOL_DG:] = (psum_ref[1, lo:hi, 2 * POOL_DG:]
                                                + psum_ref[1, lo - 4:hi - 4, 2 * POOL_DG:])
            o_lo = p0 + r_lo
            o_hi = o_lo + ODD_SUB
            tots = [psum_ref[0, o_lo:o_hi, 0:POOL_DG],
                    psum_ref[1, o_lo:o_hi, POOL_DG:2 * POOL_DG],
                    psum_ref[2, o_lo:o_hi, 2 * POOL_DG:3 * POOL_DG],
                    psum_ref[2, o_lo:o_hi, 3 * POOL_DG:] + psum_ref[2, o_lo - 8:o_hi - 8, 3 * POOL_DG:]]
            t_glob = j * tt + r_lo + lax.broadcasted_iota(jnp.int32, (ODD_SUB, 1), 0)
            for gi, win_len in enumerate(POOL_WINDOWS):
                lanes = slice(gi * POOL_DG, (gi + 1) * POOL_DG)
                cnt = jnp.minimum(win_len, t_glob + 1).astype(F32)
                pooled = tots[gi] / cnt - pbuf_ref[o_lo:o_hi, lanes]
                pooled_ref[r_lo:r_lo + ODD_SUB, lanes] = pooled.astype(BF16)

        return ([shift_copies] + [conv_block(r0) for r0 in range(r_lo, r_lo + ODD_SUB, CONV_ROWS)]
                + [norm_act, pooling])

    def output(k):
        rows = slice(k * ODD_SUB, (k + 1) * ODD_SUB)

        def pool_matmuls():
            outs = [_dot(pooled_ref[rows, gi * POOL_DG:(gi + 1) * POOL_DG], pw_ref[gi])
                    for gi in range(len(POOL_WINDOWS))]
            mixin_ref[rows, CONV_W:] = (jnp.concatenate(outs, axis=1) * ps_ref[...]).astype(BF16)

        def out_piece(c0):
            def run():
                xo_ref[rows, c0:c0 + ODD_COLS] = x_ref[rows, c0:c0 + ODD_COLS] + _dot(
                    mixin_ref[rows, :], wout_ref[:, c0:c0 + ODD_COLS])
            return run

        return [pool_matmuls] + [out_piece(c0) for c0 in range(0, xo_ref.shape[1], ODD_COLS)]

    pbuf_ref[p0:p0 + tt, :] = _dot(h_ref[...], win_ref[:, 2 * CONV_W:])
    for task in project(0):
        task()
    glu(0)
    for k in range(n_sub):
        valu_tasks = mix(k)
        mxu_tasks = output(k - 1) if k >= 1 else []
        if k + 1 < n_sub:
            mxu_tasks = project(k + 1) + mxu_tasks
            valu_tasks = valu_tasks + [functools.partial(glu, k + 1)]
        for idx, task in enumerate(valu_tasks):
            lo_m = idx * len(mxu_tasks) // len(valu_tasks)
            hi_m = (idx + 1) * len(mxu_tasks) // len(valu_tasks)
            for m_task in mxu_tasks[lo_m:hi_m]:
                m_task()
            task()
    for task in output(n_sub - 1):
        task()

    tail_g = gbuf_ref[tt:tt + CONV_PAD, :]
    tail_p = pbuf_ref[SUBLANES + tt:p0 + tt, :]
    gbuf_ref[0:CONV_PAD, :] = tail_g
    pbuf_ref[SUBLANES:p0, :] = tail_p

    @pl.when(j == pl.num_programs(1) - 1)
    def _():
        conv_ref[0] = tail_g[CONV_PAD - CONV_BUF:, :]
        pool_ref[0] = tail_p[POOL_PAD - POOL_BUF:, :]


def _odd_prompt(x, batch, g, od, *, block_rows):
    rows, d = x.shape
    t = rows // batch
    tt = min(block_rows, t)
    nj = t // tt
    full = lambda a: pl.BlockSpec(a.shape, (lambda b, j: (0,) * a.ndim), pipeline_mode=pl.Buffered(1))
    operands = (g, od["w_in"], od["conv_w"], od["conv_b"], od["ln_g"], od["ln_b"], od["pool_w"],
                od["pool_scale"], od["w_out"])
    return pl.pallas_call(
        _odd_prompt_body,
        grid=(batch, nj),
        in_specs=[pl.BlockSpec((tt, d), lambda b, j: (b * nj + j, 0))] + [full(a) for a in operands],
        out_specs=[
            pl.BlockSpec((tt, d), lambda b, j: (b * nj + j, 0)),
            pl.BlockSpec((1, CONV_BUF, CONV_W), lambda b, j: (b, 0, 0)),
            pl.BlockSpec((1, POOL_BUF, POOL_W), lambda b, j: (b, 0, 0)),
        ],
        out_shape=[
            jax.ShapeDtypeStruct((rows, d), F32),
            jax.ShapeDtypeStruct((batch, CONV_BUF, CONV_W), F32),
            jax.ShapeDtypeStruct((batch, POOL_BUF, POOL_W), F32),
        ],
        scratch_shapes=[
            pltpu.VMEM((CONV_PAD + tt + SUBLANES, CONV_W), F32),
            pltpu.VMEM((SUBLANES + POOL_PAD + tt, POOL_W), F32),
            pltpu.VMEM((SUBLANES - 1, tt + CONV_PAD - SUBLANES, CONV_W), F32),
            pltpu.VMEM((CONV_K, SUBLANES, CONV_W), F32),
            pltpu.VMEM((3, SUBLANES + POOL_PAD + tt, POOL_W), F32),
            pltpu.VMEM((tt, d), BF16),
            pltpu.VMEM((tt, CONV_W + POOL_W), BF16),
            pltpu.VMEM((tt, POOL_W), BF16),
            pltpu.VMEM((tt, 2 * CONV_W), F32),
            pltpu.VMEM((tt, CONV_W), F32),
        ],
        compiler_params=pltpu.CompilerParams(
            dimension_semantics=("arbitrary", "arbitrary"), vmem_limit_bytes=VMEM_LIMIT_BYTES),
        name="odd_prompt",
    )(x, *operands)


def _odd_sample_body(x_ref, cbuf_ref, pbuf_ref, g_ref, win32_ref, cw_ref, cb_ref, lng_ref, lnb_ref, pw32_ref,
                     ps_ref, wout32_ref, xo_ref, conv_ref, pool_ref, win_ref, pw_ref, wout_ref):
    @pl.when(pl.program_id(0) == 0)
    def _():
        win_ref[...] = win32_ref[...].astype(BF16)
        pw_ref[...] = pw32_ref[...].astype(BF16)
        wout_ref[...] = wout32_ref[...].astype(BF16)

    x = x_ref[...]
    h = _rms(x, g_ref[...]).astype(BF16)
    a = _dot(h, win_ref[:, 0:CONV_W])
    gt = _dot(h, win_ref[:, CONV_W:2 * CONV_W])
    xp = _dot(h, win_ref[:, 2 * CONV_W:])
    glu = a * jax.nn.sigmoid(gt)
    conv = glu * cw_ref[CONV_BUF:CONV_K, :] + cb_ref[...]
    for jj in range(CONV_BUF):
        conv = conv + cbuf_ref[jj] * cw_ref[jj:jj + 1, :]
    out_c = _layernorm(conv, lng_ref[...], lnb_ref[...])
    out_c = out_c * jax.nn.sigmoid(out_c)
    outs = []
    for gi, win_len in enumerate(POOL_WINDOWS):
        lanes = slice(gi * POOL_DG, (gi + 1) * POOL_DG)
        tot = xp[:, lanes]
        for jj in range(POOL_BUF - (win_len - 1), POOL_BUF):
            tot = tot + pbuf_ref[jj, :, lanes]
        pooled = tot / float(win_len) - xp[:, lanes]
        outs.append(_dot(pooled.astype(BF16), pw_ref[gi]))
    out_d = jnp.concatenate(outs, axis=1) * ps_ref[...]
    y = _dot(jnp.concatenate([out_c, out_d], axis=1).astype(BF16), wout_ref[...])
    xo_ref[...] = x + y
    conv_ref[0:CONV_BUF - 1] = cbuf_ref[1:CONV_BUF]
    conv_ref[CONV_BUF - 1] = glu
    pool_ref[0:POOL_BUF - 1] = pbuf_ref[1:POOL_BUF]
    pool_ref[POOL_BUF - 1] = xp


def _odd_sample(x, conv_buf, pool_buf, g, od, *, block_rows=32):
    rows, d = x.shape
    bb = min(block_rows, rows)
    full = lambda a: pl.BlockSpec(a.shape, (lambda i: (0,) * a.ndim), pipeline_mode=pl.Buffered(1))
    whole = lambda shape: pl.BlockSpec(shape, (lambda i: (0,) * len(shape)))
    operands = (g, od["w_in"], od["conv_w"], od["conv_b"], od["ln_g"], od["ln_b"], od["pool_w"],
                od["pool_scale"], od["w_out"])
    w_shapes = [od["w_in"].shape, od["pool_w"].shape, od["w_out"].shape]
    return pl.pallas_call(
        _odd_sample_body,
        grid=(rows // bb,),
        in_specs=[pl.BlockSpec((bb, d), lambda i: (i, 0)),
                  pl.BlockSpec((CONV_BUF, bb, CONV_W), lambda i: (0, i, 0)),
                  pl.BlockSpec((POOL_BUF, bb, POOL_W), lambda i: (0, i, 0))]
                 + [full(a) for a in operands],
        out_specs=[
            pl.BlockSpec((bb, d), lambda i: (i, 0)),
            pl.BlockSpec((CONV_BUF, bb, CONV_W), lambda i: (0, i, 0)),
            pl.BlockSpec((POOL_BUF, bb, POOL_W), lambda i: (0, i, 0)),
        ] + [whole(s) for s in w_shapes],
        out_shape=[
            jax.ShapeDtypeStruct((rows, d), F32),
            jax.ShapeDtypeStruct(conv_buf.shape, F32),
            jax.ShapeDtypeStruct(pool_buf.shape, F32),
        ] + [jax.ShapeDtypeStruct(s, BF16) for s in w_shapes],
        compiler_params=pltpu.CompilerParams(
            dimension_semantics=("arbitrary",), vmem_limit_bytes=VMEM_LIMIT_BYTES),
        name="odd_sample",
    )(x, conv_buf, pool_buf, *operands)


def _prep_odd(w_in, conv_w, conv_b, ln_g, ln_b, pool_w, pool_scale, w_out):
    return dict(w_in=w_in, conv_w=conv_w, conv_b=conv_b.reshape(1, -1), ln_g=ln_g.reshape(1, -1),
                ln_b=ln_b.reshape(1, -1), pool_w=pool_w, pool_scale=pool_scale.reshape(1, -1), w_out=w_out)


def kernel(x_prompt, x_sample, state_gla, state_conv, state_pool, norm_g, ff_in, ff_out, ev_w_in, ev_w_gate, ev_b_gate, ev_gla_g, ev_sg_ln_g, ev_sg_ln_b, ev_sg_w, ev_sg_b, ev_w_out, od_w_in, od_conv_w, od_conv_b, od_ln_g, od_ln_b, od_pool_w, od_pool_scale, od_w_out, norm_f):
    bp, t, d = x_prompt.shape
    bs = x_sample.shape[0]
    depth = norm_g.shape[0]
    xp = x_prompt.reshape(bp * t, d)
    xs = x_sample.reshape(bs, d)
    gla_p, gla_s, sgv_p, sgv_s, conv_p, conv_s, pool_p, pool_s = [], [], [], [], [], [], [], []
    for layer in range(depth):
        i = layer // 2
        last = layer == depth - 1
        xs, xp = _ffn_fused(xs, xp, norm_g[layer, 0], ff_in, ff_out, layer, 0, norm_f, final_norm=False,
                            block_rows=FFN_ROWS)
        g_mix = norm_g[layer, 1].reshape(1, d)
        if layer % 2 == 0:
            ev = _prep_even(ev_b_gate[i], ev_gla_g[i], ev_sg_ln_g[i], ev_sg_ln_b[i], ev_sg_w[i], ev_sg_b[i])
            xs, s_s, v_s, w_in_b, w_gate_b, w_out_b, sg_w_b = _even_sample(
                xs, state_gla[i], g_mix, jnp.transpose(ev_w_in[i]), ev_w_gate[i], ev_w_out[i], ev_sg_w[i], ev)
            xp, s_p, v_p = _even_prompt(xp, bp, g_mix, w_in_b, w_gate_b, ev["b_gate"], ev["gla_g"],
                                        ev["ln_g"], ev["ln_b"], sg_w_b, ev["sg_bias"], w_out_b,
                                        block_rows=512)
            gla_p.append(s_p); gla_s.append(s_s); sgv_p.append(v_p); sgv_s.append(v_s.reshape(bs, 1, SG_W))
        else:
            od = _prep_odd(od_w_in[i], od_conv_w[i], od_conv_b[i], od_ln_g[i], od_ln_b[i], od_pool_w[i],
                           od_pool_scale[i], od_w_out[i])
            xs, c_s, p_s, w_in_b, pool_w_b, w_out_b = _odd_sample(
                xs, jnp.transpose(state_conv[i], (1, 0, 2)), jnp.transpose(state_pool[i], (1, 0, 2)), g_mix, od)
            c_s, p_s = jnp.transpose(c_s, (1, 0, 2)), jnp.transpose(p_s, (1, 0, 2))
            xp, c_p, p_p = _odd_prompt(xp, bp, g_mix, dict(od, w_in=w_in_b, pool_w=pool_w_b, w_out=w_out_b),
                                       block_rows=1024)
            conv_p.append(c_p); conv_s.append(c_s); pool_p.append(p_p); pool_s.append(p_s)
        xs, xp = _ffn_fused(xs, xp, norm_g[layer, 2], ff_in, ff_out, layer, 1, norm_f, final_norm=last,
                            block_rows=FFN_ROWS)
    return (xp.reshape(bp, t, d), xs.reshape(bs, 1, d), jnp.stack(gla_p), jnp.stack(gla_s),
            jnp.stack(sgv_p), jnp.stack(sgv_s), jnp.stack(conv_p), jnp.stack(conv_s),
            jnp.stack(pool_p), jnp.stack(pool_s))
```
